```python
import math
import jax, jax.numpy as jnp
from jax import lax
import numpy as np

D_MODEL = 1024
BATCH = 16
SEQ = 256
DEPTH = 2
DEC_BATCH = 8
DEC_SEQ = 1024
PAST_LEN = 512

GRID_W = 64
N_MIXERS = 2
N_GLA_LAYERS = (DEPTH + 1) // 2
N_DIFF_LAYERS = DEPTH // 2
GLA_HEADS = 4
GLA_DK = D_MODEL // 2 // GLA_HEADS
GLA_DV = D_MODEL // GLA_HEADS
GLA_HK = GLA_HEADS * GLA_DK
GLA_HV = GLA_HEADS * GLA_DV
GLA_RANK = 16
GLA_TAU = 16.0
GLA_CHUNK = 64
DIFF_HEADS = 8
DIFF_DH = D_MODEL // DIFF_HEADS // 2
DIFF_DV = 2 * DIFF_DH
ROPE_THETA = 10000.0
Q_BLOCK = 128
D_FF = (((8 * D_MODEL + 2) // 3 + 255) // 256) * 256
EPS = 1e-6

kernel_name = 'hybrid_gla_diffattn_diffusion_step'


def rmsnorm(x, g):
    x32 = x.astype(jnp.float32)
    y = x32 * lax.rsqrt(jnp.mean(x32 * x32, axis=-1, keepdims=True) + EPS)
    return (y * g.astype(jnp.float32)).astype(x.dtype)


def adaln(cond, w, b):
    m = jax.nn.silu(cond) @ w + b
    return [t[..., None, :] for t in jnp.split(m, 6, axis=-1)]


def axial_rope(n_tokens, dim, dtype):
    rows = n_tokens // GRID_W
    r, col = jnp.meshgrid(jnp.arange(rows), jnp.arange(GRID_W), indexing='ij')
    r = r.reshape(-1).astype(jnp.float32)
    col = col.reshape(-1).astype(jnp.float32)
    n_freq = dim // 4
    inv = ROPE_THETA ** (-jnp.arange(n_freq, dtype=jnp.float32) / n_freq)
    ang = jnp.concatenate([r[:, None] * inv, col[:, None] * inv], axis=-1)
    return jnp.cos(ang).astype(dtype), jnp.sin(ang).astype(dtype)


def apply_rope(x, cos, sin):
    half = x.shape[-1] // 2
    x1, x2 = x[..., :half], x[..., half:]
    cs = cos[None, :, None, None, :]
    sn = sin[None, :, None, None, :]
    return jnp.concatenate([x1 * cs - x2 * sn, x2 * cs + x1 * sn], axis=-1)


def gla_scan(q, k, v, log_a, s0):
    B, T, H, DK = q.shape
    DV = v.shape[-1]
    C = min(GLA_CHUNK, T)
    N = T // C
    f32 = jnp.float32
    q = q.astype(f32).reshape(B, N, C, H, DK) * (DK ** -0.5)
    k = k.astype(f32).reshape(B, N, C, H, DK)
    v = v.astype(f32).reshape(B, N, C, H, DV)
    b = jnp.cumsum(log_a.reshape(B, N, C, H, DK), axis=2)
    b_last = b[:, :, -1]
    q_d = q * jnp.exp(b)
    k_d = k * jnp.exp(-b)
    k_end = k * jnp.exp(b_last[:, :, None] - b)
    mask = jnp.tril(jnp.ones((C, C), dtype=bool))
    A = jnp.where(mask, jnp.einsum('bnthd,bnshd->bnhts', q_d, k_d), 0.0)
    o_intra = jnp.einsum('bnhts,bnshv->bnthv', A, v)
    dS = jnp.einsum('bnshd,bnshv->bnhdv', k_end, v)
    decay = jnp.exp(b_last)

    def step(S, inp):
        dec, ds = inp
        return dec[..., None] * S + ds, S

    s_final, s_in = lax.scan(step, s0.astype(f32), (jnp.moveaxis(decay, 1, 0), jnp.moveaxis(dS, 1, 0)))
    s_in = jnp.moveaxis(s_in, 0, 1)
    o_inter = jnp.einsum('bnthd,bnhdv->bnthv', q_d, s_in)
    return (o_intra + o_inter).reshape(B, T, H, DV), s_final


def gla_mixer(h, s0f, s0b, w_in, w_g1, w_g2, b_g, g_head, w_out):
    B, T, _ = h.shape
    z = h @ w_in
    q = z[..., :GLA_HK].reshape(B, T, GLA_HEADS, GLA_DK)
    k = z[..., GLA_HK:2 * GLA_HK].reshape(B, T, GLA_HEADS, GLA_DK)
    v = z[..., 2 * GLA_HK:2 * GLA_HK + GLA_HV].reshape(B, T, GLA_HEADS, GLA_DV)
    r = z[..., 2 * GLA_HK + GLA_HV:]
    la_f = (jax.nn.log_sigmoid(((h @ w_g1[0]) @ w_g2[0] + b_g[0]).astype(jnp.float32)) / GLA_TAU).reshape(B, T, GLA_HEADS, GLA_DK)
    la_b = (jax.nn.log_sigmoid(((h @ w_g1[1]) @ w_g2[1] + b_g[1]).astype(jnp.float32)) / GLA_TAU).reshape(B, T, GLA_HEADS, GLA_DK)
    o_f, s_f = gla_scan(q, k, v, la_f, s0f)
    o_b, s_b = gla_scan(q[:, ::-1], k[:, ::-1], v[:, ::-1], la_b[:, ::-1], s0b)
    o = rmsnorm(o_f + o_b[:, ::-1], g_head).astype(h.dtype).reshape(B, T, GLA_HV)
    return (o * jax.nn.silu(r)) @ w_out, s_f, s_b


def diff_qkv(h, w_qkv):
    B, T, _ = h.shape
    z = h @ w_qkv
    q = z[..., :D_MODEL].reshape(B, T, DIFF_HEADS, 2, DIFF_DH)
    k = z[..., D_MODEL:2 * D_MODEL].reshape(B, T, DIFF_HEADS, 2, DIFF_DH)
    v = z[..., 2 * D_MODEL:].reshape(B, T, DIFF_HEADS, DIFF_DV)
    return q, k, v


def diff_lambda(lq, lk, lam_init):
    lq = lq.astype(jnp.float32)
    lk = lk.astype(jnp.float32)
    return jnp.exp(jnp.sum(lq[0] * lk[0])) - jnp.exp(jnp.sum(lq[1] * lk[1])) + lam_init


def diff_attend(q, k, v, lam):
    B, Tq = q.shape[:2]
    blk = min(Q_BLOCK, Tq)
    nb = Tq // blk
    qb = jnp.swapaxes(q.reshape(B, nb, blk, DIFF_HEADS, 2, DIFF_DH), 0, 1)

    def one(qi):
        s = jnp.einsum('bqhcd,bkhcd->bhcqk', qi, k).astype(jnp.float32) * (DIFF_DH ** -0.5)
        p = jax.nn.softmax(s, axis=-1)
        w = p[:, :, 0] - lam * p[:, :, 1]
        return jnp.einsum('bhqk,bkhv->bqhv', w.astype(v.dtype), v)

    o = lax.map(one, qb)
    return jnp.swapaxes(o, 0, 1).reshape(B, Tq, DIFF_HEADS, DIFF_DV)


def diff_out(o, g_head, w_out, lam_init):
    B, T = o.shape[:2]
    o = rmsnorm(o, g_head) * (1.0 - lam_init)
    return o.reshape(B, T, DIFF_HEADS * DIFF_DV) @ w_out


def swiglu(h, wg, wu, wd):
    return (jax.nn.silu(h @ wg) * (h @ wu)) @ wd


def setup_inputs(seed: int = 0) -> dict:
    key = jax.random.key(seed)
    ks = iter(jax.random.split(key, 40))
    f32 = jnp.float32

    def nrm(shape, scale):
        return jax.random.normal(next(ks), shape, f32) * scale

    D = D_MODEL
    NG, ND = N_GLA_LAYERS, N_DIFF_LAYERS
    return {
        'x_prompt': nrm((BATCH, SEQ, D), 1.0),
        'x_sample': nrm((DEC_BATCH, DEC_SEQ, D), 1.0),
        'state_gla_fwd': nrm((DEC_BATCH, NG, GLA_HEADS, GLA_DK, GLA_DV), 1.0),
        'state_gla_bwd': nrm((DEC_BATCH, NG, GLA_HEADS, GLA_DK, GLA_DV), 1.0),
        'cache_diff_k': nrm((DEC_BATCH, ND, PAST_LEN, DIFF_HEADS, 2, DIFF_DH), 1.0),
        'cache_diff_v': nrm((DEC_BATCH, ND, PAST_LEN, DIFF_HEADS, DIFF_DV), 1.0),
        'c': nrm((DEC_BATCH, D), 1.0),
        'c_ctx': nrm((D,), 1.0),
        'w_ada': nrm((DEPTH, D, 6 * D), 0.5 * D ** -0.5),
        'b_ada': nrm((DEPTH, 6 * D), 0.01),
        'g_mix_norm': 1.0 + nrm((DEPTH, D), 0.02),
        'g_ffn_norm': 1.0 + nrm((DEPTH, D), 0.02),
        'w_gla_in': nrm((NG, D, 2 * GLA_HK + 2 * GLA_HV), D ** -0.5),
        'w_gla_g1': nrm((NG, 2, D, GLA_RANK), D ** -0.5),
        'w_gla_g2': nrm((NG, 2, GLA_RANK, GLA_HK), GLA_RANK ** -0.5),
        'b_gla_g': nrm((NG, 2, GLA_HK), 0.1),
        'g_gla_head': 1.0 + nrm((NG, GLA_DV), 0.02),
        'w_gla_out': nrm((NG, GLA_HV, D), GLA_HV ** -0.5),
        'w_diff_qkv': nrm((ND, D, 3 * D), D ** -0.5),
        'lam_q': nrm((ND, 2, DIFF_DH), 0.1),
        'lam_k': nrm((ND, 2, DIFF_DH), 0.1),
        'g_diff_head': 1.0 + nrm((ND, DIFF_DV), 0.02),
        'w_diff_out': nrm((ND, DIFF_HEADS * DIFF_DV, D), (DIFF_HEADS * DIFF_DV) ** -0.5),
        'w_ffn_gate': nrm((DEPTH, D, D_FF), D ** -0.5),
        'w_ffn_up': nrm((DEPTH, D, D_FF), D ** -0.5),
        'w_ffn_down': nrm((DEPTH, D_FF, D), D_FF ** -0.5),
        'g_final': 1.0 + nrm((D,), 0.02),
    }


def reference(x_prompt, x_sample, state_gla_fwd, state_gla_bwd, cache_diff_k, cache_diff_v, c, c_ctx,
              w_ada, b_ada, g_mix_norm, g_ffn_norm, w_gla_in, w_gla_g1, w_gla_g2, b_gla_g, g_gla_head,
              w_gla_out, w_diff_qkv, lam_q, lam_k, g_diff_head, w_diff_out, w_ffn_gate, w_ffn_up,
              w_ffn_down, g_final):
    xp, xs = x_prompt, x_sample
    bp = xp.shape[0]
    cos, sin = axial_rope(xs.shape[1], DIFF_DH, xs.dtype)
    new_f, new_b, new_k, new_v = [], [], [], []
    for i in range(DEPTH):
        mp = adaln(c_ctx, w_ada[i], b_ada[i])
        ms = adaln(c, w_ada[i], b_ada[i])
        hp = rmsnorm(xp, g_mix_norm[i]) * (1 + mp[1]) + mp[0]
        hs = rmsnorm(xs, g_mix_norm[i]) * (1 + ms[1]) + ms[0]
        j = i // N_MIXERS
        if i % N_MIXERS == 0:
            gp = (w_gla_in[j], w_gla_g1[j], w_gla_g2[j], b_gla_g[j], g_gla_head[j], w_gla_out[j])
            zero = jnp.zeros((bp, GLA_HEADS, GLA_DK, GLA_DV), jnp.float32)
            op, s_f, s_b = gla_mixer(hp, zero, zero, *gp)
            os_, _, _ = gla_mixer(hs, state_gla_fwd[:, j], state_gla_bwd[:, j], *gp)
            new_f.append(s_f.astype(xp.dtype))
            new_b.append(s_b.astype(xp.dtype))
        else:
            lam_init = 0.8 - 0.6 * math.exp(-0.3 * i)
            lam = diff_lambda(lam_q[j], lam_k[j], lam_init)
            qp, kp, vp = diff_qkv(hp, w_diff_qkv[j])
            qs, ks, vs = diff_qkv(hs, w_diff_qkv[j])
            qs = apply_rope(qs, cos, sin)
            ks = apply_rope(ks, cos, sin)
            op = diff_out(diff_attend(qp, kp, vp, lam), g_diff_head[j], w_diff_out[j], lam_init)
            k_all = jnp.concatenate([cache_diff_k[:, j].astype(ks.dtype), ks], axis=1)
            v_all = jnp.concatenate([cache_diff_v[:, j].astype(vs.dtype), vs], axis=1)
            os_ = diff_out(diff_attend(qs, k_all, v_all, lam), g_diff_head[j], w_diff_out[j], lam_init)
            new_k.append(kp)
            new_v.append(vp)
        xp = xp + mp[2] * op
        xs = xs + ms[2] * os_
        hp = rmsnorm(xp, g_ffn_norm[i]) * (1 + mp[4]) + mp[3]
        hs = rmsnorm(xs, g_ffn_norm[i]) * (1 + ms[4]) + ms[3]
        xp = xp + mp[5] * swiglu(hp, w_ffn_gate[i], w_ffn_up[i], w_ffn_down[i])
        xs = xs + ms[5] * swiglu(hs, w_ffn_gate[i], w_ffn_up[i], w_ffn_down[i])
    y_prompt = rmsnorm(xp, g_final)
    y_sample = rmsnorm(xs, g_final)
    new_state_gla_fwd = jnp.stack(new_f, axis=1)
    new_state_gla_bwd = jnp.stack(new_b, axis=1)
    new_cache_diff_k = jnp.stack(new_k, axis=1)
    new_cache_diff_v = jnp.stack(new_v, axis=1)
    return (y_prompt, y_sample, new_state_gla_fwd, new_state_gla_bwd, new_cache_diff_k, new_cache_diff_v)
```

```python
import functools
import math

import jax
import jax.numpy as jnp
from jax import lax
from jax.experimental import pallas as pl
from jax.experimental.pallas import tpu as pltpu

F32 = jnp.float32
BF16 = jnp.bfloat16

D_MODEL = 1024
DEPTH = 2
GRID_W = 64
GLA_HEADS = 4
GLA_DK = 128
GLA_DV = 256
GLA_HK = GLA_HEADS * GLA_DK
GLA_HV = GLA_HEADS * GLA_DV
GLA_RANK = 16
GLA_TAU = 16.0
GLA_CHUNK = 64
DIFF_HEADS = 8
DIFF_DH = 64
DIFF_DV = 128
ROPE_THETA = 10000.0
D_FF = 2816
EPS = 1e-6
N_MOD = 6
MOD_ROWS = 16
GATE_PAD = 128

VMEM_LIMIT_BYTES = 56 * 1024 * 1024
ROW_TILE = 256
Q_TILE = 256


def _params(*sem):
    return pltpu.CompilerParams(dimension_semantics=sem, vmem_limit_bytes=VMEM_LIMIT_BYTES)


def _resident(shape):
    nd = len(shape)
    return pl.BlockSpec(shape, lambda *_: (0,) * nd, pipeline_mode=pl.Buffered(1))


def _sigmoid(x):
    return 1.0 / (1.0 + jnp.exp(-x))


def _silu(x):
    return x * _sigmoid(x)


def _rms(x, g):
    return x * lax.rsqrt(jnp.mean(x * x, axis=-1, keepdims=True) + EPS) * g


def _norm_mod(x, g, shift, scale):
    return _rms(x, g) * (1.0 + scale) + shift


def _dot(a, b):
    return jnp.dot(a, b, preferred_element_type=F32)


def _dot_nt(a, b):
    return lax.dot_general(a, b, (((1,), (1,)), ((), ())), preferred_element_type=F32)


def _dot_tn(a, b):
    return lax.dot_general(a, b, (((0,), (0,)), ((), ())), preferred_element_type=F32)


def _ada_kernel(c_ref, w_ref, b_ref, o_ref):
    a = _silu(c_ref[...]).astype(BF16)
    o_ref[...] = _dot(a, w_ref[...].astype(BF16)) + b_ref[...]


def _ada(cond, w_ada, b_ada):
    tn = 1536
    n = N_MOD * D_MODEL
    return pl.pallas_call(
        _ada_kernel,
        grid=(DEPTH, n // tn),
        in_specs=[
            pl.BlockSpec((MOD_ROWS, D_MODEL), lambda l, j: (0, 0)),
            pl.BlockSpec((None, D_MODEL, tn), lambda l, j: (l, 0, j)),
            pl.BlockSpec((None, 1, tn), lambda l, j: (l, 0, j)),
        ],
        out_specs=pl.BlockSpec((None, MOD_ROWS, tn), lambda l, j: (l, 0, j)),
        out_shape=jax.ShapeDtypeStruct((DEPTH, MOD_ROWS, n), F32),
        compiler_params=_params("arbitrary", "arbitrary"),
        name="ada",
    )(cond, w_ada, b_ada.reshape(DEPTH, 1, n))


def _mod_spec(layer, which, rows_per_batch):
    if rows_per_batch is None:
        return pl.BlockSpec((None, None, None, 1, D_MODEL), lambda i: (layer, 0, which, 0, 0))
    tiles = rows_per_batch // ROW_TILE
    return pl.BlockSpec((None, None, None, 1, D_MODEL), lambda i: (layer, 1 + i // tiles, which, 0, 0))


def _row_spec(width):
    return pl.BlockSpec((ROW_TILE, width), lambda i: (i, 0))


def _log_sigmoid(x):
    return jnp.minimum(x, 0.0) - jnp.log(1.0 + jnp.exp(-jnp.abs(x)))


def _gla_in_kernel(x_ref, g_ref, sh_ref, sc_ref, w_ref, w2_ref, bg_ref, q_ref, k_ref, v_ref, r_ref, la_ref):
    h = _norm_mod(x_ref[...], g_ref[...], sh_ref[...], sc_ref[...]).astype(BF16)
    z = _dot(h, w_ref[...])
    q_ref[...] = z[:, :GLA_HK]
    k_ref[...] = z[:, GLA_HK:2 * GLA_HK]
    v_ref[...] = z[:, 2 * GLA_HK:2 * GLA_HK + GLA_HV].astype(BF16)
    r_ref[...] = z[:, 2 * GLA_HK + GLA_HV:2 * GLA_HK + 2 * GLA_HV]
    low_rank = z[:, 2 * GLA_HK + 2 * GLA_HV:].astype(BF16)
    pre = _dot(low_rank, w2_ref[...]) + bg_ref[...]
    la_ref[...] = _log_sigmoid(pre) * (1.0 / GLA_TAU)


def _gla_in(x, mod, layer, rows_per_batch, g, w_all, w2, bg):
    rows = x.shape[0]
    n_all = w_all.shape[1]
    return pl.pallas_call(
        _gla_in_kernel,
        grid=(rows // ROW_TILE,),
        in_specs=[
            _row_spec(D_MODEL),
            _resident((1, D_MODEL)),
            _mod_spec(layer, 0, rows_per_batch),
            _mod_spec(layer, 1, rows_per_batch),
            _resident((D_MODEL, n_all)),
            _resident((GATE_PAD, 2 * GLA_HK)),
            _resident((1, 2 * GLA_HK)),
        ],
        out_specs=[_row_spec(GLA_HK), _row_spec(GLA_HK), _row_spec(GLA_HV), _row_spec(GLA_HV), _row_spec(2 * GLA_HK)],
        out_shape=[
            jax.ShapeDtypeStruct((rows, GLA_HK), F32),
            jax.ShapeDtypeStruct((rows, GLA_HK), F32),
            jax.ShapeDtypeStruct((rows, GLA_HV), BF16),
            jax.ShapeDtypeStruct((rows, GLA_HV), F32),
            jax.ShapeDtypeStruct((rows, 2 * GLA_HK), F32),
        ],
        compiler_params=_params("arbitrary"),
        name="gla_in",
    )(x, g, mod, mod, w_all, w2, bg)


def _cumsum_chunk(tri, x):
    hi = x.astype(BF16)
    rem = x - hi.astype(F32)
    mid = rem.astype(BF16)
    lo = (rem - mid.astype(F32)).astype(BF16)
    out = _dot(tri, jnp.concatenate([hi, mid, lo], axis=1))
    return out[:, :GLA_DK] + out[:, GLA_DK:2 * GLA_DK] + out[:, 2 * GLA_DK:]


def _gla_scan_kernel(*refs, n_chunks, has_s0):
    C, DK, DV = GLA_CHUNK, GLA_DK, GLA_DV
    if has_s0:
        (q_ref, k_ref, v_ref, laf_ref, lab_ref, r_ref, gh_ref, s0f_ref, s0b_ref,
         y_ref, qd_ref, kd_ref, ds_ref, dec_ref, sf_ref, sb_ref) = refs
    else:
        (q_ref, k_ref, v_ref, laf_ref, lab_ref, r_ref, gh_ref,
         y_ref, sf_ref, sb_ref, qd_ref, kd_ref, ds_ref, dec_ref) = refs

    row = lax.broadcasted_iota(jnp.int32, (C, C), 0)
    col = lax.broadcasted_iota(jnp.int32, (C, C), 1)
    lower = row >= col
    upper = row <= col
    tri_lo = jnp.where(lower, 1.0, 0.0).astype(BF16)
    tri_up = jnp.where(upper, 1.0, 0.0).astype(BF16)

    def chunk(n):
        return pl.ds(pl.multiple_of(n * C, C), C)

    def pass1(n, carry):
        sl = chunk(n)
        bf = _cumsum_chunk(tri_lo, laf_ref[sl, :])
        bb = _cumsum_chunk(tri_up, lab_ref[sl, :])
        tot_f = bf[C - 1:C, :]
        tot_b = bb[0:1, :]
        q = q_ref[sl, :] * (DK ** -0.5)
        k = k_ref[sl, :]
        qd_ref[sl, :] = jnp.concatenate([q * jnp.exp(bf), q * jnp.exp(bb)], axis=1).astype(BF16)
        kd_ref[sl, :] = jnp.concatenate([k * jnp.exp(-bf), k * jnp.exp(-bb)], axis=1).astype(BF16)
        k_end = jnp.concatenate([k * jnp.exp(tot_f - bf), k * jnp.exp(tot_b - bb)], axis=1).astype(BF16)
        ds_ref[n] = _dot_tn(k_end, v_ref[sl, :])
        dec = jnp.exp(jnp.concatenate([tot_f, tot_b], axis=0))
        dec_f = jnp.broadcast_to(dec[0:1, :], (DK, DK)).T
        dec_b = jnp.broadcast_to(dec[1:2, :], (DK, DK)).T
        dec_ref[n] = jnp.concatenate([dec_f, dec_b], axis=0)
        return carry

    lax.fori_loop(0, n_chunks, pass1, 0)

    if has_s0:
        sf_ref[...] = s0f_ref[...]
        sb_ref[...] = s0b_ref[...]
    else:
        sf_ref[...] = jnp.zeros((DK, DV), F32)
        sb_ref[...] = jnp.zeros((DK, DV), F32)

    def pass2(i, carry):
        j = n_chunks - 1 - i
        s_f = sf_ref[...]
        inc_f = ds_ref[i, :DK, :]
        ds_ref[i, :DK, :] = s_f
        d_f = dec_ref[i, :DK, :]
        sf_ref[...] = jnp.concatenate([d_f, d_f], axis=1) * s_f + inc_f
        s_b = sb_ref[...]
        inc_b = ds_ref[j, DK:, :]
        ds_ref[j, DK:, :] = s_b
        d_b = dec_ref[j, DK:, :]
        sb_ref[...] = jnp.concatenate([d_b, d_b], axis=1) * s_b + inc_b
        return carry

    lax.fori_loop(0, n_chunks, pass2, 0)

    gh = gh_ref[...]

    def pass3(n, carry):
        sl = chunk(n)
        qd = qd_ref[sl, :]
        kd = kd_ref[sl, :]
        a_f = _dot_nt(qd[:, :DK], kd[:, :DK])
        a_b = _dot_nt(qd[:, DK:], kd[:, DK:])
        a = jnp.where(lower, a_f, 0.0) + jnp.where(upper, a_b, 0.0)
        o = _dot(a.astype(BF16), v_ref[sl, :]) + _dot(qd, ds_ref[n].astype(BF16))
        y_ref[sl, :] = (_rms(o, gh) * _silu(r_ref[sl, :])).astype(BF16)
        return carry

    lax.fori_loop(0, n_chunks, pass3, 0)


def _gla_scan(q, k, v, la, r, g_head, batch, seq, s0f=None, s0b=None):
    n_chunks = seq // GLA_CHUNK
    has_s0 = s0f is not None
    rows = batch * seq
    tok = lambda w, off: pl.BlockSpec((seq, w), lambda b, h: (b, h + off))
    state = pl.BlockSpec((None, None, GLA_DK, GLA_DV), lambda b, h: (b, h, 0, 0))
    in_specs = [tok(GLA_DK, 0), tok(GLA_DK, 0), tok(GLA_DV, 0), tok(GLA_DK, 0), tok(GLA_DK, GLA_HEADS),
                tok(GLA_DV, 0), pl.BlockSpec((1, GLA_DV), lambda b, h: (0, 0))]
    args = [q, k, v, la, la, r, g_head]
    y_shape = jax.ShapeDtypeStruct((rows, GLA_HV), BF16)
    st_shape = jax.ShapeDtypeStruct((batch, GLA_HEADS, GLA_DK, GLA_DV), F32)
    scratch = [
        pltpu.VMEM((seq, 2 * GLA_DK), BF16),
        pltpu.VMEM((seq, 2 * GLA_DK), BF16),
        pltpu.VMEM((n_chunks, 2 * GLA_DK, GLA_DV), F32),
        pltpu.VMEM((n_chunks, 2 * GLA_DK, GLA_DK), F32),
    ]
    if has_s0:
        in_specs += [state, state]
        args += [s0f, s0b]
        out_specs = tok(GLA_DV, 0)
        out_shape = y_shape
        scratch += [pltpu.VMEM((GLA_DK, GLA_DV), F32), pltpu.VMEM((GLA_DK, GLA_DV), F32)]
    else:
        out_specs = [tok(GLA_DV, 0), state, state]
        out_shape = [y_shape, st_shape, st_shape]
    return pl.pallas_call(
        functools.partial(_gla_scan_kernel, n_chunks=n_chunks, has_s0=has_s0),
        grid=(batch, GLA_HEADS),
        in_specs=in_specs,
        out_specs=out_specs,
        out_shape=out_shape,
        scratch_shapes=scratch,
        compiler_params=_params("arbitrary", "arbitrary"),
        name="gla_scan",
    )(*args)


def _out_ffn_kernel(*refs, final_norm):
    if final_norm:
        (x_ref, y_ref, wo_ref, g1_ref, gn_ref, sh_ref, sc_ref, g2_ref, wg_ref, wu_ref, wd_ref, gf_ref, o_ref) = refs
    else:
        (x_ref, y_ref, wo_ref, g1_ref, gn_ref, sh_ref, sc_ref, g2_ref, wg_ref, wu_ref, wd_ref, o_ref) = refs
    x = x_ref[...] + g1_ref[...] * _dot(y_ref[...], wo_ref[...])
    h = _norm_mod(x, gn_ref[...], sh_ref[...], sc_ref[...]).astype(BF16)
    act = (_silu(_dot(h, wg_ref[...])) * _dot(h, wu_ref[...])).astype(BF16)
    x = x + g2_ref[...] * _dot(act, wd_ref[...])
    if final_norm:
        x = _rms(x, gf_ref[...])
    o_ref[...] = x


def _out_ffn(x, y, mod, layer, rows_per_batch, w_out, g_ffn, wg, wu, wd, g_final=None):
    rows = x.shape[0]
    final_norm = g_final is not None
    in_specs = [
        _row_spec(D_MODEL),
        _row_spec(D_MODEL),
        _resident((D_MODEL, D_MODEL)),
        _mod_spec(layer, 2, rows_per_batch),
        _resident((1, D_MODEL)),
        _mod_spec(layer, 3, rows_per_batch),
        _mod_spec(layer, 4, rows_per_batch),
        _mod_spec(layer, 5, rows_per_batch),
        _resident((D_MODEL, D_FF)),
        _resident((D_MODEL, D_FF)),
        _resident((D_FF, D_MODEL)),
    ]
    args = [x, y, w_out, mod, g_ffn, mod, mod, mod, wg, wu, wd]
    if final_norm:
        in_specs.append(_resident((1, D_MODEL)))
        args.append(g_final)
    return pl.pallas_call(
        functools.partial(_out_ffn_kernel, final_norm=final_norm),
        grid=(rows // ROW_TILE,),
        in_specs=in_specs,
        out_specs=_row_spec(D_MODEL),
        out_shape=jax.ShapeDtypeStruct((rows, D_MODEL), F32),
        compiler_params=_params("arbitrary"),
        name="out_ffn",
    )(*args)


def _rope_tables(n_tokens):
    rows = n_tokens // GRID_W
    r, col = jnp.meshgrid(jnp.arange(rows), jnp.arange(GRID_W), indexing="ij")
    r = r.reshape(-1).astype(F32)
    col = col.reshape(-1).astype(F32)
    n_freq = DIFF_DH // 4
    inv = ROPE_THETA ** (-jnp.arange(n_freq, dtype=F32) / n_freq)
    ang = jnp.concatenate([r[:, None] * inv, col[:, None] * inv], axis=-1)
    cos, sin = jnp.cos(ang), jnp.sin(ang)
    reps = D_MODEL // DIFF_DH
    return jnp.tile(jnp.concatenate([cos, cos], axis=-1), (1, reps)), jnp.tile(jnp.concatenate([-sin, sin], axis=-1), (1, reps))


def _rope(x, cos, sin_signed):
    half = DIFF_DH // 2
    width = x.shape[1]
    lane = lax.broadcasted_iota(jnp.int32, x.shape, 1)
    from_above = pltpu.roll(x, width - half, axis=1)
    from_below = pltpu.roll(x, half, axis=1)
    swapped = jnp.where(lane % DIFF_DH < half, from_above, from_below)
    return x * cos + swapped * sin_signed


def _qkv_kernel(*refs, rope):
    if rope:
        (x_ref, g_ref, sh_ref, sc_ref, w_ref, cos_ref, sin_ref, q_ref, k_ref, v_ref) = refs
    else:
        (x_ref, g_ref, sh_ref, sc_ref, w_ref, q_ref, k_ref, v_ref, kf_ref, vf_ref) = refs
    h = _norm_mod(x_ref[...], g_ref[...], sh_ref[...], sc_ref[...]).astype(BF16)
    z = _dot(h, w_ref[...])
    q = z[:, :D_MODEL]
    k = z[:, D_MODEL:2 * D_MODEL]
    v = z[:, 2 * D_MODEL:]
    if rope:
        q = _rope(q, cos_ref[...], sin_ref[...])
        k = _rope(k, cos_ref[...], sin_ref[...])
    else:
        kf_ref[...] = k
        vf_ref[...] = v
    q_ref[...] = (q * (DIFF_DH ** -0.5)).astype(BF16)
    k_ref[...] = k.astype(BF16)
    v_ref[...] = v.astype(BF16)


def _qkv(x, mod, layer, rows_per_batch, g, w, tables=None):
    rows = x.shape[0]
    rope = tables is not None
    in_specs = [_row_spec(D_MODEL), _resident((1, D_MODEL)), _mod_spec(layer, 0, rows_per_batch),
                _mod_spec(layer, 1, rows_per_batch), _resident((D_MODEL, 3 * D_MODEL))]
    args = [x, g, mod, mod, w]
    out_specs = [_row_spec(D_MODEL)] * 3
    out_shape = [jax.ShapeDtypeStruct((rows, D_MODEL), BF16)] * 3
    if rope:
        tiles = rows_per_batch // ROW_TILE
        tab = pl.BlockSpec((ROW_TILE, D_MODEL), lambda i: (i % tiles, 0))
        in_specs += [tab, tab]
        args += list(tables)
    else:
        out_specs += [_row_spec(D_MODEL)] * 2
        out_shape += [jax.ShapeDtypeStruct((rows, D_MODEL), F32)] * 2
    return pl.pallas_call(
        functools.partial(_qkv_kernel, rope=rope),
        grid=(rows // ROW_TILE,),
        in_specs=in_specs,
        out_specs=out_specs,
        out_shape=out_shape,
        compiler_params=_params("arbitrary"),
        name="diff_qkv",
    )(*args)


def _attn_kernel(*refs, has_cache, lam_init, n_q_tiles, q_tile):
    if has_cache:
        (lq_ref, lk_ref, q_ref, k_ref, v_ref, gh_ref, kc_ref, vc_ref, y_ref) = refs
    else:
        (lq_ref, lk_ref, q_ref, k_ref, v_ref, gh_ref, y_ref) = refs
    e = jnp.exp(jnp.sum(lq_ref[...] * lk_ref[...], axis=1, keepdims=True))
    lam = e[0:1, :] - e[1:2, :] + lam_init
    first_map = lax.broadcasted_iota(jnp.int32, (q_tile, 2 * DIFF_DH), 1) < DIFF_DH
    gh = gh_ref[...] * (1.0 - lam_init)
    keys = [k_ref[...]]
    vals = [v_ref[...]]
    if has_cache:
        keys.insert(0, kc_ref[...].astype(BF16))
        vals.insert(0, vc_ref[...].astype(BF16))

    def softmax_parts(q_map):
        s = [_dot_nt(q_map, kk) for kk in keys]
        m = functools.reduce(jnp.maximum, [jnp.max(x, axis=-1, keepdims=True) for x in s])
        p = [jnp.exp(x - m) for x in s]
        denom = functools.reduce(jnp.add, [jnp.sum(x, axis=-1, keepdims=True) for x in p])
        return p, denom

    def body(t, carry):
        sl = pl.ds(pl.multiple_of(t * q_tile, q_tile), q_tile)
        q = q_ref[sl, :]
        zero = jnp.zeros_like(q)
        p1, d1 = softmax_parts(jnp.where(first_map, q, zero))
        p2, d2 = softmax_parts(jnp.where(first_map, zero, q))
        c1 = 1.0 / d1
        c2 = lam / d2
        o = functools.reduce(jnp.add, [_dot((a * c1 - b * c2).astype(BF16), vv) for a, b, vv in zip(p1, p2, vals)])
        y_ref[sl, :] = _rms(o, gh).astype(BF16)
        return carry

    lax.fori_loop(0, n_q_tiles, body, 0)


def _attn(q, k, v, lam_q, lam_k, g_head, lam_init, batch, seq, cache_k=None, cache_v=None):
    has_cache = cache_k is not None
    tok = pl.BlockSpec((seq, 2 * DIFF_DH), lambda b, h: (b, h))
    small = lambda shape: pl.BlockSpec(shape, lambda b, h: (0, 0))
    in_specs = [small((2, DIFF_DH)), small((2, DIFF_DH)), tok, tok, tok, small((1, DIFF_DV))]
    args = [lam_q, lam_k, q, k, v, g_head]
    if has_cache:
        past = cache_k.shape[1]
        cache = pl.BlockSpec((None, past, 2 * DIFF_DH), lambda b, h: (b, 0, h))
        in_specs += [cache, cache]
        args += [cache_k, cache_v]
    q_tile = min(Q_TILE, seq)
    return pl.pallas_call(
        functools.partial(_attn_kernel, has_cache=has_cache, lam_init=lam_init, n_q_tiles=seq // q_tile, q_tile=q_tile),
        grid=(batch, DIFF_HEADS),
        in_specs=in_specs,
        out_specs=tok,
        out_shape=jax.ShapeDtypeStruct((batch * seq, D_MODEL), BF16),
        compiler_params=_params("arbitrary", "arbitrary"),
        name="diff_attn",
    )(*args)


def kernel(x_prompt, x_sample, state_gla_fwd, state_gla_bwd, cache_diff_k, cache_diff_v, c, c_ctx, w_ada, b_ada, g_mix_norm, g_ffn_norm, w_gla_in, w_gla_g1, w_gla_g2, b_gla_g, g_gla_head, w_gla_out, w_diff_qkv, lam_q, lam_k, g_diff_head, w_diff_out, w_ffn_gate, w_ffn_up, w_ffn_down, g_final):
    bp, tp, _ = x_prompt.shape
    bs, ts, _ = x_sample.shape
    xp = x_prompt.reshape(bp * tp, D_MODEL)
    xs = x_sample.reshape(bs * ts, D_MODEL)

    cond = jnp.concatenate([c_ctx[None, :], c, jnp.zeros((MOD_ROWS - 1 - bs, D_MODEL), F32)], axis=0)
    mod = _ada(cond, w_ada, b_ada).reshape(DEPTH, MOD_ROWS, N_MOD, 1, D_MODEL)

    j = 0
    gate_cols = jnp.concatenate([w_gla_g1[j, 0], w_gla_g1[j, 1],
                                 jnp.zeros((D_MODEL, GATE_PAD - 2 * GLA_RANK), F32)], axis=1)
    w_all = jnp.concatenate([w_gla_in[j], gate_cols], axis=1).astype(BF16)
    w2 = jnp.zeros((GATE_PAD, 2 * GLA_HK), F32)
    w2 = w2.at[:GLA_RANK, :GLA_HK].set(w_gla_g2[j, 0]).at[GLA_RANK:2 * GLA_RANK, GLA_HK:].set(w_gla_g2[j, 1]).astype(BF16)
    bg = b_gla_g[j].reshape(1, 2 * GLA_HK)
    g_mix = g_mix_norm[0].reshape(1, D_MODEL)
    g_head = g_gla_head[j].reshape(1, GLA_DV)
    ffn = lambda i: (g_ffn_norm[i].reshape(1, D_MODEL), w_ffn_gate[i].astype(BF16), w_ffn_up[i].astype(BF16),
                     w_ffn_down[i].astype(BF16))
    ffn0 = ffn(0)
    w_out0 = w_gla_out[j].astype(BF16)

    qp, kp, vp, rp, lap = _gla_in(xp, mod, 0, None, g_mix, w_all, w2, bg)
    qs, ks, vs, rs, las = _gla_in(xs, mod, 0, ts, g_mix, w_all, w2, bg)
    yp, new_f, new_b = _gla_scan(qp, kp, vp, lap, rp, g_head, bp, tp)
    ys = _gla_scan(qs, ks, vs, las, rs, g_head, bs, ts, state_gla_fwd[:, j], state_gla_bwd[:, j])
    xp = _out_ffn(xp, yp, mod, 0, None, w_out0, *ffn0)
    xs = _out_ffn(xs, ys, mod, 0, ts, w_out0, *ffn0)

    lam_init = 0.8 - 0.6 * math.exp(-0.3 * 1)
    g_mix = g_mix_norm[1].reshape(1, D_MODEL)
    w_qkv = w_diff_qkv[j].astype(BF16)
    g_head = g_diff_head[j].reshape(1, DIFF_DV)
    ffn1 = ffn(1)
    w_out1 = w_diff_out[j].astype(BF16)

    qp, kp, vp, kp32, vp32 = _qkv(xp, mod, 1, None, g_mix, w_qkv)
    qs, ks, vs = _qkv(xs, mod, 1, ts, g_mix, w_qkv, _rope_tables(ts))
    yp = _attn(qp, kp, vp, lam_q[j], lam_k[j], g_head, lam_init, bp, tp)
    past = cache_diff_k.shape[2]
    ys = _attn(qs, ks, vs, lam_q[j], lam_k[j], g_head, lam_init, bs, ts,
               cache_diff_k[:, j].reshape(bs, past, D_MODEL), cache_diff_v[:, j].reshape(bs, past, D_MODEL))
    g_fin = g_final.reshape(1, D_MODEL)
    yp_out = _out_ffn(xp, yp, mod, 1, None, w_out1, *ffn1, g_final=g_fin)
    ys_out = _out_ffn(xs, ys, mod, 1, ts, w_out1, *ffn1, g_final=g_fin)

    return (yp_out.reshape(bp, tp, D_MODEL),
            ys_out.reshape(bs, ts, D_MODEL),
            new_f[:, None],
            new_b[:, None],
            kp32.reshape(bp, 1, tp, DIFF_HEADS, 2, DIFF_DH),
            vp32.reshape(bp, 1, tp, DIFF_HEADS, DIFF_DV))
```

```python
import functools
import math

import jax
import jax.numpy as jnp
from jax import lax
from jax.experimental import pallas as pl
from jax.experimental.pallas import tpu as pltpu

F32 = jnp.float32
BF16 = jnp.bfloat16

D_MODEL = 1024
DEPTH = 2
GRID_W = 64
GLA_HEADS = 4
GLA_DK = 128
GLA_DV = 256
GLA_HK = GLA_HEADS * GLA_DK
GLA_HV = GLA_HEADS * GLA_DV
GLA_RANK = 16
GLA_TAU = 16.0
GLA_CHUNK = 64
GLA_GROUP = 256
DIFF_HEADS = 8
DIFF_DH = 64
DIFF_DV = 128
ROPE_THETA = 10000.0
D_FF = 2816
EPS = 1e-6
LOG2E = math.log2(math.e)
N_MOD = 6
MOD_ROWS = 16
GATE_PAD = 128

VMEM_LIMIT_BYTES = 56 * 1024 * 1024
ROW_TILE = 256
Q_TILE = 256


def _params(*sem):
    return pltpu.CompilerParams(dimension_semantics=sem, vmem_limit_bytes=VMEM_LIMIT_BYTES)


def _resident(shape):
    nd = len(shape)
    return pl.BlockSpec(shape, lambda *_: (0,) * nd, pipeline_mode=pl.Buffered(1))


def _sigmoid(x):
    return 1.0 / (1.0 + jnp.exp(-x))


def _silu(x):
    return x * _sigmoid(x)


def _rms(x, g):
    return x * lax.rsqrt(jnp.mean(x * x, axis=-1, keepdims=True) + EPS) * g


def _norm_mod(x, g, shift, scale):
    return _rms(x, g) * (1.0 + scale) + shift


def _dot(a, b):
    return jnp.dot(a, b, preferred_element_type=F32)


def _dot_nt(a, b):
    return lax.dot_general(a, b, (((1,), (1,)), ((), ())), preferred_element_type=F32)


def _dot_tn(a, b):
    return lax.dot_general(a, b, (((0,), (0,)), ((), ())), preferred_element_type=F32)


def _ada_kernel(c_ref, w_ref, b_ref, o_ref):
    a = _silu(c_ref[...]).astype(BF16)
    o_ref[...] = _dot(a, w_ref[...].astype(BF16)) + b_ref[...]


def _ada(cond, w_ada, b_ada):
    tn = 1536
    n = N_MOD * D_MODEL
    return pl.pallas_call(
        _ada_kernel,
        grid=(DEPTH, n // tn),
        in_specs=[
            pl.BlockSpec((MOD_ROWS, D_MODEL), lambda l, j: (0, 0)),
            pl.BlockSpec((None, D_MODEL, tn), lambda l, j: (l, 0, j)),
            pl.BlockSpec((None, 1, tn), lambda l, j: (l, 0, j)),
        ],
        out_specs=pl.BlockSpec((None, MOD_ROWS, tn), lambda l, j: (l, 0, j)),
        out_shape=jax.ShapeDtypeStruct((DEPTH, MOD_ROWS, n), F32),
        compiler_params=_params("arbitrary", "arbitrary"),
        name="ada",
    )(cond, w_ada, b_ada.reshape(DEPTH, 1, n))


def _mod_spec(layer, which, rows_per_batch):
    if rows_per_batch is None:
        return pl.BlockSpec((None, None, None, 1, D_MODEL), lambda i: (layer, 0, which, 0, 0))
    tiles = rows_per_batch // ROW_TILE
    return pl.BlockSpec((None, None, None, 1, D_MODEL), lambda i: (layer, 1 + i // tiles, which, 0, 0))


def _row_spec(width):
    return pl.BlockSpec((ROW_TILE, width), lambda i: (i, 0))


def _log_sigmoid(x):
    return jnp.minimum(x, 0.0) - jnp.log(1.0 + jnp.exp(-jnp.abs(x)))


def _gla_in_kernel(x_ref, g_ref, sh_ref, sc_ref, w_ref, w2_ref, bg_ref, q_ref, k_ref, v_ref, r_ref, la_ref):
    h = _norm_mod(x_ref[...], g_ref[...], sh_ref[...], sc_ref[...]).astype(BF16)
    z = _dot(h, w_ref[...])
    q_ref[...] = z[:, :GLA_HK]
    k_ref[...] = z[:, GLA_HK:2 * GLA_HK]
    v_ref[...] = z[:, 2 * GLA_HK:2 * GLA_HK + GLA_HV].astype(BF16)
    r_ref[...] = z[:, 2 * GLA_HK + GLA_HV:2 * GLA_HK + 2 * GLA_HV]
    low_rank = z[:, 2 * GLA_HK + 2 * GLA_HV:].astype(BF16)
    pre = _dot(low_rank, w2_ref[...]) + bg_ref[...]
    la_ref[...] = _log_sigmoid(pre) * (1.0 / GLA_TAU)


def _gla_in(x, mod, layer, rows_per_batch, g, w_all, w2, bg):
    rows = x.shape[0]
    n_all = w_all.shape[1]
    return pl.pallas_call(
        _gla_in_kernel,
        grid=(rows // ROW_TILE,),
        in_specs=[
            _row_spec(D_MODEL),
            _resident((1, D_MODEL)),
            _mod_spec(layer, 0, rows_per_batch),
            _mod_spec(layer, 1, rows_per_batch),
            _resident((D_MODEL, n_all)),
            _resident((GATE_PAD, 2 * GLA_HK)),
            _resident((1, 2 * GLA_HK)),
        ],
        out_specs=[_row_spec(GLA_HK), _row_spec(GLA_HK), _row_spec(GLA_HV), _row_spec(GLA_HV), _row_spec(2 * GLA_HK)],
        out_shape=[
            jax.ShapeDtypeStruct((rows, GLA_HK), F32),
            jax.ShapeDtypeStruct((rows, GLA_HK), F32),
            jax.ShapeDtypeStruct((rows, GLA_HV), BF16),
            jax.ShapeDtypeStruct((rows, GLA_HV), F32),
            jax.ShapeDtypeStruct((rows, 2 * GLA_HK), F32),
        ],
        compiler_params=_params("arbitrary"),
        name="gla_in",
    )(x, g, mod, mod, w_all, w2, bg)


def _split3(x):
    hi = x.astype(BF16)
    rem = x - hi.astype(F32)
    mid = rem.astype(BF16)
    lo = (rem - mid.astype(F32)).astype(BF16)
    return jnp.concatenate([hi, mid, lo], axis=1)


def _fold3(x):
    return x[:, :GLA_DK] + x[:, GLA_DK:2 * GLA_DK] + x[:, 2 * GLA_DK:]


def _gla_scan_kernel(*refs, n_groups, has_s0):
    C, DK, DV, G = GLA_CHUNK, GLA_DK, GLA_DV, GLA_GROUP
    cpg = G // C
    n_chunks = n_groups * cpg
    if has_s0:
        (q_ref, k_ref, v_ref, laf_ref, lab_ref, r_ref, gh_ref, s0f_ref, s0b_ref,
         y_ref, qd_ref, kd_ref, ds_ref, dec_ref, stf_ref, stb_ref) = refs
    else:
        (q_ref, k_ref, v_ref, laf_ref, lab_ref, r_ref, gh_ref,
         y_ref, sf_ref, sb_ref, qd_ref, kd_ref, ds_ref, dec_ref, stf_ref, stb_ref) = refs

    row = lax.broadcasted_iota(jnp.int32, (G, G), 0)
    col = lax.broadcasted_iota(jnp.int32, (G, G), 1)
    same_chunk = (row // C) == (col // C)
    lower = same_chunk & (row >= col)
    upper = same_chunk & (row <= col)
    one_hot = lambda m: jnp.where(m, 1.0, 0.0).astype(BF16)
    sum_f = jnp.concatenate([one_hot(lower), one_hot(same_chunk & (row < col))], axis=0)
    sum_b = jnp.concatenate([one_hot(upper), one_hot(same_chunk & (row > col))], axis=0)

    def group(g):
        return pl.ds(pl.multiple_of(g * G, G), G)

    def pass1(g, carry):
        sl = group(g)
        cf = _dot(sum_f, _split3(laf_ref[sl, :]))
        cb = _dot(sum_b, _split3(lab_ref[sl, :]))
        bf, rest_f = _fold3(cf[:G]), _fold3(cf[G:])
        bb, rest_b = _fold3(cb[:G]), _fold3(cb[G:])
        q = q_ref[sl, :] * (DK ** -0.5)
        k = k_ref[sl, :]
        qd_ref[sl, :] = jnp.concatenate([q * jnp.exp(bf), q * jnp.exp(bb)], axis=1).astype(BF16)
        kd_ref[sl, :] = jnp.concatenate([k * jnp.exp(-bf), k * jnp.exp(-bb)], axis=1).astype(BF16)
        k_end = jnp.concatenate([k * jnp.exp(rest_f), k * jnp.exp(rest_b)], axis=1).astype(BF16)
        v = v_ref[sl, :]
        for c in range(cpg):
            rows = slice(c * C, (c + 1) * C)
            ds_ref[g * cpg + c] = _dot_tn(v[rows], k_end[rows])
            total = jnp.concatenate([bf[(c + 1) * C - 1:(c + 1) * C], bb[c * C:c * C + 1]], axis=1)
            dec_ref[g * cpg + c] = jnp.exp(total)
        return carry

    lax.fori_loop(0, n_groups, pass1, 0)

    if has_s0:
        stf_ref[...] = s0f_ref[...].T
        stb_ref[...] = s0b_ref[...].T
    else:
        stf_ref[...] = jnp.zeros((DV, DK), F32)
        stb_ref[...] = jnp.zeros((DV, DK), F32)

    def pass2(i, carry):
        j = n_chunks - 1 - i
        s_f = stf_ref[...]
        inc_f = ds_ref[i, :, :DK]
        ds_ref[i, :, :DK] = s_f
        stf_ref[...] = s_f * dec_ref[i, :, :DK] + inc_f
        s_b = stb_ref[...]
        inc_b = ds_ref[j, :, DK:]
        ds_ref[j, :, DK:] = s_b
        stb_ref[...] = s_b * dec_ref[j, :, DK:] + inc_b
        return carry

    lax.fori_loop(0, n_chunks, pass2, 0)
    if not has_s0:
        sf_ref[...] = stf_ref[...].T
        sb_ref[...] = stb_ref[...].T

    gh = gh_ref[...]

    def pass3(g, carry):
        sl = group(g)
        qd = qd_ref[sl, :]
        kd = kd_ref[sl, :]
        a_f = _dot_nt(qd[:, :DK], kd[:, :DK])
        a_b = _dot_nt(qd[:, DK:], kd[:, DK:])
        a = jnp.where(lower, a_f, 0.0) + jnp.where(upper, a_b, 0.0)
        inter = [_dot_nt(qd[c * C:(c + 1) * C], ds_ref[g * cpg + c].astype(BF16)) for c in range(cpg)]
        o = _dot(a.astype(BF16), v_ref[sl, :]) + jnp.concatenate(inter, axis=0)
        y_ref[sl, :] = (_rms(o, gh) * _silu(r_ref[sl, :])).astype(BF16)
        return carry

    lax.fori_loop(0, n_groups, pass3, 0)


def _gla_scan(q, k, v, la, r, g_head, batch, seq, s0f=None, s0b=None):
    n_chunks = seq // GLA_CHUNK
    has_s0 = s0f is not None
    rows = batch * seq
    tok = lambda w, off: pl.BlockSpec((seq, w), lambda b, h: (b, h + off))
    state = pl.BlockSpec((None, None, GLA_DK, GLA_DV), lambda b, h: (b, h, 0, 0))
    in_specs = [tok(GLA_DK, 0), tok(GLA_DK, 0), tok(GLA_DV, 0), tok(GLA_DK, 0), tok(GLA_DK, GLA_HEADS),
                tok(GLA_DV, 0), pl.BlockSpec((1, GLA_DV), lambda b, h: (0, 0))]
    args = [q, k, v, la, la, r, g_head]
    y_shape = jax.ShapeDtypeStruct((rows, GLA_HV), BF16)
    st_shape = jax.ShapeDtypeStruct((batch, GLA_HEADS, GLA_DK, GLA_DV), F32)
    scratch = [
        pltpu.VMEM((seq, 2 * GLA_DK), BF16),
        pltpu.VMEM((seq, 2 * GLA_DK), BF16),
        pltpu.VMEM((n_chunks, GLA_DV, 2 * GLA_DK), F32),
        pltpu.VMEM((n_chunks, 1, 2 * GLA_DK), F32),
        pltpu.VMEM((GLA_DV, GLA_DK), F32),
        pltpu.VMEM((GLA_DV, GLA_DK), F32),
    ]
    if has_s0:
        in_specs += [state, state]
        args += [s0f, s0b]
        out_specs = tok(GLA_DV, 0)
        out_shape = y_shape
    else:
        out_specs = [tok(GLA_DV, 0), state, state]
        out_shape = [y_shape, st_shape, st_shape]
    return pl.pallas_call(
        functools.partial(_gla_scan_kernel, n_groups=seq // GLA_GROUP, has_s0=has_s0),
        grid=(batch, GLA_HEADS),
        in_specs=in_specs,
        out_specs=out_specs,
        out_shape=out_shape,
        scratch_shapes=scratch,
        compiler_params=_params("arbitrary", "arbitrary"),
        name="gla_scan",
    )(*args)


def _out_ffn_kernel(*refs, final_norm):
    if final_norm:
        (x_ref, y_ref, wo_ref, g1_ref, gn_ref, sh_ref, sc_ref, g2_ref, wg_ref, wu_ref, wd_ref, gf_ref, o_ref) = refs
    else:
        (x_ref, y_ref, wo_ref, g1_ref, gn_ref, sh_ref, sc_ref, g2_ref, wg_ref, wu_ref, wd_ref, o_ref) = refs
    x = x_ref[...] + g1_ref[...] * _dot(y_ref[...], wo_ref[...])
    h = _norm_mod(x, gn_ref[...], sh_ref[...], sc_ref[...]).astype(BF16)
    act = (_silu(_dot(h, wg_ref[...])) * _dot(h, wu_ref[...])).astype(BF16)
    x = x + g2_ref[...] * _dot(act, wd_ref[...])
    if final_norm:
        x = _rms(x, gf_ref[...])
    o_ref[...] = x


def _out_ffn(x, y, mod, layer, rows_per_batch, w_out, g_ffn, wg, wu, wd, g_final=None):
    rows = x.shape[0]
    final_norm = g_final is not None
    in_specs = [
        _row_spec(D_MODEL),
        _row_spec(D_MODEL),
        _resident((D_MODEL, D_MODEL)),
        _mod_spec(layer, 2, rows_per_batch),
        _resident((1, D_MODEL)),
        _mod_spec(layer, 3, rows_per_batch),
        _mod_spec(layer, 4, rows_per_batch),
        _mod_spec(layer, 5, rows_per_batch),
        _resident((D_MODEL, D_FF)),
        _resident((D_MODEL, D_FF)),
        _resident((D_FF, D_MODEL)),
    ]
    args = [x, y, w_out, mod, g_ffn, mod, mod, mod, wg, wu, wd]
    if final_norm:
        in_specs.append(_resident((1, D_MODEL)))
        args.append(g_final)
    return pl.pallas_call(
        functools.partial(_out_ffn_kernel, final_norm=final_norm),
        grid=(rows // ROW_TILE,),
        in_specs=in_specs,
        out_specs=_row_spec(D_MODEL),
        out_shape=jax.ShapeDtypeStruct((rows, D_MODEL), F32),
        compiler_params=_params("arbitrary"),
        name="out_ffn",
    )(*args)


def _rope_tables(n_tokens):
    rows = n_tokens // GRID_W
    r, col = jnp.meshgrid(jnp.arange(rows), jnp.arange(GRID_W), indexing="ij")
    r = r.reshape(-1).astype(F32)
    col = col.reshape(-1).astype(F32)
    n_freq = DIFF_DH // 4
    inv = ROPE_THETA ** (-jnp.arange(n_freq, dtype=F32) / n_freq)
    ang = jnp.concatenate([r[:, None] * inv, col[:, None] * inv], axis=-1)
    cos, sin = jnp.cos(ang), jnp.sin(ang)
    reps = D_MODEL // DIFF_DH
    return jnp.tile(jnp.concatenate([cos, cos], axis=-1), (1, reps)), jnp.tile(jnp.concatenate([-sin, sin], axis=-1), (1, reps))


def _rope(x, cos, sin_signed):
    half = DIFF_DH // 2
    width = x.shape[1]
    lane = lax.broadcasted_iota(jnp.int32, x.shape, 1)
    from_above = pltpu.roll(x, width - half, axis=1)
    from_below = pltpu.roll(x, half, axis=1)
    swapped = jnp.where(lane % DIFF_DH < half, from_above, from_below)
    return x * cos + swapped * sin_signed


def _qkv_kernel(*refs, rope):
    if rope:
        (x_ref, g_ref, sh_ref, sc_ref, w_ref, cos_ref, sin_ref, q_ref, k_ref, v_ref) = refs
    else:
        (x_ref, g_ref, sh_ref, sc_ref, w_ref, q_ref, k_ref, v_ref, kf_ref, vf_ref) = refs
    h = _norm_mod(x_ref[...], g_ref[...], sh_ref[...], sc_ref[...]).astype(BF16)
    z = _dot(h, w_ref[...])
    q = z[:, :D_MODEL]
    k = z[:, D_MODEL:2 * D_MODEL]
    v = z[:, 2 * D_MODEL:]
    if rope:
        q = _rope(q, cos_ref[...], sin_ref[...])
        k = _rope(k, cos_ref[...], sin_ref[...])
    else:
        kf_ref[...] = k.T
        for head in range(DIFF_HEADS):
            vf_ref[pl.ds(head, ROW_TILE, stride=DIFF_HEADS), :] = v[:, head * DIFF_DV:(head + 1) * DIFF_DV]
    q_ref[...] = (q * (DIFF_DH ** -0.5 * LOG2E)).astype(BF16)
    k_ref[...] = k.astype(BF16)
    v_ref[...] = v.astype(BF16)


def _qkv(x, mod, layer, rows_per_batch, g, w, tables=None):
    rows = x.shape[0]
    rope = tables is not None
    in_specs = [_row_spec(D_MODEL), _resident((1, D_MODEL)), _mod_spec(layer, 0, rows_per_batch),
                _mod_spec(layer, 1, rows_per_batch), _resident((D_MODEL, 3 * D_MODEL))]
    args = [x, g, mod, mod, w]
    out_specs = [_row_spec(D_MODEL)] * 3
    out_shape = [jax.ShapeDtypeStruct((rows, D_MODEL), BF16)] * 3
    if rope:
        tiles = rows_per_batch // ROW_TILE
        tab = pl.BlockSpec((ROW_TILE, D_MODEL), lambda i: (i % tiles, 0))
        in_specs += [tab, tab]
        args += list(tables)
    else:
        n_tiles = rows // ROW_TILE
        out_specs += [pl.BlockSpec((None, D_MODEL, ROW_TILE), lambda i: (i, 0, 0)),
                      pl.BlockSpec((ROW_TILE * DIFF_HEADS, DIFF_DV), lambda i: (i, 0))]
        out_shape += [jax.ShapeDtypeStruct((n_tiles, D_MODEL, ROW_TILE), F32),
                      jax.ShapeDtypeStruct((rows * DIFF_HEADS, DIFF_DV), F32)]
    return pl.pallas_call(
        functools.partial(_qkv_kernel, rope=rope),
        grid=(rows // ROW_TILE,),
        in_specs=in_specs,
        out_specs=out_specs,
        out_shape=out_shape,
        compiler_params=_params("arbitrary"),
        name="diff_qkv",
    )(*args)


def _attn_kernel(*refs, has_cache, lam_init, n_q_tiles, q_tile):
    if has_cache:
        (lq_ref, lk_ref, q_ref, k_ref, v_ref, gh_ref, kc_ref, vc_ref, y_ref, kcb_ref, vcb_ref) = refs
    else:
        (lq_ref, lk_ref, q_ref, k_ref, v_ref, gh_ref, y_ref) = refs
    e = jnp.exp(jnp.sum(lq_ref[...] * lk_ref[...], axis=1, keepdims=True))
    lam = e[0:1, :] - e[1:2, :] + lam_init
    first_map = lax.broadcasted_iota(jnp.int32, (q_tile, 2 * DIFF_DH), 1) < DIFF_DH
    gh = gh_ref[...] * (1.0 - lam_init)

    if has_cache:
        for head in range(DIFF_HEADS):
            kcb_ref[head] = kc_ref[head].astype(BF16)
            vcb_ref[head] = vc_ref[:, head, :].astype(BF16)

    def head_lanes(head):
        return slice(head * 2 * DIFF_DH, (head + 1) * 2 * DIFF_DH)

    def scores(head, sl):
        q = q_ref[sl, head_lanes(head)]
        zero = jnp.zeros_like(q)
        qq = jnp.concatenate([jnp.where(first_map, q, zero), jnp.where(first_map, zero, q)], axis=0)
        s = [_dot_nt(qq, k_ref[:, head_lanes(head)])]
        if has_cache:
            s.insert(0, _dot(qq, kcb_ref[head]))
        return s

    def attend(head, sl, s):
        vals = [v_ref[:, head_lanes(head)]]
        if has_cache:
            vals.insert(0, vcb_ref[head])
        m = functools.reduce(jnp.maximum, [jnp.max(x, axis=-1, keepdims=True) for x in s])
        p = [jnp.exp2(x - m) for x in s]
        denom = functools.reduce(jnp.add, [jnp.sum(x, axis=-1, keepdims=True) for x in p])
        acc = functools.reduce(jnp.add, [_dot(x.astype(BF16), vv) for x, vv in zip(p, vals)])
        o = acc[:q_tile] * (1.0 / denom[:q_tile]) - acc[q_tile:] * (lam / denom[q_tile:])
        y_ref[sl, head_lanes(head)] = _rms(o, gh).astype(BF16)

    def body(t, carry):
        sl = pl.ds(pl.multiple_of(t * q_tile, q_tile), q_tile)
        s_next = scores(0, sl)
        for head in range(DIFF_HEADS):
            s_cur = s_next
            if head + 1 < DIFF_HEADS:
                s_next = scores(head + 1, sl)
            attend(head, sl, s_cur)
        return carry

    lax.fori_loop(0, n_q_tiles, body, 0)


def _attn(q, k, v, lam_q, lam_k, g_head, lam_init, batch, seq, cache_k=None, cache_v=None):
    has_cache = cache_k is not None
    tok = pl.BlockSpec((seq, D_MODEL), lambda b: (b, 0))
    small = lambda shape: pl.BlockSpec(shape, lambda b: (0, 0))
    in_specs = [small((2, DIFF_DH)), small((2, DIFF_DH)), tok, tok, tok, small((1, DIFF_DV))]
    args = [lam_q, lam_k, q, k, v, g_head]
    if has_cache:
        past = cache_k.shape[-1]
        in_specs += [pl.BlockSpec((None, DIFF_HEADS, 2 * DIFF_DH, past), lambda b: (b, 0, 0, 0)),
                     pl.BlockSpec((None, past, DIFF_HEADS, DIFF_DV), lambda b: (b, 0, 0, 0))]
        args += [cache_k, cache_v]
        scratch = [pltpu.VMEM((DIFF_HEADS, 2 * DIFF_DH, past), BF16), pltpu.VMEM((DIFF_HEADS, past, DIFF_DV), BF16)]
    else:
        scratch = []
    q_tile = min(Q_TILE, seq)
    return pl.pallas_call(
        functools.partial(_attn_kernel, has_cache=has_cache, lam_init=lam_init, n_q_tiles=seq // q_tile, q_tile=q_tile),
        grid=(batch,),
        in_specs=in_specs,
        out_specs=tok,
        out_shape=jax.ShapeDtypeStruct((batch * seq, D_MODEL), BF16),
        scratch_shapes=scratch,
        compiler_params=_params("arbitrary"),
        name="diff_attn",
    )(*args)


def kernel(x_prompt, x_sample, state_gla_fwd, state_gla_bwd, cache_diff_k, cache_diff_v, c, c_ctx, w_ada, b_ada, g_mix_norm, g_ffn_norm, w_gla_in, w_gla_g1, w_gla_g2, b_gla_g, g_gla_head, w_gla_out, w_diff_qkv, lam_q, lam_k, g_diff_head, w_diff_out, w_ffn_gate, w_ffn_up, w_ffn_down, g_final):
    bp, tp, _ = x_prompt.shape
    bs, ts, _ = x_sample.shape
    assert tp == ROW_TILE and ts % ROW_TILE == 0 and tp % GLA_GROUP == 0 and ts % GLA_GROUP == 0
    xp = x_prompt.reshape(bp * tp, D_MODEL)
    xs = x_sample.reshape(bs * ts, D_MODEL)

    cond = jnp.concatenate([c_ctx[None, :], c, jnp.zeros((MOD_ROWS - 1 - bs, D_MODEL), F32)], axis=0)
    mod = _ada(cond, w_ada, b_ada).reshape(DEPTH, MOD_ROWS, N_MOD, 1, D_MODEL)

    j = 0
    gate_cols = jnp.concatenate([w_gla_g1[j, 0], w_gla_g1[j, 1],
                                 jnp.zeros((D_MODEL, GATE_PAD - 2 * GLA_RANK), F32)], axis=1)
    w_all = jnp.concatenate([w_gla_in[j], gate_cols], axis=1).astype(BF16)
    w2 = jnp.zeros((GATE_PAD, 2 * GLA_HK), F32)
    w2 = w2.at[:GLA_RANK, :GLA_HK].set(w_gla_g2[j, 0]).at[GLA_RANK:2 * GLA_RANK, GLA_HK:].set(w_gla_g2[j, 1]).astype(BF16)
    bg = b_gla_g[j].reshape(1, 2 * GLA_HK)
    g_mix = g_mix_norm[0].reshape(1, D_MODEL)
    g_head = g_gla_head[j].reshape(1, GLA_DV)
    ffn = lambda i: (g_ffn_norm[i].reshape(1, D_MODEL), w_ffn_gate[i].astype(BF16), w_ffn_up[i].astype(BF16),
                     w_ffn_down[i].astype(BF16))
    ffn0 = ffn(0)
    w_out0 = w_gla_out[j].astype(BF16)

    qp, kp, vp, rp, lap = _gla_in(xp, mod, 0, None, g_mix, w_all, w2, bg)
    qs, ks, vs, rs, las = _gla_in(xs, mod, 0, ts, g_mix, w_all, w2, bg)
    yp, new_f, new_b = _gla_scan(qp, kp, vp, lap, rp, g_head, bp, tp)
    ys = _gla_scan(qs, ks, vs, las, rs, g_head, bs, ts, state_gla_fwd[:, j], state_gla_bwd[:, j])
    xp = _out_ffn(xp, yp, mod, 0, None, w_out0, *ffn0)
    xs = _out_ffn(xs, ys, mod, 0, ts, w_out0, *ffn0)

    lam_init = 0.8 - 0.6 * math.exp(-0.3 * 1)
    g_mix = g_mix_norm[1].reshape(1, D_MODEL)
    w_qkv = w_diff_qkv[j].astype(BF16)
    g_head = g_diff_head[j].reshape(1, DIFF_DV)
    ffn1 = ffn(1)
    w_out1 = w_diff_out[j].astype(BF16)

    qp, kp, vp, kp32, vp32 = _qkv(xp, mod, 1, None, g_mix, w_qkv)
    qs, ks, vs = _qkv(xs, mod, 1, ts, g_mix, w_qkv, _rope_tables(ts))
    yp = _attn(qp, kp, vp, lam_q[j], lam_k[j], g_head, lam_init, bp, tp)
    past = cache_diff_k.shape[2]
    cache_k_t = jnp.transpose(cache_diff_k[:, j], (0, 2, 3, 4, 1)).reshape(bs, DIFF_HEADS, 2 * DIFF_DH, past)
    ys = _attn(qs, ks, vs, lam_q[j], lam_k[j], g_head, lam_init, bs, ts, cache_k_t, cache_diff_v[:, j])
    g_fin = g_final.reshape(1, D_MODEL)
    yp_out = _out_ffn(xp, yp, mod, 1, None, w_out1, *ffn1, g_final=g_fin)
    ys_out = _out_ffn(xs, ys, mod, 1, ts, w_out1, *ffn1, g_final=g_fin)

    return (yp_out.reshape(bp, tp, D_MODEL),
            ys_out.reshape(bs, ts, D_MODEL),
            new_f[:, None],
            new_b[:, None],
            jnp.transpose(kp32.reshape(bp, 1, DIFF_HEADS, 2, DIFF_DH, tp), (0, 1, 5, 2, 3, 4)),
            vp32.reshape(bp, 1, tp, DIFF_HEADS, DIFF_DV))
```

```python
import functools
import math

import jax
import jax.numpy as jnp
from jax import lax
from jax.experimental import pallas as pl
from jax.experimental.pallas import tpu as pltpu

F32 = jnp.float32
BF16 = jnp.bfloat16

D_MODEL = 1024
DEPTH = 2
GRID_W = 64
GLA_HEADS = 4
GLA_DK = 128
GLA_DV = 256
GLA_HK = GLA_HEADS * GLA_DK
GLA_HV = GLA_HEADS * GLA_DV
GLA_RANK = 16
GLA_TAU = 16.0
GLA_CHUNK = 64
GLA_GROUP = 256
GLA_HEADS_PER_STEP = 2
DIFF_HEADS = 8
DIFF_DH = 64
DIFF_DV = 128
ROPE_THETA = 10000.0
D_FF = 2816
EPS = 1e-6
LOG2E = math.log2(math.e)
N_MOD = 6
MOD_ROWS = 16
GATE_PAD = 128

VMEM_LIMIT_BYTES = 56 * 1024 * 1024
ROW_TILE = 512
SUB_TILE = 256
Q_TILE = 256


def _params(*sem):
    return pltpu.CompilerParams(dimension_semantics=sem, vmem_limit_bytes=VMEM_LIMIT_BYTES)


def _resident(shape, index=None):
    index = (0,) * len(shape) if index is None else index
    return pl.BlockSpec(shape, lambda *_: index, pipeline_mode=pl.Buffered(1))


def _sub_tiles():
    return [slice(s * SUB_TILE, (s + 1) * SUB_TILE) for s in range(ROW_TILE // SUB_TILE)]


def _sigmoid(x):
    return 1.0 / (1.0 + jnp.exp(-x))


def _silu(x):
    return x * _sigmoid(x)


def _rms(x, g):
    return x * lax.rsqrt(jnp.mean(x * x, axis=-1, keepdims=True) + EPS) * g


def _norm_mod(x, g, shift, scale):
    return _rms(x, g) * (1.0 + scale) + shift


def _dot(a, b):
    return jnp.dot(a, b, preferred_element_type=F32)


def _dot_nt(a, b):
    return lax.dot_general(a, b, (((1,), (1,)), ((), ())), preferred_element_type=F32)


def _dot_tn(a, b):
    return lax.dot_general(a, b, (((0,), (0,)), ((), ())), preferred_element_type=F32)


def _ada_kernel(c_ref, w_ref, b_ref, o_ref):
    a = _silu(c_ref[...]).astype(BF16)
    o_ref[...] = _dot(a, w_ref[...].astype(BF16)) + b_ref[...]


def _ada(cond, w_ada, b_ada):
    tn = 1536
    n = N_MOD * D_MODEL
    return pl.pallas_call(
        _ada_kernel,
        grid=(DEPTH, n // tn),
        in_specs=[
            pl.BlockSpec((MOD_ROWS, D_MODEL), lambda l, j: (0, 0)),
            pl.BlockSpec((None, D_MODEL, tn), lambda l, j: (l, 0, j)),
            pl.BlockSpec((None, 1, tn), lambda l, j: (l, 0, j)),
        ],
        out_specs=pl.BlockSpec((None, MOD_ROWS, tn), lambda l, j: (l, 0, j)),
        out_shape=jax.ShapeDtypeStruct((DEPTH, MOD_ROWS, n), F32),
        compiler_params=_params("arbitrary", "arbitrary"),
        name="ada",
    )(cond, w_ada, b_ada.reshape(DEPTH, 1, n))


def _mod_spec(layer, which, rows_per_batch):
    if rows_per_batch is None:
        return pl.BlockSpec((None, None, None, 1, D_MODEL), lambda i: (layer, 0, which, 0, 0))
    tiles = rows_per_batch // ROW_TILE
    return pl.BlockSpec((None, None, None, 1, D_MODEL), lambda i: (layer, 1 + i // tiles, which, 0, 0))


def _row_spec(width):
    return pl.BlockSpec((ROW_TILE, width), lambda i: (i, 0))


def _log_sigmoid(x):
    return jnp.minimum(x, 0.0) - jnp.log(1.0 + jnp.exp(-jnp.abs(x)))


def _gla_in_kernel(x_ref, g_ref, sh_ref, sc_ref, w_ref, w1_ref, w2_ref, bg_ref, q_ref, k_ref, v_ref, r_ref, la_ref):
    for rows in _sub_tiles():
        h = _norm_mod(x_ref[rows, :], g_ref[...], sh_ref[...], sc_ref[...]).astype(BF16)
        z = _dot(h, w_ref[...])
        q_ref[rows, :] = z[:, :GLA_HK]
        k_ref[rows, :] = z[:, GLA_HK:2 * GLA_HK]
        v_ref[rows, :] = z[:, 2 * GLA_HK:2 * GLA_HK + GLA_HV].astype(BF16)
        r_ref[rows, :] = z[:, 2 * GLA_HK + GLA_HV:]
        low_rank = _dot(h, w1_ref[...]).astype(BF16)
        pre = _dot(low_rank, w2_ref[...]) + bg_ref[...]
        la_ref[rows, :] = _log_sigmoid(pre) * (1.0 / GLA_TAU)


def _gla_in(x, mod, layer, rows_per_batch, g, w_in, w1, w2, bg):
    rows = x.shape[0]
    return pl.pallas_call(
        _gla_in_kernel,
        grid=(rows // ROW_TILE,),
        in_specs=[
            _row_spec(D_MODEL),
            _resident((1, D_MODEL)),
            _mod_spec(layer, 0, rows_per_batch),
            _mod_spec(layer, 1, rows_per_batch),
            _resident((D_MODEL, 2 * GLA_HK + 2 * GLA_HV)),
            _resident((D_MODEL, GATE_PAD)),
            _resident((GATE_PAD, 2 * GLA_HK)),
            _resident((1, 2 * GLA_HK)),
        ],
        out_specs=[_row_spec(GLA_HK), _row_spec(GLA_HK), _row_spec(GLA_HV), _row_spec(GLA_HV), _row_spec(2 * GLA_HK)],
        out_shape=[
            jax.ShapeDtypeStruct((rows, GLA_HK), F32),
            jax.ShapeDtypeStruct((rows, GLA_HK), F32),
            jax.ShapeDtypeStruct((rows, GLA_HV), BF16),
            jax.ShapeDtypeStruct((rows, GLA_HV), F32),
            jax.ShapeDtypeStruct((rows, 2 * GLA_HK), F32),
        ],
        compiler_params=_params("arbitrary"),
        name="gla_in",
    )(x, g, mod, mod, w_in, w1, w2, bg)


def _split3(x):
    hi = x.astype(BF16)
    rem = x - hi.astype(F32)
    mid = rem.astype(BF16)
    lo = (rem - mid.astype(F32)).astype(BF16)
    return jnp.concatenate([hi, mid, lo], axis=1)


def _fold3(x):
    return x[:, :GLA_DK] + x[:, GLA_DK:2 * GLA_DK] + x[:, 2 * GLA_DK:]


def _gla_scan_kernel(*refs, n_groups, has_s0):
    C, DK, DV, G, HPS = GLA_CHUNK, GLA_DK, GLA_DV, GLA_GROUP, GLA_HEADS_PER_STEP
    cpg = G // C
    n_chunks = n_groups * cpg
    if has_s0:
        (q_ref, k_ref, v_ref, laf_ref, lab_ref, r_ref, gh_ref, s0f_ref, s0b_ref,
         y_ref, qd_ref, kd_ref, ds_ref, sin_ref, dec_ref) = refs
    else:
        (q_ref, k_ref, v_ref, laf_ref, lab_ref, r_ref, gh_ref,
         y_ref, sf_ref, sb_ref, qd_ref, kd_ref, ds_ref, sin_ref, dec_ref) = refs

    row = lax.broadcasted_iota(jnp.int32, (G, G), 0)
    col = lax.broadcasted_iota(jnp.int32, (G, G), 1)
    same_chunk = (row // C) == (col // C)
    lower = same_chunk & (row >= col)
    upper = same_chunk & (row <= col)
    prefix_sum = jnp.where(lower, 1.0, 0.0).astype(BF16)
    suffix_sum = jnp.where(upper, 1.0, 0.0).astype(BF16)

    def group(g):
        return pl.ds(pl.multiple_of(g * G, G), G)

    def spread(x, first_row):
        return jnp.concatenate(
            [jnp.broadcast_to(x[first_row + c * C:first_row + c * C + 1], (C, x.shape[1])) for c in range(cpg)], axis=0)

    def pass1(g, carry):
        sl = group(g)
        for hd in range(HPS):
            dk = slice(hd * DK, (hd + 1) * DK)
            dv = slice(hd * DV, (hd + 1) * DV)
            both = slice(hd * 2 * DK, (hd + 1) * 2 * DK)
            bf = _fold3(_dot(prefix_sum, _split3(laf_ref[sl, dk])))
            bb = _fold3(_dot(suffix_sum, _split3(lab_ref[sl, dk])))
            tot_f = spread(bf, C - 1)
            tot_b = spread(bb, 0)
            q = q_ref[sl, dk] * (DK ** -0.5)
            k = k_ref[sl, dk]
            qd_ref[sl, both] = jnp.concatenate([q * jnp.exp(bf), q * jnp.exp(bb)], axis=1).astype(BF16)
            kd_ref[sl, both] = jnp.concatenate([k * jnp.exp(-bf), k * jnp.exp(-bb)], axis=1).astype(BF16)
            k_end = jnp.concatenate([k * jnp.exp(tot_f - bf), k * jnp.exp(tot_b - bb)], axis=1).astype(BF16)
            v = v_ref[sl, dv]
            for c in range(cpg):
                rows = slice(c * C, (c + 1) * C)
                ds_ref[hd, g * cpg + c] = _dot_tn(v[rows], k_end[rows])
                dec_ref[hd, g * cpg + c] = jnp.exp(
                    jnp.concatenate([tot_f[c * C:c * C + 1], tot_b[c * C:c * C + 1]], axis=1))
        return carry

    lax.fori_loop(0, n_groups, pass1, 0, unroll=min(2, n_groups))

    for hd in range(HPS):
        for forward in (True, False):
            lanes = slice(0, DK) if forward else slice(DK, 2 * DK)
            if has_s0:
                init = (s0f_ref if forward else s0b_ref)[hd].T
            else:
                init = jnp.zeros((DV, DK), F32)

            def step(i, s, hd=hd, forward=forward, lanes=lanes):
                n = i if forward else n_chunks - 1 - i
                sin_ref[hd, n, :, lanes] = s.astype(BF16)
                return s * dec_ref[hd, n, :, lanes] + ds_ref[hd, n, :, lanes]

            final = lax.fori_loop(0, n_chunks, step, init)
            if not has_s0:
                (sf_ref if forward else sb_ref)[hd] = final.T

    gh = gh_ref[...]

    def pass3(g, carry):
        sl = group(g)
        for hd in range(HPS):
            dv = slice(hd * DV, (hd + 1) * DV)
            both = slice(hd * 2 * DK, (hd + 1) * 2 * DK)
            qd = qd_ref[sl, both]
            kd = kd_ref[sl, both]
            a_f = _dot_nt(qd[:, :DK], kd[:, :DK])
            a_b = _dot_nt(qd[:, DK:], kd[:, DK:])
            a = jnp.where(lower, a_f, 0.0) + jnp.where(upper, a_b, 0.0)
            inter = [_dot_nt(qd[c * C:(c + 1) * C], sin_ref[hd, g * cpg + c]) for c in range(cpg)]
            o = _dot(a.astype(BF16), v_ref[sl, dv]) + jnp.concatenate(inter, axis=0)
            y_ref[sl, dv] = (_rms(o, gh) * _silu(r_ref[sl, dv])).astype(BF16)
        return carry

    lax.fori_loop(0, n_groups, pass3, 0, unroll=min(2, n_groups))


def _gla_scan(q, k, v, la, r, g_head, batch, seq, s0f=None, s0b=None):
    hps = GLA_HEADS_PER_STEP
    steps = GLA_HEADS // hps
    n_chunks = seq // GLA_CHUNK
    has_s0 = s0f is not None
    rows = batch * seq
    tok = lambda w, off: pl.BlockSpec((seq, w * hps), lambda b, h: (b, h + off))
    state = pl.BlockSpec((None, hps, GLA_DK, GLA_DV), lambda b, h: (b, h, 0, 0))
    in_specs = [tok(GLA_DK, 0), tok(GLA_DK, 0), tok(GLA_DV, 0), tok(GLA_DK, 0), tok(GLA_DK, steps),
                tok(GLA_DV, 0), pl.BlockSpec((1, GLA_DV), lambda b, h: (0, 0))]
    args = [q, k, v, la, la, r, g_head]
    y_shape = jax.ShapeDtypeStruct((rows, GLA_HV), BF16)
    st_shape = jax.ShapeDtypeStruct((batch, GLA_HEADS, GLA_DK, GLA_DV), F32)
    scratch = [
        pltpu.VMEM((seq, hps * 2 * GLA_DK), BF16),
        pltpu.VMEM((seq, hps * 2 * GLA_DK), BF16),
        pltpu.VMEM((hps, n_chunks, GLA_DV, 2 * GLA_DK), F32),
        pltpu.VMEM((hps, n_chunks, GLA_DV, 2 * GLA_DK), BF16),
        pltpu.VMEM((hps, n_chunks, 1, 2 * GLA_DK), F32),
    ]
    if has_s0:
        in_specs += [state, state]
        args += [s0f, s0b]
        out_specs = tok(GLA_DV, 0)
        out_shape = y_shape
    else:
        out_specs = [tok(GLA_DV, 0), state, state]
        out_shape = [y_shape, st_shape, st_shape]
    return pl.pallas_call(
        functools.partial(_gla_scan_kernel, n_groups=seq // GLA_GROUP, has_s0=has_s0),
        grid=(batch, steps),
        in_specs=in_specs,
        out_specs=out_specs,
        out_shape=out_shape,
        scratch_shapes=scratch,
        compiler_params=_params("arbitrary", "arbitrary"),
        name="gla_scan",
    )(*args)


def _out_ffn_kernel(*refs, final_norm):
    if final_norm:
        (x_ref, y_ref, wo_ref, g1_ref, gn_ref, sh_ref, sc_ref, g2_ref, wg_ref, wu_ref, wd_ref, gf_ref, o_ref) = refs
    else:
        (x_ref, y_ref, wo_ref, g1_ref, gn_ref, sh_ref, sc_ref, g2_ref, wg_ref, wu_ref, wd_ref, o_ref) = refs
    for rows in _sub_tiles():
        x = x_ref[rows, :] + g1_ref[...] * _dot(y_ref[rows, :], wo_ref[...])
        h = _norm_mod(x, gn_ref[...], sh_ref[...], sc_ref[...]).astype(BF16)
        act = (_silu(_dot(h, wg_ref[...])) * _dot(h, wu_ref[...])).astype(BF16)
        x = x + g2_ref[...] * _dot(act, wd_ref[...])
        if final_norm:
            x = _rms(x, gf_ref[...])
        o_ref[rows, :] = x


def _out_ffn(x, y, mod, layer, rows_per_batch, w_out, g_ffn, wg, wu, wd, g_final=None):
    rows = x.shape[0]
    final_norm = g_final is not None
    in_specs = [
        _row_spec(D_MODEL),
        _row_spec(D_MODEL),
        _resident((D_MODEL, D_MODEL)),
        _mod_spec(layer, 2, rows_per_batch),
        _resident((1, D_MODEL)),
        _mod_spec(layer, 3, rows_per_batch),
        _mod_spec(layer, 4, rows_per_batch),
        _mod_spec(layer, 5, rows_per_batch),
        _resident((None, D_MODEL, D_FF), (layer, 0, 0)),
        _resident((None, D_MODEL, D_FF), (layer, 0, 0)),
        _resident((None, D_FF, D_MODEL), (layer, 0, 0)),
    ]
    args = [x, y, w_out, mod, g_ffn, mod, mod, mod, wg, wu, wd]
    if final_norm:
        in_specs.append(_resident((1, D_MODEL)))
        args.append(g_final)
    return pl.pallas_call(
        functools.partial(_out_ffn_kernel, final_norm=final_norm),
        grid=(rows // ROW_TILE,),
        in_specs=in_specs,
        out_specs=_row_spec(D_MODEL),
        out_shape=jax.ShapeDtypeStruct((rows, D_MODEL), F32),
        compiler_params=_params("arbitrary"),
        name="out_ffn",
    )(*args)


def _rope_tables(n_tokens):
    rows = n_tokens // GRID_W
    r, col = jnp.meshgrid(jnp.arange(rows), jnp.arange(GRID_W), indexing="ij")
    r = r.reshape(-1).astype(F32)
    col = col.reshape(-1).astype(F32)
    n_freq = DIFF_DH // 4
    inv = ROPE_THETA ** (-jnp.arange(n_freq, dtype=F32) / n_freq)
    ang = jnp.concatenate([r[:, None] * inv, col[:, None] * inv], axis=-1)
    cos, sin = jnp.cos(ang), jnp.sin(ang)
    reps = D_MODEL // DIFF_DH
    return jnp.tile(jnp.concatenate([cos, cos], axis=-1), (1, reps)), jnp.tile(jnp.concatenate([-sin, sin], axis=-1), (1, reps))


def _rope(x, cos, sin_signed):
    half = DIFF_DH // 2
    width = x.shape[1]
    lane = lax.broadcasted_iota(jnp.int32, x.shape, 1)
    from_above = pltpu.roll(x, width - half, axis=1)
    from_below = pltpu.roll(x, half, axis=1)
    swapped = jnp.where(lane % DIFF_DH < half, from_above, from_below)
    return x * cos + swapped * sin_signed


def _qkv_kernel(*refs, rope, seq):
    if rope:
        (x_ref, g_ref, sh_ref, sc_ref, w_ref, cos_ref, sin_ref, q_ref, k_ref, v_ref) = refs
    else:
        (x_ref, g_ref, sh_ref, sc_ref, w_ref, q_ref, k_ref, v_ref, kf_ref, vf_ref) = refs
    for rows in _sub_tiles():
        h = _norm_mod(x_ref[rows, :], g_ref[...], sh_ref[...], sc_ref[...]).astype(BF16)
        z = _dot(h, w_ref[...])
        q = z[:, :D_MODEL]
        k = z[:, D_MODEL:2 * D_MODEL]
        v = z[:, 2 * D_MODEL:]
        if rope:
            q = _rope(q, cos_ref[rows, :], sin_ref[rows, :])
            k = _rope(k, cos_ref[rows, :], sin_ref[rows, :])
        else:
            for b in range(SUB_TILE // seq):
                kf_ref[rows.start // seq + b] = k[b * seq:(b + 1) * seq].T
            for head in range(DIFF_HEADS):
                vf_ref[pl.ds(rows.start * DIFF_HEADS + head, SUB_TILE, stride=DIFF_HEADS), :] = (
                    v[:, head * DIFF_DV:(head + 1) * DIFF_DV])
        q_ref[rows, :] = (q * (DIFF_DH ** -0.5 * LOG2E)).astype(BF16)
        k_ref[rows, :] = k.astype(BF16)
        v_ref[rows, :] = v.astype(BF16)


def _qkv(x, mod, layer, rows_per_batch, seq, g, w, tables=None):
    rows = x.shape[0]
    rope = tables is not None
    in_specs = [_row_spec(D_MODEL), _resident((1, D_MODEL)), _mod_spec(layer, 0, rows_per_batch),
                _mod_spec(layer, 1, rows_per_batch), _resident((D_MODEL, 3 * D_MODEL))]
    args = [x, g, mod, mod, w]
    out_specs = [_row_spec(D_MODEL)] * 3
    out_shape = [jax.ShapeDtypeStruct((rows, D_MODEL), BF16)] * 3
    if rope:
        tiles = seq // ROW_TILE
        tab = pl.BlockSpec((ROW_TILE, D_MODEL), lambda i: (i % tiles, 0))
        in_specs += [tab, tab]
        args += list(tables)
    else:
        per_tile = ROW_TILE // seq
        out_specs += [pl.BlockSpec((per_tile, D_MODEL, seq), lambda i: (i, 0, 0)),
                      pl.BlockSpec((ROW_TILE * DIFF_HEADS, DIFF_DV), lambda i: (i, 0))]
        out_shape += [jax.ShapeDtypeStruct((rows // seq, D_MODEL, seq), F32),
                      jax.ShapeDtypeStruct((rows * DIFF_HEADS, DIFF_DV), F32)]
    return pl.pallas_call(
        functools.partial(_qkv_kernel, rope=rope, seq=seq),
        grid=(rows // ROW_TILE,),
        in_specs=in_specs,
        out_specs=out_specs,
        out_shape=out_shape,
        compiler_params=_params("arbitrary"),
        name="diff_qkv",
    )(*args)


def _attn_kernel(*refs, has_cache, lam_init, n_q_tiles, q_tile):
    if has_cache:
        (lq_ref, lk_ref, q_ref, k_ref, v_ref, gh_ref, kc_ref, vc_ref, y_ref, kcb_ref, vcb_ref) = refs
    else:
        (lq_ref, lk_ref, q_ref, k_ref, v_ref, gh_ref, y_ref) = refs
    e = jnp.exp(jnp.sum(lq_ref[...] * lk_ref[...], axis=1, keepdims=True))
    lam = e[0:1, :] - e[1:2, :] + lam_init
    first_map = lax.broadcasted_iota(jnp.int32, (q_tile, 2 * DIFF_DH), 1) < DIFF_DH
    gh = gh_ref[...] * (1.0 - lam_init)

    if has_cache:
        for head in range(DIFF_HEADS):
            kcb_ref[head] = kc_ref[head].astype(BF16)
            vcb_ref[head] = vc_ref[:, head, :].astype(BF16)

    def head_lanes(head):
        return slice(head * 2 * DIFF_DH, (head + 1) * 2 * DIFF_DH)

    def scores(head, sl):
        q = q_ref[sl, head_lanes(head)]
        zero = jnp.zeros_like(q)
        qq = jnp.concatenate([jnp.where(first_map, q, zero), jnp.where(first_map, zero, q)], axis=0)
        s = [_dot_nt(qq, k_ref[:, head_lanes(head)])]
        if has_cache:
            s.insert(0, _dot(qq, kcb_ref[head]))
        return s

    def attend(head, sl, s):
        vals = [v_ref[:, head_lanes(head)]]
        if has_cache:
            vals.insert(0, vcb_ref[head])
        m = functools.reduce(jnp.maximum, [jnp.max(x, axis=-1, keepdims=True) for x in s])
        p = [jnp.exp2(x - m) for x in s]
        denom = functools.reduce(jnp.add, [jnp.sum(x, axis=-1, keepdims=True) for x in p])
        acc = functools.reduce(jnp.add, [_dot(x.astype(BF16), vv) for x, vv in zip(p, vals)])
        o = acc[:q_tile] * (1.0 / denom[:q_tile]) - acc[q_tile:] * (lam / denom[q_tile:])
        y_ref[sl, head_lanes(head)] = _rms(o, gh).astype(BF16)

    def body(t, carry):
        sl = pl.ds(pl.multiple_of(t * q_tile, q_tile), q_tile)
        s_next = scores(0, sl)
        for head in range(DIFF_HEADS):
            s_cur = s_next
            if head + 1 < DIFF_HEADS:
                s_next = scores(head + 1, sl)
            attend(head, sl, s_cur)
        return carry

    lax.fori_loop(0, n_q_tiles, body, 0)


def _attn(q, k, v, lam_q, lam_k, g_head, lam_init, batch, seq, cache_k=None, cache_v=None):
    has_cache = cache_k is not None
    tok = pl.BlockSpec((seq, D_MODEL), lambda b: (b, 0))
    small = lambda shape: pl.BlockSpec(shape, lambda b: (0, 0))
    in_specs = [small((2, DIFF_DH)), small((2, DIFF_DH)), tok, tok, tok, small((1, DIFF_DV))]
    args = [lam_q, lam_k, q, k, v, g_head]
    if has_cache:
        past = cache_k.shape[-1]
        in_specs += [pl.BlockSpec((None, DIFF_HEADS, 2 * DIFF_DH, past), lambda b: (b, 0, 0, 0)),
                     pl.BlockSpec((None, past, DIFF_HEADS, DIFF_DV), lambda b: (b, 0, 0, 0))]
        args += [cache_k, cache_v]
        scratch = [pltpu.VMEM((DIFF_HEADS, 2 * DIFF_DH, past), BF16), pltpu.VMEM((DIFF_HEADS, past, DIFF_DV), BF16)]
    else:
        scratch = []
    q_tile = min(Q_TILE, seq)
    return pl.pallas_call(
        functools.partial(_attn_kernel, has_cache=has_cache, lam_init=lam_init, n_q_tiles=seq // q_tile, q_tile=q_tile),
        grid=(batch,),
        in_specs=in_specs,
        out_specs=tok,
        out_shape=jax.ShapeDtypeStruct((batch * seq, D_MODEL), BF16),
        scratch_shapes=scratch,
        compiler_params=_params("arbitrary"),
        name="diff_attn",
    )(*args)


def kernel(x_prompt, x_sample, state_gla_fwd, state_gla_bwd, cache_diff_k, cache_diff_v, c, c_ctx, w_ada, b_ada, g_mix_norm, g_ffn_norm, w_gla_in, w_gla_g1, w_gla_g2, b_gla_g, g_gla_head, w_gla_out, w_diff_qkv, lam_q, lam_k, g_diff_head, w_diff_out, w_ffn_gate, w_ffn_up, w_ffn_down, g_final):
    bp, tp, _ = x_prompt.shape
    bs, ts, _ = x_sample.shape
    assert SUB_TILE % tp == 0 and ts % ROW_TILE == 0 and (bp * tp) % ROW_TILE == 0
    assert tp % GLA_GROUP == 0 and ts % GLA_GROUP == 0
    xp = x_prompt.reshape(bp * tp, D_MODEL)
    xs = x_sample.reshape(bs * ts, D_MODEL)

    cond = jnp.concatenate([c_ctx[None, :], c, jnp.zeros((MOD_ROWS - 1 - bs, D_MODEL), F32)], axis=0)
    mod = _ada(cond, w_ada, b_ada).reshape(DEPTH, MOD_ROWS, N_MOD, 1, D_MODEL)
    wg, wu, wd = w_ffn_gate.astype(BF16), w_ffn_up.astype(BF16), w_ffn_down.astype(BF16)

    j = 0
    w_in = w_gla_in[j].astype(BF16)
    w1 = jnp.concatenate([w_gla_g1[j, 0], w_gla_g1[j, 1], jnp.zeros((D_MODEL, GATE_PAD - 2 * GLA_RANK), F32)],
                         axis=1).astype(BF16)
    w2 = jnp.zeros((GATE_PAD, 2 * GLA_HK), F32)
    w2 = w2.at[:GLA_RANK, :GLA_HK].set(w_gla_g2[j, 0]).at[GLA_RANK:2 * GLA_RANK, GLA_HK:].set(w_gla_g2[j, 1]).astype(BF16)
    bg = b_gla_g[j].reshape(1, 2 * GLA_HK)
    g_mix = g_mix_norm[0].reshape(1, D_MODEL)
    g_head = g_gla_head[j].reshape(1, GLA_DV)
    g_ffn = g_ffn_norm[0].reshape(1, D_MODEL)
    w_out = w_gla_out[j].astype(BF16)

    qp, kp, vp, rp, lap = _gla_in(xp, mod, 0, None, g_mix, w_in, w1, w2, bg)
    qs, ks, vs, rs, las = _gla_in(xs, mod, 0, ts, g_mix, w_in, w1, w2, bg)
    yp, new_f, new_b = _gla_scan(qp, kp, vp, lap, rp, g_head, bp, tp)
    ys = _gla_scan(qs, ks, vs, las, rs, g_head, bs, ts, state_gla_fwd[:, j], state_gla_bwd[:, j])
    xp = _out_ffn(xp, yp, mod, 0, None, w_out, g_ffn, wg, wu, wd)
    xs = _out_ffn(xs, ys, mod, 0, ts, w_out, g_ffn, wg, wu, wd)

    lam_init = 0.8 - 0.6 * math.exp(-0.3 * 1)
    g_mix = g_mix_norm[1].reshape(1, D_MODEL)
    w_qkv = w_diff_qkv[j].astype(BF16)
    g_head = g_diff_head[j].reshape(1, DIFF_DV)
    g_ffn = g_ffn_norm[1].reshape(1, D_MODEL)
    w_out = w_diff_out[j].astype(BF16)

    qp, kp, vp, kp32, vp32 = _qkv(xp, mod, 1, None, tp, g_mix, w_qkv)
    qs, ks, vs = _qkv(xs, mod, 1, ts, ts, g_mix, w_qkv, _rope_tables(ts))
    yp = _attn(qp, kp, vp, lam_q[j], lam_k[j], g_head, lam_init, bp, tp)
    past = cache_diff_k.shape[2]
    cache_k_t = jnp.transpose(cache_diff_k[:, j], (0, 2, 3, 4, 1)).reshape(bs, DIFF_HEADS, 2 * DIFF_DH, past)
    ys = _attn(qs, ks, vs, lam_q[j], lam_k[j], g_head, lam_init, bs, ts, cache_k_t, cache_diff_v[:, j])
    g_fin = g_final.reshape(1, D_MODEL)
    yp_out = _out_ffn(xp, yp, mod, 1, None, w_out, g_ffn, wg, wu, wd, g_final=g_fin)
    ys_out = _out_ffn(xs, ys, mod, 1, ts, w_out, g_ffn, wg, wu, wd, g_final=g_fin)

    return (yp_out.reshape(bp, tp, D_MODEL),
            ys_out.reshape(bs, ts, D_MODEL),
            new_f[:, None],
            new_b[:, None],
            jnp.transpose(kp32.reshape(bp, 1, DIFF_HEADS, 2, DIFF_DH, tp), (0, 1, 5, 2, 3, 4)),
            vp32.reshape(bp, 1, tp, DIFF_HEADS, DIFF_DV))
```

```python
import functools
import math

import jax
import jax.numpy as jnp
from jax import lax
from jax.experimental import pallas as pl
from jax.experimental.pallas import tpu as pltpu

F32 = jnp.float32
BF16 = jnp.bfloat16

D_MODEL = 1024
DEPTH = 2
GRID_W = 64
GLA_HEADS = 4
GLA_DK = 128
GLA_DV = 256
GLA_HK = GLA_HEADS * GLA_DK
GLA_HV = GLA_HEADS * GLA_DV
GLA_RANK = 16
GLA_TAU = 16.0
GLA_CHUNK = 64
GLA_GROUP = 256
GLA_HEADS_PER_STEP = 2
DIFF_HEADS = 8
DIFF_DH = 64
DIFF_DV = 128
ROPE_THETA = 10000.0
D_FF = 2816
EPS = 1e-6
LOG2E = math.log2(math.e)
N_MOD = 6
MOD_ROWS = 16
GATE_PAD = 128

VMEM_LIMIT_BYTES = 56 * 1024 * 1024
ROW_TILE = 512
SUB_TILE = 256


def _params(*sem):
    return pltpu.CompilerParams(dimension_semantics=sem, vmem_limit_bytes=VMEM_LIMIT_BYTES)


def _resident(shape, index=None):
    index = (0,) * len(shape) if index is None else index
    return pl.BlockSpec(shape, lambda *_: index, pipeline_mode=pl.Buffered(1))


def _sub_tiles():
    return [slice(s * SUB_TILE, (s + 1) * SUB_TILE) for s in range(ROW_TILE // SUB_TILE)]


def _sigmoid(x):
    return 1.0 / (1.0 + jnp.exp(-x))


def _silu(x):
    return x * _sigmoid(x)


def _rms(x, g):
    return x * lax.rsqrt(jnp.mean(x * x, axis=-1, keepdims=True) + EPS) * g


def _norm_mod(x, g, shift, scale):
    return _rms(x, g) * (1.0 + scale) + shift


def _dot(a, b):
    return jnp.dot(a, b, preferred_element_type=F32)


def _dot_nt(a, b):
    return lax.dot_general(a, b, (((1,), (1,)), ((), ())), preferred_element_type=F32)


def _dot_tn(a, b):
    return lax.dot_general(a, b, (((0,), (0,)), ((), ())), preferred_element_type=F32)


def _ada_kernel(c_ref, w_ref, b_ref, o_ref):
    a = _silu(c_ref[...]).astype(BF16)
    o_ref[...] = _dot(a, w_ref[...].astype(BF16)) + b_ref[...]


def _ada(cond, w_ada, b_ada):
    tn = 1536
    n = N_MOD * D_MODEL
    return pl.pallas_call(
        _ada_kernel,
        grid=(DEPTH, n // tn),
        in_specs=[
            pl.BlockSpec((MOD_ROWS, D_MODEL), lambda l, j: (0, 0)),
            pl.BlockSpec((None, D_MODEL, tn), lambda l, j: (l, 0, j)),
            pl.BlockSpec((None, 1, tn), lambda l, j: (l, 0, j)),
        ],
        out_specs=pl.BlockSpec((None, MOD_ROWS, tn), lambda l, j: (l, 0, j)),
        out_shape=jax.ShapeDtypeStruct((DEPTH, MOD_ROWS, n), F32),
        compiler_params=_params("arbitrary", "arbitrary"),
        name="ada",
    )(cond, w_ada, b_ada.reshape(DEPTH, 1, n))


def _mod_spec(layer, which, rows_per_batch):
    if rows_per_batch is None:
        return pl.BlockSpec((None, None, None, 1, D_MODEL), lambda i: (layer, 0, which, 0, 0))
    tiles = rows_per_batch // ROW_TILE
    return pl.BlockSpec((None, None, None, 1, D_MODEL), lambda i: (layer, 1 + i // tiles, which, 0, 0))


def _row_spec(width):
    return pl.BlockSpec((ROW_TILE, width), lambda i: (i, 0))


def _log_sigmoid(x):
    return jnp.minimum(x, 0.0) - jnp.log(1.0 + jnp.exp(-jnp.abs(x)))


def _gla_in_kernel(x_ref, g_ref, sh_ref, sc_ref, w_ref, w1_ref, w2_ref, bg_ref, q_ref, k_ref, v_ref, r_ref, la_ref):
    for rows in _sub_tiles():
        h = _norm_mod(x_ref[rows, :], g_ref[...], sh_ref[...], sc_ref[...]).astype(BF16)
        z = _dot(h, w_ref[...])
        q_ref[rows, :] = z[:, :GLA_HK]
        k_ref[rows, :] = z[:, GLA_HK:2 * GLA_HK]
        v_ref[rows, :] = z[:, 2 * GLA_HK:2 * GLA_HK + GLA_HV].astype(BF16)
        r_ref[rows, :] = z[:, 2 * GLA_HK + GLA_HV:]
        low_rank = _dot(h, w1_ref[...]).astype(BF16)
        pre = _dot(low_rank, w2_ref[...]) + bg_ref[...]
        la_ref[rows, :] = _log_sigmoid(pre) * (1.0 / GLA_TAU)


def _gla_in(x, mod, layer, rows_per_batch, g, w_in, w1, w2, bg):
    rows = x.shape[0]
    return pl.pallas_call(
        _gla_in_kernel,
        grid=(rows // ROW_TILE,),
        in_specs=[
            _row_spec(D_MODEL),
            _resident((1, D_MODEL)),
            _mod_spec(layer, 0, rows_per_batch),
            _mod_spec(layer, 1, rows_per_batch),
            _resident((D_MODEL, 2 * GLA_HK + 2 * GLA_HV)),
            _resident((D_MODEL, GATE_PAD)),
            _resident((GATE_PAD, 2 * GLA_HK)),
            _resident((1, 2 * GLA_HK)),
        ],
        out_specs=[_row_spec(GLA_HK), _row_spec(GLA_HK), _row_spec(GLA_HV), _row_spec(GLA_HV), _row_spec(2 * GLA_HK)],
        out_shape=[
            jax.ShapeDtypeStruct((rows, GLA_HK), F32),
            jax.ShapeDtypeStruct((rows, GLA_HK), F32),
            jax.ShapeDtypeStruct((rows, GLA_HV), BF16),
            jax.ShapeDtypeStruct((rows, GLA_HV), F32),
            jax.ShapeDtypeStruct((rows, 2 * GLA_HK), F32),
        ],
        compiler_params=_params("arbitrary"),
        name="gla_in",
    )(x, g, mod, mod, w_in, w1, w2, bg)


def _split3(x):
    hi = x.astype(BF16)
    rem = x - hi.astype(F32)
    mid = rem.astype(BF16)
    lo = (rem - mid.astype(F32)).astype(BF16)
    return jnp.concatenate([hi, mid, lo], axis=1)


def _fold3(x):
    return x[:, :GLA_DK] + x[:, GLA_DK:2 * GLA_DK] + x[:, 2 * GLA_DK:]


def _gla_scan_kernel(*refs, n_groups, has_s0):
    C, DK, DV, G, HPS = GLA_CHUNK, GLA_DK, GLA_DV, GLA_GROUP, GLA_HEADS_PER_STEP
    cpg = G // C
    n_chunks = n_groups * cpg
    if has_s0:
        (q_ref, k_ref, v_ref, laf_ref, lab_ref, r_ref, gh_ref, s0f_ref, s0b_ref,
         y_ref, qd_ref, kd_ref, ds_ref, sin_ref, dec_ref) = refs
    else:
        (q_ref, k_ref, v_ref, laf_ref, lab_ref, r_ref, gh_ref,
         y_ref, sf_ref, sb_ref, qd_ref, kd_ref, ds_ref, sin_ref, dec_ref) = refs

    row = lax.broadcasted_iota(jnp.int32, (G, G), 0)
    col = lax.broadcasted_iota(jnp.int32, (G, G), 1)
    same_chunk = (row // C) == (col // C)
    lower = same_chunk & (row >= col)
    upper = same_chunk & (row <= col)
    prefix_sum = jnp.where(lower, 1.0, 0.0).astype(BF16)
    suffix_sum = jnp.where(upper, 1.0, 0.0).astype(BF16)

    def group(g):
        return pl.ds(pl.multiple_of(g * G, G), G)

    def spread(x, first_row):
        return jnp.concatenate(
            [jnp.broadcast_to(x[first_row + c * C:first_row + c * C + 1], (C, x.shape[1])) for c in range(cpg)], axis=0)

    def pass1(g, carry):
        sl = group(g)
        for hd in range(HPS):
            dk = slice(hd * DK, (hd + 1) * DK)
            dv = slice(hd * DV, (hd + 1) * DV)
            both = slice(hd * 2 * DK, (hd + 1) * 2 * DK)
            bf = _fold3(_dot(prefix_sum, _split3(laf_ref[sl, dk])))
            bb = _fold3(_dot(suffix_sum, _split3(lab_ref[sl, dk])))
            tot_f = spread(bf, C - 1)
            tot_b = spread(bb, 0)
            q = q_ref[sl, dk] * (DK ** -0.5)
            k = k_ref[sl, dk]
            qd_ref[sl, both] = jnp.concatenate([q * jnp.exp(bf), q * jnp.exp(bb)], axis=1).astype(BF16)
            kd_ref[sl, both] = jnp.concatenate([k * jnp.exp(-bf), k * jnp.exp(-bb)], axis=1).astype(BF16)
            k_end = jnp.concatenate([k * jnp.exp(tot_f - bf), k * jnp.exp(tot_b - bb)], axis=1).astype(BF16)
            v = v_ref[sl, dv]
            for c in range(cpg):
                rows = slice(c * C, (c + 1) * C)
                ds_ref[hd, g * cpg + c] = _dot_tn(v[rows], k_end[rows])
                dec_ref[hd, g * cpg + c] = jnp.exp(
                    jnp.concatenate([tot_f[c * C:c * C + 1], tot_b[c * C:c * C + 1]], axis=1))
        return carry

    lax.fori_loop(0, n_groups, pass1, 0, unroll=min(2, n_groups))

    for hd in range(HPS):
        for forward in (True, False):
            lanes = slice(0, DK) if forward else slice(DK, 2 * DK)
            if has_s0:
                init = (s0f_ref if forward else s0b_ref)[hd].T
            else:
                init = jnp.zeros((DV, DK), F32)

            def step(i, s, hd=hd, forward=forward, lanes=lanes):
                n = i if forward else n_chunks - 1 - i
                sin_ref[hd, n, :, lanes] = s.astype(BF16)
                return s * dec_ref[hd, n, :, lanes] + ds_ref[hd, n, :, lanes]

            final = lax.fori_loop(0, n_chunks, step, init)
            if not has_s0:
                (sf_ref if forward else sb_ref)[hd] = final.T

    gh = gh_ref[...]

    def pass3(g, carry):
        sl = group(g)
        for hd in range(HPS):
            dv = slice(hd * DV, (hd + 1) * DV)
            both = slice(hd * 2 * DK, (hd + 1) * 2 * DK)
            qd = qd_ref[sl, both]
            kd = kd_ref[sl, both]
            a_f = _dot_nt(qd[:, :DK], kd[:, :DK])
            a_b = _dot_nt(qd[:, DK:], kd[:, DK:])
            a = jnp.where(lower, a_f, 0.0) + jnp.where(upper, a_b, 0.0)
            inter = [_dot_nt(qd[c * C:(c + 1) * C], sin_ref[hd, g * cpg + c]) for c in range(cpg)]
            o = _dot(a.astype(BF16), v_ref[sl, dv]) + jnp.concatenate(inter, axis=0)
            y_ref[sl, dv] = (_rms(o, gh) * _silu(r_ref[sl, dv])).astype(BF16)
        return carry

    lax.fori_loop(0, n_groups, pass3, 0, unroll=min(2, n_groups))


def _gla_scan(q, k, v, la, r, g_head, batch, seq, s0f=None, s0b=None):
    hps = GLA_HEADS_PER_STEP
    steps = GLA_HEADS // hps
    n_chunks = seq // GLA_CHUNK
    has_s0 = s0f is not None
    rows = batch * seq
    tok = lambda w, off: pl.BlockSpec((seq, w * hps), lambda b, h: (b, h + off))
    state = pl.BlockSpec((None, hps, GLA_DK, GLA_DV), lambda b, h: (b, h, 0, 0))
    in_specs = [tok(GLA_DK, 0), tok(GLA_DK, 0), tok(GLA_DV, 0), tok(GLA_DK, 0), tok(GLA_DK, steps),
                tok(GLA_DV, 0), pl.BlockSpec((1, GLA_DV), lambda b, h: (0, 0))]
    args = [q, k, v, la, la, r, g_head]
    y_shape = jax.ShapeDtypeStruct((rows, GLA_HV), BF16)
    st_shape = jax.ShapeDtypeStruct((batch, GLA_HEADS, GLA_DK, GLA_DV), F32)
    scratch = [
        pltpu.VMEM((seq, hps * 2 * GLA_DK), BF16),
        pltpu.VMEM((seq, hps * 2 * GLA_DK), BF16),
        pltpu.VMEM((hps, n_chunks, GLA_DV, 2 * GLA_DK), F32),
        pltpu.VMEM((hps, n_chunks, GLA_DV, 2 * GLA_DK), BF16),
        pltpu.VMEM((hps, n_chunks, 1, 2 * GLA_DK), F32),
    ]
    if has_s0:
        in_specs += [state, state]
        args += [s0f, s0b]
        out_specs = tok(GLA_DV, 0)
        out_shape = y_shape
    else:
        out_specs = [tok(GLA_DV, 0), state, state]
        out_shape = [y_shape, st_shape, st_shape]
    return pl.pallas_call(
        functools.partial(_gla_scan_kernel, n_groups=seq // GLA_GROUP, has_s0=has_s0),
        grid=(batch, steps),
        in_specs=in_specs,
        out_specs=out_specs,
        out_shape=out_shape,
        scratch_shapes=scratch,
        compiler_params=_params("arbitrary", "arbitrary"),
        name="gla_scan",
    )(*args)


def _out_ffn_kernel(*refs, final_norm):
    if final_norm:
        (x_ref, y_ref, wo_ref, g1_ref, gn_ref, sh_ref, sc_ref, g2_ref, wg_ref, wu_ref, wd_ref, gf_ref, o_ref) = refs
    else:
        (x_ref, y_ref, wo_ref, g1_ref, gn_ref, sh_ref, sc_ref, g2_ref, wg_ref, wu_ref, wd_ref, o_ref) = refs
    for rows in _sub_tiles():
        x = x_ref[rows, :] + g1_ref[...] * _dot(y_ref[rows, :], wo_ref[...])
        h = _norm_mod(x, gn_ref[...], sh_ref[...], sc_ref[...]).astype(BF16)
        act = (_silu(_dot(h, wg_ref[...])) * _dot(h, wu_ref[...])).astype(BF16)
        x = x + g2_ref[...] * _dot(act, wd_ref[...])
        if final_norm:
            x = _rms(x, gf_ref[...])
        o_ref[rows, :] = x


def _out_ffn(x, y, mod, layer, rows_per_batch, w_out, g_ffn, wg, wu, wd, g_final=None):
    rows = x.shape[0]
    final_norm = g_final is not None
    in_specs = [
        _row_spec(D_MODEL),
        _row_spec(D_MODEL),
        _resident((D_MODEL, D_MODEL)),
        _mod_spec(layer, 2, rows_per_batch),
        _resident((1, D_MODEL)),
        _mod_spec(layer, 3, rows_per_batch),
        _mod_spec(layer, 4, rows_per_batch),
        _mod_spec(layer, 5, rows_per_batch),
        _resident((None, D_MODEL, D_FF), (layer, 0, 0)),
        _resident((None, D_MODEL, D_FF), (layer, 0, 0)),
        _resident((None, D_FF, D_MODEL), (layer, 0, 0)),
    ]
    args = [x, y, w_out, mod, g_ffn, mod, mod, mod, wg, wu, wd]
    if final_norm:
        in_specs.append(_resident((1, D_MODEL)))
        args.append(g_final)
    return pl.pallas_call(
        functools.partial(_out_ffn_kernel, final_norm=final_norm),
        grid=(rows // ROW_TILE,),
        in_specs=in_specs,
        out_specs=_row_spec(D_MODEL),
        out_shape=jax.ShapeDtypeStruct((rows, D_MODEL), F32),
        compiler_params=_params("arbitrary"),
        name="out_ffn",
    )(*args)


def _rope_tables(n_tokens):
    rows = n_tokens // GRID_W
    r, col = jnp.meshgrid(jnp.arange(rows), jnp.arange(GRID_W), indexing="ij")
    r = r.reshape(-1).astype(F32)
    col = col.reshape(-1).astype(F32)
    n_freq = DIFF_DH // 4
    inv = ROPE_THETA ** (-jnp.arange(n_freq, dtype=F32) / n_freq)
    ang = jnp.concatenate([r[:, None] * inv, col[:, None] * inv], axis=-1)
    cos, sin = jnp.cos(ang), jnp.sin(ang)
    reps = D_MODEL // DIFF_DH
    cos = jnp.tile(jnp.concatenate([cos, cos], axis=-1), (1, reps))
    sin = jnp.tile(jnp.concatenate([-sin, sin], axis=-1), (1, reps))
    return cos, sin, cos.T, sin.T


def _rope(x, cos, sin_signed, axis):
    half = DIFF_DH // 2
    size = x.shape[axis]
    pos = lax.broadcasted_iota(jnp.int32, x.shape, axis)
    from_above = pltpu.roll(x, size - half, axis=axis)
    from_below = pltpu.roll(x, half, axis=axis)
    swapped = jnp.where(pos % DIFF_DH < half, from_above, from_below)
    return x * cos + swapped * sin_signed


def _qkv_kernel(*refs, rope, seq):
    if rope:
        (x_ref, g_ref, sh_ref, sc_ref, wq_ref, wk_ref, wv_ref, cos_ref, sin_ref, cos_t_ref, sin_t_ref,
         q_ref, k_ref, v_ref) = refs
    else:
        (x_ref, g_ref, sh_ref, sc_ref, wq_ref, wk_ref, wv_ref, q_ref, k_ref, v_ref, kf_ref, vf_ref) = refs
    for s, rows in enumerate(_sub_tiles()):
        h32 = _norm_mod(x_ref[rows, :], g_ref[...], sh_ref[...], sc_ref[...])
        h = h32.astype(BF16)
        h_t = h32.T.astype(BF16)
        q_t = _dot(wq_ref[...], h_t)
        v_t = _dot(wv_ref[...], h_t)
        k = _dot(h, wk_ref[...])
        if rope:
            q_t = _rope(q_t, cos_t_ref[:, rows], sin_t_ref[:, rows], 0)
            k = _rope(k, cos_ref[rows, :], sin_ref[rows, :], 1)
        else:
            for b in range(SUB_TILE // seq):
                kf_ref[rows.start // seq + b] = k[b * seq:(b + 1) * seq].T
            v = v_t.T
            for head in range(DIFF_HEADS):
                vf_ref[pl.ds(rows.start * DIFF_HEADS + head, SUB_TILE, stride=DIFF_HEADS), :] = (
                    v[:, head * DIFF_DV:(head + 1) * DIFF_DV])
        q_ref[s] = (q_t * (DIFF_DH ** -0.5 * LOG2E)).astype(BF16)
        k_ref[rows, :] = k.astype(BF16)
        if seq >= ROW_TILE:
            v_ref[:, rows] = v_t.astype(BF16)
        else:
            for b in range(SUB_TILE // seq):
                v_ref[rows.start // seq + b] = v_t[:, b * seq:(b + 1) * seq].astype(BF16)


def _qkv(x, mod, layer, rows_per_batch, seq, g, wq_t, wk, wv_t, tables=None):
    rows = x.shape[0]
    rope = tables is not None
    per_tile = ROW_TILE // SUB_TILE
    tiles = max(seq // ROW_TILE, 1)
    in_specs = [_row_spec(D_MODEL), _resident((1, D_MODEL)), _mod_spec(layer, 0, rows_per_batch),
                _mod_spec(layer, 1, rows_per_batch), _resident((D_MODEL, D_MODEL)), _resident((D_MODEL, D_MODEL)),
                _resident((D_MODEL, D_MODEL))]
    args = [x, g, mod, mod, wq_t, wk, wv_t]
    if seq >= ROW_TILE:
        v_spec = pl.BlockSpec((None, D_MODEL, ROW_TILE), lambda i: (i // tiles, 0, i % tiles))
    else:
        v_spec = pl.BlockSpec((ROW_TILE // seq, D_MODEL, seq), lambda i: (i, 0, 0))
    out_specs = [pl.BlockSpec((per_tile, D_MODEL, SUB_TILE), lambda i: (i, 0, 0)), _row_spec(D_MODEL), v_spec]
    out_shape = [jax.ShapeDtypeStruct((rows // SUB_TILE, D_MODEL, SUB_TILE), BF16),
                 jax.ShapeDtypeStruct((rows, D_MODEL), BF16),
                 jax.ShapeDtypeStruct((rows // seq, D_MODEL, seq), BF16)]
    if rope:
        tab = pl.BlockSpec((ROW_TILE, D_MODEL), lambda i: (i % tiles, 0))
        tab_t = pl.BlockSpec((D_MODEL, ROW_TILE), lambda i: (0, i % tiles))
        in_specs += [tab, tab, tab_t, tab_t]
        args += list(tables)
    else:
        out_specs += [pl.BlockSpec((ROW_TILE // seq, D_MODEL, seq), lambda i: (i, 0, 0)),
                      pl.BlockSpec((ROW_TILE * DIFF_HEADS, DIFF_DV), lambda i: (i, 0))]
        out_shape += [jax.ShapeDtypeStruct((rows // seq, D_MODEL, seq), F32),
                      jax.ShapeDtypeStruct((rows * DIFF_HEADS, DIFF_DV), F32)]
    return pl.pallas_call(
        functools.partial(_qkv_kernel, rope=rope, seq=seq),
        grid=(rows // ROW_TILE,),
        in_specs=in_specs,
        out_specs=out_specs,
        out_shape=out_shape,
        compiler_params=_params("arbitrary"),
        name="diff_qkv",
    )(*args)


def _attn_kernel(*refs, has_cache, lam_init, n_q_tiles, q_tile):
    if has_cache:
        (lq_ref, lk_ref, q_ref, k_ref, v_ref, gh_ref, kc_ref, vc_ref, y_ref, kcb_ref, vcb_ref) = refs
    else:
        (lq_ref, lk_ref, q_ref, k_ref, v_ref, gh_ref, y_ref) = refs
    e = jnp.exp(jnp.sum(lq_ref[...] * lk_ref[...], axis=1, keepdims=True))
    lam = e[0:1, :] - e[1:2, :] + lam_init
    first_map = lax.broadcasted_iota(jnp.int32, (2 * DIFF_DH, q_tile), 0) < DIFF_DH
    gh = gh_ref[...] * (1.0 - lam_init)

    def features(head):
        return slice(head * 2 * DIFF_DH, (head + 1) * 2 * DIFF_DH)

    if has_cache:
        for head in range(DIFF_HEADS):
            kcb_ref[head] = kc_ref[head].T.astype(BF16)
            vcb_ref[head] = vc_ref[:, features(head)].astype(F32).T.astype(BF16)

    def scores(head, t):
        q_t = q_ref[t, features(head), :]
        zero = jnp.zeros_like(q_t)
        qq = jnp.concatenate([jnp.where(first_map, q_t, zero), jnp.where(first_map, zero, q_t)], axis=1)
        s = [_dot(k_ref[:, features(head)], qq)]
        if has_cache:
            s.insert(0, _dot(kcb_ref[head], qq))
        return s

    def attend(head, t, s):
        vals = [v_ref[features(head), :]]
        if has_cache:
            vals.insert(0, vcb_ref[head])
        m = functools.reduce(jnp.maximum, [jnp.max(x, axis=0, keepdims=True) for x in s])
        p = [jnp.exp2(x - m) for x in s]
        denom = functools.reduce(jnp.add, [jnp.sum(x, axis=0, keepdims=True) for x in p])
        acc = functools.reduce(jnp.add, [_dot(vv, x.astype(BF16)) for x, vv in zip(p, vals)])
        o = acc[:, :q_tile] * (1.0 / denom[:, :q_tile]) - acc[:, q_tile:] * (lam / denom[:, q_tile:])
        o = o * lax.rsqrt(jnp.mean(o * o, axis=0, keepdims=True) + EPS) * gh
        rows = pl.ds(pl.multiple_of(t * q_tile, q_tile), q_tile)
        y_ref[rows, features(head)] = o.T.astype(BF16)

    def body(t, carry):
        s_next = scores(0, t)
        for head in range(DIFF_HEADS):
            s_cur = s_next
            if head + 1 < DIFF_HEADS:
                s_next = scores(head + 1, t)
            attend(head, t, s_cur)
        return carry

    lax.fori_loop(0, n_q_tiles, body, 0)


def _attn(q, k, v, lam_q, lam_k, g_head, lam_init, batch, seq, cache_k=None, cache_v=None):
    has_cache = cache_k is not None
    q_tile = SUB_TILE
    n_q_tiles = seq // q_tile
    tok = pl.BlockSpec((seq, D_MODEL), lambda b: (b, 0))
    small = lambda shape: pl.BlockSpec(shape, lambda b: (0, 0))
    in_specs = [small((2, DIFF_DH)), small((2, DIFF_DH)),
                pl.BlockSpec((n_q_tiles, D_MODEL, q_tile), lambda b: (b, 0, 0)),
                tok,
                pl.BlockSpec((None, D_MODEL, seq), lambda b: (b, 0, 0)),
                small((DIFF_DV, 1))]
    args = [lam_q, lam_k, q, k, v, g_head]
    if has_cache:
        past = cache_k.shape[-1]
        in_specs += [pl.BlockSpec((None, DIFF_HEADS, 2 * DIFF_DH, past), lambda b: (b, 0, 0, 0)),
                     pl.BlockSpec((None, past, D_MODEL), lambda b: (b, 0, 0))]
        args += [cache_k, cache_v]
        scratch = [pltpu.VMEM((DIFF_HEADS, past, 2 * DIFF_DH), BF16), pltpu.VMEM((DIFF_HEADS, DIFF_DV, past), BF16)]
    else:
        scratch = []
    return pl.pallas_call(
        functools.partial(_attn_kernel, has_cache=has_cache, lam_init=lam_init, n_q_tiles=n_q_tiles, q_tile=q_tile),
        grid=(batch,),
        in_specs=in_specs,
        out_specs=tok,
        out_shape=jax.ShapeDtypeStruct((batch * seq, D_MODEL), BF16),
        scratch_shapes=scratch,
        compiler_params=_params("arbitrary"),
        name="diff_attn",
    )(*args)


def kernel(x_prompt, x_sample, state_gla_fwd, state_gla_bwd, cache_diff_k, cache_diff_v, c, c_ctx, w_ada, b_ada, g_mix_norm, g_ffn_norm, w_gla_in, w_gla_g1, w_gla_g2, b_gla_g, g_gla_head, w_gla_out, w_diff_qkv, lam_q, lam_k, g_diff_head, w_diff_out, w_ffn_gate, w_ffn_up, w_ffn_down, g_final):
    bp, tp, _ = x_prompt.shape
    bs, ts, _ = x_sample.shape
    assert SUB_TILE % tp == 0 and ts % ROW_TILE == 0 and (bp * tp) % ROW_TILE == 0
    assert tp % GLA_GROUP == 0 and ts % GLA_GROUP == 0
    xp = x_prompt.reshape(bp * tp, D_MODEL)
    xs = x_sample.reshape(bs * ts, D_MODEL)

    cond = jnp.concatenate([c_ctx[None, :], c, jnp.zeros((MOD_ROWS - 1 - bs, D_MODEL), F32)], axis=0)
    mod = _ada(cond, w_ada, b_ada).reshape(DEPTH, MOD_ROWS, N_MOD, 1, D_MODEL)
    wg, wu, wd = w_ffn_gate.astype(BF16), w_ffn_up.astype(BF16), w_ffn_down.astype(BF16)

    j = 0
    w_in = w_gla_in[j].astype(BF16)
    w1 = jnp.concatenate([w_gla_g1[j, 0], w_gla_g1[j, 1], jnp.zeros((D_MODEL, GATE_PAD - 2 * GLA_RANK), F32)],
                         axis=1).astype(BF16)
    w2 = jnp.zeros((GATE_PAD, 2 * GLA_HK), F32)
    w2 = w2.at[:GLA_RANK, :GLA_HK].set(w_gla_g2[j, 0]).at[GLA_RANK:2 * GLA_RANK, GLA_HK:].set(w_gla_g2[j, 1]).astype(BF16)
    bg = b_gla_g[j].reshape(1, 2 * GLA_HK)
    g_mix = g_mix_norm[0].reshape(1, D_MODEL)
    g_head = g_gla_head[j].reshape(1, GLA_DV)
    g_ffn = g_ffn_norm[0].reshape(1, D_MODEL)
    w_out = w_gla_out[j].astype(BF16)

    qp, kp, vp, rp, lap = _gla_in(xp, mod, 0, None, g_mix, w_in, w1, w2, bg)
    qs, ks, vs, rs, las = _gla_in(xs, mod, 0, ts, g_mix, w_in, w1, w2, bg)
    yp, new_f, new_b = _gla_scan(qp, kp, vp, lap, rp, g_head, bp, tp)
    ys = _gla_scan(qs, ks, vs, las, rs, g_head, bs, ts, state_gla_fwd[:, j], state_gla_bwd[:, j])
    xp = _out_ffn(xp, yp, mod, 0, None, w_out, g_ffn, wg, wu, wd)
    xs = _out_ffn(xs, ys, mod, 0, ts, w_out, g_ffn, wg, wu, wd)

    lam_init = 0.8 - 0.6 * math.exp(-0.3 * 1)
    g_mix = g_mix_norm[1].reshape(1, D_MODEL)
    w_qkv = w_diff_qkv[j].astype(BF16)
    wq_t, wk, wv_t = w_qkv[:, :D_MODEL].T, w_qkv[:, D_MODEL:2 * D_MODEL], w_qkv[:, 2 * D_MODEL:].T
    g_head = g_diff_head[j].reshape(DIFF_DV, 1)
    g_ffn = g_ffn_norm[1].reshape(1, D_MODEL)
    w_out = w_diff_out[j].astype(BF16)

    qp, kp, vp, kp32, vp32 = _qkv(xp, mod, 1, None, tp, g_mix, wq_t, wk, wv_t)
    qs, ks, vs = _qkv(xs, mod, 1, ts, ts, g_mix, wq_t, wk, wv_t, _rope_tables(ts))
    yp = _attn(qp, kp, vp, lam_q[j], lam_k[j], g_head, lam_init, bp, tp)
    past = cache_diff_k.shape[2]
    cache_k_t = jnp.transpose(cache_diff_k[:, j], (0, 2, 3, 4, 1)).reshape(bs, DIFF_HEADS, 2 * DIFF_DH, past)
    cache_v = cache_diff_v[:, j].reshape(bs, past, D_MODEL).astype(BF16)
    ys = _attn(qs, ks, vs, lam_q[j], lam_k[j], g_head, lam_init, bs, ts, cache_k_t, cache_v)
    g_fin = g_final.reshape(1, D_MODEL)
    yp_out = _out_ffn(xp, yp, mod, 1, None, w_out, g_ffn, wg, wu, wd, g_final=g_fin)
    ys_out = _out_ffn(xs, ys, mod, 1, ts, w_out, g_ffn, wg, wu, wd, g_final=g_fin)

    return (yp_out.reshape(bp, tp, D_MODEL),
            ys_out.reshape(bs, ts, D_MODEL),
            new_f[:, None],
            new_b[:, None],
            jnp.transpose(kp32.reshape(bp, 1, DIFF_HEADS, 2, DIFF_DH, tp), (0, 1, 5, 2, 3, 4)),
            vp32.reshape(bp, 1, tp, DIFF_HEADS, DIFF_DV))
```

```python
import functools
import math

import jax
import jax.numpy as jnp
from jax import lax
from jax.experimental import pallas as pl
from jax.experimental.pallas import tpu as pltpu

F32 = jnp.float32
BF16 = jnp.bfloat16

D_MODEL = 1024
DEPTH = 2
GRID_W = 64
GLA_HEADS = 4
GLA_DK = 128
GLA_DV = 256
GLA_HK = GLA_HEADS * GLA_DK
GLA_HV = GLA_HEADS * GLA_DV
GLA_RANK = 16
GLA_TAU = 16.0
GLA_CHUNK = 64
GLA_GROUP = 256
GLA_HEADS_PER_STEP = 2
DIFF_HEADS = 8
DIFF_DH = 64
DIFF_DV = 128
ROPE_THETA = 10000.0
D_FF = 2816
EPS = 1e-6
LOG2E = math.log2(math.e)
N_MOD = 6
MOD_ROWS = 16
GATE_PAD = 128

VMEM_LIMIT_BYTES = 56 * 1024 * 1024
ROW_TILE = 512
SUB_TILE = 256
KEY_BLOCK = 256


def _params(*sem):
    return pltpu.CompilerParams(dimension_semantics=sem, vmem_limit_bytes=VMEM_LIMIT_BYTES)


def _resident(shape, index=None):
    index = (0,) * len(shape) if index is None else index
    return pl.BlockSpec(shape, lambda *_: index, pipeline_mode=pl.Buffered(1))


def _sub_tiles():
    return [slice(s * SUB_TILE, (s + 1) * SUB_TILE) for s in range(ROW_TILE // SUB_TILE)]


def _sigmoid(x):
    return 1.0 / (1.0 + jnp.exp(-x))


def _silu(x):
    return x * _sigmoid(x)


def _rms(x, g):
    return x * lax.rsqrt(jnp.mean(x * x, axis=-1, keepdims=True) + EPS) * g


def _norm_mod(x, g, shift, scale):
    return _rms(x, g) * (1.0 + scale) + shift


def _dot(a, b):
    return jnp.dot(a, b, preferred_element_type=F32)


def _dot_nt(a, b):
    return lax.dot_general(a, b, (((1,), (1,)), ((), ())), preferred_element_type=F32)


def _dot_tn(a, b):
    return lax.dot_general(a, b, (((0,), (0,)), ((), ())), preferred_element_type=F32)


def _ada_kernel(c_ref, w_ref, b_ref, o_ref):
    a = _silu(c_ref[...]).astype(BF16)
    o_ref[...] = _dot(a, w_ref[...].astype(BF16)) + b_ref[...]


def _ada(cond, w_ada, b_ada):
    tn = 1536
    n = N_MOD * D_MODEL
    return pl.pallas_call(
        _ada_kernel,
        grid=(DEPTH, n // tn),
        in_specs=[
            pl.BlockSpec((MOD_ROWS, D_MODEL), lambda l, j: (0, 0)),
            pl.BlockSpec((None, D_MODEL, tn), lambda l, j: (l, 0, j)),
            pl.BlockSpec((None, 1, tn), lambda l, j: (l, 0, j)),
        ],
        out_specs=pl.BlockSpec((None, MOD_ROWS, tn), lambda l, j: (l, 0, j)),
        out_shape=jax.ShapeDtypeStruct((DEPTH, MOD_ROWS, n), F32),
        compiler_params=_params("arbitrary", "arbitrary"),
        name="ada",
    )(cond, w_ada, b_ada.reshape(DEPTH, 1, n))


def _mod_spec(layer, which, rows_per_batch):
    if rows_per_batch is None:
        return pl.BlockSpec((None, None, None, 1, D_MODEL), lambda i: (layer, 0, which, 0, 0))
    tiles = rows_per_batch // ROW_TILE
    return pl.BlockSpec((None, None, None, 1, D_MODEL), lambda i: (layer, 1 + i // tiles, which, 0, 0))


def _row_spec(width):
    return pl.BlockSpec((ROW_TILE, width), lambda i: (i, 0))


def _log_sigmoid(x):
    return jnp.minimum(x, 0.0) - jnp.log(1.0 + jnp.exp(-jnp.abs(x)))


def _gla_in_kernel(x_ref, g_ref, sh_ref, sc_ref, w_ref, w1_ref, w2_ref, bg_ref, q_ref, k_ref, v_ref, r_ref, la_ref):
    for rows in _sub_tiles():
        h = _norm_mod(x_ref[rows, :], g_ref[...], sh_ref[...], sc_ref[...]).astype(BF16)
        z = _dot(h, w_ref[...])
        q_ref[rows, :] = z[:, :GLA_HK]
        k_ref[rows, :] = z[:, GLA_HK:2 * GLA_HK]
        v_ref[rows, :] = z[:, 2 * GLA_HK:2 * GLA_HK + GLA_HV].astype(BF16)
        r_ref[rows, :] = z[:, 2 * GLA_HK + GLA_HV:]
        low_rank = _dot(h, w1_ref[...]).astype(BF16)
        pre = _dot(low_rank, w2_ref[...]) + bg_ref[...]
        la_ref[rows, :] = _log_sigmoid(pre) * (1.0 / GLA_TAU)


def _gla_in(x, mod, layer, rows_per_batch, g, w_in, w1, w2, bg):
    rows = x.shape[0]
    return pl.pallas_call(
        _gla_in_kernel,
        grid=(rows // ROW_TILE,),
        in_specs=[
            _row_spec(D_MODEL),
            _resident((1, D_MODEL)),
            _mod_spec(layer, 0, rows_per_batch),
            _mod_spec(layer, 1, rows_per_batch),
            _resident((D_MODEL, 2 * GLA_HK + 2 * GLA_HV)),
            _resident((D_MODEL, GATE_PAD)),
            _resident((GATE_PAD, 2 * GLA_HK)),
            _resident((1, 2 * GLA_HK)),
        ],
        out_specs=[_row_spec(GLA_HK), _row_spec(GLA_HK), _row_spec(GLA_HV), _row_spec(GLA_HV), _row_spec(2 * GLA_HK)],
        out_shape=[
            jax.ShapeDtypeStruct((rows, GLA_HK), F32),
            jax.ShapeDtypeStruct((rows, GLA_HK), F32),
            jax.ShapeDtypeStruct((rows, GLA_HV), BF16),
            jax.ShapeDtypeStruct((rows, GLA_HV), F32),
            jax.ShapeDtypeStruct((rows, 2 * GLA_HK), F32),
        ],
        compiler_params=_params("arbitrary"),
        name="gla_in",
    )(x, g, mod, mod, w_in, w1, w2, bg)


def _split3(x):
    hi = x.astype(BF16)
    rem = x - hi.astype(F32)
    mid = rem.astype(BF16)
    lo = (rem - mid.astype(F32)).astype(BF16)
    return jnp.concatenate([hi, mid, lo], axis=1)


def _fold3(x):
    return x[:, :GLA_DK] + x[:, GLA_DK:2 * GLA_DK] + x[:, 2 * GLA_DK:]


def _gla_scan_kernel(*refs, n_groups, has_s0):
    C, DK, DV, G, HPS = GLA_CHUNK, GLA_DK, GLA_DV, GLA_GROUP, GLA_HEADS_PER_STEP
    cpg = G // C
    n_chunks = n_groups * cpg
    if has_s0:
        (q_ref, k_ref, v_ref, laf_ref, lab_ref, r_ref, gh_ref, s0f_ref, s0b_ref,
         y_ref, qd_ref, kd_ref, ds_ref, sin_ref, dec_ref) = refs
    else:
        (q_ref, k_ref, v_ref, laf_ref, lab_ref, r_ref, gh_ref,
         y_ref, sf_ref, sb_ref, qd_ref, kd_ref, ds_ref, sin_ref, dec_ref) = refs

    row = lax.broadcasted_iota(jnp.int32, (G, G), 0)
    col = lax.broadcasted_iota(jnp.int32, (G, G), 1)
    same_chunk = (row // C) == (col // C)
    lower = same_chunk & (row >= col)
    upper = same_chunk & (row <= col)
    prefix_sum = jnp.where(lower, 1.0, 0.0).astype(BF16)
    suffix_sum = jnp.where(upper, 1.0, 0.0).astype(BF16)

    def group(g):
        return pl.ds(pl.multiple_of(g * G, G), G)

    def spread(x, first_row):
        return jnp.concatenate(
            [jnp.broadcast_to(x[first_row + c * C:first_row + c * C + 1], (C, x.shape[1])) for c in range(cpg)], axis=0)

    def pass1(g, carry):
        sl = group(g)
        for hd in range(HPS):
            dk = slice(hd * DK, (hd + 1) * DK)
            dv = slice(hd * DV, (hd + 1) * DV)
            both = slice(hd * 2 * DK, (hd + 1) * 2 * DK)
            bf = _fold3(_dot(prefix_sum, _split3(laf_ref[sl, dk])))
            bb = _fold3(_dot(suffix_sum, _split3(lab_ref[sl, dk])))
            tot_f = spread(bf, C - 1)
            tot_b = spread(bb, 0)
            q = q_ref[sl, dk] * (DK ** -0.5)
            k = k_ref[sl, dk]
            qd_ref[sl, both] = jnp.concatenate([q * jnp.exp(bf), q * jnp.exp(bb)], axis=1).astype(BF16)
            kd_ref[sl, both] = jnp.concatenate([k * jnp.exp(-bf), k * jnp.exp(-bb)], axis=1).astype(BF16)
            k_end = jnp.concatenate([k * jnp.exp(tot_f - bf), k * jnp.exp(tot_b - bb)], axis=1).astype(BF16)
            v = v_ref[sl, dv]
            for c in range(cpg):
                rows = slice(c * C, (c + 1) * C)
                ds_ref[hd, g * cpg + c] = _dot_tn(v[rows], k_end[rows])
                dec_ref[hd, g * cpg + c] = jnp.exp(
                    jnp.concatenate([tot_f[c * C:c * C + 1], tot_b[c * C:c * C + 1]], axis=1))
        return carry

    lax.fori_loop(0, n_groups, pass1, 0, unroll=min(2, n_groups))

    for hd in range(HPS):
        for forward in (True, False):
            lanes = slice(0, DK) if forward else slice(DK, 2 * DK)
            if has_s0:
                init = (s0f_ref if forward else s0b_ref)[hd].T
            else:
                init = jnp.zeros((DV, DK), F32)

            def step(i, s, hd=hd, forward=forward, lanes=lanes):
                n = i if forward else n_chunks - 1 - i
                sin_ref[hd, n, :, lanes] = s.astype(BF16)
                return s * dec_ref[hd, n, :, lanes] + ds_ref[hd, n, :, lanes]

            final = lax.fori_loop(0, n_chunks, step, init)
            if not has_s0:
                (sf_ref if forward else sb_ref)[hd] = final.T

    gh = gh_ref[...]

    def pass3(g, carry):
        sl = group(g)
        for hd in range(HPS):
            dv = slice(hd * DV, (hd + 1) * DV)
            both = slice(hd * 2 * DK, (hd + 1) * 2 * DK)
            qd = qd_ref[sl, both]
            kd = kd_ref[sl, both]
            a_f = _dot_nt(qd[:, :DK], kd[:, :DK])
            a_b = _dot_nt(qd[:, DK:], kd[:, DK:])
            a = jnp.where(lower, a_f, 0.0) + jnp.where(upper, a_b, 0.0)
            inter = [_dot_nt(qd[c * C:(c + 1) * C], sin_ref[hd, g * cpg + c]) for c in range(cpg)]
            o = _dot(a.astype(BF16), v_ref[sl, dv]) + jnp.concatenate(inter, axis=0)
            y_ref[sl, dv] = (_rms(o, gh) * _silu(r_ref[sl, dv])).astype(BF16)
        return carry

    lax.fori_loop(0, n_groups, pass3, 0, unroll=min(2, n_groups))


def _gla_scan(q, k, v, la, r, g_head, batch, seq, s0f=None, s0b=None):
    hps = GLA_HEADS_PER_STEP
    steps = GLA_HEADS // hps
    n_chunks = seq // GLA_CHUNK
    has_s0 = s0f is not None
    rows = batch * seq
    tok = lambda w, off: pl.BlockSpec((seq, w * hps), lambda b, h: (b, h + off))
    state = pl.BlockSpec((None, hps, GLA_DK, GLA_DV), lambda b, h: (b, h, 0, 0))
    in_specs = [tok(GLA_DK, 0), tok(GLA_DK, 0), tok(GLA_DV, 0), tok(GLA_DK, 0), tok(GLA_DK, steps),
                tok(GLA_DV, 0), pl.BlockSpec((1, GLA_DV), lambda b, h: (0, 0))]
    args = [q, k, v, la, la, r, g_head]
    y_shape = jax.ShapeDtypeStruct((rows, GLA_HV), BF16)
    st_shape = jax.ShapeDtypeStruct((batch, GLA_HEADS, GLA_DK, GLA_DV), F32)
    scratch = [
        pltpu.VMEM((seq, hps * 2 * GLA_DK), BF16),
        pltpu.VMEM((seq, hps * 2 * GLA_DK), BF16),
        pltpu.VMEM((hps, n_chunks, GLA_DV, 2 * GLA_DK), F32),
        pltpu.VMEM((hps, n_chunks, GLA_DV, 2 * GLA_DK), BF16),
        pltpu.VMEM((hps, n_chunks, 1, 2 * GLA_DK), F32),
    ]
    if has_s0:
        in_specs += [state, state]
        args += [s0f, s0b]
        out_specs = tok(GLA_DV, 0)
        out_shape = y_shape
    else:
        out_specs = [tok(GLA_DV, 0), state, state]
        out_shape = [y_shape, st_shape, st_shape]
    return pl.pallas_call(
        functools.partial(_gla_scan_kernel, n_groups=seq // GLA_GROUP, has_s0=has_s0),
        grid=(batch, steps),
        in_specs=in_specs,
        out_specs=out_specs,
        out_shape=out_shape,
        scratch_shapes=scratch,
        compiler_params=_params("arbitrary", "arbitrary"),
        name="gla_scan",
    )(*args)


def _out_ffn_kernel(*refs, final_norm):
    if final_norm:
        (x_ref, y_ref, wo_ref, g1_ref, gn_ref, sh_ref, sc_ref, g2_ref, wg_ref, wu_ref, wd_ref, gf_ref, o_ref) = refs
    else:
        (x_ref, y_ref, wo_ref, g1_ref, gn_ref, sh_ref, sc_ref, g2_ref, wg_ref, wu_ref, wd_ref, o_ref) = refs
    for rows in _sub_tiles():
        x = x_ref[rows, :] + g1_ref[...] * _dot(y_ref[rows, :], wo_ref[...])
        h = _norm_mod(x, gn_ref[...], sh_ref[...], sc_ref[...]).astype(BF16)
        act = (_silu(_dot(h, wg_ref[...])) * _dot(h, wu_ref[...])).astype(BF16)
        x = x + g2_ref[...] * _dot(act, wd_ref[...])
        if final_norm:
            x = _rms(x, gf_ref[...])
        o_ref[rows, :] = x


def _out_ffn(x, y, mod, layer, rows_per_batch, w_out, g_ffn, wg, wu, wd, g_final=None):
    rows = x.shape[0]
    final_norm = g_final is not None
    in_specs = [
        _row_spec(D_MODEL),
        _row_spec(D_MODEL),
        _resident((D_MODEL, D_MODEL)),
        _mod_spec(layer, 2, rows_per_batch),
        _resident((1, D_MODEL)),
        _mod_spec(layer, 3, rows_per_batch),
        _mod_spec(layer, 4, rows_per_batch),
        _mod_spec(layer, 5, rows_per_batch),
        _resident((None, D_MODEL, D_FF), (layer, 0, 0)),
        _resident((None, D_MODEL, D_FF), (layer, 0, 0)),
        _resident((None, D_FF, D_MODEL), (layer, 0, 0)),
    ]
    args = [x, y, w_out, mod, g_ffn, mod, mod, mod, wg, wu, wd]
    if final_norm:
        in_specs.append(_resident((1, D_MODEL)))
        args.append(g_final)
    return pl.pallas_call(
        functools.partial(_out_ffn_kernel, final_norm=final_norm),
        grid=(rows // ROW_TILE,),
        in_specs=in_specs,
        out_specs=_row_spec(D_MODEL),
        out_shape=jax.ShapeDtypeStruct((rows, D_MODEL), F32),
        compiler_params=_params("arbitrary"),
        name="out_ffn",
    )(*args)


def _rope_tables(n_tokens):
    rows = n_tokens // GRID_W
    r, col = jnp.meshgrid(jnp.arange(rows), jnp.arange(GRID_W), indexing="ij")
    r = r.reshape(-1).astype(F32)
    col = col.reshape(-1).astype(F32)
    n_freq = DIFF_DH // 4
    inv = ROPE_THETA ** (-jnp.arange(n_freq, dtype=F32) / n_freq)
    ang = jnp.concatenate([r[:, None] * inv, col[:, None] * inv], axis=-1)
    cos, sin = jnp.cos(ang), jnp.sin(ang)
    reps = D_MODEL // DIFF_DH
    cos = jnp.tile(jnp.concatenate([cos, cos], axis=-1), (1, reps))
    sin = jnp.tile(jnp.concatenate([-sin, sin], axis=-1), (1, reps))
    return cos, sin, cos.T, sin.T


def _rope(x, cos, sin_signed, axis):
    half = DIFF_DH // 2
    size = x.shape[axis]
    pos = lax.broadcasted_iota(jnp.int32, x.shape, axis)
    from_above = pltpu.roll(x, size - half, axis=axis)
    from_below = pltpu.roll(x, half, axis=axis)
    swapped = jnp.where(pos % DIFF_DH < half, from_above, from_below)
    return x * cos + swapped * sin_signed


def _qkv_kernel(*refs, rope, seq):
    if rope:
        (x_ref, g_ref, sh_ref, sc_ref, wq_ref, wk_ref, wv_ref, cos_ref, sin_ref, cos_t_ref, sin_t_ref,
         q_ref, k_ref, v_ref, wq_t_ref, wv_t_ref) = refs
    else:
        (x_ref, g_ref, sh_ref, sc_ref, wq_ref, wk_ref, wv_ref, q_ref, k_ref, v_ref, kf_ref, vf_ref,
         wq_t_ref, wv_t_ref) = refs

    @pl.when(pl.program_id(0) == 0)
    def _():
        wq_t_ref[...] = wq_ref[...].astype(F32).T.astype(BF16)
        wv_t_ref[...] = wv_ref[...].astype(F32).T.astype(BF16)

    for s, rows in enumerate(_sub_tiles()):
        h32 = _norm_mod(x_ref[rows, :], g_ref[...], sh_ref[...], sc_ref[...])
        h = h32.astype(BF16)
        h_t = h32.T.astype(BF16)
        q_t = _dot(wq_t_ref[...], h_t)
        v_t = _dot(wv_t_ref[...], h_t)
        k = _dot(h, wk_ref[...])
        if rope:
            q_t = _rope(q_t, cos_t_ref[:, rows], sin_t_ref[:, rows], 0)
            k = _rope(k, cos_ref[rows, :], sin_ref[rows, :], 1)
        else:
            for b in range(SUB_TILE // seq):
                kf_ref[rows.start // seq + b] = k[b * seq:(b + 1) * seq].T
            v = v_t.T
            for head in range(DIFF_HEADS):
                vf_ref[pl.ds(rows.start * DIFF_HEADS + head, SUB_TILE, stride=DIFF_HEADS), :] = (
                    v[:, head * DIFF_DV:(head + 1) * DIFF_DV])
        q_b = (q_t * (DIFF_DH ** -0.5 * LOG2E)).astype(BF16)
        k_b = k.astype(BF16)
        v_b = v_t.astype(BF16)
        for head in range(DIFF_HEADS):
            f = slice(head * 2 * DIFF_DH, (head + 1) * 2 * DIFF_DH)
            q_ref[s, head] = q_b[f, :]
            if seq >= ROW_TILE:
                k_ref[head, rows, :] = k_b[:, f]
                v_ref[head, :, rows] = v_b[f, :]
            else:
                for b in range(SUB_TILE // seq):
                    k_ref[rows.start // seq + b, head] = k_b[b * seq:(b + 1) * seq, f]
                    v_ref[rows.start // seq + b, head] = v_b[f, b * seq:(b + 1) * seq]


def _qkv(x, mod, layer, rows_per_batch, seq, g, w, tables=None):
    rows = x.shape[0]
    rope = tables is not None
    per_tile = ROW_TILE // SUB_TILE
    tiles = max(seq // ROW_TILE, 1)
    in_specs = [_row_spec(D_MODEL), _resident((1, D_MODEL)), _mod_spec(layer, 0, rows_per_batch),
                _mod_spec(layer, 1, rows_per_batch), _resident((D_MODEL, D_MODEL), (0, 0)),
                _resident((D_MODEL, D_MODEL), (0, 1)), _resident((D_MODEL, D_MODEL), (0, 2))]
    args = [x, g, mod, mod, w, w, w]
    dh2 = 2 * DIFF_DH
    if seq >= ROW_TILE:
        k_spec = pl.BlockSpec((None, DIFF_HEADS, ROW_TILE, dh2), lambda i: (i // tiles, 0, i % tiles, 0))
        v_spec = pl.BlockSpec((None, DIFF_HEADS, DIFF_DV, ROW_TILE), lambda i: (i // tiles, 0, 0, i % tiles))
    else:
        k_spec = pl.BlockSpec((ROW_TILE // seq, DIFF_HEADS, seq, dh2), lambda i: (i, 0, 0, 0))
        v_spec = pl.BlockSpec((ROW_TILE // seq, DIFF_HEADS, DIFF_DV, seq), lambda i: (i, 0, 0, 0))
    out_specs = [pl.BlockSpec((per_tile, DIFF_HEADS, dh2, SUB_TILE), lambda i: (i, 0, 0, 0)), k_spec, v_spec]
    out_shape = [jax.ShapeDtypeStruct((rows // SUB_TILE, DIFF_HEADS, dh2, SUB_TILE), BF16),
                 jax.ShapeDtypeStruct((rows // seq, DIFF_HEADS, seq, dh2), BF16),
                 jax.ShapeDtypeStruct((rows // seq, DIFF_HEADS, DIFF_DV, seq), BF16)]
    if rope:
        tab = pl.BlockSpec((ROW_TILE, D_MODEL), lambda i: (i % tiles, 0))
        tab_t = pl.BlockSpec((D_MODEL, ROW_TILE), lambda i: (0, i % tiles))
        in_specs += [tab, tab, tab_t, tab_t]
        args += list(tables)
    else:
        out_specs += [pl.BlockSpec((ROW_TILE // seq, D_MODEL, seq), lambda i: (i, 0, 0)),
                      pl.BlockSpec((ROW_TILE * DIFF_HEADS, DIFF_DV), lambda i: (i, 0))]
        out_shape += [jax.ShapeDtypeStruct((rows // seq, D_MODEL, seq), F32),
                      jax.ShapeDtypeStruct((rows * DIFF_HEADS, DIFF_DV), F32)]
    return pl.pallas_call(
        functools.partial(_qkv_kernel, rope=rope, seq=seq),
        grid=(rows // ROW_TILE,),
        in_specs=in_specs,
        out_specs=out_specs,
        out_shape=out_shape,
        scratch_shapes=[pltpu.VMEM((D_MODEL, D_MODEL), BF16), pltpu.VMEM((D_MODEL, D_MODEL), BF16)],
        compiler_params=_params("arbitrary"),
        name="diff_qkv",
    )(*args)


def _attn_kernel(*refs, has_cache, lam_init, n_q_tiles, q_tile):
    if has_cache:
        (lq_ref, lk_ref, q_ref, k_ref, v_ref, gh_ref, kc_ref, vc_ref, y_ref, *scratch) = refs
        kcb_ref, vcb_ref = scratch[10:]
    else:
        (lq_ref, lk_ref, q_ref, k_ref, v_ref, gh_ref, y_ref, *scratch) = refs
    s_refs, m_refs, l_refs, yb_ref = scratch[0:3], scratch[3:6], scratch[6:9], scratch[9]
    e = jnp.exp(jnp.sum(lq_ref[...] * lk_ref[...], axis=1, keepdims=True))
    lam = e[0:1, :] - e[1:2, :] + lam_init
    first_map = lax.broadcasted_iota(jnp.int32, (2 * DIFF_DH, q_tile), 0) < DIFF_DH
    gh = gh_ref[...] * (1.0 - lam_init)

    def features(head):
        return slice(head * 2 * DIFF_DH, (head + 1) * 2 * DIFF_DH)

    if has_cache:
        for head in range(DIFF_HEADS):
            kcb_ref[head] = kc_ref[head].T.astype(BF16)
            vcb_ref[head] = vc_ref[:, features(head)].astype(F32).T.astype(BF16)

    n_new = k_ref.shape[1] // KEY_BLOCK
    n_past = kcb_ref.shape[1] // KEY_BLOCK if has_cache else 0
    n_blocks = n_past + n_new
    n_items = n_q_tiles * DIFF_HEADS
    assert (n_items - 2) % 3 == 0 and n_items >= 2

    def split(item):
        if isinstance(item, int):
            return item // DIFF_HEADS, item % DIFF_HEADS
        return lax.shift_right_logical(item, DIFF_HEADS.bit_length() - 1), item & (DIFF_HEADS - 1)

    def key_block(head, j):
        if j < n_past:
            rows = slice(j * KEY_BLOCK, (j + 1) * KEY_BLOCK)
            return kcb_ref[head, rows, :], vcb_ref[head, :, rows]
        rows = slice((j - n_past) * KEY_BLOCK, (j - n_past + 1) * KEY_BLOCK)
        return k_ref[head, rows, :], v_ref[head, :, rows]

    def by_sublane(x):
        return x.reshape(KEY_BLOCK // 8, 8, 2 * q_tile)

    def scores(item, slot):
        t, head = split(item)
        q_t = q_ref[t, head]
        zero = jnp.zeros_like(q_t)
        qq = jnp.concatenate([jnp.where(first_map, q_t, zero), jnp.where(first_map, zero, q_t)], axis=1)
        m = None
        for j in range(n_blocks):
            s = _dot(key_block(head, j)[0], qq)
            s_refs[slot][j] = s
            top = jnp.max(by_sublane(s), axis=0)
            m = top if m is None else jnp.maximum(m, top)
        m_refs[slot][...] = jnp.max(m, axis=0, keepdims=True)

    def exponentials(slot):
        m = m_refs[slot][...]
        denom = None
        for j in range(n_blocks):
            p = jnp.exp2(s_refs[slot][j] - m)
            s_refs[slot][j] = p
            part = jnp.sum(by_sublane(p), axis=0)
            denom = part if denom is None else denom + part
        l_refs[slot][...] = jnp.sum(denom, axis=0, keepdims=True)

    def values(item, slot):
        t, head = split(item)
        denom = l_refs[slot][...]
        ratio = lam * denom[:, :q_tile] / denom[:, q_tile:]
        o = None
        for j in range(n_blocks):
            p = s_refs[slot][j]
            w = (p[:, :q_tile] - p[:, q_tile:] * ratio).astype(BF16)
            part = _dot(key_block(head, j)[1], w)
            o = part if o is None else o + part
        o = o * (1.0 / denom[:, :q_tile])
        o = o * lax.rsqrt(jnp.mean(o * o, axis=0, keepdims=True) + EPS) * gh
        yb_ref[head, pl.ds(pl.multiple_of(t * q_tile, q_tile), q_tile), :] = o.T.astype(BF16)

    scores(0, 0)
    exponentials(0)
    scores(1, 1)

    def body(trip, carry):
        first = 2 + 3 * trip
        for r in range(3):
            values(first + r - 2, r)
            exponentials((r + 1) % 3)
            scores(first + r, (r + 2) % 3)
        return carry

    lax.fori_loop(0, (n_items - 2) // 3, body, 0)
    values(n_items - 2, (n_items - 2) % 3)
    exponentials((n_items - 1) % 3)
    values(n_items - 1, (n_items - 1) % 3)
    for head in range(DIFF_HEADS):
        y_ref[:, features(head)] = yb_ref[head]


def _attn(q, k, v, lam_q, lam_k, g_head, lam_init, batch, seq, cache_k=None, cache_v=None):
    has_cache = cache_k is not None
    q_tile = SUB_TILE
    n_q_tiles = seq // q_tile
    dh2 = 2 * DIFF_DH
    tok = pl.BlockSpec((seq, D_MODEL), lambda b: (b, 0))
    small = lambda shape: pl.BlockSpec(shape, lambda b: (0, 0))
    in_specs = [small((2, DIFF_DH)), small((2, DIFF_DH)),
                pl.BlockSpec((n_q_tiles, DIFF_HEADS, dh2, q_tile), lambda b: (b, 0, 0, 0)),
                pl.BlockSpec((None, DIFF_HEADS, seq, dh2), lambda b: (b, 0, 0, 0)),
                pl.BlockSpec((None, DIFF_HEADS, DIFF_DV, seq), lambda b: (b, 0, 0, 0)),
                small((DIFF_DV, 1))]
    args = [lam_q, lam_k, q, k, v, g_head]
    past = cache_k.shape[-1] if has_cache else 0
    assert seq % KEY_BLOCK == 0 and past % KEY_BLOCK == 0
    scratch = ([pltpu.VMEM(((seq + past) // KEY_BLOCK, KEY_BLOCK, 2 * q_tile), F32)] * 3
               + [pltpu.VMEM((1, 2 * q_tile), F32)] * 6
               + [pltpu.VMEM((DIFF_HEADS, seq, DIFF_DV), BF16)])
    if has_cache:
        in_specs += [pl.BlockSpec((None, DIFF_HEADS, 2 * DIFF_DH, past), lambda b: (b, 0, 0, 0)),
                     pl.BlockSpec((None, past, D_MODEL), lambda b: (b, 0, 0))]
        args += [cache_k, cache_v]
        scratch += [pltpu.VMEM((DIFF_HEADS, past, 2 * DIFF_DH), BF16), pltpu.VMEM((DIFF_HEADS, DIFF_DV, past), BF16)]
    return pl.pallas_call(
        functools.partial(_attn_kernel, has_cache=has_cache, lam_init=lam_init, n_q_tiles=n_q_tiles, q_tile=q_tile),
        grid=(batch,),
        in_specs=in_specs,
        out_specs=tok,
        out_shape=jax.ShapeDtypeStruct((batch * seq, D_MODEL), BF16),
        scratch_shapes=scratch,
        compiler_params=_params("arbitrary"),
        name="diff_attn",
    )(*args)


def kernel(x_prompt, x_sample, state_gla_fwd, state_gla_bwd, cache_diff_k, cache_diff_v, c, c_ctx, w_ada, b_ada, g_mix_norm, g_ffn_norm, w_gla_in, w_gla_g1, w_gla_g2, b_gla_g, g_gla_head, w_gla_out, w_diff_qkv, lam_q, lam_k, g_diff_head, w_diff_out, w_ffn_gate, w_ffn_up, w_ffn_down, g_final):
    bp, tp, _ = x_prompt.shape
    bs, ts, _ = x_sample.shape
    assert SUB_TILE % tp == 0 and ts % ROW_TILE == 0 and (bp * tp) % ROW_TILE == 0
    assert tp % GLA_GROUP == 0 and ts % GLA_GROUP == 0
    xp = x_prompt.reshape(bp * tp, D_MODEL)
    xs = x_sample.reshape(bs * ts, D_MODEL)

    cond = jnp.concatenate([c_ctx[None, :], c, jnp.zeros((MOD_ROWS - 1 - bs, D_MODEL), F32)], axis=0)
    mod = _ada(cond, w_ada, b_ada).reshape(DEPTH, MOD_ROWS, N_MOD, 1, D_MODEL)
    wg, wu, wd = w_ffn_gate.astype(BF16), w_ffn_up.astype(BF16), w_ffn_down.astype(BF16)

    j = 0
    w_in = w_gla_in[j].astype(BF16)
    w1 = jnp.concatenate([w_gla_g1[j, 0], w_gla_g1[j, 1], jnp.zeros((D_MODEL, GATE_PAD - 2 * GLA_RANK), F32)],
                         axis=1).astype(BF16)
    w2 = jnp.zeros((GATE_PAD, 2 * GLA_HK), F32)
    w2 = w2.at[:GLA_RANK, :GLA_HK].set(w_gla_g2[j, 0]).at[GLA_RANK:2 * GLA_RANK, GLA_HK:].set(w_gla_g2[j, 1]).astype(BF16)
    bg = b_gla_g[j].reshape(1, 2 * GLA_HK)
    g_mix = g_mix_norm[0].reshape(1, D_MODEL)
    g_head = g_gla_head[j].reshape(1, GLA_DV)
    g_ffn = g_ffn_norm[0].reshape(1, D_MODEL)
    w_out = w_gla_out[j].astype(BF16)

    qp, kp, vp, rp, lap = _gla_in(xp, mod, 0, None, g_mix, w_in, w1, w2, bg)
    qs, ks, vs, rs, las = _gla_in(xs, mod, 0, ts, g_mix, w_in, w1, w2, bg)
    yp, new_f, new_b = _gla_scan(qp, kp, vp, lap, rp, g_head, bp, tp)
    ys = _gla_scan(qs, ks, vs, las, rs, g_head, bs, ts, state_gla_fwd[:, j], state_gla_bwd[:, j])
    xp = _out_ffn(xp, yp, mod, 0, None, w_out, g_ffn, wg, wu, wd)
    xs = _out_ffn(xs, ys, mod, 0, ts, w_out, g_ffn, wg, wu, wd)

    lam_init = 0.8 - 0.6 * math.exp(-0.3 * 1)
    g_mix = g_mix_norm[1].reshape(1, D_MODEL)
    w_qkv = w_diff_qkv[j].astype(BF16)
    g_head = g_diff_head[j].reshape(DIFF_DV, 1)
    g_ffn = g_ffn_norm[1].reshape(1, D_MODEL)
    w_out = w_diff_out[j].astype(BF16)

    qp, kp, vp, kp32, vp32 = _qkv(xp, mod, 1, None, tp, g_mix, w_qkv)
    qs, ks, vs = _qkv(xs, mod, 1, ts, ts, g_mix, w_qkv, _rope_tables(ts))
    yp = _attn(qp, kp, vp, lam_q[j], lam_k[j], g_head, lam_init, bp, tp)
    past = cache_diff_k.shape[2]
    cache_k_t = jnp.transpose(cache_diff_k[:, j], (0, 2, 3, 4, 1)).reshape(bs, DIFF_HEADS, 2 * DIFF_DH, past)
    cache_v = cache_diff_v[:, j].reshape(bs, past, D_MODEL).astype(BF16)
    ys = _attn(qs, ks, vs, lam_q[j], lam_k[j], g_head, lam_init, bs, ts, cache_k_t, cache_v)
    g_fin = g_final.reshape(1, D_MODEL)
    yp_out = _out_ffn(xp, yp, mod, 1, None, w_out, g_ffn, wg, wu, wd, g_final=g_fin)
    ys_out = _out_ffn(xs, ys, mod, 1, ts, w_out, g_ffn, wg, wu, wd, g_final=g_fin)

    return (yp_out.reshape(bp, tp, D_MODEL),
            ys_out.reshape(bs, ts, D_MODEL),
            new_f[:, None],
            new_b[:, None],
            jnp.transpose(kp32.reshape(bp, 1, DIFF_HEADS, 2, DIFF_DH, tp), (0, 1, 5, 2, 3, 4)),
            vp32.reshape(bp, 1, tp, DIFF_HEADS, DIFF_DV))
```

```python
import functools
import math

import jax
import jax.numpy as jnp
from jax import lax
from jax.experimental import pallas as pl
from jax.experimental.pallas import tpu as pltpu

F32 = jnp.float32
BF16 = jnp.bfloat16

D_MODEL = 1024
DEPTH = 2
GRID_W = 64
GLA_HEADS = 4
GLA_DK = 128
GLA_DV = 256
GLA_HK = GLA_HEADS * GLA_DK
GLA_HV = GLA_HEADS * GLA_DV
GLA_RANK = 16
GLA_TAU = 16.0
GLA_CHUNK = 64
GLA_GROUP = 256
GLA_HEADS_PER_STEP = 2
DIFF_HEADS = 8
DIFF_DH = 64
DIFF_DV = 128
ROPE_THETA = 10000.0
D_FF = 2816
EPS = 1e-6
LOG2E = math.log2(math.e)
N_MOD = 6
MOD_ROWS = 16
GATE_PAD = 128

VMEM_LIMIT_BYTES = 56 * 1024 * 1024
ROW_TILE = 512
SUB_TILE = 256
KEY_BLOCK = 256


def _params(*sem):
    return pltpu.CompilerParams(dimension_semantics=sem, vmem_limit_bytes=VMEM_LIMIT_BYTES)


def _resident(shape, index=None):
    index = (0,) * len(shape) if index is None else index
    return pl.BlockSpec(shape, lambda *_: index, pipeline_mode=pl.Buffered(1))


def _sub_tiles():
    return [slice(s * SUB_TILE, (s + 1) * SUB_TILE) for s in range(ROW_TILE // SUB_TILE)]


def _sigmoid(x):
    return 1.0 / (1.0 + jnp.exp(-x))


def _silu(x):
    return x * _sigmoid(x)


def _rms(x, g):
    return x * lax.rsqrt(jnp.mean(x * x, axis=-1, keepdims=True) + EPS) * g


def _norm_mod(x, g, shift, scale):
    return _rms(x, g) * (1.0 + scale) + shift


def _dot(a, b):
    return jnp.dot(a, b, preferred_element_type=F32)


def _dot_nt(a, b):
    return lax.dot_general(a, b, (((1,), (1,)), ((), ())), preferred_element_type=F32)


def _dot_tn(a, b):
    return lax.dot_general(a, b, (((0,), (0,)), ((), ())), preferred_element_type=F32)


def _ada_kernel(c_ref, w_ref, b_ref, o_ref):
    a = _silu(c_ref[...]).astype(BF16)
    o_ref[...] = _dot(a, w_ref[...].astype(BF16)) + b_ref[...]


def _ada(cond, w_ada, b_ada):
    tn = 1536
    n = N_MOD * D_MODEL
    return pl.pallas_call(
        _ada_kernel,
        grid=(DEPTH, n // tn),
        in_specs=[
            pl.BlockSpec((MOD_ROWS, D_MODEL), lambda l, j: (0, 0)),
            pl.BlockSpec((None, D_MODEL, tn), lambda l, j: (l, 0, j)),
            pl.BlockSpec((None, 1, tn), lambda l, j: (l, 0, j)),
        ],
        out_specs=pl.BlockSpec((None, MOD_ROWS, tn), lambda l, j: (l, 0, j)),
        out_shape=jax.ShapeDtypeStruct((DEPTH, MOD_ROWS, n), F32),
        compiler_params=_params("arbitrary", "arbitrary"),
        name="ada",
    )(cond, w_ada, b_ada.reshape(DEPTH, 1, n))


def _mod_spec(layer, which, rows_per_batch):
    if rows_per_batch is None:
        return pl.BlockSpec((None, None, None, 1, D_MODEL), lambda i: (layer, 0, which, 0, 0))
    tiles = rows_per_batch // ROW_TILE
    return pl.BlockSpec((None, None, None, 1, D_MODEL), lambda i: (layer, 1 + i // tiles, which, 0, 0))


def _row_spec(width):
    return pl.BlockSpec((ROW_TILE, width), lambda i: (i, 0))


def _log_sigmoid(x):
    return jnp.minimum(x, 0.0) - jnp.log(1.0 + jnp.exp(-jnp.abs(x)))


def _split3(x):
    hi = x.astype(BF16)
    rem = x - hi.astype(F32)
    mid = rem.astype(BF16)
    lo = (rem - mid.astype(F32)).astype(BF16)
    return jnp.concatenate([hi, mid, lo], axis=1)


def _fold3(x):
    return x[:, :GLA_DK] + x[:, GLA_DK:2 * GLA_DK] + x[:, 2 * GLA_DK:]


def _gla_kernel(*refs, n_groups, has_s0):
    C, DK, DV, G, HPS = GLA_CHUNK, GLA_DK, GLA_DV, GLA_GROUP, GLA_HEADS_PER_STEP
    cpg = G // C
    n_chunks = n_groups * cpg
    (x_ref, g_ref, sh_ref, sc_ref, wq_ref, wk_ref, wv_ref, wr_ref, w1_ref, w2f_ref, w2b_ref, bgf_ref, bgb_ref,
     gh_ref) = refs[:14]
    if has_s0:
        (s0f_ref, s0b_ref, y_ref, h_ref, v_ref, r_ref, qd_ref, kd_ref, ds_ref, sin_ref, dec_ref) = refs[14:]
    else:
        (y_ref, sf_ref, sb_ref, h_ref, v_ref, r_ref, qd_ref, kd_ref, ds_ref, sin_ref, dec_ref) = refs[14:]

    row = lax.broadcasted_iota(jnp.int32, (G, G), 0)
    col = lax.broadcasted_iota(jnp.int32, (G, G), 1)
    same_chunk = (row // C) == (col // C)
    lower = same_chunk & (row >= col)
    upper = same_chunk & (row <= col)
    prefix_sum = jnp.where(lower, 1.0, 0.0).astype(BF16)
    suffix_sum = jnp.where(upper, 1.0, 0.0).astype(BF16)

    def group(g):
        return pl.ds(pl.multiple_of(g * G, G), G)

    def spread(x, first_row):
        return jnp.concatenate(
            [jnp.broadcast_to(x[first_row + c * C:first_row + c * C + 1], (C, x.shape[1])) for c in range(cpg)], axis=0)

    @pl.when(pl.program_id(1) == 0)
    def _():
        def norm_group(g, carry):
            sl = group(g)
            h_ref[sl, :] = _norm_mod(x_ref[sl, :], g_ref[...], sh_ref[...], sc_ref[...]).astype(BF16)
            return carry

        lax.fori_loop(0, n_groups, norm_group, 0)

    def pass1(g, carry):
        sl = group(g)
        h = h_ref[sl, :]
        q_all = _dot(h, wq_ref[...]) * (DK ** -0.5)
        k_all = _dot(h, wk_ref[...])
        v_all = _dot(h, wv_ref[...]).astype(BF16)
        v_ref[sl, :] = v_all
        r_ref[sl, :] = _dot(h, wr_ref[...])
        low_rank = _dot(h, w1_ref[...]).astype(BF16)
        la_f = _log_sigmoid(_dot(low_rank, w2f_ref[...]) + bgf_ref[...]) * (1.0 / GLA_TAU)
        la_b = _log_sigmoid(_dot(low_rank, w2b_ref[...]) + bgb_ref[...]) * (1.0 / GLA_TAU)
        for hd in range(HPS):
            dk = slice(hd * DK, (hd + 1) * DK)
            dv = slice(hd * DV, (hd + 1) * DV)
            both = slice(hd * 2 * DK, (hd + 1) * 2 * DK)
            bf = _fold3(_dot(prefix_sum, _split3(la_f[:, dk])))
            bb = _fold3(_dot(suffix_sum, _split3(la_b[:, dk])))
            tot_f = spread(bf, C - 1)
            tot_b = spread(bb, 0)
            q = q_all[:, dk]
            k = k_all[:, dk]
            qd_ref[sl, both] = jnp.concatenate([q * jnp.exp(bf), q * jnp.exp(bb)], axis=1).astype(BF16)
            kd_ref[sl, both] = jnp.concatenate([k * jnp.exp(-bf), k * jnp.exp(-bb)], axis=1).astype(BF16)
            k_end = jnp.concatenate([k * jnp.exp(tot_f - bf), k * jnp.exp(tot_b - bb)], axis=1).astype(BF16)
            v = v_all[:, dv]
            for c in range(cpg):
                rows = slice(c * C, (c + 1) * C)
                ds_ref[hd, g * cpg + c] = _dot_tn(v[rows], k_end[rows])
                dec_ref[hd, g * cpg + c] = jnp.exp(
                    jnp.concatenate([tot_f[c * C:c * C + 1], tot_b[c * C:c * C + 1]], axis=1))
        return carry

    lax.fori_loop(0, n_groups, pass1, 0, unroll=min(2, n_groups))

    for hd in range(HPS):
        for forward in (True, False):
            lanes = slice(0, DK) if forward else slice(DK, 2 * DK)
            if has_s0:
                init = (s0f_ref if forward else s0b_ref)[hd].T
            else:
                init = jnp.zeros((DV, DK), F32)

            def step(i, s, hd=hd, forward=forward, lanes=lanes):
                n = i if forward else n_chunks - 1 - i
                sin_ref[hd, n, :, lanes] = s.astype(BF16)
                return s * dec_ref[hd, n, :, lanes] + ds_ref[hd, n, :, lanes]

            final = lax.fori_loop(0, n_chunks, step, init)
            if not has_s0:
                (sf_ref if forward else sb_ref)[hd] = final.T

    gh = gh_ref[...]

    def pass3(g, carry):
        sl = group(g)
        for hd in range(HPS):
            dv = slice(hd * DV, (hd + 1) * DV)
            both = slice(hd * 2 * DK, (hd + 1) * 2 * DK)
            qd = qd_ref[sl, both]
            kd = kd_ref[sl, both]
            a_f = _dot_nt(qd[:, :DK], kd[:, :DK])
            a_b = _dot_nt(qd[:, DK:], kd[:, DK:])
            a = jnp.where(lower, a_f, 0.0) + jnp.where(upper, a_b, 0.0)
            inter = [_dot_nt(qd[c * C:(c + 1) * C], sin_ref[hd, g * cpg + c]) for c in range(cpg)]
            o = _dot(a.astype(BF16), v_ref[sl, dv]) + jnp.concatenate(inter, axis=0)
            y_ref[sl, dv] = (_rms(o, gh) * _silu(r_ref[sl, dv])).astype(BF16)
        return carry

    lax.fori_loop(0, n_groups, pass3, 0, unroll=min(2, n_groups))


def _gla(x, mod, layer, mod_row, batch, seq, g, w_in, w1, w2, bg, g_head, s0f=None, s0b=None):
    hps = GLA_HEADS_PER_STEP
    steps = GLA_HEADS // hps
    n_chunks = seq // GLA_CHUNK
    has_s0 = s0f is not None
    rows = batch * seq
    tok = pl.BlockSpec((seq, GLA_DV * hps), lambda b, h: (b, h))
    state = pl.BlockSpec((None, hps, GLA_DK, GLA_DV), lambda b, h: (b, h, 0, 0))
    fixed = lambda shape: pl.BlockSpec(shape, lambda b, h: (0,) * len(shape))
    mod_spec = lambda which: pl.BlockSpec((None, None, None, 1, D_MODEL), lambda b, h: (layer, mod_row(b), which, 0, 0))
    cols = lambda width, first: pl.BlockSpec((D_MODEL, width), lambda b, h: (0, first + h))
    kw, vw = hps * GLA_DK, hps * GLA_DV
    in_specs = [
        pl.BlockSpec((seq, D_MODEL), lambda b, h: (b, 0)), fixed((1, D_MODEL)), mod_spec(0), mod_spec(1),
        cols(kw, 0), cols(kw, GLA_HK // kw), cols(vw, 2 * GLA_HK // vw), cols(vw, (2 * GLA_HK + GLA_HV) // vw),
        fixed((D_MODEL, GATE_PAD)),
        pl.BlockSpec((GATE_PAD, kw), lambda b, h: (0, h)), pl.BlockSpec((GATE_PAD, kw), lambda b, h: (0, steps + h)),
        pl.BlockSpec((1, kw), lambda b, h: (0, h)), pl.BlockSpec((1, kw), lambda b, h: (0, steps + h)),
        fixed((1, GLA_DV)),
    ]
    args = [x, g, mod, mod, w_in, w_in, w_in, w_in, w1, w2, w2, bg, bg, g_head]
    y_shape = jax.ShapeDtypeStruct((rows, GLA_HV), BF16)
    st_shape = jax.ShapeDtypeStruct((batch, GLA_HEADS, GLA_DK, GLA_DV), F32)
    scratch = [
        pltpu.VMEM((seq, D_MODEL), BF16),
        pltpu.VMEM((seq, vw), BF16),
        pltpu.VMEM((seq, vw), F32),
        pltpu.VMEM((seq, hps * 2 * GLA_DK), BF16),
        pltpu.VMEM((seq, hps * 2 * GLA_DK), BF16),
        pltpu.VMEM((hps, n_chunks, GLA_DV, 2 * GLA_DK), F32),
        pltpu.VMEM((hps, n_chunks, GLA_DV, 2 * GLA_DK), BF16),
        pltpu.VMEM((hps, n_chunks, 1, 2 * GLA_DK), F32),
    ]
    if has_s0:
        in_specs += [state, state]
        args += [s0f, s0b]
        out_specs = tok
        out_shape = y_shape
    else:
        out_specs = [tok, state, state]
        out_shape = [y_shape, st_shape, st_shape]
    return pl.pallas_call(
        functools.partial(_gla_kernel, n_groups=seq // GLA_GROUP, has_s0=has_s0),
        grid=(batch, steps),
        in_specs=in_specs,
        out_specs=out_specs,
        out_shape=out_shape,
        scratch_shapes=scratch,
        compiler_params=_params("arbitrary", "arbitrary"),
        name="gla",
    )(*args)


def _out_ffn_kernel(*refs, final_norm):
    if final_norm:
        (x_ref, y_ref, wo_ref, g1_ref, gn_ref, sh_ref, sc_ref, g2_ref, wg_ref, wu_ref, wd_ref, gf_ref, o_ref) = refs
    else:
        (x_ref, y_ref, wo_ref, g1_ref, gn_ref, sh_ref, sc_ref, g2_ref, wg_ref, wu_ref, wd_ref, o_ref) = refs
    for rows in _sub_tiles():
        x = x_ref[rows, :] + g1_ref[...] * _dot(y_ref[rows, :], wo_ref[...])
        h = _norm_mod(x, gn_ref[...], sh_ref[...], sc_ref[...]).astype(BF16)
        act = (_silu(_dot(h, wg_ref[...])) * _dot(h, wu_ref[...])).astype(BF16)
        x = x + g2_ref[...] * _dot(act, wd_ref[...])
        if final_norm:
            x = _rms(x, gf_ref[...])
        o_ref[rows, :] = x


def _out_ffn(x, y, mod, layer, rows_per_batch, w_out, g_ffn, wg, wu, wd, g_final=None):
    rows = x.shape[0]
    final_norm = g_final is not None
    in_specs = [
        _row_spec(D_MODEL),
        _row_spec(D_MODEL),
        _resident((D_MODEL, D_MODEL)),
        _mod_spec(layer, 2, rows_per_batch),
        _resident((1, D_MODEL)),
        _mod_spec(layer, 3, rows_per_batch),
        _mod_spec(layer, 4, rows_per_batch),
        _mod_spec(layer, 5, rows_per_batch),
        _resident((None, D_MODEL, D_FF), (layer, 0, 0)),
        _resident((None, D_MODEL, D_FF), (layer, 0, 0)),
        _resident((None, D_FF, D_MODEL), (layer, 0, 0)),
    ]
    args = [x, y, w_out, mod, g_ffn, mod, mod, mod, wg, wu, wd]
    if final_norm:
        in_specs.append(_resident((1, D_MODEL)))
        args.append(g_final)
    return pl.pallas_call(
        functools.partial(_out_ffn_kernel, final_norm=final_norm),
        grid=(rows // ROW_TILE,),
        in_specs=in_specs,
        out_specs=_row_spec(D_MODEL),
        out_shape=jax.ShapeDtypeStruct((rows, D_MODEL), F32),
        compiler_params=_params("arbitrary"),
        name="out_ffn",
    )(*args)


def _rope_tables(n_tokens):
    rows = n_tokens // GRID_W
    r, col = jnp.meshgrid(jnp.arange(rows), jnp.arange(GRID_W), indexing="ij")
    r = r.reshape(-1).astype(F32)
    col = col.reshape(-1).astype(F32)
    n_freq = DIFF_DH // 4
    inv = ROPE_THETA ** (-jnp.arange(n_freq, dtype=F32) / n_freq)
    ang = jnp.concatenate([r[:, None] * inv, col[:, None] * inv], axis=-1)
    cos, sin = jnp.cos(ang), jnp.sin(ang)
    reps = D_MODEL // DIFF_DH
    cos = jnp.tile(jnp.concatenate([cos, cos], axis=-1), (1, reps))
    sin = jnp.tile(jnp.concatenate([-sin, sin], axis=-1), (1, reps))
    return cos, sin, cos.T, sin.T


def _rope(x, cos, sin_signed, axis):
    half = DIFF_DH // 2
    size = x.shape[axis]
    pos = lax.broadcasted_iota(jnp.int32, x.shape, axis)
    from_above = pltpu.roll(x, size - half, axis=axis)
    from_below = pltpu.roll(x, half, axis=axis)
    swapped = jnp.where(pos % DIFF_DH < half, from_above, from_below)
    return x * cos + swapped * sin_signed


def _qkv_kernel(*refs, rope, seq):
    if rope:
        (x_ref, g_ref, sh_ref, sc_ref, wq_ref, wk_ref, wv_ref, cos_ref, sin_ref, cos_t_ref, sin_t_ref,
         q_ref, k_ref, v_ref, wq_t_ref, wv_t_ref) = refs
    else:
        (x_ref, g_ref, sh_ref, sc_ref, wq_ref, wk_ref, wv_ref, q_ref, k_ref, v_ref, kf_ref, vf_ref,
         wq_t_ref, wv_t_ref) = refs

    @pl.when(pl.program_id(0) == 0)
    def _():
        wq_t_ref[...] = wq_ref[...].astype(F32).T.astype(BF16)
        wv_t_ref[...] = wv_ref[...].astype(F32).T.astype(BF16)

    for s, rows in enumerate(_sub_tiles()):
        h32 = _norm_mod(x_ref[rows, :], g_ref[...], sh_ref[...], sc_ref[...])
        h = h32.astype(BF16)
        h_t = h32.T.astype(BF16)
        q_t = _dot(wq_t_ref[...], h_t)
        v_t = _dot(wv_t_ref[...], h_t)
        k = _dot(h, wk_ref[...])
        if rope:
            q_t = _rope(q_t, cos_t_ref[:, rows], sin_t_ref[:, rows], 0)
            k = _rope(k, cos_ref[rows, :], sin_ref[rows, :], 1)
        else:
            for b in range(SUB_TILE // seq):
                kf_ref[rows.start // seq + b] = k[b * seq:(b + 1) * seq].T
            v = v_t.T
            for head in range(DIFF_HEADS):
                vf_ref[pl.ds(rows.start * DIFF_HEADS + head, SUB_TILE, stride=DIFF_HEADS), :] = (
                    v[:, head * DIFF_DV:(head + 1) * DIFF_DV])
        q_b = (q_t * (DIFF_DH ** -0.5 * LOG2E)).astype(BF16)
        k_b = k.astype(BF16)
        v_b = v_t.astype(BF16)
        for head in range(DIFF_HEADS):
            f = slice(head * 2 * DIFF_DH, (head + 1) * 2 * DIFF_DH)
            q_ref[s, head] = q_b[f, :]
            if seq >= ROW_TILE:
                k_ref[head, rows, :] = k_b[:, f]
                v_ref[head, :, rows] = v_b[f, :]
            else:
                for b in range(SUB_TILE // seq):
                    k_ref[rows.start // seq + b, head] = k_b[b * seq:(b + 1) * seq, f]
                    v_ref[rows.start // seq + b, head] = v_b[f, b * seq:(b + 1) * seq]


def _qkv(x, mod, layer, rows_per_batch, seq, g, w, tables=None):
    rows = x.shape[0]
    rope = tables is not None
    per_tile = ROW_TILE // SUB_TILE
    tiles = max(seq // ROW_TILE, 1)
    in_specs = [_row_spec(D_MODEL), _resident((1, D_MODEL)), _mod_spec(layer, 0, rows_per_batch),
                _mod_spec(layer, 1, rows_per_batch), _resident((D_MODEL, D_MODEL), (0, 0)),
                _resident((D_MODEL, D_MODEL), (0, 1)), _resident((D_MODEL, D_MODEL), (0, 2))]
    args = [x, g, mod, mod, w, w, w]
    dh2 = 2 * DIFF_DH
    if seq >= ROW_TILE:
        k_spec = pl.BlockSpec((None, DIFF_HEADS, ROW_TILE, dh2), lambda i: (i // tiles, 0, i % tiles, 0))
        v_spec = pl.BlockSpec((None, DIFF_HEADS, DIFF_DV, ROW_TILE), lambda i: (i // tiles, 0, 0, i % tiles))
    else:
        k_spec = pl.BlockSpec((ROW_TILE // seq, DIFF_HEADS, seq, dh2), lambda i: (i, 0, 0, 0))
        v_spec = pl.BlockSpec((ROW_TILE // seq, DIFF_HEADS, DIFF_DV, seq), lambda i: (i, 0, 0, 0))
    out_specs = [pl.BlockSpec((per_tile, DIFF_HEADS, dh2, SUB_TILE), lambda i: (i, 0, 0, 0)), k_spec, v_spec]
    out_shape = [jax.ShapeDtypeStruct((rows // SUB_TILE, DIFF_HEADS, dh2, SUB_TILE), BF16),
                 jax.ShapeDtypeStruct((rows // seq, DIFF_HEADS, seq, dh2), BF16),
                 jax.ShapeDtypeStruct((rows // seq, DIFF_HEADS, DIFF_DV, seq), BF16)]
    if rope:
        tab = pl.BlockSpec((ROW_TILE, D_MODEL), lambda i: (i % tiles, 0))
        tab_t = pl.BlockSpec((D_MODEL, ROW_TILE), lambda i: (0, i % tiles))
        in_specs += [tab, tab, tab_t, tab_t]
        args += list(tables)
    else:
        out_specs += [pl.BlockSpec((ROW_TILE // seq, D_MODEL, seq), lambda i: (i, 0, 0)),
                      pl.BlockSpec((ROW_TILE * DIFF_HEADS, DIFF_DV), lambda i: (i, 0))]
        out_shape += [jax.ShapeDtypeStruct((rows // seq, D_MODEL, seq), F32),
                      jax.ShapeDtypeStruct((rows * DIFF_HEADS, DIFF_DV), F32)]
    return pl.pallas_call(
        functools.partial(_qkv_kernel, rope=rope, seq=seq),
        grid=(rows // ROW_TILE,),
        in_specs=in_specs,
        out_specs=out_specs,
        out_shape=out_shape,
        scratch_shapes=[pltpu.VMEM((D_MODEL, D_MODEL), BF16), pltpu.VMEM((D_MODEL, D_MODEL), BF16)],
        compiler_params=_params("arbitrary"),
        name="diff_qkv",
    )(*args)


def _attn_kernel(*refs, has_cache, lam_init, n_q_tiles, q_tile):
    if has_cache:
        (lq_ref, lk_ref, q_ref, k_ref, v_ref, gh_ref, kc_ref, vc_ref, y_ref, *scratch) = refs
        kcb_ref, vcb_ref = scratch[10:]
    else:
        (lq_ref, lk_ref, q_ref, k_ref, v_ref, gh_ref, y_ref, *scratch) = refs
    s_refs, m_refs, l_refs, yb_ref = scratch[0:3], scratch[3:6], scratch[6:9], scratch[9]
    e = jnp.exp(jnp.sum(lq_ref[...] * lk_ref[...], axis=1, keepdims=True))
    lam = e[0:1, :] - e[1:2, :] + lam_init
    first_map = lax.broadcasted_iota(jnp.int32, (2 * DIFF_DH, q_tile), 0) < DIFF_DH
    gh = gh_ref[...] * (1.0 - lam_init)

    def features(head):
        return slice(head * 2 * DIFF_DH, (head + 1) * 2 * DIFF_DH)

    if has_cache:
        for head in range(DIFF_HEADS):
            kcb_ref[head] = kc_ref[head].T.astype(BF16)
            vcb_ref[head] = vc_ref[:, features(head)].astype(F32).T.astype(BF16)

    n_new = k_ref.shape[1] // KEY_BLOCK
    n_past = kcb_ref.shape[1] // KEY_BLOCK if has_cache else 0
    n_blocks = n_past + n_new
    n_items = n_q_tiles * DIFF_HEADS
    assert (n_items - 2) % 3 == 0 and n_items >= 2

    def split(item):
        if isinstance(item, int):
            return item // DIFF_HEADS, item % DIFF_HEADS
        return lax.shift_right_logical(item, DIFF_HEADS.bit_length() - 1), item & (DIFF_HEADS - 1)

    def key_block(head, j):
        if j < n_past:
            rows = slice(j * KEY_BLOCK, (j + 1) * KEY_BLOCK)
            return kcb_ref[head, rows, :], vcb_ref[head, :, rows]
        rows = slice((j - n_past) * KEY_BLOCK, (j - n_past + 1) * KEY_BLOCK)
        return k_ref[head, rows, :], v_ref[head, :, rows]

    def by_sublane(x):
        return x.reshape(KEY_BLOCK // 8, 8, 2 * q_tile)

    def scores(item, slot):
        t, head = split(item)
        q_t = q_ref[t, head]
        zero = jnp.zeros_like(q_t)
        qq = jnp.concatenate([jnp.where(first_map, q_t, zero), jnp.where(first_map, zero, q_t)], axis=1)
        m = None
        for j in range(n_blocks):
            s = _dot(key_block(head, j)[0], qq)
            s_refs[slot][j] = s
            top = jnp.max(by_sublane(s), axis=0)
            m = top if m is None else jnp.maximum(m, top)
        m_refs[slot][...] = jnp.max(m, axis=0, keepdims=True)

    def exponentials(slot):
        m = m_refs[slot][...]
        denom = None
        for j in range(n_blocks):
            p = jnp.exp2(s_refs[slot][j] - m)
            s_refs[slot][j] = p
            part = jnp.sum(by_sublane(p), axis=0)
            denom = part if denom is None else denom + part
        l_refs[slot][...] = jnp.sum(denom, axis=0, keepdims=True)

    def values(item, slot):
        t, head = split(item)
        denom = l_refs[slot][...]
        ratio = lam * denom[:, :q_tile] / denom[:, q_tile:]
        o = None
        for j in range(n_blocks):
            p = s_refs[slot][j]
            w = (p[:, :q_tile] - p[:, q_tile:] * ratio).astype(BF16)
            part = _dot(key_block(head, j)[1], w)
            o = part if o is None else o + part
        o = o * (1.0 / denom[:, :q_tile])
        o = o * lax.rsqrt(jnp.mean(o * o, axis=0, keepdims=True) + EPS) * gh
        yb_ref[head, pl.ds(pl.multiple_of(t * q_tile, q_tile), q_tile), :] = o.T.astype(BF16)

    scores(0, 0)
    exponentials(0)
    scores(1, 1)

    def body(trip, carry):
        first = 2 + 3 * trip
        for r in range(3):
            values(first + r - 2, r)
            exponentials((r + 1) % 3)
            scores(first + r, (r + 2) % 3)
        return carry

    lax.fori_loop(0, (n_items - 2) // 3, body, 0)
    values(n_items - 2, (n_items - 2) % 3)
    exponentials((n_items - 1) % 3)
    values(n_items - 1, (n_items - 1) % 3)
    for head in range(DIFF_HEADS):
        y_ref[:, features(head)] = yb_ref[head]


def _attn(q, k, v, lam_q, lam_k, g_head, lam_init, batch, seq, cache_k=None, cache_v=None):
    has_cache = cache_k is not None
    q_tile = SUB_TILE
    n_q_tiles = seq // q_tile
    dh2 = 2 * DIFF_DH
    tok = pl.BlockSpec((seq, D_MODEL), lambda b: (b, 0))
    small = lambda shape: pl.BlockSpec(shape, lambda b: (0, 0))
    in_specs = [small((2, DIFF_DH)), small((2, DIFF_DH)),
                pl.BlockSpec((n_q_tiles, DIFF_HEADS, dh2, q_tile), lambda b: (b, 0, 0, 0)),
                pl.BlockSpec((None, DIFF_HEADS, seq, dh2), lambda b: (b, 0, 0, 0)),
                pl.BlockSpec((None, DIFF_HEADS, DIFF_DV, seq), lambda b: (b, 0, 0, 0)),
                small((DIFF_DV, 1))]
    args = [lam_q, lam_k, q, k, v, g_head]
    past = cache_k.shape[-1] if has_cache else 0
    assert seq % KEY_BLOCK == 0 and past % KEY_BLOCK == 0
    scratch = ([pltpu.VMEM(((seq + past) // KEY_BLOCK, KEY_BLOCK, 2 * q_tile), F32)] * 3
               + [pltpu.VMEM((1, 2 * q_tile), F32)] * 6
               + [pltpu.VMEM((DIFF_HEADS, seq, DIFF_DV), BF16)])
    if has_cache:
        in_specs += [pl.BlockSpec((None, DIFF_HEADS, 2 * DIFF_DH, past), lambda b: (b, 0, 0, 0)),
                     pl.BlockSpec((None, past, D_MODEL), lambda b: (b, 0, 0))]
        args += [cache_k, cache_v]
        scratch += [pltpu.VMEM((DIFF_HEADS, past, 2 * DIFF_DH), BF16), pltpu.VMEM((DIFF_HEADS, DIFF_DV, past), BF16)]
    return pl.pallas_call(
        functools.partial(_attn_kernel, has_cache=has_cache, lam_init=lam_init, n_q_tiles=n_q_tiles, q_tile=q_tile),
        grid=(batch,),
        in_specs=in_specs,
        out_specs=tok,
        out_shape=jax.ShapeDtypeStruct((batch * seq, D_MODEL), BF16),
        scratch_shapes=scratch,
        compiler_params=_params("arbitrary"),
        name="diff_attn",
    )(*args)


def kernel(x_prompt, x_sample, state_gla_fwd, state_gla_bwd, cache_diff_k, cache_diff_v, c, c_ctx, w_ada, b_ada, g_mix_norm, g_ffn_norm, w_gla_in, w_gla_g1, w_gla_g2, b_gla_g, g_gla_head, w_gla_out, w_diff_qkv, lam_q, lam_k, g_diff_head, w_diff_out, w_ffn_gate, w_ffn_up, w_ffn_down, g_final):
    bp, tp, _ = x_prompt.shape
    bs, ts, _ = x_sample.shape
    assert SUB_TILE % tp == 0 and ts % ROW_TILE == 0 and (bp * tp) % ROW_TILE == 0
    assert tp % GLA_GROUP == 0 and ts % GLA_GROUP == 0
    xp = x_prompt.reshape(bp * tp, D_MODEL)
    xs = x_sample.reshape(bs * ts, D_MODEL)

    cond = jnp.concatenate([c_ctx[None, :], c, jnp.zeros((MOD_ROWS - 1 - bs, D_MODEL), F32)], axis=0)
    mod = _ada(cond, w_ada, b_ada).reshape(DEPTH, MOD_ROWS, N_MOD, 1, D_MODEL)
    wg, wu, wd = w_ffn_gate.astype(BF16), w_ffn_up.astype(BF16), w_ffn_down.astype(BF16)

    j = 0
    w_in = w_gla_in[j].astype(BF16)
    w1 = jnp.concatenate([w_gla_g1[j, 0], w_gla_g1[j, 1], jnp.zeros((D_MODEL, GATE_PAD - 2 * GLA_RANK), F32)],
                         axis=1).astype(BF16)
    w2 = jnp.zeros((GATE_PAD, 2 * GLA_HK), F32)
    w2 = w2.at[:GLA_RANK, :GLA_HK].set(w_gla_g2[j, 0]).at[GLA_RANK:2 * GLA_RANK, GLA_HK:].set(w_gla_g2[j, 1]).astype(BF16)
    bg = b_gla_g[j].reshape(1, 2 * GLA_HK)
    g_mix = g_mix_norm[0].reshape(1, D_MODEL)
    g_head = g_gla_head[j].reshape(1, GLA_DV)
    g_ffn = g_ffn_norm[0].reshape(1, D_MODEL)
    w_out = w_gla_out[j].astype(BF16)

    yp, new_f, new_b = _gla(xp, mod, 0, lambda b: 0, bp, tp, g_mix, w_in, w1, w2, bg, g_head)
    ys = _gla(xs, mod, 0, lambda b: 1 + b, bs, ts, g_mix, w_in, w1, w2, bg, g_head,
              state_gla_fwd[:, j], state_gla_bwd[:, j])
    xp = _out_ffn(xp, yp, mod, 0, None, w_out, g_ffn, wg, wu, wd)
    xs = _out_ffn(xs, ys, mod, 0, ts, w_out, g_ffn, wg, wu, wd)

    lam_init = 0.8 - 0.6 * math.exp(-0.3 * 1)
    g_mix = g_mix_norm[1].reshape(1, D_MODEL)
    w_qkv = w_diff_qkv[j].astype(BF16)
    g_head = g_diff_head[j].reshape(DIFF_DV, 1)
    g_ffn = g_ffn_norm[1].reshape(1, D_MODEL)
    w_out = w_diff_out[j].astype(BF16)

    qp, kp, vp, kp32, vp32 = _qkv(xp, mod, 1, None, tp, g_mix, w_qkv)
    qs, ks, vs = _qkv(xs, mod, 1, ts, ts, g_mix, w_qkv, _rope_tables(ts))
    yp = _attn(qp, kp, vp, lam_q[j], lam_k[j], g_head, lam_init, bp, tp)
    past = cache_diff_k.shape[2]
    cache_k_t = jnp.transpose(cache_diff_k[:, j], (0, 2, 3, 4, 1)).reshape(bs, DIFF_HEADS, 2 * DIFF_DH, past)
    cache_v = cache_diff_v[:, j].reshape(bs, past, D_MODEL).astype(BF16)
    ys = _attn(qs, ks, vs, lam_q[j], lam_k[j], g_head, lam_init, bs, ts, cache_k_t, cache_v)
    g_fin = g_final.reshape(1, D_MODEL)
    yp_out = _out_ffn(xp, yp, mod, 1, None, w_out, g_ffn, wg, wu, wd, g_final=g_fin)
    ys_out = _out_ffn(xs, ys, mod, 1, ts, w_out, g_ffn, wg, wu, wd, g_final=g_fin)

    return (yp_out.reshape(bp, tp, D_MODEL),
            ys_out.reshape(bs, ts, D_MODEL),
            new_f[:, None],
            new_b[:, None],
            jnp.transpose(kp32.reshape(bp, 1, DIFF_HEADS, 2, DIFF_DH, tp), (0, 1, 5, 2, 3, 4)),
            vp32.reshape(bp, 1, tp, DIFF_HEADS, DIFF_DV))
```

```python
import functools
import math

import jax
import jax.numpy as jnp
from jax import lax
from jax.experimental import pallas as pl
from jax.experimental.pallas import tpu as pltpu

F32 = jnp.float32
BF16 = jnp.bfloat16

D_MODEL = 1024
DEPTH = 2
GRID_W = 64
GLA_HEADS = 4
GLA_DK = 128
GLA_DV = 256
GLA_HK = GLA_HEADS * GLA_DK
GLA_HV = GLA_HEADS * GLA_DV
GLA_RANK = 16
GLA_TAU = 16.0
GLA_CHUNK = 64
GLA_GROUP = 256
GLA_CHAINS = 8
DIFF_HEADS = 8
DIFF_DH = 64
DIFF_DV = 128
ROPE_THETA = 10000.0
D_FF = 2816
EPS = 1e-6
LOG2E = math.log2(math.e)
N_MOD = 6
MOD_ROWS = 16
GATE_PAD = 128

VMEM_LIMIT_BYTES = 56 * 1024 * 1024
ROW_TILE = 512
SUB_TILE = 512
Q_TILE = 256


def _params(*sem):
    return pltpu.CompilerParams(dimension_semantics=sem, vmem_limit_bytes=VMEM_LIMIT_BYTES)


def _resident(shape, index=None):
    index = (0,) * len(shape) if index is None else index
    return pl.BlockSpec(shape, lambda *_: index, pipeline_mode=pl.Buffered(1))


def _sub_tiles():
    return [slice(s * SUB_TILE, (s + 1) * SUB_TILE) for s in range(ROW_TILE // SUB_TILE)]


def _sigmoid(x):
    return 1.0 / (1.0 + jnp.exp(-x))


def _silu(x):
    return x * _sigmoid(x)


def _rms(x, g):
    return x * lax.rsqrt(jnp.mean(x * x, axis=-1, keepdims=True) + EPS) * g


def _norm_mod(x, g, shift, scale):
    return _rms(x, g) * (1.0 + scale) + shift


def _dot(a, b):
    return jnp.dot(a, b, preferred_element_type=F32)


def _dot_nt(a, b):
    return lax.dot_general(a, b, (((1,), (1,)), ((), ())), preferred_element_type=F32)


def _dot_tn(a, b):
    return lax.dot_general(a, b, (((0,), (0,)), ((), ())), preferred_element_type=F32)


def _ada_kernel(c_ref, w_ref, b_ref, o_ref):
    a = _silu(c_ref[...]).astype(BF16)
    o_ref[...] = _dot(a, w_ref[...].astype(BF16)) + b_ref[...]


def _ada(cond, w_ada, b_ada):
    tn = 1536
    n = N_MOD * D_MODEL
    return pl.pallas_call(
        _ada_kernel,
        grid=(DEPTH, n // tn),
        in_specs=[
            pl.BlockSpec((MOD_ROWS, D_MODEL), lambda l, j: (0, 0)),
            pl.BlockSpec((None, D_MODEL, tn), lambda l, j: (l, 0, j)),
            pl.BlockSpec((None, 1, tn), lambda l, j: (l, 0, j)),
        ],
        out_specs=pl.BlockSpec((None, MOD_ROWS, tn), lambda l, j: (l, 0, j)),
        out_shape=jax.ShapeDtypeStruct((DEPTH, MOD_ROWS, n), F32),
        compiler_params=_params("arbitrary", "arbitrary"),
        name="ada",
    )(cond, w_ada, b_ada.reshape(DEPTH, 1, n))


def _mod_spec(layer, which, rows_per_batch):
    if rows_per_batch is None:
        return pl.BlockSpec((None, None, None, 1, D_MODEL), lambda i: (layer, 0, which, 0, 0))
    tiles = rows_per_batch // ROW_TILE
    return pl.BlockSpec((None, None, None, 1, D_MODEL), lambda i: (layer, 1 + i // tiles, which, 0, 0))


def _row_spec(width):
    return pl.BlockSpec((ROW_TILE, width), lambda i: (i, 0))


def _log_sigmoid(x):
    return jnp.minimum(x, 0.0) - jnp.log(1.0 + jnp.exp(-jnp.abs(x)))


def _gla_in_kernel(x_ref, g_ref, sh_ref, sc_ref, w_ref, w1_ref, w2_ref, bg_ref, q_ref, k_ref, v_ref, r_ref, la_ref):
    for rows in _sub_tiles():
        h = _norm_mod(x_ref[rows, :], g_ref[...], sh_ref[...], sc_ref[...]).astype(BF16)
        z = _dot(h, w_ref[...])
        q_ref[rows, :] = z[:, :GLA_HK]
        k_ref[rows, :] = z[:, GLA_HK:2 * GLA_HK]
        v_ref[rows, :] = z[:, 2 * GLA_HK:2 * GLA_HK + GLA_HV].astype(BF16)
        r_ref[rows, :] = z[:, 2 * GLA_HK + GLA_HV:]
        low_rank = _dot(h, w1_ref[...]).astype(BF16)
        pre = _dot(low_rank, w2_ref[...]) + bg_ref[...]
        la_ref[rows, :] = _log_sigmoid(pre) * (1.0 / GLA_TAU)


def _gla_in(x, mod, layer, rows_per_batch, g, w_in, w1, w2, bg):
    rows = x.shape[0]
    return pl.pallas_call(
        _gla_in_kernel,
        grid=(rows // ROW_TILE,),
        in_specs=[
            _row_spec(D_MODEL),
            _resident((1, D_MODEL)),
            _mod_spec(layer, 0, rows_per_batch),
            _mod_spec(layer, 1, rows_per_batch),
            _resident((D_MODEL, 2 * GLA_HK + 2 * GLA_HV)),
            _resident((D_MODEL, GATE_PAD)),
            _resident((GATE_PAD, 2 * GLA_HK)),
            _resident((1, 2 * GLA_HK)),
        ],
        out_specs=[_row_spec(GLA_HK), _row_spec(GLA_HK), _row_spec(GLA_HV), _row_spec(GLA_HV), _row_spec(2 * GLA_HK)],
        out_shape=[
            jax.ShapeDtypeStruct((rows, GLA_HK), F32),
            jax.ShapeDtypeStruct((rows, GLA_HK), F32),
            jax.ShapeDtypeStruct((rows, GLA_HV), BF16),
            jax.ShapeDtypeStruct((rows, GLA_HV), F32),
            jax.ShapeDtypeStruct((rows, 2 * GLA_HK), F32),
        ],
        compiler_params=_params("arbitrary"),
        name="gla_in",
    )(x, g, mod, mod, w_in, w1, w2, bg)


def _split3(x):
    hi = x.astype(BF16)
    rem = x - hi.astype(F32)
    mid = rem.astype(BF16)
    lo = (rem - mid.astype(F32)).astype(BF16)
    return jnp.concatenate([hi, mid, lo], axis=1)


def _fold3(x):
    return x[:, :GLA_DK] + x[:, GLA_DK:2 * GLA_DK] + x[:, 2 * GLA_DK:]


def _gla_scan_kernel(*refs, n_groups, hps, has_s0):
    C, DK, DV, G = GLA_CHUNK, GLA_DK, GLA_DV, GLA_GROUP
    cpg = G // C
    n_chunks = n_groups * cpg
    unroll = max(1, min(n_groups, GLA_CHAINS // hps))
    if has_s0:
        (q_ref, k_ref, v_ref, laf_ref, lab_ref, r_ref, gh_ref, s0f_ref, s0b_ref,
         y_ref, qd_ref, kd_ref, ds_ref, sin_ref, dec_ref) = refs
    else:
        (q_ref, k_ref, v_ref, laf_ref, lab_ref, r_ref, gh_ref,
         y_ref, sf_ref, sb_ref, qd_ref, kd_ref, ds_ref, sin_ref, dec_ref) = refs

    row = lax.broadcasted_iota(jnp.int32, (G, G), 0)
    col = lax.broadcasted_iota(jnp.int32, (G, G), 1)
    same_chunk = (row // C) == (col // C)
    lower = same_chunk & (row >= col)
    upper = same_chunk & (row <= col)
    prefix_sum = jnp.where(lower, 1.0, 0.0).astype(BF16)
    suffix_sum = jnp.where(upper, 1.0, 0.0).astype(BF16)

    def group(g):
        return pl.ds(pl.multiple_of(g * G, G), G)

    def spread(x, first_row):
        return jnp.concatenate(
            [jnp.broadcast_to(x[first_row + c * C:first_row + c * C + 1], (C, x.shape[1])) for c in range(cpg)], axis=0)

    def pass1(g, carry):
        sl = group(g)
        for hd in range(hps):
            dk = slice(hd * DK, (hd + 1) * DK)
            dv = slice(hd * DV, (hd + 1) * DV)
            both = slice(hd * 2 * DK, (hd + 1) * 2 * DK)
            bf = _fold3(_dot(prefix_sum, _split3(laf_ref[sl, dk])))
            bb = _fold3(_dot(suffix_sum, _split3(lab_ref[sl, dk])))
            tot_f = spread(bf, C - 1)
            tot_b = spread(bb, 0)
            q = q_ref[sl, dk] * (DK ** -0.5)
            k = k_ref[sl, dk]
            qd_ref[sl, both] = jnp.concatenate([q * jnp.exp(bf), q * jnp.exp(bb)], axis=1).astype(BF16)
            kd_ref[sl, both] = jnp.concatenate([k * jnp.exp(-bf), k * jnp.exp(-bb)], axis=1).astype(BF16)
            k_end = jnp.concatenate([k * jnp.exp(tot_f - bf), k * jnp.exp(tot_b - bb)], axis=1).astype(BF16)
            v = v_ref[sl, dv]
            for c in range(cpg):
                rows = slice(c * C, (c + 1) * C)
                ds_ref[hd, g * cpg + c] = _dot_tn(v[rows], k_end[rows])
                dec_ref[hd, g * cpg + c] = jnp.exp(
                    jnp.concatenate([tot_f[c * C:c * C + 1], tot_b[c * C:c * C + 1]], axis=1))
        return carry

    lax.fori_loop(0, n_groups, pass1, 0, unroll=unroll)

    for hd in range(hps):
        for forward in (True, False):
            lanes = slice(0, DK) if forward else slice(DK, 2 * DK)
            if has_s0:
                init = (s0f_ref if forward else s0b_ref)[hd].T
            else:
                init = jnp.zeros((DV, DK), F32)

            def step(i, s, hd=hd, forward=forward, lanes=lanes):
                n = i if forward else n_chunks - 1 - i
                sin_ref[hd, n, :, lanes] = s.astype(BF16)
                return s * dec_ref[hd, n, :, lanes] + ds_ref[hd, n, :, lanes]

            final = lax.fori_loop(0, n_chunks, step, init)
            if not has_s0:
                (sf_ref if forward else sb_ref)[hd] = final.T

    gh = gh_ref[...]

    def pass3(g, carry):
        sl = group(g)
        for hd in range(hps):
            dv = slice(hd * DV, (hd + 1) * DV)
            both = slice(hd * 2 * DK, (hd + 1) * 2 * DK)
            qd = qd_ref[sl, both]
            kd = kd_ref[sl, both]
            a_f = _dot_nt(qd[:, :DK], kd[:, :DK])
            a_b = _dot_nt(qd[:, DK:], kd[:, DK:])
            a = jnp.where(lower, a_f, 0.0) + jnp.where(upper, a_b, 0.0)
            inter = [_dot_nt(qd[c * C:(c + 1) * C], sin_ref[hd, g * cpg + c]) for c in range(cpg)]
            o = _dot(a.astype(BF16), v_ref[sl, dv]) + jnp.concatenate(inter, axis=0)
            y_ref[sl, dv] = (_rms(o, gh) * _silu(r_ref[sl, dv])).astype(BF16)
        return carry

    lax.fori_loop(0, n_groups, pass3, 0, unroll=unroll)


def _gla_scan(q, k, v, la, r, g_head, batch, seq, hps, s0f=None, s0b=None):
    steps = GLA_HEADS // hps
    n_chunks = seq // GLA_CHUNK
    has_s0 = s0f is not None
    rows = batch * seq
    tok = lambda w, off: pl.BlockSpec((seq, w * hps), lambda b, h: (b, h + off))
    state = pl.BlockSpec((None, hps, GLA_DK, GLA_DV), lambda b, h: (b, h, 0, 0))
    in_specs = [tok(GLA_DK, 0), tok(GLA_DK, 0), tok(GLA_DV, 0), tok(GLA_DK, 0), tok(GLA_DK, steps),
                tok(GLA_DV, 0), pl.BlockSpec((1, GLA_DV), lambda b, h: (0, 0))]
    args = [q, k, v, la, la, r, g_head]
    y_shape = jax.ShapeDtypeStruct((rows, GLA_HV), BF16)
    st_shape = jax.ShapeDtypeStruct((batch, GLA_HEADS, GLA_DK, GLA_DV), F32)
    scratch = [
        pltpu.VMEM((seq, hps * 2 * GLA_DK), BF16),
        pltpu.VMEM((seq, hps * 2 * GLA_DK), BF16),
        pltpu.VMEM((hps, n_chunks, GLA_DV, 2 * GLA_DK), F32),
        pltpu.VMEM((hps, n_chunks, GLA_DV, 2 * GLA_DK), BF16),
        pltpu.VMEM((hps, n_chunks, 1, 2 * GLA_DK), F32),
    ]
    if has_s0:
        in_specs += [state, state]
        args += [s0f, s0b]
        out_specs = tok(GLA_DV, 0)
        out_shape = y_shape
    else:
        out_specs = [tok(GLA_DV, 0), state, state]
        out_shape = [y_shape, st_shape, st_shape]
    return pl.pallas_call(
        functools.partial(_gla_scan_kernel, n_groups=seq // GLA_GROUP, hps=hps, has_s0=has_s0),
        grid=(batch, steps),
        in_specs=in_specs,
        out_specs=out_specs,
        out_shape=out_shape,
        scratch_shapes=scratch,
        compiler_params=_params("arbitrary", "arbitrary"),
        name="gla_scan",
    )(*args)


def _out_ffn_kernel(*refs, final_norm):
    if final_norm:
        (x_ref, y_ref, wo_ref, g1_ref, gn_ref, sh_ref, sc_ref, g2_ref, wg_ref, wu_ref, wd_ref, gf_ref, o_ref) = refs
    else:
        (x_ref, y_ref, wo_ref, g1_ref, gn_ref, sh_ref, sc_ref, g2_ref, wg_ref, wu_ref, wd_ref, o_ref) = refs
    for rows in _sub_tiles():
        x = x_ref[rows, :] + g1_ref[...] * _dot(y_ref[rows, :], wo_ref[...])
        h = _norm_mod(x, gn_ref[...], sh_ref[...], sc_ref[...]).astype(BF16)
        act = (_silu(_dot(h, wg_ref[...])) * _dot(h, wu_ref[...])).astype(BF16)
        x = x + g2_ref[...] * _dot(act, wd_ref[...])
        if final_norm:
            x = _rms(x, gf_ref[...])
        o_ref[rows, :] = x


def _out_ffn(x, y, mod, layer, rows_per_batch, w_out, g_ffn, wg, wu, wd, g_final=None):
    rows = x.shape[0]
    final_norm = g_final is not None
    in_specs = [
        _row_spec(D_MODEL),
        _row_spec(D_MODEL),
        _resident((D_MODEL, D_MODEL)),
        _mod_spec(layer, 2, rows_per_batch),
        _resident((1, D_MODEL)),
        _mod_spec(layer, 3, rows_per_batch),
        _mod_spec(layer, 4, rows_per_batch),
        _mod_spec(layer, 5, rows_per_batch),
        _resident((None, D_MODEL, D_FF), (layer, 0, 0)),
        _resident((None, D_MODEL, D_FF), (layer, 0, 0)),
        _resident((None, D_FF, D_MODEL), (layer, 0, 0)),
    ]
    args = [x, y, w_out, mod, g_ffn, mod, mod, mod, wg, wu, wd]
    if final_norm:
        in_specs.append(_resident((1, D_MODEL)))
        args.append(g_final)
    return pl.pallas_call(
        functools.partial(_out_ffn_kernel, final_norm=final_norm),
        grid=(rows // ROW_TILE,),
        in_specs=in_specs,
        out_specs=_row_spec(D_MODEL),
        out_shape=jax.ShapeDtypeStruct((rows, D_MODEL), F32),
        compiler_params=_params("arbitrary"),
        name="out_ffn",
    )(*args)


def _rope_tables(n_tokens):
    rows = n_tokens // GRID_W
    r, col = jnp.meshgrid(jnp.arange(rows), jnp.arange(GRID_W), indexing="ij")
    r = r.reshape(-1).astype(F32)
    col = col.reshape(-1).astype(F32)
    n_freq = DIFF_DH // 4
    inv = ROPE_THETA ** (-jnp.arange(n_freq, dtype=F32) / n_freq)
    ang = jnp.concatenate([r[:, None] * inv, col[:, None] * inv], axis=-1)
    cos, sin = jnp.cos(ang), jnp.sin(ang)
    reps = D_MODEL // DIFF_DH
    return jnp.tile(jnp.concatenate([cos, cos], axis=-1), (1, reps)), jnp.tile(jnp.concatenate([-sin, sin], axis=-1), (1, reps))


def _rope(x, cos, sin_signed):
    half = DIFF_DH // 2
    width = x.shape[1]
    lane = lax.broadcasted_iota(jnp.int32, x.shape, 1)
    from_above = pltpu.roll(x, width - half, axis=1)
    from_below = pltpu.roll(x, half, axis=1)
    swapped = jnp.where(lane % DIFF_DH < half, from_above, from_below)
    return x * cos + swapped * sin_signed


def _qkv_kernel(*refs, rope, seq):
    if rope:
        (x_ref, g_ref, sh_ref, sc_ref, w_ref, cos_ref, sin_ref, q_ref, k_ref, v_ref) = refs
    else:
        (x_ref, g_ref, sh_ref, sc_ref, w_ref, q_ref, k_ref, v_ref, kf_ref, vf_ref) = refs
    for rows in _sub_tiles():
        h = _norm_mod(x_ref[rows, :], g_ref[...], sh_ref[...], sc_ref[...]).astype(BF16)
        z = _dot(h, w_ref[...])
        q = z[:, :D_MODEL]
        k = z[:, D_MODEL:2 * D_MODEL]
        v = z[:, 2 * D_MODEL:]
        if rope:
            q = _rope(q, cos_ref[rows, :], sin_ref[rows, :])
            k = _rope(k, cos_ref[rows, :], sin_ref[rows, :])
        else:
            for b in range(SUB_TILE // seq):
                kf_ref[rows.start // seq + b] = k[b * seq:(b + 1) * seq].T
            for head in range(DIFF_HEADS):
                vf_ref[pl.ds(rows.start * DIFF_HEADS + head, SUB_TILE, stride=DIFF_HEADS), :] = (
                    v[:, head * DIFF_DV:(head + 1) * DIFF_DV])
        q_ref[rows, :] = (q * (DIFF_DH ** -0.5 * LOG2E)).astype(BF16)
        k_ref[rows, :] = k.astype(BF16)
        v_ref[rows, :] = v.astype(BF16)


def _qkv(x, mod, layer, rows_per_batch, seq, g, w, tables=None):
    rows = x.shape[0]
    rope = tables is not None
    in_specs = [_row_spec(D_MODEL), _resident((1, D_MODEL)), _mod_spec(layer, 0, rows_per_batch),
                _mod_spec(layer, 1, rows_per_batch), _resident((D_MODEL, 3 * D_MODEL))]
    args = [x, g, mod, mod, w]
    out_specs = [_row_spec(D_MODEL)] * 3
    out_shape = [jax.ShapeDtypeStruct((rows, D_MODEL), BF16)] * 3
    if rope:
        tiles = seq // ROW_TILE
        tab = pl.BlockSpec((ROW_TILE, D_MODEL), lambda i: (i % tiles, 0))
        in_specs += [tab, tab]
        args += list(tables)
    else:
        per_tile = ROW_TILE // seq
        out_specs += [pl.BlockSpec((per_tile, D_MODEL, seq), lambda i: (i, 0, 0)),
                      pl.BlockSpec((ROW_TILE * DIFF_HEADS, DIFF_DV), lambda i: (i, 0))]
        out_shape += [jax.ShapeDtypeStruct((rows // seq, D_MODEL, seq), F32),
                      jax.ShapeDtypeStruct((rows * DIFF_HEADS, DIFF_DV), F32)]
    return pl.pallas_call(
        functools.partial(_qkv_kernel, rope=rope, seq=seq),
        grid=(rows // ROW_TILE,),
        in_specs=in_specs,
        out_specs=out_specs,
        out_shape=out_shape,
        compiler_params=_params("arbitrary"),
        name="diff_qkv",
    )(*args)


def _attn_kernel(*refs, has_cache, lam_init, n_q_tiles, q_tile):
    if has_cache:
        (lq_ref, lk_ref, q_ref, k_ref, v_ref, gh_ref, kc_ref, vc_ref, y_ref, kcb_ref, vcb_ref) = refs
    else:
        (lq_ref, lk_ref, q_ref, k_ref, v_ref, gh_ref, y_ref) = refs
    e = jnp.exp(jnp.sum(lq_ref[...] * lk_ref[...], axis=1, keepdims=True))
    lam = e[0:1, :] - e[1:2, :] + lam_init
    first_map = lax.broadcasted_iota(jnp.int32, (q_tile, 2 * DIFF_DH), 1) < DIFF_DH
    gh = gh_ref[...] * (1.0 - lam_init)

    if has_cache:
        for head in range(DIFF_HEADS):
            kcb_ref[head] = kc_ref[head].astype(BF16)
            vcb_ref[head] = vc_ref[:, head, :].astype(BF16)

    def head_lanes(head):
        return slice(head * 2 * DIFF_DH, (head + 1) * 2 * DIFF_DH)

    def scores(head, sl):
        q = q_ref[sl, head_lanes(head)]
        zero = jnp.zeros_like(q)
        qq = jnp.concatenate([jnp.where(first_map, q, zero), jnp.where(first_map, zero, q)], axis=0)
        s = [_dot_nt(qq, k_ref[:, head_lanes(head)])]
        if has_cache:
            s.insert(0, _dot(qq, kcb_ref[head]))
        return s

    def attend(head, sl, s):
        vals = [v_ref[:, head_lanes(head)]]
        if has_cache:
            vals.insert(0, vcb_ref[head])
        m = functools.reduce(jnp.maximum, [jnp.max(x, axis=-1, keepdims=True) for x in s])
        p = [jnp.exp2(x - m) for x in s]
        denom = functools.reduce(jnp.add, [jnp.sum(x, axis=-1, keepdims=True) for x in p])
        acc = functools.reduce(jnp.add, [_dot(x.astype(BF16), vv) for x, vv in zip(p, vals)])
        o = acc[:q_tile] * (1.0 / denom[:q_tile]) - acc[q_tile:] * (lam / denom[q_tile:])
        y_ref[sl, head_lanes(head)] = _rms(o, gh).astype(BF16)

    def body(t, carry):
        sl = pl.ds(pl.multiple_of(t * q_tile, q_tile), q_tile)
        s_next = scores(0, sl)
        for head in range(DIFF_HEADS):
            s_cur = s_next
            if head + 1 < DIFF_HEADS:
                s_next = scores(head + 1, sl)
            attend(head, sl, s_cur)
        return carry

    lax.fori_loop(0, n_q_tiles, body, 0)


def _attn(q, k, v, lam_q, lam_k, g_head, lam_init, batch, seq, cache_k=None, cache_v=None):
    has_cache = cache_k is not None
    tok = pl.BlockSpec((seq, D_MODEL), lambda b: (b, 0))
    small = lambda shape: pl.BlockSpec(shape, lambda b: (0, 0))
    in_specs = [small((2, DIFF_DH)), small((2, DIFF_DH)), tok, tok, tok, small((1, DIFF_DV))]
    args = [lam_q, lam_k, q, k, v, g_head]
    if has_cache:
        past = cache_k.shape[-1]
        in_specs += [pl.BlockSpec((None, DIFF_HEADS, 2 * DIFF_DH, past), lambda b: (b, 0, 0, 0)),
                     pl.BlockSpec((None, past, DIFF_HEADS, DIFF_DV), lambda b: (b, 0, 0, 0))]
        args += [cache_k, cache_v]
        scratch = [pltpu.VMEM((DIFF_HEADS, 2 * DIFF_DH, past), BF16), pltpu.VMEM((DIFF_HEADS, past, DIFF_DV), BF16)]
    else:
        scratch = []
    q_tile = min(Q_TILE, seq)
    return pl.pallas_call(
        functools.partial(_attn_kernel, has_cache=has_cache, lam_init=lam_init, n_q_tiles=seq // q_tile, q_tile=q_tile),
        grid=(batch,),
        in_specs=in_specs,
        out_specs=tok,
        out_shape=jax.ShapeDtypeStruct((batch * seq, D_MODEL), BF16),
        scratch_shapes=scratch,
        compiler_params=_params("arbitrary"),
        name="diff_attn",
    )(*args)


def kernel(x_prompt, x_sample, state_gla_fwd, state_gla_bwd, cache_diff_k, cache_diff_v, c, c_ctx, w_ada, b_ada, g_mix_norm, g_ffn_norm, w_gla_in, w_gla_g1, w_gla_g2, b_gla_g, g_gla_head, w_gla_out, w_diff_qkv, lam_q, lam_k, g_diff_head, w_diff_out, w_ffn_gate, w_ffn_up, w_ffn_down, g_final):
    bp, tp, _ = x_prompt.shape
    bs, ts, _ = x_sample.shape
    assert SUB_TILE % tp == 0 and ts % ROW_TILE == 0 and (bp * tp) % ROW_TILE == 0
    assert tp % GLA_GROUP == 0 and ts % GLA_GROUP == 0
    xp = x_prompt.reshape(bp * tp, D_MODEL)
    xs = x_sample.reshape(bs * ts, D_MODEL)

    cond = jnp.concatenate([c_ctx[None, :], c, jnp.zeros((MOD_ROWS - 1 - bs, D_MODEL), F32)], axis=0)
    mod = _ada(cond, w_ada, b_ada).reshape(DEPTH, MOD_ROWS, N_MOD, 1, D_MODEL)
    wg, wu, wd = w_ffn_gate.astype(BF16), w_ffn_up.astype(BF16), w_ffn_down.astype(BF16)

    j = 0
    w_in = w_gla_in[j].astype(BF16)
    w1 = jnp.concatenate([w_gla_g1[j, 0], w_gla_g1[j, 1], jnp.zeros((D_MODEL, GATE_PAD - 2 * GLA_RANK), F32)],
                         axis=1).astype(BF16)
    w2 = jnp.zeros((GATE_PAD, 2 * GLA_HK), F32)
    w2 = w2.at[:GLA_RANK, :GLA_HK].set(w_gla_g2[j, 0]).at[GLA_RANK:2 * GLA_RANK, GLA_HK:].set(w_gla_g2[j, 1]).astype(BF16)
    bg = b_gla_g[j].reshape(1, 2 * GLA_HK)
    g_mix = g_mix_norm[0].reshape(1, D_MODEL)
    g_head = g_gla_head[j].reshape(1, GLA_DV)
    g_ffn = g_ffn_norm[0].reshape(1, D_MODEL)
    w_out = w_gla_out[j].astype(BF16)

    qp, kp, vp, rp, lap = _gla_in(xp, mod, 0, None, g_mix, w_in, w1, w2, bg)
    qs, ks, vs, rs, las = _gla_in(xs, mod, 0, ts, g_mix, w_in, w1, w2, bg)
    yp, new_f, new_b = _gla_scan(qp, kp, vp, lap, rp, g_head, bp, tp, GLA_HEADS)
    ys = _gla_scan(qs, ks, vs, las, rs, g_head, bs, ts, 2, state_gla_fwd[:, j], state_gla_bwd[:, j])
    xp = _out_ffn(xp, yp, mod, 0, None, w_out, g_ffn, wg, wu, wd)
    xs = _out_ffn(xs, ys, mod, 0, ts, w_out, g_ffn, wg, wu, wd)

    lam_init = 0.8 - 0.6 * math.exp(-0.3 * 1)
    g_mix = g_mix_norm[1].reshape(1, D_MODEL)
    w_qkv = w_diff_qkv[j].astype(BF16)
    g_head = g_diff_head[j].reshape(1, DIFF_DV)
    g_ffn = g_ffn_norm[1].reshape(1, D_MODEL)
    w_out = w_diff_out[j].astype(BF16)

    qp, kp, vp, kp32, vp32 = _qkv(xp, mod, 1, None, tp, g_mix, w_qkv)
    qs, ks, vs = _qkv(xs, mod, 1, ts, ts, g_mix, w_qkv, _rope_tables(ts))
    yp = _attn(qp, kp, vp, lam_q[j], lam_k[j], g_head, lam_init, bp, tp)
    past = cache_diff_k.shape[2]
    cache_k_t = jnp.transpose(cache_diff_k[:, j], (0, 2, 3, 4, 1)).reshape(bs, DIFF_HEADS, 2 * DIFF_DH, past)
    ys = _attn(qs, ks, vs, lam_q[j], lam_k[j], g_head, lam_init, bs, ts, cache_k_t, cache_diff_v[:, j])
    g_fin = g_final.reshape(1, D_MODEL)
    yp_out = _out_ffn(xp, yp, mod, 1, None, w_out, g_ffn, wg, wu, wd, g_final=g_fin)
    ys_out = _out_ffn(xs, ys, mod, 1, ts, w_out, g_ffn, wg, wu, wd, g_final=g_fin)

    return (yp_out.reshape(bp, tp, D_MODEL),
            ys_out.reshape(bs, ts, D_MODEL),
            new_f[:, None],
            new_b[:, None],
            jnp.transpose(kp32.reshape(bp, 1, DIFF_HEADS, 2, DIFF_DH, tp), (0, 1, 5, 2, 3, 4)),
            vp32.reshape(bp, 1, tp, DIFF_HEADS, DIFF_DV))
```

```python
import functools
import math

import jax
import jax.numpy as jnp
from jax import lax
from jax.experimental import pallas as pl
from jax.experimental.pallas import tpu as pltpu

F32 = jnp.float32
BF16 = jnp.bfloat16

D_MODEL = 1024
DEPTH = 2
GRID_W = 64
GLA_HEADS = 4
GLA_DK = 128
GLA_DV = 256
GLA_HK = GLA_HEADS * GLA_DK
GLA_HV = GLA_HEADS * GLA_DV
GLA_RANK = 16
GLA_TAU = 16.0
GLA_CHUNK = 64
GLA_GROUP = 256
GLA_CHAINS = 8
DIFF_HEADS = 8
DIFF_DH = 64
DIFF_DV = 128
ROPE_THETA = 10000.0
D_FF = 2816
EPS = 1e-6
LOG2E = math.log2(math.e)
N_MOD = 6
MOD_ROWS = 16
GATE_PAD = 128

VMEM_LIMIT_BYTES = 56 * 1024 * 1024
ROW_TILE = 512
SUB_TILE = 512
Q_TILE = 256


def _params(*sem):
    return pltpu.CompilerParams(dimension_semantics=sem, vmem_limit_bytes=VMEM_LIMIT_BYTES)


def _resident(shape, index=None):
    index = (0,) * len(shape) if index is None else index
    return pl.BlockSpec(shape, lambda *_: index, pipeline_mode=pl.Buffered(1))


def _sub_tiles():
    return [slice(s * SUB_TILE, (s + 1) * SUB_TILE) for s in range(ROW_TILE // SUB_TILE)]


def _sigmoid(x):
    return 1.0 / (1.0 + jnp.exp(-x))


def _silu(x):
    return x * _sigmoid(x)


def _rms(x, g):
    return x * lax.rsqrt(jnp.mean(x * x, axis=-1, keepdims=True) + EPS) * g


def _norm_mod(x, g, shift, scale):
    return x * lax.rsqrt(jnp.mean(x * x, axis=-1, keepdims=True) + EPS) * (g * (1.0 + scale)) + shift


def _dot(a, b):
    return jnp.dot(a, b, preferred_element_type=F32)


def _dot_nt(a, b):
    return lax.dot_general(a, b, (((1,), (1,)), ((), ())), preferred_element_type=F32)


def _dot_tn(a, b):
    return lax.dot_general(a, b, (((0,), (0,)), ((), ())), preferred_element_type=F32)


def _ada_kernel(c_ref, w_ref, b_ref, o_ref):
    a = _silu(c_ref[...]).astype(BF16)
    o_ref[...] = _dot(a, w_ref[...].astype(BF16)) + b_ref[...]


def _ada(cond, w_ada, b_ada):
    tn = 1536
    n = N_MOD * D_MODEL
    return pl.pallas_call(
        _ada_kernel,
        grid=(DEPTH, n // tn),
        in_specs=[
            pl.BlockSpec((MOD_ROWS, D_MODEL), lambda l, j: (0, 0)),
            pl.BlockSpec((None, D_MODEL, tn), lambda l, j: (l, 0, j)),
            pl.BlockSpec((None, 1, tn), lambda l, j: (l, 0, j)),
        ],
        out_specs=pl.BlockSpec((None, MOD_ROWS, tn), lambda l, j: (l, 0, j)),
        out_shape=jax.ShapeDtypeStruct((DEPTH, MOD_ROWS, n), F32),
        compiler_params=_params("arbitrary", "arbitrary"),
        name="ada",
    )(cond, w_ada, b_ada.reshape(DEPTH, 1, n))


def _mod_spec(layer, which, rows_per_batch):
    if rows_per_batch is None:
        return pl.BlockSpec((None, None, None, 1, D_MODEL), lambda i: (layer, 0, which, 0, 0))
    tiles = rows_per_batch // ROW_TILE
    return pl.BlockSpec((None, None, None, 1, D_MODEL), lambda i: (layer, 1 + i // tiles, which, 0, 0))


def _row_spec(width):
    return pl.BlockSpec((ROW_TILE, width), lambda i: (i, 0))


def _log_sigmoid(x):
    return jnp.minimum(x, 0.0) - jnp.log(1.0 + jnp.exp(-jnp.abs(x)))


def _prenorm_specs(layer, rows_per_batch, n_tiles):
    nxt = lambda i: jnp.minimum(i + 1, n_tiles - 1)
    if rows_per_batch is None:
        row = lambda t: 0
    else:
        row = lambda t: 1 + t // (rows_per_batch // ROW_TILE)
    mod = lambda which, tile: pl.BlockSpec((None, None, None, 1, D_MODEL),
                                           lambda i: (layer, row(tile(i)), which, 0, 0))
    first = lambda i: 0
    return [pl.BlockSpec((ROW_TILE, D_MODEL), lambda i: (0, 0)),
            pl.BlockSpec((ROW_TILE, D_MODEL), lambda i: (nxt(i), 0)),
            _resident((1, D_MODEL)), mod(0, first), mod(1, first), mod(0, nxt), mod(1, nxt)]


def _prenorm(x0_ref, xn_ref, g_ref, sh0_ref, sc0_ref, shn_ref, scn_ref, h_ref):
    i = pl.program_id(0)

    @pl.when(i == 0)
    def _():
        h_ref[0] = _norm_mod(x0_ref[...], g_ref[...], sh0_ref[...], sc0_ref[...]).astype(BF16)

    def prepare_next():
        h_ref[(i + 1) % 2] = _norm_mod(xn_ref[...], g_ref[...], shn_ref[...], scn_ref[...]).astype(BF16)

    return i % 2, prepare_next


def _gla_in_kernel(x0_ref, xn_ref, g_ref, sh0_ref, sc0_ref, shn_ref, scn_ref, w_ref, w1_ref, w2_ref, bg_ref,
                   q_ref, k_ref, v_ref, r_ref, la_ref, h_ref):
    slot, prepare_next = _prenorm(x0_ref, xn_ref, g_ref, sh0_ref, sc0_ref, shn_ref, scn_ref, h_ref)
    h = h_ref[slot]
    z = _dot(h, w_ref[:, :2 * GLA_HK])
    q_ref[...] = z[:, :GLA_HK]
    k_ref[...] = z[:, GLA_HK:]
    low_rank = _dot(h, w1_ref[...]).astype(BF16)
    pre = _dot(low_rank, w2_ref[...]) + bg_ref[...]
    la_ref[...] = _log_sigmoid(pre) * (1.0 / GLA_TAU)
    prepare_next()
    h = h_ref[slot]
    v_ref[...] = _dot(h, w_ref[:, 2 * GLA_HK:2 * GLA_HK + GLA_HV]).astype(BF16)
    r_ref[...] = _dot(h, w_ref[:, 2 * GLA_HK + GLA_HV:])


def _gla_in(x, mod, layer, rows_per_batch, g, w_in, w1, w2, bg):
    rows = x.shape[0]
    return pl.pallas_call(
        _gla_in_kernel,
        grid=(rows // ROW_TILE,),
        in_specs=_prenorm_specs(layer, rows_per_batch, rows // ROW_TILE) + [
            _resident((D_MODEL, 2 * GLA_HK + 2 * GLA_HV)),
            _resident((D_MODEL, GATE_PAD)),
            _resident((GATE_PAD, 2 * GLA_HK)),
            _resident((1, 2 * GLA_HK)),
        ],
        out_specs=[_row_spec(GLA_HK), _row_spec(GLA_HK), _row_spec(GLA_HV), _row_spec(GLA_HV), _row_spec(2 * GLA_HK)],
        out_shape=[
            jax.ShapeDtypeStruct((rows, GLA_HK), F32),
            jax.ShapeDtypeStruct((rows, GLA_HK), F32),
            jax.ShapeDtypeStruct((rows, GLA_HV), BF16),
            jax.ShapeDtypeStruct((rows, GLA_HV), F32),
            jax.ShapeDtypeStruct((rows, 2 * GLA_HK), F32),
        ],
        scratch_shapes=[pltpu.VMEM((2, ROW_TILE, D_MODEL), BF16)],
        compiler_params=_params("arbitrary"),
        name="gla_in",
    )(x, x, g, mod, mod, mod, mod, w_in, w1, w2, bg)


def _split3(x):
    hi = x.astype(BF16)
    rem = x - hi.astype(F32)
    mid = rem.astype(BF16)
    lo = (rem - mid.astype(F32)).astype(BF16)
    return jnp.concatenate([hi, mid, lo], axis=1)


def _fold3(x):
    return x[:, :GLA_DK] + x[:, GLA_DK:2 * GLA_DK] + x[:, 2 * GLA_DK:]


def _gla_scan_kernel(*refs, n_groups, hps, has_s0):
    C, DK, DV, G = GLA_CHUNK, GLA_DK, GLA_DV, GLA_GROUP
    cpg = G // C
    n_chunks = n_groups * cpg
    unroll = max(1, min(n_groups, GLA_CHAINS // hps))
    if has_s0:
        (q_ref, k_ref, v_ref, laf_ref, lab_ref, r_ref, gh_ref, s0f_ref, s0b_ref,
         y_ref, qd_ref, kd_ref, ds_ref, sin_ref, dec_ref) = refs
    else:
        (q_ref, k_ref, v_ref, laf_ref, lab_ref, r_ref, gh_ref,
         y_ref, sf_ref, sb_ref, qd_ref, kd_ref, ds_ref, sin_ref, dec_ref) = refs

    row = lax.broadcasted_iota(jnp.int32, (G, G), 0)
    col = lax.broadcasted_iota(jnp.int32, (G, G), 1)
    same_chunk = (row // C) == (col // C)
    lower = same_chunk & (row >= col)
    upper = same_chunk & (row <= col)
    prefix_sum = jnp.where(lower, 1.0, 0.0).astype(BF16)
    suffix_sum = jnp.where(upper, 1.0, 0.0).astype(BF16)

    def group(g):
        return pl.ds(pl.multiple_of(g * G, G), G)

    def spread(x, first_row):
        return jnp.concatenate(
            [jnp.broadcast_to(x[first_row + c * C:first_row + c * C + 1], (C, x.shape[1])) for c in range(cpg)], axis=0)

    def pass1(g, carry):
        sl = group(g)
        for hd in range(hps):
            dk = slice(hd * DK, (hd + 1) * DK)
            dv = slice(hd * DV, (hd + 1) * DV)
            both = slice(hd * 2 * DK, (hd + 1) * 2 * DK)
            bf = _fold3(_dot(prefix_sum, _split3(laf_ref[sl, dk])))
            bb = _fold3(_dot(suffix_sum, _split3(lab_ref[sl, dk])))
            tot_f = spread(bf, C - 1)
            tot_b = spread(bb, 0)
            q = q_ref[sl, dk] * (DK ** -0.5)
            k = k_ref[sl, dk]
            qd_ref[sl, both] = jnp.concatenate([q * jnp.exp(bf), q * jnp.exp(bb)], axis=1).astype(BF16)
            kd_ref[sl, both] = jnp.concatenate([k * jnp.exp(-bf), k * jnp.exp(-bb)], axis=1).astype(BF16)
            k_end = jnp.concatenate([k * jnp.exp(tot_f - bf), k * jnp.exp(tot_b - bb)], axis=1).astype(BF16)
            v = v_ref[sl, dv]
            for c in range(cpg):
                rows = slice(c * C, (c + 1) * C)
                ds_ref[hd, g * cpg + c] = _dot_tn(v[rows], k_end[rows])
                dec_ref[hd, g * cpg + c] = jnp.exp(
                    jnp.concatenate([tot_f[c * C:c * C + 1], tot_b[c * C:c * C + 1]], axis=1))
        return carry

    lax.fori_loop(0, n_groups, pass1, 0, unroll=unroll)

    for hd in range(hps):
        for forward in (True, False):
            lanes = slice(0, DK) if forward else slice(DK, 2 * DK)
            if has_s0:
                init = (s0f_ref if forward else s0b_ref)[hd].T
            else:
                init = jnp.zeros((DV, DK), F32)

            def step(i, s, hd=hd, forward=forward, lanes=lanes):
                n = i if forward else n_chunks - 1 - i
                sin_ref[hd, n, :, lanes] = s.astype(BF16)
                return s * dec_ref[hd, n, :, lanes] + ds_ref[hd, n, :, lanes]

            final = lax.fori_loop(0, n_chunks, step, init)
            if not has_s0:
                (sf_ref if forward else sb_ref)[hd] = final.T

    gh = gh_ref[...]

    def pass3(g, carry):
        sl = group(g)
        for hd in range(hps):
            dv = slice(hd * DV, (hd + 1) * DV)
            both = slice(hd * 2 * DK, (hd + 1) * 2 * DK)
            qd = qd_ref[sl, both]
            kd = kd_ref[sl, both]
            a_f = _dot_nt(qd[:, :DK], kd[:, :DK])
            a_b = _dot_nt(qd[:, DK:], kd[:, DK:])
            a = jnp.where(lower, a_f, 0.0) + jnp.where(upper, a_b, 0.0)
            inter = [_dot_nt(qd[c * C:(c + 1) * C], sin_ref[hd, g * cpg + c]) for c in range(cpg)]
            o = _dot(a.astype(BF16), v_ref[sl, dv]) + jnp.concatenate(inter, axis=0)
            y_ref[sl, dv] = (_rms(o, gh) * _silu(r_ref[sl, dv])).astype(BF16)
        return carry

    lax.fori_loop(0, n_groups, pass3, 0, unroll=unroll)


def _gla_scan(q, k, v, la, r, g_head, batch, seq, hps, s0f=None, s0b=None):
    steps = GLA_HEADS // hps
    n_chunks = seq // GLA_CHUNK
    has_s0 = s0f is not None
    rows = batch * seq
    tok = lambda w, off: pl.BlockSpec((seq, w * hps), lambda b, h: (b, h + off))
    state = pl.BlockSpec((None, hps, GLA_DK, GLA_DV), lambda b, h: (b, h, 0, 0))
    in_specs = [tok(GLA_DK, 0), tok(GLA_DK, 0), tok(GLA_DV, 0), tok(GLA_DK, 0), tok(GLA_DK, steps),
                tok(GLA_DV, 0), pl.BlockSpec((1, GLA_DV), lambda b, h: (0, 0))]
    args = [q, k, v, la, la, r, g_head]
    y_shape = jax.ShapeDtypeStruct((rows, GLA_HV), BF16)
    st_shape = jax.ShapeDtypeStruct((batch, GLA_HEADS, GLA_DK, GLA_DV), F32)
    scratch = [
        pltpu.VMEM((seq, hps * 2 * GLA_DK), BF16),
        pltpu.VMEM((seq, hps * 2 * GLA_DK), BF16),
        pltpu.VMEM((hps, n_chunks, GLA_DV, 2 * GLA_DK), F32),
        pltpu.VMEM((hps, n_chunks, GLA_DV, 2 * GLA_DK), BF16),
        pltpu.VMEM((hps, n_chunks, 1, 2 * GLA_DK), F32),
    ]
    if has_s0:
        in_specs += [state, state]
        args += [s0f, s0b]
        out_specs = tok(GLA_DV, 0)
        out_shape = y_shape
    else:
        out_specs = [tok(GLA_DV, 0), state, state]
        out_shape = [y_shape, st_shape, st_shape]
    return pl.pallas_call(
        functools.partial(_gla_scan_kernel, n_groups=seq // GLA_GROUP, hps=hps, has_s0=has_s0),
        grid=(batch, steps),
        in_specs=in_specs,
        out_specs=out_specs,
        out_shape=out_shape,
        scratch_shapes=scratch,
        compiler_params=_params("arbitrary", "arbitrary"),
        name="gla_scan",
    )(*args)


def _out_ffn_kernel(*refs, final_norm):
    if final_norm:
        (x_ref, y_ref, wo_ref, g1_ref, gn_ref, sh_ref, sc_ref, g2_ref, wg_ref, wu_ref, wd_ref, gf_ref, o_ref) = refs
    else:
        (x_ref, y_ref, wo_ref, g1_ref, gn_ref, sh_ref, sc_ref, g2_ref, wg_ref, wu_ref, wd_ref, o_ref) = refs
    for rows in _sub_tiles():
        x = x_ref[rows, :] + g1_ref[...] * _dot(y_ref[rows, :], wo_ref[...])
        h = _norm_mod(x, gn_ref[...], sh_ref[...], sc_ref[...]).astype(BF16)
        act = (_silu(_dot(h, wg_ref[...])) * _dot(h, wu_ref[...])).astype(BF16)
        x = x + g2_ref[...] * _dot(act, wd_ref[...])
        if final_norm:
            x = _rms(x, gf_ref[...])
        o_ref[rows, :] = x


def _out_ffn(x, y, mod, layer, rows_per_batch, w_out, g_ffn, wg, wu, wd, g_final=None):
    rows = x.shape[0]
    final_norm = g_final is not None
    in_specs = [
        _row_spec(D_MODEL),
        _row_spec(D_MODEL),
        _resident((D_MODEL, D_MODEL)),
        _mod_spec(layer, 2, rows_per_batch),
        _resident((1, D_MODEL)),
        _mod_spec(layer, 3, rows_per_batch),
        _mod_spec(layer, 4, rows_per_batch),
        _mod_spec(layer, 5, rows_per_batch),
        _resident((None, D_MODEL, D_FF), (layer, 0, 0)),
        _resident((None, D_MODEL, D_FF), (layer, 0, 0)),
        _resident((None, D_FF, D_MODEL), (layer, 0, 0)),
    ]
    args = [x, y, w_out, mod, g_ffn, mod, mod, mod, wg, wu, wd]
    if final_norm:
        in_specs.append(_resident((1, D_MODEL)))
        args.append(g_final)
    return pl.pallas_call(
        functools.partial(_out_ffn_kernel, final_norm=final_norm),
        grid=(rows // ROW_TILE,),
        in_specs=in_specs,
        out_specs=_row_spec(D_MODEL),
        out_shape=jax.ShapeDtypeStruct((rows, D_MODEL), F32),
        compiler_params=_params("arbitrary"),
        name="out_ffn",
    )(*args)


def _rope_tables(n_tokens):
    rows = n_tokens // GRID_W
    r, col = jnp.meshgrid(jnp.arange(rows), jnp.arange(GRID_W), indexing="ij")
    r = r.reshape(-1).astype(F32)
    col = col.reshape(-1).astype(F32)
    n_freq = DIFF_DH // 4
    inv = ROPE_THETA ** (-jnp.arange(n_freq, dtype=F32) / n_freq)
    ang = jnp.concatenate([r[:, None] * inv, col[:, None] * inv], axis=-1)
    cos, sin = jnp.cos(ang), jnp.sin(ang)
    reps = D_MODEL // DIFF_DH
    return jnp.tile(jnp.concatenate([cos, cos], axis=-1), (1, reps)), jnp.tile(jnp.concatenate([-sin, sin], axis=-1), (1, reps))


def _rope(x, cos, sin_signed):
    half = DIFF_DH // 2
    width = x.shape[1]
    lane = lax.broadcasted_iota(jnp.int32, x.shape, 1)
    from_above = pltpu.roll(x, width - half, axis=1)
    from_below = pltpu.roll(x, half, axis=1)
    swapped = jnp.where(lane % DIFF_DH < half, from_above, from_below)
    return x * cos + swapped * sin_signed


def _qkv_kernel(*refs, rope, seq):
    if rope:
        (x_ref, g_ref, sh_ref, sc_ref, w_ref, cos_ref, sin_ref, q_ref, k_ref, v_ref) = refs
    else:
        (x_ref, g_ref, sh_ref, sc_ref, w_ref, q_ref, k_ref, v_ref, kf_ref, vf_ref) = refs
    for rows in _sub_tiles():
        h = _norm_mod(x_ref[rows, :], g_ref[...], sh_ref[...], sc_ref[...]).astype(BF16)
        z = _dot(h, w_ref[...])
        q = z[:, :D_MODEL]
        k = z[:, D_MODEL:2 * D_MODEL]
        v = z[:, 2 * D_MODEL:]
        if rope:
            first = (pl.program_id(0) % (seq // ROW_TILE)) * ROW_TILE + rows.start
            pos = pl.ds(pl.multiple_of(first, SUB_TILE), SUB_TILE)
            q = _rope(q, cos_ref[pos, :], sin_ref[pos, :])
            k = _rope(k, cos_ref[pos, :], sin_ref[pos, :])
        else:
            for b in range(SUB_TILE // seq):
                kf_ref[rows.start // seq + b] = k[b * seq:(b + 1) * seq].T
            for head in range(DIFF_HEADS):
                vf_ref[pl.ds(rows.start * DIFF_HEADS + head, SUB_TILE, stride=DIFF_HEADS), :] = (
                    v[:, head * DIFF_DV:(head + 1) * DIFF_DV])
        q_ref[rows, :] = (q * (DIFF_DH ** -0.5 * LOG2E)).astype(BF16)
        k_ref[rows, :] = k.astype(BF16)
        v_ref[rows, :] = v.astype(BF16)


def _qkv(x, mod, layer, rows_per_batch, seq, g, w, tables=None):
    rows = x.shape[0]
    rope = tables is not None
    in_specs = [_row_spec(D_MODEL), _resident((1, D_MODEL)), _mod_spec(layer, 0, rows_per_batch),
                _mod_spec(layer, 1, rows_per_batch), _resident((D_MODEL, 3 * D_MODEL))]
    args = [x, g, mod, mod, w]
    out_specs = [_row_spec(D_MODEL)] * 3
    out_shape = [jax.ShapeDtypeStruct((rows, D_MODEL), BF16)] * 3
    if rope:
        in_specs += [_resident((seq, D_MODEL)), _resident((seq, D_MODEL))]
        args += list(tables)
    else:
        per_tile = ROW_TILE // seq
        out_specs += [pl.BlockSpec((per_tile, D_MODEL, seq), lambda i: (i, 0, 0)),
                      pl.BlockSpec((ROW_TILE * DIFF_HEADS, DIFF_DV), lambda i: (i, 0))]
        out_shape += [jax.ShapeDtypeStruct((rows // seq, D_MODEL, seq), F32),
                      jax.ShapeDtypeStruct((rows * DIFF_HEADS, DIFF_DV), F32)]
    return pl.pallas_call(
        functools.partial(_qkv_kernel, rope=rope, seq=seq),
        grid=(rows // ROW_TILE,),
        in_specs=in_specs,
        out_specs=out_specs,
        out_shape=out_shape,
        compiler_params=_params("arbitrary"),
        name="diff_qkv",
    )(*args)


def _attn_kernel(*refs, has_cache, lam_init, n_q_tiles, q_tile):
    if has_cache:
        (lq_ref, lk_ref, q_ref, k_ref, v_ref, gh_ref, kc_ref, vc_ref, y_ref, kcb_ref, vcb_ref) = refs
    else:
        (lq_ref, lk_ref, q_ref, k_ref, v_ref, gh_ref, y_ref) = refs
    e = jnp.exp(jnp.sum(lq_ref[...] * lk_ref[...], axis=1, keepdims=True))
    lam = e[0:1, :] - e[1:2, :] + lam_init
    first_map = lax.broadcasted_iota(jnp.int32, (q_tile, 2 * DIFF_DH), 1) < DIFF_DH
    gh = gh_ref[...] * (1.0 - lam_init)

    if has_cache:
        for head in range(DIFF_HEADS):
            kcb_ref[head] = kc_ref[head].astype(BF16)
            vcb_ref[head] = vc_ref[:, head, :].astype(BF16)

    def head_lanes(head):
        return slice(head * 2 * DIFF_DH, (head + 1) * 2 * DIFF_DH)

    def scores(head, sl):
        q = q_ref[sl, head_lanes(head)]
        zero = jnp.zeros_like(q)
        qq = jnp.concatenate([jnp.where(first_map, q, zero), jnp.where(first_map, zero, q)], axis=0)
        s = [_dot_nt(qq, k_ref[:, head_lanes(head)])]
        if has_cache:
            s.insert(0, _dot(qq, kcb_ref[head]))
        return s

    def attend(head, sl, s):
        vals = [v_ref[:, head_lanes(head)]]
        if has_cache:
            vals.insert(0, vcb_ref[head])
        m = functools.reduce(jnp.maximum, [jnp.max(x, axis=-1, keepdims=True) for x in s])
        p = [jnp.exp2(x - m) for x in s]
        denom = functools.reduce(jnp.add, [jnp.sum(x, axis=-1, keepdims=True) for x in p])
        ratio = lam * denom[:q_tile] / denom[q_tile:]
        acc = functools.reduce(
            jnp.add, [_dot((x[:q_tile] - x[q_tile:] * ratio).astype(BF16), vv) for x, vv in zip(p, vals)])
        o = acc * (1.0 / denom[:q_tile])
        y_ref[sl, head_lanes(head)] = _rms(o, gh).astype(BF16)

    def body(t, carry):
        sl = pl.ds(pl.multiple_of(t * q_tile, q_tile), q_tile)
        s_next = scores(0, sl)
        for head in range(DIFF_HEADS):
            s_cur = s_next
            if head + 1 < DIFF_HEADS:
                s_next = scores(head + 1, sl)
            attend(head, sl, s_cur)
        return carry

    lax.fori_loop(0, n_q_tiles, body, 0)


def _attn(q, k, v, lam_q, lam_k, g_head, lam_init, batch, seq, cache_k=None, cache_v=None):
    has_cache = cache_k is not None
    tok = pl.BlockSpec((seq, D_MODEL), lambda b: (b, 0))
    small = lambda shape: pl.BlockSpec(shape, lambda b: (0, 0))
    in_specs = [small((2, DIFF_DH)), small((2, DIFF_DH)), tok, tok, tok, small((1, DIFF_DV))]
    args = [lam_q, lam_k, q, k, v, g_head]
    if has_cache:
        past = cache_k.shape[-1]
        in_specs += [pl.BlockSpec((None, DIFF_HEADS, 2 * DIFF_DH, past), lambda b: (b, 0, 0, 0)),
                     pl.BlockSpec((None, past, DIFF_HEADS, DIFF_DV), lambda b: (b, 0, 0, 0))]
        args += [cache_k, cache_v]
        scratch = [pltpu.VMEM((DIFF_HEADS, 2 * DIFF_DH, past), BF16), pltpu.VMEM((DIFF_HEADS, past, DIFF_DV), BF16)]
    else:
        scratch = []
    q_tile = min(Q_TILE, seq)
    return pl.pallas_call(
        functools.partial(_attn_kernel, has_cache=has_cache, lam_init=lam_init, n_q_tiles=seq // q_tile, q_tile=q_tile),
        grid=(batch,),
        in_specs=in_specs,
        out_specs=tok,
        out_shape=jax.ShapeDtypeStruct((batch * seq, D_MODEL), BF16),
        scratch_shapes=scratch,
        compiler_params=_params("arbitrary"),
        name="diff_attn",
    )(*args)


def kernel(x_prompt, x_sample, state_gla_fwd, state_gla_bwd, cache_diff_k, cache_diff_v, c, c_ctx, w_ada, b_ada, g_mix_norm, g_ffn_norm, w_gla_in, w_gla_g1, w_gla_g2, b_gla_g, g_gla_head, w_gla_out, w_diff_qkv, lam_q, lam_k, g_diff_head, w_diff_out, w_ffn_gate, w_ffn_up, w_ffn_down, g_final):
    bp, tp, _ = x_prompt.shape
    bs, ts, _ = x_sample.shape
    assert SUB_TILE % tp == 0 and ts % ROW_TILE == 0 and (bp * tp) % ROW_TILE == 0
    assert tp % GLA_GROUP == 0 and ts % GLA_GROUP == 0
    xp = x_prompt.reshape(bp * tp, D_MODEL)
    xs = x_sample.reshape(bs * ts, D_MODEL)

    cond = jnp.concatenate([c_ctx[None, :], c, jnp.zeros((MOD_ROWS - 1 - bs, D_MODEL), F32)], axis=0)
    mod = _ada(cond, w_ada, b_ada).reshape(DEPTH, MOD_ROWS, N_MOD, 1, D_MODEL)
    wg, wu, wd = w_ffn_gate.astype(BF16), w_ffn_up.astype(BF16), w_ffn_down.astype(BF16)

    j = 0
    w_in = w_gla_in[j].astype(BF16)
    w1 = jnp.concatenate([w_gla_g1[j, 0], w_gla_g1[j, 1], jnp.zeros((D_MODEL, GATE_PAD - 2 * GLA_RANK), F32)],
                         axis=1).astype(BF16)
    w2 = jnp.zeros((GATE_PAD, 2 * GLA_HK), F32)
    w2 = w2.at[:GLA_RANK, :GLA_HK].set(w_gla_g2[j, 0]).at[GLA_RANK:2 * GLA_RANK, GLA_HK:].set(w_gla_g2[j, 1]).astype(BF16)
    bg = b_gla_g[j].reshape(1, 2 * GLA_HK)
    g_mix = g_mix_norm[0].reshape(1, D_MODEL)
    g_head = g_gla_head[j].reshape(1, GLA_DV)
    g_ffn = g_ffn_norm[0].reshape(1, D_MODEL)
    w_out = w_gla_out[j].astype(BF16)

    qp, kp, vp, rp, lap = _gla_in(xp, mod, 0, None, g_mix, w_in, w1, w2, bg)
    qs, ks, vs, rs, las = _gla_in(xs, mod, 0, ts, g_mix, w_in, w1, w2, bg)
    yp, new_f, new_b = _gla_scan(qp, kp, vp, lap, rp, g_head, bp, tp, GLA_HEADS)
    ys = _gla_scan(qs, ks, vs, las, rs, g_head, bs, ts, 2, state_gla_fwd[:, j], state_gla_bwd[:, j])
    xp = _out_ffn(xp, yp, mod, 0, None, w_out, g_ffn, wg, wu, wd)
    xs = _out_ffn(xs, ys, mod, 0, ts, w_out, g_ffn, wg, wu, wd)

    lam_init = 0.8 - 0.6 * math.exp(-0.3 * 1)
    g_mix = g_mix_norm[1].reshape(1, D_MODEL)
    w_qkv = w_diff_qkv[j].astype(BF16)
    g_head = g_diff_head[j].reshape(1, DIFF_DV)
    g_ffn = g_ffn_norm[1].reshape(1, D_MODEL)
    w_out = w_diff_out[j].astype(BF16)

    qp, kp, vp, kp32, vp32 = _qkv(xp, mod, 1, None, tp, g_mix, w_qkv)
    qs, ks, vs = _qkv(xs, mod, 1, ts, ts, g_mix, w_qkv, _rope_tables(ts))
    yp = _attn(qp, kp, vp, lam_q[j], lam_k[j], g_head, lam_init, bp, tp)
    past = cache_diff_k.shape[2]
    cache_k_t = jnp.transpose(cache_diff_k[:, j], (0, 2, 3, 4, 1)).reshape(bs, DIFF_HEADS, 2 * DIFF_DH, past)
    ys = _attn(qs, ks, vs, lam_q[j], lam_k[j], g_head, lam_init, bs, ts, cache_k_t, cache_diff_v[:, j])
    g_fin = g_final.reshape(1, D_MODEL)
    yp_out = _out_ffn(xp, yp, mod, 1, None, w_out, g_ffn, wg, wu, wd, g_final=g_fin)
    ys_out = _out_ffn(xs, ys, mod, 1, ts, w_out, g_ffn, wg, wu, wd, g_final=g_fin)

    return (yp_out.reshape(bp, tp, D_MODEL),
            ys_out.reshape(bs, ts, D_MODEL),
            new_f[:, None],
            new_b[:, None],
            jnp.transpose(kp32.reshape(bp, 1, DIFF_HEADS, 2, DIFF_DH, tp), (0, 1, 5, 2, 3, 4)),
            vp32.reshape(bp, 1, tp, DIFF_HEADS, DIFF_DV))
```

```python
import functools
import math

import jax
import jax.numpy as jnp
from jax import lax
from jax.experimental import pallas as pl
from jax.experimental.pallas import tpu as pltpu

F32 = jnp.float32
BF16 = jnp.bfloat16

D_MODEL = 1024
DEPTH = 2
GRID_W = 64
GLA_HEADS = 4
GLA_DK = 128
GLA_DV = 256
GLA_HK = GLA_HEADS * GLA_DK
GLA_HV = GLA_HEADS * GLA_DV
GLA_RANK = 16
GLA_TAU = 16.0
GLA_CHUNK = 64
GLA_GROUP = 256
GLA_CHAINS = 8
DIFF_HEADS = 8
DIFF_DH = 64
DIFF_DV = 128
ROPE_THETA = 10000.0
D_FF = 2816
EPS = 1e-6
LOG2E = math.log2(math.e)
N_MOD = 6
MOD_ROWS = 16
GATE_PAD = 128

VMEM_LIMIT_BYTES = 56 * 1024 * 1024
ROW_TILE = 512
SUB_TILE = 512
Q_TILE = 256


def _params(*sem):
    return pltpu.CompilerParams(dimension_semantics=sem, vmem_limit_bytes=VMEM_LIMIT_BYTES)


def _resident(shape, index=None):
    index = (0,) * len(shape) if index is None else index
    return pl.BlockSpec(shape, lambda *_: index, pipeline_mode=pl.Buffered(1))


def _sub_tiles():
    return [slice(s * SUB_TILE, (s + 1) * SUB_TILE) for s in range(ROW_TILE // SUB_TILE)]


def _sigmoid(x):
    return 1.0 / (1.0 + jnp.exp(-x))


def _silu(x):
    return x * _sigmoid(x)


def _rms(x, g):
    return x * lax.rsqrt(jnp.mean(x * x, axis=-1, keepdims=True) + EPS) * g


def _norm_mod(x, g, shift, scale):
    return x * lax.rsqrt(jnp.mean(x * x, axis=-1, keepdims=True) + EPS) * (g * (1.0 + scale)) + shift


def _dot(a, b):
    return jnp.dot(a, b, preferred_element_type=F32)


def _dot_nt(a, b):
    return lax.dot_general(a, b, (((1,), (1,)), ((), ())), preferred_element_type=F32)


def _dot_tn(a, b):
    return lax.dot_general(a, b, (((0,), (0,)), ((), ())), preferred_element_type=F32)


def _ada_kernel(c_ref, w_ref, b_ref, o_ref):
    a = _silu(c_ref[...]).astype(BF16)
    o_ref[...] = _dot(a, w_ref[...].astype(BF16)) + b_ref[...]


def _ada(cond, w_ada, b_ada):
    tn = 1536
    n = N_MOD * D_MODEL
    return pl.pallas_call(
        _ada_kernel,
        grid=(DEPTH, n // tn),
        in_specs=[
            pl.BlockSpec((MOD_ROWS, D_MODEL), lambda l, j: (0, 0)),
            pl.BlockSpec((None, D_MODEL, tn), lambda l, j: (l, 0, j)),
            pl.BlockSpec((None, 1, tn), lambda l, j: (l, 0, j)),
        ],
        out_specs=pl.BlockSpec((None, MOD_ROWS, tn), lambda l, j: (l, 0, j)),
        out_shape=jax.ShapeDtypeStruct((DEPTH, MOD_ROWS, n), F32),
        compiler_params=_params("arbitrary", "arbitrary"),
        name="ada",
    )(cond, w_ada, b_ada.reshape(DEPTH, 1, n))


def _mod_spec(layer, which, rows_per_batch):
    if rows_per_batch is None:
        return pl.BlockSpec((None, None, None, 1, D_MODEL), lambda i: (layer, 0, which, 0, 0))
    tiles = rows_per_batch // ROW_TILE
    return pl.BlockSpec((None, None, None, 1, D_MODEL), lambda i: (layer, 1 + i // tiles, which, 0, 0))


def _row_spec(width):
    return pl.BlockSpec((ROW_TILE, width), lambda i: (i, 0))


def _log_sigmoid(x):
    return jnp.minimum(x, 0.0) - jnp.log(1.0 + jnp.exp(-jnp.abs(x)))


def _gla_in_kernel(x_ref, g_ref, sh_ref, sc_ref, w_ref, w1_ref, w2_ref, bg_ref, q_ref, k_ref, v_ref, r_ref, la_ref):
    for rows in _sub_tiles():
        h = _norm_mod(x_ref[rows, :], g_ref[...], sh_ref[...], sc_ref[...]).astype(BF16)
        z = _dot(h, w_ref[...])
        q_ref[rows, :] = z[:, :GLA_HK]
        k_ref[rows, :] = z[:, GLA_HK:2 * GLA_HK]
        v_ref[rows, :] = z[:, 2 * GLA_HK:2 * GLA_HK + GLA_HV].astype(BF16)
        r_ref[rows, :] = z[:, 2 * GLA_HK + GLA_HV:]
        low_rank = _dot(h, w1_ref[...]).astype(BF16)
        pre = _dot(low_rank, w2_ref[...]) + bg_ref[...]
        la_ref[rows, :] = _log_sigmoid(pre) * (1.0 / GLA_TAU)


def _gla_in(x, mod, layer, rows_per_batch, g, w_in, w1, w2, bg):
    rows = x.shape[0]
    return pl.pallas_call(
        _gla_in_kernel,
        grid=(rows // ROW_TILE,),
        in_specs=[
            _row_spec(D_MODEL),
            _resident((1, D_MODEL)),
            _mod_spec(layer, 0, rows_per_batch),
            _mod_spec(layer, 1, rows_per_batch),
            _resident((D_MODEL, 2 * GLA_HK + 2 * GLA_HV)),
            _resident((D_MODEL, GATE_PAD)),
            _resident((GATE_PAD, 2 * GLA_HK)),
            _resident((1, 2 * GLA_HK)),
        ],
        out_specs=[_row_spec(GLA_HK), _row_spec(GLA_HK), _row_spec(GLA_HV), _row_spec(GLA_HV), _row_spec(2 * GLA_HK)],
        out_shape=[
            jax.ShapeDtypeStruct((rows, GLA_HK), F32),
            jax.ShapeDtypeStruct((rows, GLA_HK), F32),
            jax.ShapeDtypeStruct((rows, GLA_HV), BF16),
            jax.ShapeDtypeStruct((rows, GLA_HV), F32),
            jax.ShapeDtypeStruct((rows, 2 * GLA_HK), F32),
        ],
        compiler_params=_params("arbitrary"),
        name="gla_in",
    )(x, g, mod, mod, w_in, w1, w2, bg)


def _split3(x):
    hi = x.astype(BF16)
    rem = x - hi.astype(F32)
    mid = rem.astype(BF16)
    lo = (rem - mid.astype(F32)).astype(BF16)
    return jnp.concatenate([hi, mid, lo], axis=1)


def _fold3(x):
    return x[:, :GLA_DK] + x[:, GLA_DK:2 * GLA_DK] + x[:, 2 * GLA_DK:]


def _gla_scan_kernel(*refs, n_groups, hps, has_s0):
    C, DK, DV, G = GLA_CHUNK, GLA_DK, GLA_DV, GLA_GROUP
    cpg = G // C
    n_chunks = n_groups * cpg
    unroll = max(1, min(n_groups, GLA_CHAINS // hps))
    if has_s0:
        (q_ref, k_ref, v_ref, laf_ref, lab_ref, r_ref, gh_ref, s0f_ref, s0b_ref,
         y_ref, qd_ref, kd_ref, ds_ref, sin_ref, dec_ref) = refs
    else:
        (q_ref, k_ref, v_ref, laf_ref, lab_ref, r_ref, gh_ref,
         y_ref, sf_ref, sb_ref, qd_ref, kd_ref, ds_ref, sin_ref, dec_ref) = refs

    row = lax.broadcasted_iota(jnp.int32, (G, G), 0)
    col = lax.broadcasted_iota(jnp.int32, (G, G), 1)
    same_chunk = (row // C) == (col // C)
    lower = same_chunk & (row >= col)
    upper = same_chunk & (row <= col)
    prefix_sum = jnp.where(lower, 1.0, 0.0).astype(BF16)
    suffix_sum = jnp.where(upper, 1.0, 0.0).astype(BF16)

    def group(g):
        return pl.ds(pl.multiple_of(g * G, G), G)

    def spread(x, first_row):
        return jnp.concatenate(
            [jnp.broadcast_to(x[first_row + c * C:first_row + c * C + 1], (C, x.shape[1])) for c in range(cpg)], axis=0)

    def pass1(g, carry):
        sl = group(g)
        for hd in range(hps):
            dk = slice(hd * DK, (hd + 1) * DK)
            dv = slice(hd * DV, (hd + 1) * DV)
            both = slice(hd * 2 * DK, (hd + 1) * 2 * DK)
            bf = _fold3(_dot(prefix_sum, _split3(laf_ref[sl, dk])))
            bb = _fold3(_dot(suffix_sum, _split3(lab_ref[sl, dk])))
            tot_f = spread(bf, C - 1)
            tot_b = spread(bb, 0)
            q = q_ref[sl, dk] * (DK ** -0.5)
            k = k_ref[sl, dk]
            qd_ref[sl, both] = jnp.concatenate([q * jnp.exp(bf), q * jnp.exp(bb)], axis=1).astype(BF16)
            kd_ref[sl, both] = jnp.concatenate([k * jnp.exp(-bf), k * jnp.exp(-bb)], axis=1).astype(BF16)
            k_end = jnp.concatenate([k * jnp.exp(tot_f - bf), k * jnp.exp(tot_b - bb)], axis=1).astype(BF16)
            v = v_ref[sl, dv]
            for c in range(cpg):
                rows = slice(c * C, (c + 1) * C)
                ds_ref[hd, g * cpg + c] = _dot_tn(v[rows], k_end[rows])
                dec_ref[hd, g * cpg + c] = jnp.exp(
                    jnp.concatenate([tot_f[c * C:c * C + 1], tot_b[c * C:c * C + 1]], axis=1))
        return carry

    lax.fori_loop(0, n_groups, pass1, 0, unroll=unroll)

    for hd in range(hps):
        for forward in (True, False):
            lanes = slice(0, DK) if forward else slice(DK, 2 * DK)
            if has_s0:
                init = (s0f_ref if forward else s0b_ref)[hd].T
            else:
                init = jnp.zeros((DV, DK), F32)

            def step(i, s, hd=hd, forward=forward, lanes=lanes):
                n = i if forward else n_chunks - 1 - i
                sin_ref[hd, n, :, lanes] = s.astype(BF16)
                return s * dec_ref[hd, n, :, lanes] + ds_ref[hd, n, :, lanes]

            final = lax.fori_loop(0, n_chunks, step, init)
            if not has_s0:
                (sf_ref if forward else sb_ref)[hd] = final.T

    gh = gh_ref[...]

    def pass3(g, carry):
        sl = group(g)
        for hd in range(hps):
            dv = slice(hd * DV, (hd + 1) * DV)
            both = slice(hd * 2 * DK, (hd + 1) * 2 * DK)
            qd = qd_ref[sl, both]
            kd = kd_ref[sl, both]
            a_f = _dot_nt(qd[:, :DK], kd[:, :DK])
            a_b = _dot_nt(qd[:, DK:], kd[:, DK:])
            a = jnp.where(lower, a_f, 0.0) + jnp.where(upper, a_b, 0.0)
            inter = [_dot_nt(qd[c * C:(c + 1) * C], sin_ref[hd, g * cpg + c]) for c in range(cpg)]
            o = _dot(a.astype(BF16), v_ref[sl, dv]) + jnp.concatenate(inter, axis=0)
            y_ref[sl, dv] = (_rms(o, gh) * _silu(r_ref[sl, dv])).astype(BF16)
        return carry

    lax.fori_loop(0, n_groups, pass3, 0, unroll=unroll)


def _gla_scan(q, k, v, la, r, g_head, batch, seq, hps, s0f=None, s0b=None):
    steps = GLA_HEADS // hps
    n_chunks = seq // GLA_CHUNK
    has_s0 = s0f is not None
    rows = batch * seq
    tok = lambda w, off: pl.BlockSpec((seq, w * hps), lambda b, h: (b, h + off))
    state = pl.BlockSpec((None, hps, GLA_DK, GLA_DV), lambda b, h: (b, h, 0, 0))
    in_specs = [tok(GLA_DK, 0), tok(GLA_DK, 0), tok(GLA_DV, 0), tok(GLA_DK, 0), tok(GLA_DK, steps),
                tok(GLA_DV, 0), pl.BlockSpec((1, GLA_DV), lambda b, h: (0, 0))]
    args = [q, k, v, la, la, r, g_head]
    y_shape = jax.ShapeDtypeStruct((rows, GLA_HV), BF16)
    st_shape = jax.ShapeDtypeStruct((batch, GLA_HEADS, GLA_DK, GLA_DV), F32)
    scratch = [
        pltpu.VMEM((seq, hps * 2 * GLA_DK), BF16),
        pltpu.VMEM((seq, hps * 2 * GLA_DK), BF16),
        pltpu.VMEM((hps, n_chunks, GLA_DV, 2 * GLA_DK), F32),
        pltpu.VMEM((hps, n_chunks, GLA_DV, 2 * GLA_DK), BF16),
        pltpu.VMEM((hps, n_chunks, 1, 2 * GLA_DK), F32),
    ]
    if has_s0:
        in_specs += [state, state]
        args += [s0f, s0b]
        out_specs = tok(GLA_DV, 0)
        out_shape = y_shape
    else:
        out_specs = [tok(GLA_DV, 0), state, state]
        out_shape = [y_shape, st_shape, st_shape]
    return pl.pallas_call(
        functools.partial(_gla_scan_kernel, n_groups=seq // GLA_GROUP, hps=hps, has_s0=has_s0),
        grid=(batch, steps),
        in_specs=in_specs,
        out_specs=out_specs,
        out_shape=out_shape,
        scratch_shapes=scratch,
        compiler_params=_params("arbitrary", "arbitrary"),
        name="gla_scan",
    )(*args)


def _out_ffn_kernel(*refs, final_norm, ctx_tiles):
    if final_norm:
        (xc_ref, xl_ref, yc_ref, yl_ref, wo_ref, g1_ref, gn_ref, sh_ref, sc_ref, g2_ref, wg_ref, wu_ref, wd_ref,
         gf_ref, oc_ref, ol_ref) = refs
    else:
        (xc_ref, xl_ref, yc_ref, yl_ref, wo_ref, g1_ref, gn_ref, sh_ref, sc_ref, g2_ref, wg_ref, wu_ref, wd_ref,
         oc_ref, ol_ref) = refs
    def tile(x_ref, y_ref, o_ref):
        x = x_ref[...] + g1_ref[...] * _dot(y_ref[...], wo_ref[...])
        h = _norm_mod(x, gn_ref[...], sh_ref[...], sc_ref[...]).astype(BF16)
        act = (_silu(_dot(h, wg_ref[...])) * _dot(h, wu_ref[...])).astype(BF16)
        x = x + g2_ref[...] * _dot(act, wd_ref[...])
        if final_norm:
            x = _rms(x, gf_ref[...])
        o_ref[...] = x

    is_ctx = pl.program_id(0) < ctx_tiles
    pl.when(is_ctx)(lambda: tile(xc_ref, yc_ref, oc_ref))
    pl.when(jnp.logical_not(is_ctx))(lambda: tile(xl_ref, yl_ref, ol_ref))


def _out_ffn(xc, xl, yc, yl, mod, layer, rows_per_batch, w_out, g_ffn, wg, wu, wd, g_final=None):
    ctx_tiles = xc.shape[0] // ROW_TILE
    lat_tiles = xl.shape[0] // ROW_TILE
    per_batch = rows_per_batch // ROW_TILE
    final_norm = g_final is not None
    ctx = lambda width: pl.BlockSpec((ROW_TILE, width), lambda i: (jnp.minimum(i, ctx_tiles - 1), 0))
    lat = lambda width: pl.BlockSpec((ROW_TILE, width), lambda i: (jnp.maximum(i - ctx_tiles, 0), 0))
    mod_row = lambda i: jnp.where(i < ctx_tiles, 0, 1 + (i - ctx_tiles) // per_batch)
    mod_spec = lambda which: pl.BlockSpec((None, None, None, 1, D_MODEL), lambda i: (layer, mod_row(i), which, 0, 0))
    in_specs = [
        ctx(D_MODEL), lat(D_MODEL), ctx(D_MODEL), lat(D_MODEL),
        _resident((D_MODEL, D_MODEL)),
        mod_spec(2),
        _resident((1, D_MODEL)),
        mod_spec(3), mod_spec(4), mod_spec(5),
        _resident((None, D_MODEL, D_FF), (layer, 0, 0)),
        _resident((None, D_MODEL, D_FF), (layer, 0, 0)),
        _resident((None, D_FF, D_MODEL), (layer, 0, 0)),
    ]
    args = [xc, xl, yc, yl, w_out, mod, g_ffn, mod, mod, mod, wg, wu, wd]
    if final_norm:
        in_specs.append(_resident((1, D_MODEL)))
        args.append(g_final)
    return pl.pallas_call(
        functools.partial(_out_ffn_kernel, final_norm=final_norm, ctx_tiles=ctx_tiles),
        grid=(ctx_tiles + lat_tiles,),
        in_specs=in_specs,
        out_specs=[ctx(D_MODEL), lat(D_MODEL)],
        out_shape=[jax.ShapeDtypeStruct(xc.shape, F32), jax.ShapeDtypeStruct(xl.shape, F32)],
        compiler_params=_params("arbitrary"),
        name="out_ffn",
    )(*args)


def _rope_tables(n_tokens):
    rows = n_tokens // GRID_W
    r, col = jnp.meshgrid(jnp.arange(rows), jnp.arange(GRID_W), indexing="ij")
    r = r.reshape(-1).astype(F32)
    col = col.reshape(-1).astype(F32)
    n_freq = DIFF_DH // 4
    inv = ROPE_THETA ** (-jnp.arange(n_freq, dtype=F32) / n_freq)
    ang = jnp.concatenate([r[:, None] * inv, col[:, None] * inv], axis=-1)
    cos, sin = jnp.cos(ang), jnp.sin(ang)
    reps = D_MODEL // DIFF_DH
    return jnp.tile(jnp.concatenate([cos, cos], axis=-1), (1, reps)), jnp.tile(jnp.concatenate([-sin, sin], axis=-1), (1, reps))


def _rope(x, cos, sin_signed):
    half = DIFF_DH // 2
    width = x.shape[1]
    lane = lax.broadcasted_iota(jnp.int32, x.shape, 1)
    from_above = pltpu.roll(x, width - half, axis=1)
    from_below = pltpu.roll(x, half, axis=1)
    swapped = jnp.where(lane % DIFF_DH < half, from_above, from_below)
    return x * cos + swapped * sin_signed


def _qkv_kernel(*refs, rope, seq):
    if rope:
        (x_ref, g_ref, sh_ref, sc_ref, w_ref, cos_ref, sin_ref, q_ref, k_ref, v_ref) = refs
    else:
        (x_ref, g_ref, sh_ref, sc_ref, w_ref, q_ref, k_ref, v_ref, kf_ref, vf_ref) = refs
    for rows in _sub_tiles():
        h = _norm_mod(x_ref[rows, :], g_ref[...], sh_ref[...], sc_ref[...]).astype(BF16)
        z = _dot(h, w_ref[...])
        q = z[:, :D_MODEL]
        k = z[:, D_MODEL:2 * D_MODEL]
        v = z[:, 2 * D_MODEL:]
        if rope:
            q = _rope(q, cos_ref[rows, :], sin_ref[rows, :])
            k = _rope(k, cos_ref[rows, :], sin_ref[rows, :])
        else:
            for b in range(SUB_TILE // seq):
                kf_ref[rows.start // seq + b] = k[b * seq:(b + 1) * seq].T
            for head in range(DIFF_HEADS):
                vf_ref[pl.ds(rows.start * DIFF_HEADS + head, SUB_TILE, stride=DIFF_HEADS), :] = (
                    v[:, head * DIFF_DV:(head + 1) * DIFF_DV])
        q_ref[rows, :] = (q * (DIFF_DH ** -0.5 * LOG2E)).astype(BF16)
        k_ref[rows, :] = k.astype(BF16)
        v_ref[rows, :] = v.astype(BF16)


def _qkv(x, mod, layer, rows_per_batch, seq, g, w, tables=None):
    rows = x.shape[0]
    rope = tables is not None
    in_specs = [_row_spec(D_MODEL), _resident((1, D_MODEL)), _mod_spec(layer, 0, rows_per_batch),
                _mod_spec(layer, 1, rows_per_batch), _resident((D_MODEL, 3 * D_MODEL))]
    args = [x, g, mod, mod, w]
    out_specs = [_row_spec(D_MODEL)] * 3
    out_shape = [jax.ShapeDtypeStruct((rows, D_MODEL), BF16)] * 3
    if rope:
        tiles = seq // ROW_TILE
        tab = pl.BlockSpec((ROW_TILE, D_MODEL), lambda i: (i % tiles, 0))
        in_specs += [tab, tab]
        args += list(tables)
    else:
        per_tile = ROW_TILE // seq
        out_specs += [pl.BlockSpec((per_tile, D_MODEL, seq), lambda i: (i, 0, 0)),
                      pl.BlockSpec((ROW_TILE * DIFF_HEADS, DIFF_DV), lambda i: (i, 0))]
        out_shape += [jax.ShapeDtypeStruct((rows // seq, D_MODEL, seq), F32),
                      jax.ShapeDtypeStruct((rows * DIFF_HEADS, DIFF_DV), F32)]
    return pl.pallas_call(
        functools.partial(_qkv_kernel, rope=rope, seq=seq),
        grid=(rows // ROW_TILE,),
        in_specs=in_specs,
        out_specs=out_specs,
        out_shape=out_shape,
        compiler_params=_params("arbitrary"),
        name="diff_qkv",
    )(*args)


def _attn_kernel(*refs, has_cache, lam_init, n_q_tiles, q_tile):
    if has_cache:
        (lq_ref, lk_ref, q_ref, k_ref, v_ref, gh_ref, kc_ref, vc_ref, y_ref, kcb_ref, vcb_ref) = refs
    else:
        (lq_ref, lk_ref, q_ref, k_ref, v_ref, gh_ref, y_ref) = refs
    e = jnp.exp(jnp.sum(lq_ref[...] * lk_ref[...], axis=1, keepdims=True))
    lam = e[0:1, :] - e[1:2, :] + lam_init
    first_map = lax.broadcasted_iota(jnp.int32, (q_tile, 2 * DIFF_DH), 1) < DIFF_DH
    gh = gh_ref[...] * (1.0 - lam_init)

    if has_cache:
        for head in range(DIFF_HEADS):
            kcb_ref[head] = kc_ref[head].astype(BF16)
            vcb_ref[head] = vc_ref[:, head, :].astype(BF16)

    def head_lanes(head):
        return slice(head * 2 * DIFF_DH, (head + 1) * 2 * DIFF_DH)

    def scores(head, sl):
        q = q_ref[sl, head_lanes(head)]
        zero = jnp.zeros_like(q)
        qq = jnp.concatenate([jnp.where(first_map, q, zero), jnp.where(first_map, zero, q)], axis=0)
        s = [_dot_nt(qq, k_ref[:, head_lanes(head)])]
        if has_cache:
            s.insert(0, _dot(qq, kcb_ref[head]))
        return s

    def attend(head, sl, s):
        vals = [v_ref[:, head_lanes(head)]]
        if has_cache:
            vals.insert(0, vcb_ref[head])
        m = functools.reduce(jnp.maximum, [jnp.max(x, axis=-1, keepdims=True) for x in s])
        p = [jnp.exp2(x - m) for x in s]
        denom = functools.reduce(jnp.add, [jnp.sum(x, axis=-1, keepdims=True) for x in p])
        acc = functools.reduce(jnp.add, [_dot(x.astype(BF16), vv) for x, vv in zip(p, vals)])
        o = acc[:q_tile] * (1.0 / denom[:q_tile]) - acc[q_tile:] * (lam / denom[q_tile:])
        y_ref[sl, head_lanes(head)] = _rms(o, gh).astype(BF16)

    def body(t, carry):
        sl = pl.ds(pl.multiple_of(t * q_tile, q_tile), q_tile)
        s_next = scores(0, sl)
        for head in range(DIFF_HEADS):
            s_cur = s_next
            if head + 1 < DIFF_HEADS:
                s_next = scores(head + 1, sl)
            attend(head, sl, s_cur)
        return carry

    lax.fori_loop(0, n_q_tiles, body, 0)


def _attn(q, k, v, lam_q, lam_k, g_head, lam_init, batch, seq, cache_k=None, cache_v=None):
    has_cache = cache_k is not None
    tok = pl.BlockSpec((seq, D_MODEL), lambda b: (b, 0))
    small = lambda shape: pl.BlockSpec(shape, lambda b: (0, 0))
    in_specs = [small((2, DIFF_DH)), small((2, DIFF_DH)), tok, tok, tok, small((1, DIFF_DV))]
    args = [lam_q, lam_k, q, k, v, g_head]
    if has_cache:
        past = cache_k.shape[-1]
        in_specs += [pl.BlockSpec((None, DIFF_HEADS, 2 * DIFF_DH, past), lambda b: (b, 0, 0, 0)),
                     pl.BlockSpec((None, past, DIFF_HEADS, DIFF_DV), lambda b: (b, 0, 0, 0))]
        args += [cache_k, cache_v]
        scratch = [pltpu.VMEM((DIFF_HEADS, 2 * DIFF_DH, past), BF16), pltpu.VMEM((DIFF_HEADS, past, DIFF_DV), BF16)]
    else:
        scratch = []
    q_tile = min(Q_TILE, seq)
    return pl.pallas_call(
        functools.partial(_attn_kernel, has_cache=has_cache, lam_init=lam_init, n_q_tiles=seq // q_tile, q_tile=q_tile),
        grid=(batch,),
        in_specs=in_specs,
        out_specs=tok,
        out_shape=jax.ShapeDtypeStruct((batch * seq, D_MODEL), BF16),
        scratch_shapes=scratch,
        compiler_params=_params("arbitrary"),
        name="diff_attn",
    )(*args)


def kernel(x_prompt, x_sample, state_gla_fwd, state_gla_bwd, cache_diff_k, cache_diff_v, c, c_ctx, w_ada, b_ada, g_mix_norm, g_ffn_norm, w_gla_in, w_gla_g1, w_gla_g2, b_gla_g, g_gla_head, w_gla_out, w_diff_qkv, lam_q, lam_k, g_diff_head, w_diff_out, w_ffn_gate, w_ffn_up, w_ffn_down, g_final):
    bp, tp, _ = x_prompt.shape
    bs, ts, _ = x_sample.shape
    assert SUB_TILE % tp == 0 and ts % ROW_TILE == 0 and (bp * tp) % ROW_TILE == 0
    assert tp % GLA_GROUP == 0 and ts % GLA_GROUP == 0
    xp = x_prompt.reshape(bp * tp, D_MODEL)
    xs = x_sample.reshape(bs * ts, D_MODEL)

    cond = jnp.concatenate([c_ctx[None, :], c, jnp.zeros((MOD_ROWS - 1 - bs, D_MODEL), F32)], axis=0)
    mod = _ada(cond, w_ada, b_ada).reshape(DEPTH, MOD_ROWS, N_MOD, 1, D_MODEL)
    wg, wu, wd = w_ffn_gate.astype(BF16), w_ffn_up.astype(BF16), w_ffn_down.astype(BF16)

    j = 0
    w_in = w_gla_in[j].astype(BF16)
    w1 = jnp.concatenate([w_gla_g1[j, 0], w_gla_g1[j, 1], jnp.zeros((D_MODEL, GATE_PAD - 2 * GLA_RANK), F32)],
                         axis=1).astype(BF16)
    w2 = jnp.zeros((GATE_PAD, 2 * GLA_HK), F32)
    w2 = w2.at[:GLA_RANK, :GLA_HK].set(w_gla_g2[j, 0]).at[GLA_RANK:2 * GLA_RANK, GLA_HK:].set(w_gla_g2[j, 1]).astype(BF16)
    bg = b_gla_g[j].reshape(1, 2 * GLA_HK)
    g_mix = g_mix_norm[0].reshape(1, D_MODEL)
    g_head = g_gla_head[j].reshape(1, GLA_DV)
    g_ffn = g_ffn_norm[0].reshape(1, D_MODEL)
    w_out = w_gla_out[j].astype(BF16)

    qp, kp, vp, rp, lap = _gla_in(xp, mod, 0, None, g_mix, w_in, w1, w2, bg)
    qs, ks, vs, rs, las = _gla_in(xs, mod, 0, ts, g_mix, w_in, w1, w2, bg)
    yp, new_f, new_b = _gla_scan(qp, kp, vp, lap, rp, g_head, bp, tp, GLA_HEADS)
    ys = _gla_scan(qs, ks, vs, las, rs, g_head, bs, ts, 2, state_gla_fwd[:, j], state_gla_bwd[:, j])
    xp, xs = _out_ffn(xp, xs, yp, ys, mod, 0, ts, w_out, g_ffn, wg, wu, wd)

    lam_init = 0.8 - 0.6 * math.exp(-0.3 * 1)
    g_mix = g_mix_norm[1].reshape(1, D_MODEL)
    w_qkv = w_diff_qkv[j].astype(BF16)
    g_head = g_diff_head[j].reshape(1, DIFF_DV)
    g_ffn = g_ffn_norm[1].reshape(1, D_MODEL)
    w_out = w_diff_out[j].astype(BF16)

    qp, kp, vp, kp32, vp32 = _qkv(xp, mod, 1, None, tp, g_mix, w_qkv)
    qs, ks, vs = _qkv(xs, mod, 1, ts, ts, g_mix, w_qkv, _rope_tables(ts))
    yp = _attn(qp, kp, vp, lam_q[j], lam_k[j], g_head, lam_init, bp, tp)
    past = cache_diff_k.shape[2]
    cache_k_t = jnp.transpose(cache_diff_k[:, j], (0, 2, 3, 4, 1)).reshape(bs, DIFF_HEADS, 2 * DIFF_DH, past)
    ys = _attn(qs, ks, vs, lam_q[j], lam_k[j], g_head, lam_init, bs, ts, cache_k_t, cache_diff_v[:, j])
    g_fin = g_final.reshape(1, D_MODEL)
    yp_out, ys_out = _out_ffn(xp, xs, yp, ys, mod, 1, ts, w_out, g_ffn, wg, wu, wd, g_final=g_fin)

    return (yp_out.reshape(bp, tp, D_MODEL),
            ys_out.reshape(bs, ts, D_MODEL),
            new_f[:, None],
            new_b[:, None],
            jnp.transpose(kp32.reshape(bp, 1, DIFF_HEADS, 2, DIFF_DH, tp), (0, 1, 5, 2, 3, 4)),
            vp32.reshape(bp, 1, tp, DIFF_HEADS, DIFF_DV))
```

```python
import functools
import math

import jax
import jax.numpy as jnp
from jax import lax
from jax.experimental import pallas as pl
from jax.experimental.pallas import tpu as pltpu

F32 = jnp.float32
BF16 = jnp.bfloat16

D_MODEL = 1024
DEPTH = 2
GRID_W = 64
GLA_HEADS = 4
GLA_DK = 128
GLA_DV = 256
GLA_HK = GLA_HEADS * GLA_DK
GLA_HV = GLA_HEADS * GLA_DV
GLA_RANK = 16
GLA_TAU = 16.0
GLA_CHUNK = 64
GLA_GROUP = 256
GLA_CHAINS = 8
DIFF_HEADS = 8
DIFF_DH = 64
DIFF_DV = 128
ROPE_THETA = 10000.0
D_FF = 2816
EPS = 1e-6
LOG2E = math.log2(math.e)
N_MOD = 6
MOD_ROWS = 16
GATE_PAD = 128

VMEM_LIMIT_BYTES = 56 * 1024 * 1024
ROW_TILE = 512
SUB_TILE = 512
Q_TILE = 256


def _params(*sem):
    return pltpu.CompilerParams(dimension_semantics=sem, vmem_limit_bytes=VMEM_LIMIT_BYTES)


def _resident(shape, index=None):
    index = (0,) * len(shape) if index is None else index
    return pl.BlockSpec(shape, lambda *_: index, pipeline_mode=pl.Buffered(1))


def _sub_tiles():
    return [slice(s * SUB_TILE, (s + 1) * SUB_TILE) for s in range(ROW_TILE // SUB_TILE)]


def _sigmoid(x):
    return 1.0 / (1.0 + jnp.exp(-x))


def _silu(x):
    return x * _sigmoid(x)


def _rms(x, g):
    return x * lax.rsqrt(jnp.mean(x * x, axis=-1, keepdims=True) + EPS) * g


def _norm_mod(x, g, shift, scale):
    return x * lax.rsqrt(jnp.mean(x * x, axis=-1, keepdims=True) + EPS) * (g * (1.0 + scale)) + shift


def _dot(a, b):
    return jnp.dot(a, b, preferred_element_type=F32)


def _dot_nt(a, b):
    return lax.dot_general(a, b, (((1,), (1,)), ((), ())), preferred_element_type=F32)


def _dot_tn(a, b):
    return lax.dot_general(a, b, (((0,), (0,)), ((), ())), preferred_element_type=F32)


def _ada_kernel(c_ref, w_ref, b_ref, o_ref):
    a = _silu(c_ref[...]).astype(BF16)
    o_ref[...] = _dot(a, w_ref[...].astype(BF16)) + b_ref[...]


def _ada(cond, w_ada, b_ada):
    tn = 1536
    n = N_MOD * D_MODEL
    return pl.pallas_call(
        _ada_kernel,
        grid=(DEPTH, n // tn),
        in_specs=[
            pl.BlockSpec((MOD_ROWS, D_MODEL), lambda l, j: (0, 0)),
            pl.BlockSpec((None, D_MODEL, tn), lambda l, j: (l, 0, j)),
            pl.BlockSpec((None, 1, tn), lambda l, j: (l, 0, j)),
        ],
        out_specs=pl.BlockSpec((None, MOD_ROWS, tn), lambda l, j: (l, 0, j)),
        out_shape=jax.ShapeDtypeStruct((DEPTH, MOD_ROWS, n), F32),
        compiler_params=_params("arbitrary", "arbitrary"),
        name="ada",
    )(cond, w_ada, b_ada.reshape(DEPTH, 1, n))


def _mod_spec(layer, which, rows_per_batch):
    if rows_per_batch is None:
        return pl.BlockSpec((None, None, None, 1, D_MODEL), lambda i: (layer, 0, which, 0, 0))
    tiles = rows_per_batch // ROW_TILE
    return pl.BlockSpec((None, None, None, 1, D_MODEL), lambda i: (layer, 1 + i // tiles, which, 0, 0))


def _row_spec(width):
    return pl.BlockSpec((ROW_TILE, width), lambda i: (i, 0))


def _log_sigmoid(x):
    return jnp.minimum(x, 0.0) - jnp.log(1.0 + jnp.exp(-jnp.abs(x)))


def _gla_in_kernel(x_ref, g_ref, sh_ref, sc_ref, w_ref, w1_ref, w2_ref, bg_ref, q_ref, k_ref, v_ref, r_ref, la_ref):
    for rows in _sub_tiles():
        h = _norm_mod(x_ref[rows, :], g_ref[...], sh_ref[...], sc_ref[...]).astype(BF16)
        z = _dot(h, w_ref[...])
        q_ref[rows, :] = z[:, :GLA_HK]
        k_ref[rows, :] = z[:, GLA_HK:2 * GLA_HK]
        v_ref[rows, :] = z[:, 2 * GLA_HK:2 * GLA_HK + GLA_HV].astype(BF16)
        r_ref[rows, :] = z[:, 2 * GLA_HK + GLA_HV:]
        low_rank = _dot(h, w1_ref[...]).astype(BF16)
        pre = _dot(low_rank, w2_ref[...]) + bg_ref[...]
        la_ref[rows, :] = _log_sigmoid(pre) * (1.0 / GLA_TAU)


def _gla_in(x, mod, layer, rows_per_batch, g, w_in, w1, w2, bg):
    rows = x.shape[0]
    return pl.pallas_call(
        _gla_in_kernel,
        grid=(rows // ROW_TILE,),
        in_specs=[
            _row_spec(D_MODEL),
            _resident((1, D_MODEL)),
            _mod_spec(layer, 0, rows_per_batch),
            _mod_spec(layer, 1, rows_per_batch),
            _resident((D_MODEL, 2 * GLA_HK + 2 * GLA_HV)),
            _resident((D_MODEL, GATE_PAD)),
            _resident((GATE_PAD, 2 * GLA_HK)),
            _resident((1, 2 * GLA_HK)),
        ],
        out_specs=[_row_spec(GLA_HK), _row_spec(GLA_HK), _row_spec(GLA_HV), _row_spec(GLA_HV), _row_spec(2 * GLA_HK)],
        out_shape=[
            jax.ShapeDtypeStruct((rows, GLA_HK), F32),
            jax.ShapeDtypeStruct((rows, GLA_HK), F32),
            jax.ShapeDtypeStruct((rows, GLA_HV), BF16),
            jax.ShapeDtypeStruct((rows, GLA_HV), F32),
            jax.ShapeDtypeStruct((rows, 2 * GLA_HK), F32),
        ],
        compiler_params=_params("arbitrary"),
        name="gla_in",
    )(x, g, mod, mod, w_in, w1, w2, bg)


def _split3(x):
    hi = x.astype(BF16)
    rem = x - hi.astype(F32)
    mid = rem.astype(BF16)
    lo = (rem - mid.astype(F32)).astype(BF16)
    return jnp.concatenate([hi, mid, lo], axis=1)


def _fold3(x):
    return x[:, :GLA_DK] + x[:, GLA_DK:2 * GLA_DK] + x[:, 2 * GLA_DK:]


def _gla_scan_kernel(*refs, n_groups, hps, has_s0):
    C, DK, DV, G = GLA_CHUNK, GLA_DK, GLA_DV, GLA_GROUP
    cpg = G // C
    n_chunks = n_groups * cpg
    unroll = max(1, min(n_groups, GLA_CHAINS // hps))
    if has_s0:
        (q_ref, k_ref, v_ref, laf_ref, lab_ref, r_ref, gh_ref, s0f_ref, s0b_ref,
         y_ref, qd_ref, kd_ref, ds_ref, sin_ref, dec_ref) = refs
    else:
        (q_ref, k_ref, v_ref, laf_ref, lab_ref, r_ref, gh_ref,
         y_ref, sf_ref, sb_ref, qd_ref, kd_ref, ds_ref, sin_ref, dec_ref) = refs

    row = lax.broadcasted_iota(jnp.int32, (G, G), 0)
    col = lax.broadcasted_iota(jnp.int32, (G, G), 1)
    same_chunk = (row // C) == (col // C)
    lower = same_chunk & (row >= col)
    upper = same_chunk & (row <= col)
    prefix_sum = jnp.where(lower, 1.0, 0.0).astype(BF16)
    suffix_sum = jnp.where(upper, 1.0, 0.0).astype(BF16)

    def group(g):
        return pl.ds(pl.multiple_of(g * G, G), G)

    def spread(x, first_row):
        return jnp.concatenate(
            [jnp.broadcast_to(x[first_row + c * C:first_row + c * C + 1], (C, x.shape[1])) for c in range(cpg)], axis=0)

    def pass1(g, carry):
        sl = group(g)
        for hd in range(hps):
            dk = slice(hd * DK, (hd + 1) * DK)
            dv = slice(hd * DV, (hd + 1) * DV)
            both = slice(hd * 2 * DK, (hd + 1) * 2 * DK)
            bf = _fold3(_dot(prefix_sum, _split3(laf_ref[sl, dk])))
            bb = _fold3(_dot(suffix_sum, _split3(lab_ref[sl, dk])))
            tot_f = spread(bf, C - 1)
            tot_b = spread(bb, 0)
            q = q_ref[sl, dk] * (DK ** -0.5)
            k = k_ref[sl, dk]
            qd_ref[sl, both] = jnp.concatenate([q * jnp.exp(bf), q * jnp.exp(bb)], axis=1).astype(BF16)
            kd_ref[sl, both] = jnp.concatenate([k * jnp.exp(-bf), k * jnp.exp(-bb)], axis=1).astype(BF16)
            k_end = jnp.concatenate([k * jnp.exp(tot_f - bf), k * jnp.exp(tot_b - bb)], axis=1).astype(BF16)
            v = v_ref[sl, dv]
            for c in range(cpg):
                rows = slice(c * C, (c + 1) * C)
                ds_ref[hd, g * cpg + c] = _dot_tn(v[rows], k_end[rows])
                dec_ref[hd, g * cpg + c] = jnp.exp(
                    jnp.concatenate([tot_f[c * C:c * C + 1], tot_b[c * C:c * C + 1]], axis=1))
        return carry

    lax.fori_loop(0, n_groups, pass1, 0, unroll=unroll)

    for hd in range(hps):
        for forward in (True, False):
            lanes = slice(0, DK) if forward else slice(DK, 2 * DK)
            if has_s0:
                init = (s0f_ref if forward else s0b_ref)[hd].T
            else:
                init = jnp.zeros((DV, DK), F32)

            def step(i, s, hd=hd, forward=forward, lanes=lanes):
                n = i if forward else n_chunks - 1 - i
                sin_ref[hd, n, :, lanes] = s.astype(BF16)
                return s * dec_ref[hd, n, :, lanes] + ds_ref[hd, n, :, lanes]

            final = lax.fori_loop(0, n_chunks, step, init)
            if not has_s0:
                (sf_ref if forward else sb_ref)[hd] = final.T

    gh = gh_ref[...]

    def pass3(g, carry):
        sl = group(g)
        for hd in range(hps):
            dv = slice(hd * DV, (hd + 1) * DV)
            both = slice(hd * 2 * DK, (hd + 1) * 2 * DK)
            qd = qd_ref[sl, both]
            kd = kd_ref[sl, both]
            a_f = _dot_nt(qd[:, :DK], kd[:, :DK])
            a_b = _dot_nt(qd[:, DK:], kd[:, DK:])
            a = jnp.where(lower, a_f, 0.0) + jnp.where(upper, a_b, 0.0)
            inter = [_dot_nt(qd[c * C:(c + 1) * C], sin_ref[hd, g * cpg + c]) for c in range(cpg)]
            o = _dot(a.astype(BF16), v_ref[sl, dv]) + jnp.concatenate(inter, axis=0)
            y_ref[sl, dv] = (_rms(o, gh) * _silu(r_ref[sl, dv])).astype(BF16)
        return carry

    lax.fori_loop(0, n_groups, pass3, 0, unroll=unroll)


def _gla_scan(q, k, v, la, r, g_head, batch, seq, hps, s0f=None, s0b=None):
    steps = GLA_HEADS // hps
    n_chunks = seq // GLA_CHUNK
    has_s0 = s0f is not None
    rows = batch * seq
    tok = lambda w, off: pl.BlockSpec((seq, w * hps), lambda b, h: (b, h + off))
    state = pl.BlockSpec((None, hps, GLA_DK, GLA_DV), lambda b, h: (b, h, 0, 0))
    in_specs = [tok(GLA_DK, 0), tok(GLA_DK, 0), tok(GLA_DV, 0), tok(GLA_DK, 0), tok(GLA_DK, steps),
                tok(GLA_DV, 0), pl.BlockSpec((1, GLA_DV), lambda b, h: (0, 0))]
    args = [q, k, v, la, la, r, g_head]
    y_shape = jax.ShapeDtypeStruct((rows, GLA_HV), BF16)
    st_shape = jax.ShapeDtypeStruct((batch, GLA_HEADS, GLA_DK, GLA_DV), F32)
    scratch = [
        pltpu.VMEM((seq, hps * 2 * GLA_DK), BF16),
        pltpu.VMEM((seq, hps * 2 * GLA_DK), BF16),
        pltpu.VMEM((hps, n_chunks, GLA_DV, 2 * GLA_DK), F32),
        pltpu.VMEM((hps, n_chunks, GLA_DV, 2 * GLA_DK), BF16),
        pltpu.VMEM((hps, n_chunks, 1, 2 * GLA_DK), F32),
    ]
    if has_s0:
        in_specs += [state, state]
        args += [s0f, s0b]
        out_specs = tok(GLA_DV, 0)
        out_shape = y_shape
    else:
        out_specs = [tok(GLA_DV, 0), state, state]
        out_shape = [y_shape, st_shape, st_shape]
    return pl.pallas_call(
        functools.partial(_gla_scan_kernel, n_groups=seq // GLA_GROUP, hps=hps, has_s0=has_s0),
        grid=(batch, steps),
        in_specs=in_specs,
        out_specs=out_specs,
        out_shape=out_shape,
        scratch_shapes=scratch,
        compiler_params=_params("arbitrary", "arbitrary"),
        name="gla_scan",
    )(*args)


def _out_ffn_kernel(*refs, final_norm, ctx_tiles):
    if final_norm:
        (xc_ref, xl_ref, yc_ref, yl_ref, wo_ref, g1_ref, gn_ref, sh_ref, sc_ref, g2_ref, wg_ref, wu_ref, wd_ref,
         gf_ref, oc_ref, ol_ref) = refs
    else:
        (xc_ref, xl_ref, yc_ref, yl_ref, wo_ref, g1_ref, gn_ref, sh_ref, sc_ref, g2_ref, wg_ref, wu_ref, wd_ref,
         oc_ref, ol_ref) = refs
    def tile(x_ref, y_ref, o_ref):
        x = x_ref[...] + g1_ref[...] * _dot(y_ref[...], wo_ref[...])
        h = _norm_mod(x, gn_ref[...], sh_ref[...], sc_ref[...]).astype(BF16)
        act = (_silu(_dot(h, wg_ref[...])) * _dot(h, wu_ref[...])).astype(BF16)
        x = x + g2_ref[...] * _dot(act, wd_ref[...])
        if final_norm:
            x = _rms(x, gf_ref[...])
        o_ref[...] = x

    is_ctx = pl.program_id(0) < ctx_tiles
    pl.when(is_ctx)(lambda: tile(xc_ref, yc_ref, oc_ref))
    pl.when(jnp.logical_not(is_ctx))(lambda: tile(xl_ref, yl_ref, ol_ref))


def _out_ffn(xc, xl, yc, yl, mod, layer, rows_per_batch, w_out, g_ffn, wg, wu, wd, g_final=None):
    ctx_tiles = xc.shape[0] // ROW_TILE
    lat_tiles = xl.shape[0] // ROW_TILE
    per_batch = rows_per_batch // ROW_TILE
    final_norm = g_final is not None
    ctx = lambda width: pl.BlockSpec((ROW_TILE, width), lambda i: (jnp.minimum(i, ctx_tiles - 1), 0))
    lat = lambda width: pl.BlockSpec((ROW_TILE, width), lambda i: (jnp.maximum(i - ctx_tiles, 0), 0))
    mod_row = lambda i: jnp.where(i < ctx_tiles, 0, 1 + (i - ctx_tiles) // per_batch)
    mod_spec = lambda which: pl.BlockSpec((None, None, None, 1, D_MODEL), lambda i: (layer, mod_row(i), which, 0, 0))
    in_specs = [
        ctx(D_MODEL), lat(D_MODEL), ctx(D_MODEL), lat(D_MODEL),
        _resident((D_MODEL, D_MODEL)),
        mod_spec(2),
        _resident((1, D_MODEL)),
        mod_spec(3), mod_spec(4), mod_spec(5),
        _resident((None, D_MODEL, D_FF), (layer, 0, 0)),
        _resident((None, D_MODEL, D_FF), (layer, 0, 0)),
        _resident((None, D_FF, D_MODEL), (layer, 0, 0)),
    ]
    args = [xc, xl, yc, yl, w_out, mod, g_ffn, mod, mod, mod, wg, wu, wd]
    if final_norm:
        in_specs.append(_resident((1, D_MODEL)))
        args.append(g_final)
    return pl.pallas_call(
        functools.partial(_out_ffn_kernel, final_norm=final_norm, ctx_tiles=ctx_tiles),
        grid=(ctx_tiles + lat_tiles,),
        in_specs=in_specs,
        out_specs=[ctx(D_MODEL), lat(D_MODEL)],
        out_shape=[jax.ShapeDtypeStruct(xc.shape, F32), jax.ShapeDtypeStruct(xl.shape, F32)],
        compiler_params=_params("arbitrary"),
        name="out_ffn",
    )(*args)


def _rope_tables(n_tokens):
    rows = n_tokens // GRID_W
    r, col = jnp.meshgrid(jnp.arange(rows), jnp.arange(GRID_W), indexing="ij")
    r = r.reshape(-1).astype(F32)
    col = col.reshape(-1).astype(F32)
    n_freq = DIFF_DH // 4
    inv = ROPE_THETA ** (-jnp.arange(n_freq, dtype=F32) / n_freq)
    ang = jnp.concatenate([r[:, None] * inv, col[:, None] * inv], axis=-1)
    cos, sin = jnp.cos(ang), jnp.sin(ang)
    reps = D_MODEL // DIFF_DH
    return jnp.tile(jnp.concatenate([cos, cos], axis=-1), (1, reps)), jnp.tile(jnp.concatenate([-sin, sin], axis=-1), (1, reps))


def _rope(x, cos, sin_signed):
    half = DIFF_DH // 2
    width = x.shape[1]
    lane = lax.broadcasted_iota(jnp.int32, x.shape, 1)
    from_above = pltpu.roll(x, width - half, axis=1)
    from_below = pltpu.roll(x, half, axis=1)
    swapped = jnp.where(lane % DIFF_DH < half, from_above, from_below)
    return x * cos + swapped * sin_signed


def _qkv_kernel(*refs, rope, seq):
    if rope:
        (x_ref, g_ref, sh_ref, sc_ref, w_ref, cos_ref, sin_ref, q_ref, k_ref, v_ref) = refs
    else:
        (x_ref, g_ref, sh_ref, sc_ref, w_ref, q_ref, k_ref, v_ref, kf_ref, vf_ref) = refs
    for rows in _sub_tiles():
        h = _norm_mod(x_ref[rows, :], g_ref[...], sh_ref[...], sc_ref[...]).astype(BF16)
        z = _dot(h, w_ref[...])
        q = z[:, :D_MODEL]
        k = z[:, D_MODEL:2 * D_MODEL]
        v = z[:, 2 * D_MODEL:]
        if rope:
            q = _rope(q, cos_ref[rows, :], sin_ref[rows, :])
            k = _rope(k, cos_ref[rows, :], sin_ref[rows, :])
        else:
            for b in range(SUB_TILE // seq):
                kf_ref[rows.start // seq + b] = k[b * seq:(b + 1) * seq].T
            for head in range(DIFF_HEADS):
                vf_ref[pl.ds(rows.start * DIFF_HEADS + head, SUB_TILE, stride=DIFF_HEADS), :] = (
                    v[:, head * DIFF_DV:(head + 1) * DIFF_DV])
        q_ref[rows, :] = (q * (DIFF_DH ** -0.5 * LOG2E)).astype(BF16)
        k_ref[rows, :] = k.astype(BF16)
        v_ref[rows, :] = v.astype(BF16)


def _qkv(x, mod, layer, rows_per_batch, seq, g, w, tables=None):
    rows = x.shape[0]
    rope = tables is not None
    in_specs = [_row_spec(D_MODEL), _resident((1, D_MODEL)), _mod_spec(layer, 0, rows_per_batch),
                _mod_spec(layer, 1, rows_per_batch), _resident((D_MODEL, 3 * D_MODEL))]
    args = [x, g, mod, mod, w]
    out_specs = [_row_spec(D_MODEL)] * 3
    out_shape = [jax.ShapeDtypeStruct((rows, D_MODEL), BF16)] * 3
    if rope:
        tiles = seq // ROW_TILE
        tab = pl.BlockSpec((ROW_TILE, D_MODEL), lambda i: (i % tiles, 0))
        in_specs += [tab, tab]
        args += list(tables)
    else:
        per_tile = ROW_TILE // seq
        out_specs += [pl.BlockSpec((per_tile, D_MODEL, seq), lambda i: (i, 0, 0)),
                      pl.BlockSpec((ROW_TILE * DIFF_HEADS, DIFF_DV), lambda i: (i, 0))]
        out_shape += [jax.ShapeDtypeStruct((rows // seq, D_MODEL, seq), F32),
                      jax.ShapeDtypeStruct((rows * DIFF_HEADS, DIFF_DV), F32)]
    return pl.pallas_call(
        functools.partial(_qkv_kernel, rope=rope, seq=seq),
        grid=(rows // ROW_TILE,),
        in_specs=in_specs,
        out_specs=out_specs,
        out_shape=out_shape,
        compiler_params=_params("arbitrary"),
        name="diff_qkv",
    )(*args)


def _attn_kernel(*refs, has_cache, lam_init, n_q_tiles, q_tile):
    if has_cache:
        (lq_ref, lk_ref, q_ref, k_ref, v_ref, gh_ref, kc_ref, vc_ref, y_ref, kcb_ref, vcb_ref) = refs
    else:
        (lq_ref, lk_ref, q_ref, k_ref, v_ref, gh_ref, y_ref) = refs
    e = jnp.exp(jnp.sum(lq_ref[...] * lk_ref[...], axis=1, keepdims=True))
    lam = e[0:1, :] - e[1:2, :] + lam_init
    first_map = lax.broadcasted_iota(jnp.int32, (q_tile, 2 * DIFF_DH), 1) < DIFF_DH
    gh = gh_ref[...] * (1.0 - lam_init)

    if has_cache:
        for head in range(DIFF_HEADS):
            kcb_ref[head] = kc_ref[head].astype(BF16)
        vcb_ref[...] = pltpu.einshape("thd->htd", vc_ref[...]).astype(BF16)

    def head_lanes(head):
        return slice(head * 2 * DIFF_DH, (head + 1) * 2 * DIFF_DH)

    def scores(head, sl):
        q = q_ref[sl, head_lanes(head)]
        zero = jnp.zeros_like(q)
        qq = jnp.concatenate([jnp.where(first_map, q, zero), jnp.where(first_map, zero, q)], axis=0)
        s = [_dot_nt(qq, k_ref[:, head_lanes(head)])]
        if has_cache:
            s.insert(0, _dot(qq, kcb_ref[head]))
        return s

    def attend(head, sl, s):
        vals = [v_ref[:, head_lanes(head)]]
        if has_cache:
            vals.insert(0, vcb_ref[head])
        m = functools.reduce(jnp.maximum, [jnp.max(x, axis=-1, keepdims=True) for x in s])
        p = [jnp.exp2(x - m) for x in s]
        denom = functools.reduce(jnp.add, [jnp.sum(x, axis=-1, keepdims=True) for x in p])
        acc = functools.reduce(jnp.add, [_dot(x.astype(BF16), vv) for x, vv in zip(p, vals)])
        o = acc[:q_tile] * (1.0 / denom[:q_tile]) - acc[q_tile:] * (lam / denom[q_tile:])
        y_ref[sl, head_lanes(head)] = _rms(o, gh).astype(BF16)

    def body(t, carry):
        sl = pl.ds(pl.multiple_of(t * q_tile, q_tile), q_tile)
        s_next = scores(0, sl)
        for head in range(DIFF_HEADS):
            s_cur = s_next
            if head + 1 < DIFF_HEADS:
                s_next = scores(head + 1, sl)
            attend(head, sl, s_cur)
        return carry

    lax.fori_loop(0, n_q_tiles, body, 0)


def _attn(q, k, v, lam_q, lam_k, g_head, lam_init, batch, seq, cache_k=None, cache_v=None):
    has_cache = cache_k is not None
    tok = pl.BlockSpec((seq, D_MODEL), lambda b: (b, 0))
    small = lambda shape: pl.BlockSpec(shape, lambda b: (0, 0))
    in_specs = [small((2, DIFF_DH)), small((2, DIFF_DH)), tok, tok, tok, small((1, DIFF_DV))]
    args = [lam_q, lam_k, q, k, v, g_head]
    if has_cache:
        past = cache_k.shape[-1]
        in_specs += [pl.BlockSpec((None, DIFF_HEADS, 2 * DIFF_DH, past), lambda b: (b, 0, 0, 0)),
                     pl.BlockSpec((None, past, DIFF_HEADS, DIFF_DV), lambda b: (b, 0, 0, 0))]
        args += [cache_k, cache_v]
        scratch = [pltpu.VMEM((DIFF_HEADS, 2 * DIFF_DH, past), BF16), pltpu.VMEM((DIFF_HEADS, past, DIFF_DV), BF16)]
    else:
        scratch = []
    q_tile = min(Q_TILE, seq)
    return pl.pallas_call(
        functools.partial(_attn_kernel, has_cache=has_cache, lam_init=lam_init, n_q_tiles=seq // q_tile, q_tile=q_tile),
        grid=(batch,),
        in_specs=in_specs,
        out_specs=tok,
        out_shape=jax.ShapeDtypeStruct((batch * seq, D_MODEL), BF16),
        scratch_shapes=scratch,
        compiler_params=_params("arbitrary"),
        name="diff_attn",
    )(*args)


def kernel(x_prompt, x_sample, state_gla_fwd, state_gla_bwd, cache_diff_k, cache_diff_v, c, c_ctx, w_ada, b_ada, g_mix_norm, g_ffn_norm, w_gla_in, w_gla_g1, w_gla_g2, b_gla_g, g_gla_head, w_gla_out, w_diff_qkv, lam_q, lam_k, g_diff_head, w_diff_out, w_ffn_gate, w_ffn_up, w_ffn_down, g_final):
    bp, tp, _ = x_prompt.shape
    bs, ts, _ = x_sample.shape
    assert SUB_TILE % tp == 0 and ts % ROW_TILE == 0 and (bp * tp) % ROW_TILE == 0
    assert tp % GLA_GROUP == 0 and ts % GLA_GROUP == 0
    xp = x_prompt.reshape(bp * tp, D_MODEL)
    xs = x_sample.reshape(bs * ts, D_MODEL)

    cond = jnp.concatenate([c_ctx[None, :], c, jnp.zeros((MOD_ROWS - 1 - bs, D_MODEL), F32)], axis=0)
    mod = _ada(cond, w_ada, b_ada).reshape(DEPTH, MOD_ROWS, N_MOD, 1, D_MODEL)
    wg, wu, wd = w_ffn_gate.astype(BF16), w_ffn_up.astype(BF16), w_ffn_down.astype(BF16)

    j = 0
    w_in = w_gla_in[j].astype(BF16)
    w1 = jnp.concatenate([w_gla_g1[j, 0], w_gla_g1[j, 1], jnp.zeros((D_MODEL, GATE_PAD - 2 * GLA_RANK), F32)],
                         axis=1).astype(BF16)
    w2 = jnp.zeros((GATE_PAD, 2 * GLA_HK), F32)
    w2 = w2.at[:GLA_RANK, :GLA_HK].set(w_gla_g2[j, 0]).at[GLA_RANK:2 * GLA_RANK, GLA_HK:].set(w_gla_g2[j, 1]).astype(BF16)
    bg = b_gla_g[j].reshape(1, 2 * GLA_HK)
    g_mix = g_mix_norm[0].reshape(1, D_MODEL)
    g_head = g_gla_head[j].reshape(1, GLA_DV)
    g_ffn = g_ffn_norm[0].reshape(1, D_MODEL)
    w_out = w_gla_out[j].astype(BF16)

    qp, kp, vp, rp, lap = _gla_in(xp, mod, 0, None, g_mix, w_in, w1, w2, bg)
    qs, ks, vs, rs, las = _gla_in(xs, mod, 0, ts, g_mix, w_in, w1, w2, bg)
    yp, new_f, new_b = _gla_scan(qp, kp, vp, lap, rp, g_head, bp, tp, GLA_HEADS)
    ys = _gla_scan(qs, ks, vs, las, rs, g_head, bs, ts, 2, state_gla_fwd[:, j], state_gla_bwd[:, j])
    xp, xs = _out_ffn(xp, xs, yp, ys, mod, 0, ts, w_out, g_ffn, wg, wu, wd)

    lam_init = 0.8 - 0.6 * math.exp(-0.3 * 1)
    g_mix = g_mix_norm[1].reshape(1, D_MODEL)
    w_qkv = w_diff_qkv[j].astype(BF16)
    g_head = g_diff_head[j].reshape(1, DIFF_DV)
    g_ffn = g_ffn_norm[1].reshape(1, D_MODEL)
    w_out = w_diff_out[j].astype(BF16)

    qp, kp, vp, kp32, vp32 = _qkv(xp, mod, 1, None, tp, g_mix, w_qkv)
    qs, ks, vs = _qkv(xs, mod, 1, ts, ts, g_mix, w_qkv, _rope_tables(ts))
    yp = _attn(qp, kp, vp, lam_q[j], lam_k[j], g_head, lam_init, bp, tp)
    past = cache_diff_k.shape[2]
    cache_k_t = jnp.transpose(cache_diff_k[:, j], (0, 2, 3, 4, 1)).reshape(bs, DIFF_HEADS, 2 * DIFF_DH, past)
    ys = _attn(qs, ks, vs, lam_q[j], lam_k[j], g_head, lam_init, bs, ts, cache_k_t, cache_diff_v[:, j])
    g_fin = g_final.reshape(1, D_MODEL)
    yp_out, ys_out = _out_ffn(xp, xs, yp, ys, mod, 1, ts, w_out, g_ffn, wg, wu, wd, g_final=g_fin)

    return (yp_out.reshape(bp, tp, D_MODEL),
            ys_out.reshape(bs, ts, D_MODEL),
            new_f[:, None],
            new_b[:, None],
            jnp.transpose(kp32.reshape(bp, 1, DIFF_HEADS, 2, DIFF_DH, tp), (0, 1, 5, 2, 3, 4)),
            vp32.reshape(bp, 1, tp, DIFF_HEADS, DIFF_DV))
```

```python
import functools
import math

import jax
import jax.numpy as jnp
from jax import lax
from jax.experimental import pallas as pl
from jax.experimental.pallas import tpu as pltpu

F32 = jnp.float32
BF16 = jnp.bfloat16

D_MODEL = 1024
DEPTH = 2
GRID_W = 64
GLA_HEADS = 4
GLA_DK = 128
GLA_DV = 256
GLA_HK = GLA_HEADS * GLA_DK
GLA_HV = GLA_HEADS * GLA_DV
GLA_RANK = 16
GLA_TAU = 16.0
GLA_CHUNK = 64
GLA_GROUP = 256
GLA_CHAINS = 8
DIFF_HEADS = 8
DIFF_DH = 64
DIFF_DV = 128
ROPE_THETA = 10000.0
D_FF = 2816
EPS = 1e-6
LOG2E = math.log2(math.e)
N_MOD = 6
MOD_ROWS = 16
GATE_PAD = 128

VMEM_LIMIT_BYTES = 56 * 1024 * 1024
ROW_TILE = 512
SUB_TILE = 512
Q_TILE = 256


def _params(*sem):
    return pltpu.CompilerParams(dimension_semantics=sem, vmem_limit_bytes=VMEM_LIMIT_BYTES)


def _resident(shape, index=None):
    index = (0,) * len(shape) if index is None else index
    return pl.BlockSpec(shape, lambda *_: index, pipeline_mode=pl.Buffered(1))


def _sub_tiles():
    return [slice(s * SUB_TILE, (s + 1) * SUB_TILE) for s in range(ROW_TILE // SUB_TILE)]


def _sigmoid(x):
    return 1.0 / (1.0 + jnp.exp(-x))


def _silu(x):
    return x * _sigmoid(x)


def _rms(x, g):
    return x * lax.rsqrt(jnp.mean(x * x, axis=-1, keepdims=True) + EPS) * g


def _norm_mod(x, g, shift, scale):
    return x * lax.rsqrt(jnp.mean(x * x, axis=-1, keepdims=True) + EPS) * (g * (1.0 + scale)) + shift


def _dot(a, b):
    return jnp.dot(a, b, preferred_element_type=F32)


def _dot_nt(a, b):
    return lax.dot_general(a, b, (((1,), (1,)), ((), ())), preferred_element_type=F32)


def _dot_tn(a, b):
    return lax.dot_general(a, b, (((0,), (0,)), ((), ())), preferred_element_type=F32)


def _ada_kernel(c_ref, w_ref, b_ref, o_ref):
    a = _silu(c_ref[...]).astype(BF16)
    o_ref[...] = _dot(a, w_ref[...].astype(BF16)) + b_ref[...]


def _ada(cond, w_ada, b_ada):
    tn = 1536
    n = N_MOD * D_MODEL
    return pl.pallas_call(
        _ada_kernel,
        grid=(DEPTH, n // tn),
        in_specs=[
            pl.BlockSpec((MOD_ROWS, D_MODEL), lambda l, j: (0, 0)),
            pl.BlockSpec((None, D_MODEL, tn), lambda l, j: (l, 0, j)),
            pl.BlockSpec((None, 1, tn), lambda l, j: (l, 0, j)),
        ],
        out_specs=pl.BlockSpec((None, MOD_ROWS, tn), lambda l, j: (l, 0, j)),
        out_shape=jax.ShapeDtypeStruct((DEPTH, MOD_ROWS, n), F32),
        compiler_params=_params("arbitrary", "arbitrary"),
        name="ada",
    )(cond, w_ada, b_ada.reshape(DEPTH, 1, n))


def _mod_spec(layer, which, rows_per_batch):
    if rows_per_batch is None:
        return pl.BlockSpec((None, None, None, 1, D_MODEL), lambda i: (layer, 0, which, 0, 0))
    tiles = rows_per_batch // ROW_TILE
    return pl.BlockSpec((None, None, None, 1, D_MODEL), lambda i: (layer, 1 + i // tiles, which, 0, 0))


def _row_spec(width):
    return pl.BlockSpec((ROW_TILE, width), lambda i: (i, 0))


def _log_sigmoid(x):
    return jnp.minimum(x, 0.0) - jnp.log(1.0 + jnp.exp(-jnp.abs(x)))


def _gla_in_kernel(x_ref, g_ref, sh_ref, sc_ref, w_ref, w1_ref, w2_ref, bg_ref, q_ref, k_ref, v_ref, r_ref, la_ref):
    for rows in _sub_tiles():
        h = _norm_mod(x_ref[rows, :], g_ref[...], sh_ref[...], sc_ref[...]).astype(BF16)
        z = _dot(h, w_ref[...])
        q_ref[rows, :] = z[:, :GLA_HK]
        k_ref[rows, :] = z[:, GLA_HK:2 * GLA_HK]
        v_ref[rows, :] = z[:, 2 * GLA_HK:2 * GLA_HK + GLA_HV].astype(BF16)
        r_ref[rows, :] = z[:, 2 * GLA_HK + GLA_HV:]
        low_rank = _dot(h, w1_ref[...]).astype(BF16)
        pre = _dot(low_rank, w2_ref[...]) + bg_ref[...]
        la_ref[rows, :] = _log_sigmoid(pre) * (1.0 / GLA_TAU)


def _gla_in(x, mod, layer, rows_per_batch, g, w_in, w1, w2, bg):
    rows = x.shape[0]
    return pl.pallas_call(
        _gla_in_kernel,
        grid=(rows // ROW_TILE,),
        in_specs=[
            _row_spec(D_MODEL),
            _resident((1, D_MODEL)),
            _mod_spec(layer, 0, rows_per_batch),
            _mod_spec(layer, 1, rows_per_batch),
            _resident((D_MODEL, 2 * GLA_HK + 2 * GLA_HV)),
            _resident((D_MODEL, GATE_PAD)),
            _resident((GATE_PAD, 2 * GLA_HK)),
            _resident((1, 2 * GLA_HK)),
        ],
        out_specs=[_row_spec(GLA_HK), _row_spec(GLA_HK), _row_spec(GLA_HV), _row_spec(GLA_HV), _row_spec(2 * GLA_HK)],
        out_shape=[
            jax.ShapeDtypeStruct((rows, GLA_HK), F32),
            jax.ShapeDtypeStruct((rows, GLA_HK), F32),
            jax.ShapeDtypeStruct((rows, GLA_HV), BF16),
            jax.ShapeDtypeStruct((rows, GLA_HV), F32),
            jax.ShapeDtypeStruct((rows, 2 * GLA_HK), F32),
        ],
        compiler_params=_params("arbitrary"),
        name="gla_in",
    )(x, g, mod, mod, w_in, w1, w2, bg)


def _split3(x):
    hi = x.astype(BF16)
    rem = x - hi.astype(F32)
    mid = rem.astype(BF16)
    lo = (rem - mid.astype(F32)).astype(BF16)
    return jnp.concatenate([hi, mid, lo], axis=1)


def _fold3(x):
    return x[:, :GLA_DK] + x[:, GLA_DK:2 * GLA_DK] + x[:, 2 * GLA_DK:]


def _gla_scan_kernel(*refs, n_groups, hps, has_s0):
    C, DK, DV, G = GLA_CHUNK, GLA_DK, GLA_DV, GLA_GROUP
    cpg = G // C
    n_chunks = n_groups * cpg
    unroll = max(1, min(n_groups, GLA_CHAINS // hps))
    if has_s0:
        (q_ref, k_ref, v_ref, laf_ref, lab_ref, r_ref, gh_ref, s0f_ref, s0b_ref,
         y_ref, qd_ref, kd_ref, ds_ref, sin_ref, dec_ref) = refs
    else:
        (q_ref, k_ref, v_ref, laf_ref, lab_ref, r_ref, gh_ref,
         y_ref, sf_ref, sb_ref, qd_ref, kd_ref, ds_ref, sin_ref, dec_ref) = refs

    row = lax.broadcasted_iota(jnp.int32, (G, G), 0)
    col = lax.broadcasted_iota(jnp.int32, (G, G), 1)
    same_chunk = (row // C) == (col // C)
    lower = same_chunk & (row >= col)
    upper = same_chunk & (row <= col)
    prefix_sum = jnp.where(lower, 1.0, 0.0).astype(BF16)
    suffix_sum = jnp.where(upper, 1.0, 0.0).astype(BF16)

    def group(g):
        return pl.ds(pl.multiple_of(g * G, G), G)

    def spread(x, first_row):
        return jnp.concatenate(
            [jnp.broadcast_to(x[first_row + c * C:first_row + c * C + 1], (C, x.shape[1])) for c in range(cpg)], axis=0)

    def pass1(g, carry):
        sl = group(g)
        for hd in range(hps):
            dk = slice(hd * DK, (hd + 1) * DK)
            dv = slice(hd * DV, (hd + 1) * DV)
            both = slice(hd * 2 * DK, (hd + 1) * 2 * DK)
            bf = _fold3(_dot(prefix_sum, _split3(laf_ref[sl, dk])))
            bb = _fold3(_dot(suffix_sum, _split3(lab_ref[sl, dk])))
            tot_f = spread(bf, C - 1)
            tot_b = spread(bb, 0)
            q = q_ref[sl, dk] * (DK ** -0.5)
            k = k_ref[sl, dk]
            qd_ref[sl, both] = jnp.concatenate([q * jnp.exp(bf), q * jnp.exp(bb)], axis=1).astype(BF16)
            kd_ref[sl, both] = jnp.concatenate([k * jnp.exp(-bf), k * jnp.exp(-bb)], axis=1).astype(BF16)
            k_end = jnp.concatenate([k * jnp.exp(tot_f - bf), k * jnp.exp(tot_b - bb)], axis=1).astype(BF16)
            v = v_ref[sl, dv]
            for c in range(cpg):
                rows = slice(c * C, (c + 1) * C)
                ds_ref[hd, g * cpg + c] = _dot_tn(v[rows], k_end[rows])
                dec_ref[hd, g * cpg + c] = jnp.exp(
                    jnp.concatenate([tot_f[c * C:c * C + 1], tot_b[c * C:c * C + 1]], axis=1))
        return carry

    lax.fori_loop(0, n_groups, pass1, 0, unroll=unroll)

    for hd in range(hps):
        for forward in (True, False):
            lanes = slice(0, DK) if forward else slice(DK, 2 * DK)
            if has_s0:
                init = (s0f_ref if forward else s0b_ref)[hd].T
            else:
                init = jnp.zeros((DV, DK), F32)

            def step(i, s, hd=hd, forward=forward, lanes=lanes):
                n = i if forward else n_chunks - 1 - i
                sin_ref[hd, n, :, lanes] = s.astype(BF16)
                return s * dec_ref[hd, n, :, lanes] + ds_ref[hd, n, :, lanes]

            final = lax.fori_loop(0, n_chunks, step, init)
            if not has_s0:
                (sf_ref if forward else sb_ref)[hd] = final.T

    gh = gh_ref[...]

    def pass3(g, carry):
        sl = group(g)
        for hd in range(hps):
            dv = slice(hd * DV, (hd + 1) * DV)
            both = slice(hd * 2 * DK, (hd + 1) * 2 * DK)
            qd = qd_ref[sl, both]
            kd = kd_ref[sl, both]
            a_f = _dot_nt(qd[:, :DK], kd[:, :DK])
            a_b = _dot_nt(qd[:, DK:], kd[:, DK:])
            a = jnp.where(lower, a_f, 0.0) + jnp.where(upper, a_b, 0.0)
            inter = [_dot_nt(qd[c * C:(c + 1) * C], sin_ref[hd, g * cpg + c]) for c in range(cpg)]
            o = _dot(a.astype(BF16), v_ref[sl, dv]) + jnp.concatenate(inter, axis=0)
            y_ref[sl, dv] = (_rms(o, gh) * _silu(r_ref[sl, dv])).astype(BF16)
        return carry

    lax.fori_loop(0, n_groups, pass3, 0, unroll=unroll)


def _gla_scan(q, k, v, la, r, g_head, batch, seq, hps, s0f=None, s0b=None):
    steps = GLA_HEADS // hps
    n_chunks = seq // GLA_CHUNK
    has_s0 = s0f is not None
    rows = batch * seq
    tok = lambda w, off: pl.BlockSpec((seq, w * hps), lambda b, h: (b, h + off))
    state = pl.BlockSpec((None, hps, GLA_DK, GLA_DV), lambda b, h: (b, h, 0, 0))
    in_specs = [tok(GLA_DK, 0), tok(GLA_DK, 0), tok(GLA_DV, 0), tok(GLA_DK, 0), tok(GLA_DK, steps),
                tok(GLA_DV, 0), pl.BlockSpec((1, GLA_DV), lambda b, h: (0, 0))]
    args = [q, k, v, la, la, r, g_head]
    y_shape = jax.ShapeDtypeStruct((rows, GLA_HV), BF16)
    st_shape = jax.ShapeDtypeStruct((batch, GLA_HEADS, GLA_DK, GLA_DV), F32)
    scratch = [
        pltpu.VMEM((seq, hps * 2 * GLA_DK), BF16),
        pltpu.VMEM((seq, hps * 2 * GLA_DK), BF16),
        pltpu.VMEM((hps, n_chunks, GLA_DV, 2 * GLA_DK), F32),
        pltpu.VMEM((hps, n_chunks, GLA_DV, 2 * GLA_DK), BF16),
        pltpu.VMEM((hps, n_chunks, 1, 2 * GLA_DK), F32),
    ]
    if has_s0:
        in_specs += [state, state]
        args += [s0f, s0b]
        out_specs = tok(GLA_DV, 0)
        out_shape = y_shape
    else:
        out_specs = [tok(GLA_DV, 0), state, state]
        out_shape = [y_shape, st_shape, st_shape]
    return pl.pallas_call(
        functools.partial(_gla_scan_kernel, n_groups=seq // GLA_GROUP, hps=hps, has_s0=has_s0),
        grid=(batch, steps),
        in_specs=in_specs,
        out_specs=out_specs,
        out_shape=out_shape,
        scratch_shapes=scratch,
        compiler_params=_params("arbitrary", "arbitrary"),
        name="gla_scan",
    )(*args)


def _out_ffn_kernel(*refs, final_norm, ctx_tiles):
    if final_norm:
        (xc_ref, xl_ref, yc_ref, yl_ref, wo_ref, g1_ref, gn_ref, sh_ref, sc_ref, g2_ref, wg_ref, wu_ref, wd_ref,
         gf_ref, oc_ref, ol_ref) = refs
    else:
        (xc_ref, xl_ref, yc_ref, yl_ref, wo_ref, g1_ref, gn_ref, sh_ref, sc_ref, g2_ref, wg_ref, wu_ref, wd_ref,
         oc_ref, ol_ref) = refs
    def tile(x_ref, y_ref, o_ref):
        x = x_ref[...] + g1_ref[...] * _dot(y_ref[...], wo_ref[...])
        h = _norm_mod(x, gn_ref[...], sh_ref[...], sc_ref[...]).astype(BF16)
        act = (_silu(_dot(h, wg_ref[...])) * _dot(h, wu_ref[...])).astype(BF16)
        x = x + g2_ref[...] * _dot(act, wd_ref[...])
        if final_norm:
            x = _rms(x, gf_ref[...])
        o_ref[...] = x

    is_ctx = pl.program_id(0) < ctx_tiles
    pl.when(is_ctx)(lambda: tile(xc_ref, yc_ref, oc_ref))
    pl.when(jnp.logical_not(is_ctx))(lambda: tile(xl_ref, yl_ref, ol_ref))


def _out_ffn(xc, xl, yc, yl, mod, layer, rows_per_batch, w_out, g_ffn, wg, wu, wd, g_final=None):
    ctx_tiles = xc.shape[0] // ROW_TILE
    lat_tiles = xl.shape[0] // ROW_TILE
    per_batch = rows_per_batch // ROW_TILE
    final_norm = g_final is not None
    ctx = lambda width: pl.BlockSpec((ROW_TILE, width), lambda i: (jnp.minimum(i, ctx_tiles - 1), 0))
    lat = lambda width: pl.BlockSpec((ROW_TILE, width), lambda i: (jnp.maximum(i - ctx_tiles, 0), 0))
    mod_row = lambda i: jnp.where(i < ctx_tiles, 0, 1 + (i - ctx_tiles) // per_batch)
    mod_spec = lambda which: pl.BlockSpec((None, None, None, 1, D_MODEL), lambda i: (layer, mod_row(i), which, 0, 0))
    in_specs = [
        ctx(D_MODEL), lat(D_MODEL), ctx(D_MODEL), lat(D_MODEL),
        _resident((D_MODEL, D_MODEL)),
        mod_spec(2),
        _resident((1, D_MODEL)),
        mod_spec(3), mod_spec(4), mod_spec(5),
        _resident((None, D_MODEL, D_FF), (layer, 0, 0)),
        _resident((None, D_MODEL, D_FF), (layer, 0, 0)),
        _resident((None, D_FF, D_MODEL), (layer, 0, 0)),
    ]
    args = [xc, xl, yc, yl, w_out, mod, g_ffn, mod, mod, mod, wg, wu, wd]
    if final_norm:
        in_specs.append(_resident((1, D_MODEL)))
        args.append(g_final)
    return pl.pallas_call(
        functools.partial(_out_ffn_kernel, final_norm=final_norm, ctx_tiles=ctx_tiles),
        grid=(ctx_tiles + lat_tiles,),
        in_specs=in_specs,
        out_specs=[ctx(D_MODEL), lat(D_MODEL)],
        out_shape=[jax.ShapeDtypeStruct(xc.shape, F32), jax.ShapeDtypeStruct(xl.shape, F32)],
        compiler_params=_params("arbitrary"),
        name="out_ffn",
    )(*args)


def _rope_tables(n_tokens):
    rows = n_tokens // GRID_W
    r, col = jnp.meshgrid(jnp.arange(rows), jnp.arange(GRID_W), indexing="ij")
    r = r.reshape(-1).astype(F32)
    col = col.reshape(-1).astype(F32)
    n_freq = DIFF_DH // 4
    inv = ROPE_THETA ** (-jnp.arange(n_freq, dtype=F32) / n_freq)
    ang = jnp.concatenate([r[:, None] * inv, col[:, None] * inv], axis=-1)
    cos, sin = jnp.cos(ang), jnp.sin(ang)
    reps = D_MODEL // DIFF_DH
    return jnp.tile(jnp.concatenate([cos, cos], axis=-1), (1, reps)), jnp.tile(jnp.concatenate([-sin, sin], axis=-1), (1, reps))


def _rope(x, cos, sin_signed):
    half = DIFF_DH // 2
    width = x.shape[1]
    lane = lax.broadcasted_iota(jnp.int32, x.shape, 1)
    from_above = pltpu.roll(x, width - half, axis=1)
    from_below = pltpu.roll(x, half, axis=1)
    swapped = jnp.where(lane % DIFF_DH < half, from_above, from_below)
    return x * cos + swapped * sin_signed


def _qkv_kernel(*refs, rope, seq):
    if rope:
        (x_ref, g_ref, sh_ref, sc_ref, w_ref, cos_ref, sin_ref, q_ref, k_ref, v_ref) = refs
    else:
        (x_ref, g_ref, sh_ref, sc_ref, w_ref, q_ref, k_ref, v_ref, kf_ref, vf_ref) = refs
    for rows in _sub_tiles():
        h = _norm_mod(x_ref[rows, :], g_ref[...], sh_ref[...], sc_ref[...]).astype(BF16)
        z = _dot(h, w_ref[...])
        q = z[:, :D_MODEL]
        k = z[:, D_MODEL:2 * D_MODEL]
        v = z[:, 2 * D_MODEL:]
        if rope:
            q = _rope(q, cos_ref[rows, :], sin_ref[rows, :])
            k = _rope(k, cos_ref[rows, :], sin_ref[rows, :])
        else:
            for b in range(SUB_TILE // seq):
                kf_ref[rows.start // seq + b] = k[b * seq:(b + 1) * seq].T
            by_head = jnp.stack([v[:, head * DIFF_DV:(head + 1) * DIFF_DV] for head in range(DIFF_HEADS)], axis=0)
            vf_ref[rows] = jnp.swapaxes(by_head, 0, 1)
        q_ref[rows, :] = (q * (DIFF_DH ** -0.5 * LOG2E)).astype(BF16)
        k_ref[rows, :] = k.astype(BF16)
        v_ref[rows, :] = v.astype(BF16)


def _qkv(x, mod, layer, rows_per_batch, seq, g, w, tables=None):
    rows = x.shape[0]
    rope = tables is not None
    in_specs = [_row_spec(D_MODEL), _resident((1, D_MODEL)), _mod_spec(layer, 0, rows_per_batch),
                _mod_spec(layer, 1, rows_per_batch), _resident((D_MODEL, 3 * D_MODEL))]
    args = [x, g, mod, mod, w]
    out_specs = [_row_spec(D_MODEL)] * 3
    out_shape = [jax.ShapeDtypeStruct((rows, D_MODEL), BF16)] * 3
    if rope:
        tiles = seq // ROW_TILE
        tab = pl.BlockSpec((ROW_TILE, D_MODEL), lambda i: (i % tiles, 0))
        in_specs += [tab, tab]
        args += list(tables)
    else:
        per_tile = ROW_TILE // seq
        out_specs += [pl.BlockSpec((per_tile, D_MODEL, seq), lambda i: (i, 0, 0)),
                      pl.BlockSpec((ROW_TILE, DIFF_HEADS, DIFF_DV), lambda i: (i, 0, 0))]
        out_shape += [jax.ShapeDtypeStruct((rows // seq, D_MODEL, seq), F32),
                      jax.ShapeDtypeStruct((rows, DIFF_HEADS, DIFF_DV), F32)]
    return pl.pallas_call(
        functools.partial(_qkv_kernel, rope=rope, seq=seq),
        grid=(rows // ROW_TILE,),
        in_specs=in_specs,
        out_specs=out_specs,
        out_shape=out_shape,
        compiler_params=_params("arbitrary"),
        name="diff_qkv",
    )(*args)


def _attn_kernel(*refs, has_cache, lam_init, n_q_tiles, q_tile):
    if has_cache:
        (lq_ref, lk_ref, q_ref, k_ref, v_ref, gh_ref, kc_ref, vc_ref, y_ref, kcb_ref, vcb_ref) = refs
    else:
        (lq_ref, lk_ref, q_ref, k_ref, v_ref, gh_ref, y_ref) = refs
    e = jnp.exp(jnp.sum(lq_ref[...] * lk_ref[...], axis=1, keepdims=True))
    lam = e[0:1, :] - e[1:2, :] + lam_init
    first_map = lax.broadcasted_iota(jnp.int32, (q_tile, 2 * DIFF_DH), 1) < DIFF_DH
    gh = gh_ref[...] * (1.0 - lam_init)

    if has_cache:
        for head in range(DIFF_HEADS):
            kcb_ref[head] = kc_ref[head].astype(BF16)
        vcb_ref[...] = jnp.swapaxes(vc_ref[...], 0, 1).astype(BF16)

    def head_lanes(head):
        return slice(head * 2 * DIFF_DH, (head + 1) * 2 * DIFF_DH)

    def scores(head, sl):
        q = q_ref[sl, head_lanes(head)]
        zero = jnp.zeros_like(q)
        qq = jnp.concatenate([jnp.where(first_map, q, zero), jnp.where(first_map, zero, q)], axis=0)
        s = [_dot_nt(qq, k_ref[:, head_lanes(head)])]
        if has_cache:
            s.insert(0, _dot(qq, kcb_ref[head]))
        return s

    def attend(head, sl, s):
        vals = [v_ref[:, head_lanes(head)]]
        if has_cache:
            vals.insert(0, vcb_ref[head])
        m = functools.reduce(jnp.maximum, [jnp.max(x, axis=-1, keepdims=True) for x in s])
        p = [jnp.exp2(x - m) for x in s]
        denom = functools.reduce(jnp.add, [jnp.sum(x, axis=-1, keepdims=True) for x in p])
        acc = functools.reduce(jnp.add, [_dot(x.astype(BF16), vv) for x, vv in zip(p, vals)])
        o = acc[:q_tile] * (1.0 / denom[:q_tile]) - acc[q_tile:] * (lam / denom[q_tile:])
        y_ref[sl, head_lanes(head)] = _rms(o, gh).astype(BF16)

    def body(t, carry):
        sl = pl.ds(pl.multiple_of(t * q_tile, q_tile), q_tile)
        s_next = scores(0, sl)
        for head in range(DIFF_HEADS):
            s_cur = s_next
            if head + 1 < DIFF_HEADS:
                s_next = scores(head + 1, sl)
            attend(head, sl, s_cur)
        return carry

    lax.fori_loop(0, n_q_tiles, body, 0)


def _attn(q, k, v, lam_q, lam_k, g_head, lam_init, batch, seq, cache_k=None, cache_v=None):
    has_cache = cache_k is not None
    tok = pl.BlockSpec((seq, D_MODEL), lambda b: (b, 0))
    small = lambda shape: pl.BlockSpec(shape, lambda b: (0, 0))
    in_specs = [small((2, DIFF_DH)), small((2, DIFF_DH)), tok, tok, tok, small((1, DIFF_DV))]
    args = [lam_q, lam_k, q, k, v, g_head]
    if has_cache:
        past = cache_k.shape[-1]
        in_specs += [pl.BlockSpec((None, DIFF_HEADS, 2 * DIFF_DH, past), lambda b: (b, 0, 0, 0)),
                     pl.BlockSpec((None, past, DIFF_HEADS, DIFF_DV), lambda b: (b, 0, 0, 0))]
        args += [cache_k, cache_v]
        scratch = [pltpu.VMEM((DIFF_HEADS, 2 * DIFF_DH, past), BF16), pltpu.VMEM((DIFF_HEADS, past, DIFF_DV), BF16)]
    else:
        scratch = []
    q_tile = min(Q_TILE, seq)
    return pl.pallas_call(
        functools.partial(_attn_kernel, has_cache=has_cache, lam_init=lam_init, n_q_tiles=seq // q_tile, q_tile=q_tile),
        grid=(batch,),
        in_specs=in_specs,
        out_specs=tok,
        out_shape=jax.ShapeDtypeStruct((batch * seq, D_MODEL), BF16),
        scratch_shapes=scratch,
        compiler_params=_params("arbitrary"),
        name="diff_attn",
    )(*args)


def kernel(x_prompt, x_sample, state_gla_fwd, state_gla_bwd, cache_diff_k, cache_diff_v, c, c_ctx, w_ada, b_ada, g_mix_norm, g_ffn_norm, w_gla_in, w_gla_g1, w_gla_g2, b_gla_g, g_gla_head, w_gla_out, w_diff_qkv, lam_q, lam_k, g_diff_head, w_diff_out, w_ffn_gate, w_ffn_up, w_ffn_down, g_final):
    bp, tp, _ = x_prompt.shape
    bs, ts, _ = x_sample.shape
    assert SUB_TILE % tp == 0 and ts % ROW_TILE == 0 and (bp * tp) % ROW_TILE == 0
    assert tp % GLA_GROUP == 0 and ts % GLA_GROUP == 0
    xp = x_prompt.reshape(bp * tp, D_MODEL)
    xs = x_sample.reshape(bs * ts, D_MODEL)

    cond = jnp.concatenate([c_ctx[None, :], c, jnp.zeros((MOD_ROWS - 1 - bs, D_MODEL), F32)], axis=0)
    mod = _ada(cond, w_ada, b_ada).reshape(DEPTH, MOD_ROWS, N_MOD, 1, D_MODEL)
    wg, wu, wd = w_ffn_gate.astype(BF16), w_ffn_up.astype(BF16), w_ffn_down.astype(BF16)

    j = 0
    w_in = w_gla_in[j].astype(BF16)
    w1 = jnp.concatenate([w_gla_g1[j, 0], w_gla_g1[j, 1], jnp.zeros((D_MODEL, GATE_PAD - 2 * GLA_RANK), F32)],
                         axis=1).astype(BF16)
    w2 = jnp.zeros((GATE_PAD, 2 * GLA_HK), F32)
    w2 = w2.at[:GLA_RANK, :GLA_HK].set(w_gla_g2[j, 0]).at[GLA_RANK:2 * GLA_RANK, GLA_HK:].set(w_gla_g2[j, 1]).astype(BF16)
    bg = b_gla_g[j].reshape(1, 2 * GLA_HK)
    g_mix = g_mix_norm[0].reshape(1, D_MODEL)
    g_head = g_gla_head[j].reshape(1, GLA_DV)
    g_ffn = g_ffn_norm[0].reshape(1, D_MODEL)
    w_out = w_gla_out[j].astype(BF16)

    qp, kp, vp, rp, lap = _gla_in(xp, mod, 0, None, g_mix, w_in, w1, w2, bg)
    qs, ks, vs, rs, las = _gla_in(xs, mod, 0, ts, g_mix, w_in, w1, w2, bg)
    yp, new_f, new_b = _gla_scan(qp, kp, vp, lap, rp, g_head, bp, tp, GLA_HEADS)
    ys = _gla_scan(qs, ks, vs, las, rs, g_head, bs, ts, 2, state_gla_fwd[:, j], state_gla_bwd[:, j])
    xp, xs = _out_ffn(xp, xs, yp, ys, mod, 0, ts, w_out, g_ffn, wg, wu, wd)

    lam_init = 0.8 - 0.6 * math.exp(-0.3 * 1)
    g_mix = g_mix_norm[1].reshape(1, D_MODEL)
    w_qkv = w_diff_qkv[j].astype(BF16)
    g_head = g_diff_head[j].reshape(1, DIFF_DV)
    g_ffn = g_ffn_norm[1].reshape(1, D_MODEL)
    w_out = w_diff_out[j].astype(BF16)

    qp, kp, vp, kp32, vp32 = _qkv(xp, mod, 1, None, tp, g_mix, w_qkv)
    qs, ks, vs = _qkv(xs, mod, 1, ts, ts, g_mix, w_qkv, _rope_tables(ts))
    yp = _attn(qp, kp, vp, lam_q[j], lam_k[j], g_head, lam_init, bp, tp)
    past = cache_diff_k.shape[2]
    cache_k_t = jnp.transpose(cache_diff_k[:, j], (0, 2, 3, 4, 1)).reshape(bs, DIFF_HEADS, 2 * DIFF_DH, past)
    ys = _attn(qs, ks, vs, lam_q[j], lam_k[j], g_head, lam_init, bs, ts, cache_k_t, cache_diff_v[:, j])
    g_fin = g_final.reshape(1, D_MODEL)
    yp_out, ys_out = _out_ffn(xp, xs, yp, ys, mod, 1, ts, w_out, g_ffn, wg, wu, wd, g_final=g_fin)

    return (yp_out.reshape(bp, tp, D_MODEL),
            ys_out.reshape(bs, ts, D_MODEL),
            new_f[:, None],
            new_b[:, None],
            jnp.transpose(kp32.reshape(bp, 1, DIFF_HEADS, 2, DIFF_DH, tp), (0, 1, 5, 2, 3, 4)),
            vp32.reshape(bp, 1, tp, DIFF_HEADS, DIFF_DV))
```

```python
import functools
import math

import jax
import jax.numpy as jnp
from jax import lax
from jax.experimental import pallas as pl
from jax.experimental.pallas import tpu as pltpu

F32 = jnp.float32
BF16 = jnp.bfloat16

D_MODEL = 1024
DEPTH = 2
GRID_W = 64
GLA_HEADS = 4
GLA_DK = 128
GLA_DV = 256
GLA_HK = GLA_HEADS * GLA_DK
GLA_HV = GLA_HEADS * GLA_DV
GLA_RANK = 16
GLA_TAU = 16.0
GLA_CHUNK = 64
GLA_GROUP = 256
GLA_CHAINS = 8
DIFF_HEADS = 8
DIFF_DH = 64
DIFF_DV = 128
ROPE_THETA = 10000.0
D_FF = 2816
EPS = 1e-6
LOG2E = math.log2(math.e)
N_MOD = 6
MOD_ROWS = 16
GATE_PAD = 128

VMEM_LIMIT_BYTES = 56 * 1024 * 1024
ROW_TILE = 512
SUB_TILE = 512
FFN_WEIGHT_STEPS = 16
Q_TILE = 256


def _params(*sem):
    return pltpu.CompilerParams(dimension_semantics=sem, vmem_limit_bytes=VMEM_LIMIT_BYTES)


def _resident(shape, index=None):
    index = (0,) * len(shape) if index is None else index
    return pl.BlockSpec(shape, lambda *_: index, pipeline_mode=pl.Buffered(1))


def _sub_tiles():
    return [slice(s * SUB_TILE, (s + 1) * SUB_TILE) for s in range(ROW_TILE // SUB_TILE)]


def _sigmoid(x):
    return 1.0 / (1.0 + jnp.exp(-x))


def _silu(x):
    return x * _sigmoid(x)


def _rms(x, g):
    return x * lax.rsqrt(jnp.mean(x * x, axis=-1, keepdims=True) + EPS) * g


def _norm_mod(x, g, shift, scale):
    return x * lax.rsqrt(jnp.mean(x * x, axis=-1, keepdims=True) + EPS) * (g * (1.0 + scale)) + shift


def _dot(a, b):
    return jnp.dot(a, b, preferred_element_type=F32)


def _dot_nt(a, b):
    return lax.dot_general(a, b, (((1,), (1,)), ((), ())), preferred_element_type=F32)


def _dot_tn(a, b):
    return lax.dot_general(a, b, (((0,), (0,)), ((), ())), preferred_element_type=F32)


def _ada_kernel(c_ref, w_ref, b_ref, o_ref):
    a = _silu(c_ref[...]).astype(BF16)
    o_ref[...] = _dot(a, w_ref[...].astype(BF16)) + b_ref[...]


def _ada(cond, w_ada, b_ada):
    tn = 1536
    n = N_MOD * D_MODEL
    return pl.pallas_call(
        _ada_kernel,
        grid=(DEPTH, n // tn),
        in_specs=[
            pl.BlockSpec((MOD_ROWS, D_MODEL), lambda l, j: (0, 0)),
            pl.BlockSpec((None, D_MODEL, tn), lambda l, j: (l, 0, j)),
            pl.BlockSpec((None, 1, tn), lambda l, j: (l, 0, j)),
        ],
        out_specs=pl.BlockSpec((None, MOD_ROWS, tn), lambda l, j: (l, 0, j)),
        out_shape=jax.ShapeDtypeStruct((DEPTH, MOD_ROWS, n), F32),
        compiler_params=_params("arbitrary", "arbitrary"),
        name="ada",
    )(cond, w_ada, b_ada.reshape(DEPTH, 1, n))


def _mod_spec(layer, which, rows_per_batch):
    if rows_per_batch is None:
        return pl.BlockSpec((None, None, None, 1, D_MODEL), lambda i: (layer, 0, which, 0, 0))
    tiles = rows_per_batch // ROW_TILE
    return pl.BlockSpec((None, None, None, 1, D_MODEL), lambda i: (layer, 1 + i // tiles, which, 0, 0))


def _row_spec(width):
    return pl.BlockSpec((ROW_TILE, width), lambda i: (i, 0))


def _log_sigmoid(x):
    return jnp.minimum(x, 0.0) - jnp.log(1.0 + jnp.exp(-jnp.abs(x)))


def _gla_in_kernel(x_ref, g_ref, sh_ref, sc_ref, w_ref, w1_ref, w2_ref, bg_ref, q_ref, k_ref, v_ref, r_ref, la_ref):
    for rows in _sub_tiles():
        h = _norm_mod(x_ref[rows, :], g_ref[...], sh_ref[...], sc_ref[...]).astype(BF16)
        z = _dot(h, w_ref[...])
        q_ref[rows, :] = z[:, :GLA_HK]
        k_ref[rows, :] = z[:, GLA_HK:2 * GLA_HK]
        v_ref[rows, :] = z[:, 2 * GLA_HK:2 * GLA_HK + GLA_HV].astype(BF16)
        r_ref[rows, :] = z[:, 2 * GLA_HK + GLA_HV:]
        low_rank = _dot(h, w1_ref[...]).astype(BF16)
        pre = _dot(low_rank, w2_ref[...]) + bg_ref[...]
        la_ref[rows, :] = _log_sigmoid(pre) * (1.0 / GLA_TAU)


def _gla_in(x, mod, layer, rows_per_batch, g, w_in, w1, w2, bg):
    rows = x.shape[0]
    return pl.pallas_call(
        _gla_in_kernel,
        grid=(rows // ROW_TILE,),
        in_specs=[
            _row_spec(D_MODEL),
            _resident((1, D_MODEL)),
            _mod_spec(layer, 0, rows_per_batch),
            _mod_spec(layer, 1, rows_per_batch),
            _resident((D_MODEL, 2 * GLA_HK + 2 * GLA_HV)),
            _resident((D_MODEL, GATE_PAD)),
            _resident((GATE_PAD, 2 * GLA_HK)),
            _resident((1, 2 * GLA_HK)),
        ],
        out_specs=[_row_spec(GLA_HK), _row_spec(GLA_HK), _row_spec(GLA_HV), _row_spec(GLA_HV), _row_spec(2 * GLA_HK)],
        out_shape=[
            jax.ShapeDtypeStruct((rows, GLA_HK), F32),
            jax.ShapeDtypeStruct((rows, GLA_HK), F32),
            jax.ShapeDtypeStruct((rows, GLA_HV), BF16),
            jax.ShapeDtypeStruct((rows, GLA_HV), F32),
            jax.ShapeDtypeStruct((rows, 2 * GLA_HK), F32),
        ],
        compiler_params=_params("arbitrary"),
        name="gla_in",
    )(x, g, mod, mod, w_in, w1, w2, bg)


def _split3(x):
    hi = x.astype(BF16)
    rem = x - hi.astype(F32)
    mid = rem.astype(BF16)
    lo = (rem - mid.astype(F32)).astype(BF16)
    return jnp.concatenate([hi, mid, lo], axis=1)


def _fold3(x):
    return x[:, :GLA_DK] + x[:, GLA_DK:2 * GLA_DK] + x[:, 2 * GLA_DK:]


def _gla_scan_kernel(*refs, n_groups, hps, has_s0):
    C, DK, DV, G = GLA_CHUNK, GLA_DK, GLA_DV, GLA_GROUP
    cpg = G // C
    n_chunks = n_groups * cpg
    unroll = max(1, min(n_groups, GLA_CHAINS // hps))
    if has_s0:
        (q_ref, k_ref, v_ref, laf_ref, lab_ref, r_ref, gh_ref, s0f_ref, s0b_ref,
         y_ref, qd_ref, kd_ref, ds_ref, sin_ref, dec_ref) = refs
    else:
        (q_ref, k_ref, v_ref, laf_ref, lab_ref, r_ref, gh_ref,
         y_ref, sf_ref, sb_ref, qd_ref, kd_ref, ds_ref, sin_ref, dec_ref) = refs

    row = lax.broadcasted_iota(jnp.int32, (G, G), 0)
    col = lax.broadcasted_iota(jnp.int32, (G, G), 1)
    same_chunk = (row // C) == (col // C)
    lower = same_chunk & (row >= col)
    upper = same_chunk & (row <= col)
    prefix_sum = jnp.where(lower, 1.0, 0.0).astype(BF16)
    suffix_sum = jnp.where(upper, 1.0, 0.0).astype(BF16)

    def group(g):
        return pl.ds(pl.multiple_of(g * G, G), G)

    def spread(x, first_row):
        return jnp.concatenate(
            [jnp.broadcast_to(x[first_row + c * C:first_row + c * C + 1], (C, x.shape[1])) for c in range(cpg)], axis=0)

    def pass1(g, carry):
        sl = group(g)
        for hd in range(hps):
            dk = slice(hd * DK, (hd + 1) * DK)
            dv = slice(hd * DV, (hd + 1) * DV)
            both = slice(hd * 2 * DK, (hd + 1) * 2 * DK)
            bf = _fold3(_dot(prefix_sum, _split3(laf_ref[sl, dk])))
            bb = _fold3(_dot(suffix_sum, _split3(lab_ref[sl, dk])))
            tot_f = spread(bf, C - 1)
            tot_b = spread(bb, 0)
            q = q_ref[sl, dk] * (DK ** -0.5)
            k = k_ref[sl, dk]
            qd_ref[sl, both] = jnp.concatenate([q * jnp.exp(bf), q * jnp.exp(bb)], axis=1).astype(BF16)
            kd_ref[sl, both] = jnp.concatenate([k * jnp.exp(-bf), k * jnp.exp(-bb)], axis=1).astype(BF16)
            k_end = jnp.concatenate([k * jnp.exp(tot_f - bf), k * jnp.exp(tot_b - bb)], axis=1).astype(BF16)
            v = v_ref[sl, dv]
            for c in range(cpg):
                rows = slice(c * C, (c + 1) * C)
                ds_ref[hd, g * cpg + c] = _dot_tn(v[rows], k_end[rows])
                dec_ref[hd, g * cpg + c] = jnp.exp(
                    jnp.concatenate([tot_f[c * C:c * C + 1], tot_b[c * C:c * C + 1]], axis=1))
        return carry

    lax.fori_loop(0, n_groups, pass1, 0, unroll=unroll)

    for hd in range(hps):
        for forward in (True, False):
            lanes = slice(0, DK) if forward else slice(DK, 2 * DK)
            if has_s0:
                init = (s0f_ref if forward else s0b_ref)[hd].T
            else:
                init = jnp.zeros((DV, DK), F32)

            def step(i, s, hd=hd, forward=forward, lanes=lanes):
                n = i if forward else n_chunks - 1 - i
                sin_ref[hd, n, :, lanes] = s.astype(BF16)
                return s * dec_ref[hd, n, :, lanes] + ds_ref[hd, n, :, lanes]

            final = lax.fori_loop(0, n_chunks, step, init)
            if not has_s0:
                (sf_ref if forward else sb_ref)[hd] = final.T

    gh = gh_ref[...]

    def pass3(g, carry):
        sl = group(g)
        for hd in range(hps):
            dv = slice(hd * DV, (hd + 1) * DV)
            both = slice(hd * 2 * DK, (hd + 1) * 2 * DK)
            qd = qd_ref[sl, both]
            kd = kd_ref[sl, both]
            a_f = _dot_nt(qd[:, :DK], kd[:, :DK])
            a_b = _dot_nt(qd[:, DK:], kd[:, DK:])
            a = jnp.where(lower, a_f, 0.0) + jnp.where(upper, a_b, 0.0)
            inter = [_dot_nt(qd[c * C:(c + 1) * C], sin_ref[hd, g * cpg + c]) for c in range(cpg)]
            o = _dot(a.astype(BF16), v_ref[sl, dv]) + jnp.concatenate(inter, axis=0)
            y_ref[sl, dv] = (_rms(o, gh) * _silu(r_ref[sl, dv])).astype(BF16)
        return carry

    lax.fori_loop(0, n_groups, pass3, 0, unroll=unroll)


def _gla_scan(q, k, v, la, r, g_head, batch, seq, hps, s0f=None, s0b=None):
    steps = GLA_HEADS // hps
    n_chunks = seq // GLA_CHUNK
    has_s0 = s0f is not None
    rows = batch * seq
    tok = lambda w, off: pl.BlockSpec((seq, w * hps), lambda b, h: (b, h + off))
    state = pl.BlockSpec((None, hps, GLA_DK, GLA_DV), lambda b, h: (b, h, 0, 0))
    in_specs = [tok(GLA_DK, 0), tok(GLA_DK, 0), tok(GLA_DV, 0), tok(GLA_DK, 0), tok(GLA_DK, steps),
                tok(GLA_DV, 0), pl.BlockSpec((1, GLA_DV), lambda b, h: (0, 0))]
    args = [q, k, v, la, la, r, g_head]
    y_shape = jax.ShapeDtypeStruct((rows, GLA_HV), BF16)
    st_shape = jax.ShapeDtypeStruct((batch, GLA_HEADS, GLA_DK, GLA_DV), F32)
    scratch = [
        pltpu.VMEM((seq, hps * 2 * GLA_DK), BF16),
        pltpu.VMEM((seq, hps * 2 * GLA_DK), BF16),
        pltpu.VMEM((hps, n_chunks, GLA_DV, 2 * GLA_DK), F32),
        pltpu.VMEM((hps, n_chunks, GLA_DV, 2 * GLA_DK), BF16),
        pltpu.VMEM((hps, n_chunks, 1, 2 * GLA_DK), F32),
    ]
    if has_s0:
        in_specs += [state, state]
        args += [s0f, s0b]
        out_specs = tok(GLA_DV, 0)
        out_shape = y_shape
    else:
        out_specs = [tok(GLA_DV, 0), state, state]
        out_shape = [y_shape, st_shape, st_shape]
    return pl.pallas_call(
        functools.partial(_gla_scan_kernel, n_groups=seq // GLA_GROUP, hps=hps, has_s0=has_s0),
        grid=(batch, steps),
        in_specs=in_specs,
        out_specs=out_specs,
        out_shape=out_shape,
        scratch_shapes=scratch,
        compiler_params=_params("arbitrary", "arbitrary"),
        name="gla_scan",
    )(*args)


def _out_ffn_kernel(*refs, final_norm, ctx_tiles):
    if final_norm:
        (xc_ref, xl_ref, yc_ref, yl_ref, wo_ref, g1_ref, gn_ref, sh_ref, sc_ref, g2_ref, wg_ref, wu_ref, wd_ref,
         gf_ref, oc_ref, ol_ref, wgb_ref, wub_ref, wdb_ref) = refs
    else:
        (xc_ref, xl_ref, yc_ref, yl_ref, wo_ref, g1_ref, gn_ref, sh_ref, sc_ref, g2_ref, wg_ref, wu_ref, wd_ref,
         oc_ref, ol_ref, wgb_ref, wub_ref, wdb_ref) = refs
    step = pl.program_id(0)

    @pl.when(step < FFN_WEIGHT_STEPS)
    def _():
        up_rows, down_rows = wg_ref.shape[0], wd_ref.shape[0]
        up = pl.ds(pl.multiple_of(step * up_rows, up_rows), up_rows)
        down = pl.ds(pl.multiple_of(step * down_rows, down_rows), down_rows)
        wgb_ref[up, :] = wg_ref[...].astype(BF16)
        wub_ref[up, :] = wu_ref[...].astype(BF16)
        wdb_ref[down, :] = wd_ref[...].astype(BF16)

    def tile(x_ref, y_ref, o_ref):
        x = x_ref[...] + g1_ref[...] * _dot(y_ref[...], wo_ref[...])
        h = _norm_mod(x, gn_ref[...], sh_ref[...], sc_ref[...]).astype(BF16)
        act = (_silu(_dot(h, wgb_ref[...])) * _dot(h, wub_ref[...])).astype(BF16)
        x = x + g2_ref[...] * _dot(act, wdb_ref[...])
        if final_norm:
            x = _rms(x, gf_ref[...])
        o_ref[...] = x

    first_lat = FFN_WEIGHT_STEPS + ctx_tiles
    pl.when((step >= FFN_WEIGHT_STEPS) & (step < first_lat))(lambda: tile(xc_ref, yc_ref, oc_ref))
    pl.when(step >= first_lat)(lambda: tile(xl_ref, yl_ref, ol_ref))


def _out_ffn(xc, xl, yc, yl, mod, layer, rows_per_batch, w_out, g_ffn, wg, wu, wd, g_final=None):
    ctx_tiles = xc.shape[0] // ROW_TILE
    lat_tiles = xl.shape[0] // ROW_TILE
    per_batch = rows_per_batch // ROW_TILE
    final_norm = g_final is not None
    first_ctx, first_lat = FFN_WEIGHT_STEPS, FFN_WEIGHT_STEPS + ctx_tiles
    ctx = lambda width: pl.BlockSpec((ROW_TILE, width), lambda i: (jnp.clip(i - first_ctx, 0, ctx_tiles - 1), 0))
    lat = lambda width: pl.BlockSpec((ROW_TILE, width), lambda i: (jnp.maximum(i - first_lat, 0), 0))
    mod_row = lambda i: jnp.where(i < first_lat, 0, 1 + (i - first_lat) // per_batch)
    mod_spec = lambda which: pl.BlockSpec((None, None, None, 1, D_MODEL), lambda i: (layer, mod_row(i), which, 0, 0))
    slab = lambda rows, cols: pl.BlockSpec((None, rows // FFN_WEIGHT_STEPS, cols),
                                           lambda i: (layer, jnp.minimum(i, FFN_WEIGHT_STEPS - 1), 0))
    in_specs = [
        ctx(D_MODEL), lat(D_MODEL), ctx(D_MODEL), lat(D_MODEL),
        _resident((D_MODEL, D_MODEL)),
        mod_spec(2),
        _resident((1, D_MODEL)),
        mod_spec(3), mod_spec(4), mod_spec(5),
        slab(D_MODEL, D_FF), slab(D_MODEL, D_FF), slab(D_FF, D_MODEL),
    ]
    args = [xc, xl, yc, yl, w_out, mod, g_ffn, mod, mod, mod, wg, wu, wd]
    if final_norm:
        in_specs.append(_resident((1, D_MODEL)))
        args.append(g_final)
    return pl.pallas_call(
        functools.partial(_out_ffn_kernel, final_norm=final_norm, ctx_tiles=ctx_tiles),
        grid=(FFN_WEIGHT_STEPS + ctx_tiles + lat_tiles,),
        in_specs=in_specs,
        out_specs=[ctx(D_MODEL), lat(D_MODEL)],
        out_shape=[jax.ShapeDtypeStruct(xc.shape, F32), jax.ShapeDtypeStruct(xl.shape, F32)],
        scratch_shapes=[pltpu.VMEM((D_MODEL, D_FF), BF16), pltpu.VMEM((D_MODEL, D_FF), BF16),
                        pltpu.VMEM((D_FF, D_MODEL), BF16)],
        compiler_params=_params("arbitrary"),
        name="out_ffn",
    )(*args)


def _rope_tables(n_tokens):
    rows = n_tokens // GRID_W
    r, col = jnp.meshgrid(jnp.arange(rows), jnp.arange(GRID_W), indexing="ij")
    r = r.reshape(-1).astype(F32)
    col = col.reshape(-1).astype(F32)
    n_freq = DIFF_DH // 4
    inv = ROPE_THETA ** (-jnp.arange(n_freq, dtype=F32) / n_freq)
    ang = jnp.concatenate([r[:, None] * inv, col[:, None] * inv], axis=-1)
    cos, sin = jnp.cos(ang), jnp.sin(ang)
    reps = D_MODEL // DIFF_DH
    return jnp.tile(jnp.concatenate([cos, cos], axis=-1), (1, reps)), jnp.tile(jnp.concatenate([-sin, sin], axis=-1), (1, reps))


def _rope(x, cos, sin_signed):
    half = DIFF_DH // 2
    width = x.shape[1]
    lane = lax.broadcasted_iota(jnp.int32, x.shape, 1)
    from_above = pltpu.roll(x, width - half, axis=1)
    from_below = pltpu.roll(x, half, axis=1)
    swapped = jnp.where(lane % DIFF_DH < half, from_above, from_below)
    return x * cos + swapped * sin_signed


def _qkv_kernel(*refs, rope, seq):
    if rope:
        (x_ref, g_ref, sh_ref, sc_ref, w_ref, cos_ref, sin_ref, q_ref, k_ref, v_ref) = refs
    else:
        (x_ref, g_ref, sh_ref, sc_ref, w_ref, q_ref, k_ref, v_ref, kf_ref, vf_ref) = refs
    for rows in _sub_tiles():
        h = _norm_mod(x_ref[rows, :], g_ref[...], sh_ref[...], sc_ref[...]).astype(BF16)
        z = _dot(h, w_ref[...])
        q = z[:, :D_MODEL]
        k = z[:, D_MODEL:2 * D_MODEL]
        v = z[:, 2 * D_MODEL:]
        if rope:
            q = _rope(q, cos_ref[rows, :], sin_ref[rows, :])
            k = _rope(k, cos_ref[rows, :], sin_ref[rows, :])
        else:
            for b in range(SUB_TILE // seq):
                kf_ref[rows.start // seq + b] = k[b * seq:(b + 1) * seq].T
            by_head = jnp.stack([v[:, head * DIFF_DV:(head + 1) * DIFF_DV] for head in range(DIFF_HEADS)], axis=0)
            vf_ref[rows] = jnp.swapaxes(by_head, 0, 1)
        q_ref[rows, :] = (q * (DIFF_DH ** -0.5 * LOG2E)).astype(BF16)
        k_ref[rows, :] = k.astype(BF16)
        v_ref[rows, :] = v.astype(BF16)


def _qkv(x, mod, layer, rows_per_batch, seq, g, w, tables=None):
    rows = x.shape[0]
    rope = tables is not None
    in_specs = [_row_spec(D_MODEL), _resident((1, D_MODEL)), _mod_spec(layer, 0, rows_per_batch),
                _mod_spec(layer, 1, rows_per_batch), _resident((D_MODEL, 3 * D_MODEL))]
    args = [x, g, mod, mod, w]
    out_specs = [_row_spec(D_MODEL)] * 3
    out_shape = [jax.ShapeDtypeStruct((rows, D_MODEL), BF16)] * 3
    if rope:
        tiles = seq // ROW_TILE
        tab = pl.BlockSpec((ROW_TILE, D_MODEL), lambda i: (i % tiles, 0))
        in_specs += [tab, tab]
        args += list(tables)
    else:
        per_tile = ROW_TILE // seq
        out_specs += [pl.BlockSpec((per_tile, D_MODEL, seq), lambda i: (i, 0, 0)),
                      pl.BlockSpec((ROW_TILE, DIFF_HEADS, DIFF_DV), lambda i: (i, 0, 0))]
        out_shape += [jax.ShapeDtypeStruct((rows // seq, D_MODEL, seq), F32),
                      jax.ShapeDtypeStruct((rows, DIFF_HEADS, DIFF_DV), F32)]
    return pl.pallas_call(
        functools.partial(_qkv_kernel, rope=rope, seq=seq),
        grid=(rows // ROW_TILE,),
        in_specs=in_specs,
        out_specs=out_specs,
        out_shape=out_shape,
        compiler_params=_params("arbitrary"),
        name="diff_qkv",
    )(*args)


def _attn_kernel(*refs, has_cache, lam_init, n_q_tiles, q_tile):
    if has_cache:
        (lq_ref, lk_ref, q_ref, k_ref, v_ref, gh_ref, kc_ref, vc_ref, y_ref, kcb_ref, vcb_ref) = refs
    else:
        (lq_ref, lk_ref, q_ref, k_ref, v_ref, gh_ref, y_ref) = refs
    e = jnp.exp(jnp.sum(lq_ref[...] * lk_ref[...], axis=1, keepdims=True))
    lam = e[0:1, :] - e[1:2, :] + lam_init
    first_map = lax.broadcasted_iota(jnp.int32, (q_tile, 2 * DIFF_DH), 1) < DIFF_DH
    gh = gh_ref[...] * (1.0 - lam_init)

    if has_cache:
        for head in range(DIFF_HEADS):
            kcb_ref[head] = kc_ref[head].astype(BF16)
        vcb_ref[...] = jnp.swapaxes(vc_ref[...], 0, 1).astype(BF16)

    def head_lanes(head):
        return slice(head * 2 * DIFF_DH, (head + 1) * 2 * DIFF_DH)

    def scores(head, sl):
        q = q_ref[sl, head_lanes(head)]
        zero = jnp.zeros_like(q)
        qq = jnp.concatenate([jnp.where(first_map, q, zero), jnp.where(first_map, zero, q)], axis=0)
        s = [_dot_nt(qq, k_ref[:, head_lanes(head)])]
        if has_cache:
            s.insert(0, _dot(qq, kcb_ref[head]))
        return s

    def attend(head, sl, s):
        vals = [v_ref[:, head_lanes(head)]]
        if has_cache:
            vals.insert(0, vcb_ref[head])
        m = functools.reduce(jnp.maximum, [jnp.max(x, axis=-1, keepdims=True) for x in s])
        p = [jnp.exp2(x - m) for x in s]
        denom = functools.reduce(jnp.add, [jnp.sum(x, axis=-1, keepdims=True) for x in p])
        acc = functools.reduce(jnp.add, [_dot(x.astype(BF16), vv) for x, vv in zip(p, vals)])
        o = acc[:q_tile] * (1.0 / denom[:q_tile]) - acc[q_tile:] * (lam / denom[q_tile:])
        y_ref[sl, head_lanes(head)] = _rms(o, gh).astype(BF16)

    def body(t, carry):
        sl = pl.ds(pl.multiple_of(t * q_tile, q_tile), q_tile)
        s_next = scores(0, sl)
        for head in range(DIFF_HEADS):
            s_cur = s_next
            if head + 1 < DIFF_HEADS:
                s_next = scores(head + 1, sl)
            attend(head, sl, s_cur)
        return carry

    lax.fori_loop(0, n_q_tiles, body, 0)


def _attn(q, k, v, lam_q, lam_k, g_head, lam_init, batch, seq, cache_k=None, cache_v=None):
    has_cache = cache_k is not None
    tok = pl.BlockSpec((seq, D_MODEL), lambda b: (b, 0))
    small = lambda shape: pl.BlockSpec(shape, lambda b: (0, 0))
    in_specs = [small((2, DIFF_DH)), small((2, DIFF_DH)), tok, tok, tok, small((1, DIFF_DV))]
    args = [lam_q, lam_k, q, k, v, g_head]
    if has_cache:
        past = cache_k.shape[-1]
        in_specs += [pl.BlockSpec((None, DIFF_HEADS, 2 * DIFF_DH, past), lambda b: (b, 0, 0, 0)),
                     pl.BlockSpec((None, past, DIFF_HEADS, DIFF_DV), lambda b: (b, 0, 0, 0))]
        args += [cache_k, cache_v]
        scratch = [pltpu.VMEM((DIFF_HEADS, 2 * DIFF_DH, past), BF16), pltpu.VMEM((DIFF_HEADS, past, DIFF_DV), BF16)]
    else:
        scratch = []
    q_tile = min(Q_TILE, seq)
    return pl.pallas_call(
        functools.partial(_attn_kernel, has_cache=has_cache, lam_init=lam_init, n_q_tiles=seq // q_tile, q_tile=q_tile),
        grid=(batch,),
        in_specs=in_specs,
        out_specs=tok,
        out_shape=jax.ShapeDtypeStruct((batch * seq, D_MODEL), BF16),
        scratch_shapes=scratch,
        compiler_params=_params("arbitrary"),
        name="diff_attn",
    )(*args)


def kernel(x_prompt, x_sample, state_gla_fwd, state_gla_bwd, cache_diff_k, cache_diff_v, c, c_ctx, w_ada, b_ada, g_mix_norm, g_ffn_norm, w_gla_in, w_gla_g1, w_gla_g2, b_gla_g, g_gla_head, w_gla_out, w_diff_qkv, lam_q, lam_k, g_diff_head, w_diff_out, w_ffn_gate, w_ffn_up, w_ffn_down, g_final):
    bp, tp, _ = x_prompt.shape
    bs, ts, _ = x_sample.shape
    assert SUB_TILE % tp == 0 and ts % ROW_TILE == 0 and (bp * tp) % ROW_TILE == 0
    assert tp % GLA_GROUP == 0 and ts % GLA_GROUP == 0
    xp = x_prompt.reshape(bp * tp, D_MODEL)
    xs = x_sample.reshape(bs * ts, D_MODEL)

    cond = jnp.concatenate([c_ctx[None, :], c, jnp.zeros((MOD_ROWS - 1 - bs, D_MODEL), F32)], axis=0)
    mod = _ada(cond, w_ada, b_ada).reshape(DEPTH, MOD_ROWS, N_MOD, 1, D_MODEL)
    wg, wu, wd = w_ffn_gate, w_ffn_up, w_ffn_down

    j = 0
    w_in = w_gla_in[j].astype(BF16)
    w1 = jnp.concatenate([w_gla_g1[j, 0], w_gla_g1[j, 1], jnp.zeros((D_MODEL, GATE_PAD - 2 * GLA_RANK), F32)],
                         axis=1).astype(BF16)
    w2 = jnp.zeros((GATE_PAD, 2 * GLA_HK), F32)
    w2 = w2.at[:GLA_RANK, :GLA_HK].set(w_gla_g2[j, 0]).at[GLA_RANK:2 * GLA_RANK, GLA_HK:].set(w_gla_g2[j, 1]).astype(BF16)
    bg = b_gla_g[j].reshape(1, 2 * GLA_HK)
    g_mix = g_mix_norm[0].reshape(1, D_MODEL)
    g_head = g_gla_head[j].reshape(1, GLA_DV)
    g_ffn = g_ffn_norm[0].reshape(1, D_MODEL)
    w_out = w_gla_out[j].astype(BF16)

    qp, kp, vp, rp, lap = _gla_in(xp, mod, 0, None, g_mix, w_in, w1, w2, bg)
    qs, ks, vs, rs, las = _gla_in(xs, mod, 0, ts, g_mix, w_in, w1, w2, bg)
    yp, new_f, new_b = _gla_scan(qp, kp, vp, lap, rp, g_head, bp, tp, GLA_HEADS)
    ys = _gla_scan(qs, ks, vs, las, rs, g_head, bs, ts, 2, state_gla_fwd[:, j], state_gla_bwd[:, j])
    xp, xs = _out_ffn(xp, xs, yp, ys, mod, 0, ts, w_out, g_ffn, wg, wu, wd)

    lam_init = 0.8 - 0.6 * math.exp(-0.3 * 1)
    g_mix = g_mix_norm[1].reshape(1, D_MODEL)
    w_qkv = w_diff_qkv[j].astype(BF16)
    g_head = g_diff_head[j].reshape(1, DIFF_DV)
    g_ffn = g_ffn_norm[1].reshape(1, D_MODEL)
    w_out = w_diff_out[j].astype(BF16)

    qp, kp, vp, kp32, vp32 = _qkv(xp, mod, 1, None, tp, g_mix, w_qkv)
    qs, ks, vs = _qkv(xs, mod, 1, ts, ts, g_mix, w_qkv, _rope_tables(ts))
    yp = _attn(qp, kp, vp, lam_q[j], lam_k[j], g_head, lam_init, bp, tp)
    past = cache_diff_k.shape[2]
    cache_k_t = jnp.transpose(cache_diff_k[:, j], (0, 2, 3, 4, 1)).reshape(bs, DIFF_HEADS, 2 * DIFF_DH, past)
    ys = _attn(qs, ks, vs, lam_q[j], lam_k[j], g_head, lam_init, bs, ts, cache_k_t, cache_diff_v[:, j])
    g_fin = g_final.reshape(1, D_MODEL)
    yp_out, ys_out = _out_ffn(xp, xs, yp, ys, mod, 1, ts, w_out, g_ffn, wg, wu, wd, g_final=g_fin)

    return (yp_out.reshape(bp, tp, D_MODEL),
            ys_out.reshape(bs, ts, D_MODEL),
            new_f[:, None],
            new_b[:, None],
            jnp.transpose(kp32.reshape(bp, 1, DIFF_HEADS, 2, DIFF_DH, tp), (0, 1, 5, 2, 3, 4)),
            vp32.reshape(bp, 1, tp, DIFF_HEADS, DIFF_DV))
```

```python
import functools
import math

import jax
import jax.numpy as jnp
from jax import lax
from jax.experimental import pallas as pl
from jax.experimental.pallas import tpu as pltpu

F32 = jnp.float32
BF16 = jnp.bfloat16

D_MODEL = 1024
DEPTH = 2
GRID_W = 64
GLA_HEADS = 4
GLA_DK = 128
GLA_DV = 256
GLA_HK = GLA_HEADS * GLA_DK
GLA_HV = GLA_HEADS * GLA_DV
GLA_RANK = 16
GLA_TAU = 16.0
GLA_CHUNK = 64
GLA_GROUP = 256
GLA_CHAINS = 8
DIFF_HEADS = 8
DIFF_DH = 64
DIFF_DV = 128
ROPE_THETA = 10000.0
D_FF = 2816
EPS = 1e-6
LOG2E = math.log2(math.e)
N_MOD = 6
MOD_ROWS = 16
GATE_PAD = 128

VMEM_LIMIT_BYTES = 58 * 1024 * 1024
ROW_TILE = 512
SUB_TILE = 512
FFN_WEIGHT_STEPS = 8
GLA_WEIGHT_STEPS = 4
Q_TILE = 256


def _params(*sem):
    return pltpu.CompilerParams(dimension_semantics=sem, vmem_limit_bytes=VMEM_LIMIT_BYTES)


def _resident(shape, index=None):
    index = (0,) * len(shape) if index is None else index
    return pl.BlockSpec(shape, lambda *_: index, pipeline_mode=pl.Buffered(1))


def _sub_tiles():
    return [slice(s * SUB_TILE, (s + 1) * SUB_TILE) for s in range(ROW_TILE // SUB_TILE)]


def _sigmoid(x):
    return 1.0 / (1.0 + jnp.exp(-x))


def _silu(x):
    return x * _sigmoid(x)


def _rms(x, g):
    return x * lax.rsqrt(jnp.mean(x * x, axis=-1, keepdims=True) + EPS) * g


def _norm_mod(x, g, shift, scale):
    return x * lax.rsqrt(jnp.mean(x * x, axis=-1, keepdims=True) + EPS) * (g * (1.0 + scale)) + shift


def _dot(a, b):
    return jnp.dot(a, b, preferred_element_type=F32)


def _dot_nt(a, b):
    return lax.dot_general(a, b, (((1,), (1,)), ((), ())), preferred_element_type=F32)


def _dot_tn(a, b):
    return lax.dot_general(a, b, (((0,), (0,)), ((), ())), preferred_element_type=F32)


def _ada_kernel(c_ref, w_ref, b_ref, o_ref):
    a = _silu(c_ref[...]).astype(BF16)
    o_ref[...] = _dot(a, w_ref[...].astype(BF16)) + b_ref[...]


def _ada(cond, w_ada, b_ada):
    tn = 1536
    n = N_MOD * D_MODEL
    return pl.pallas_call(
        _ada_kernel,
        grid=(DEPTH, n // tn),
        in_specs=[
            pl.BlockSpec((MOD_ROWS, D_MODEL), lambda l, j: (0, 0)),
            pl.BlockSpec((None, D_MODEL, tn), lambda l, j: (l, 0, j)),
            pl.BlockSpec((None, 1, tn), lambda l, j: (l, 0, j)),
        ],
        out_specs=pl.BlockSpec((None, MOD_ROWS, tn), lambda l, j: (l, 0, j)),
        out_shape=jax.ShapeDtypeStruct((DEPTH, MOD_ROWS, n), F32),
        compiler_params=_params("arbitrary", "arbitrary"),
        name="ada",
    )(cond, w_ada, b_ada.reshape(DEPTH, 1, n))


def _mod_spec(layer, which, rows_per_batch):
    if rows_per_batch is None:
        return pl.BlockSpec((None, None, None, 1, D_MODEL), lambda i: (layer, 0, which, 0, 0))
    tiles = rows_per_batch // ROW_TILE
    return pl.BlockSpec((None, None, None, 1, D_MODEL), lambda i: (layer, 1 + i // tiles, which, 0, 0))


def _row_spec(width):
    return pl.BlockSpec((ROW_TILE, width), lambda i: (i, 0))


def _log_sigmoid(x):
    return jnp.minimum(x, 0.0) - jnp.log(1.0 + jnp.exp(-jnp.abs(x)))


class _TwoStreams:
    def __init__(self, ctx_rows, lat_rows, lat_seq, weight_steps):
        self.weight_steps = weight_steps
        self.ctx_tiles = ctx_rows // ROW_TILE
        self.first_lat = weight_steps + self.ctx_tiles
        self.grid = (self.first_lat + lat_rows // ROW_TILE,)
        self.tiles_per_batch = lat_seq // ROW_TILE

    def ctx(self, width):
        return pl.BlockSpec((ROW_TILE, width), lambda i: (jnp.clip(i - self.weight_steps, 0, self.ctx_tiles - 1), 0))

    def lat(self, width):
        return pl.BlockSpec((ROW_TILE, width), lambda i: (jnp.maximum(i - self.first_lat, 0), 0))

    def mod(self, layer, which):
        row = lambda i: jnp.where(i < self.first_lat, 0, 1 + (i - self.first_lat) // self.tiles_per_batch)
        return pl.BlockSpec((None, None, None, 1, D_MODEL), lambda i: (layer, row(i), which, 0, 0))

    def slab(self, layer, rows, cols):
        return pl.BlockSpec((None, rows // self.weight_steps, cols),
                            lambda i: (layer, jnp.minimum(i, self.weight_steps - 1), 0))

    def run(self, load_weights, ctx_tile, lat_tile):
        step = pl.program_id(0)
        pl.when(step < self.weight_steps)(lambda: load_weights(step))
        pl.when((step >= self.weight_steps) & (step < self.first_lat))(ctx_tile)
        pl.when(step >= self.first_lat)(lat_tile)


def _round_slab(step, src_ref, dst_ref):
    rows = src_ref.shape[0]
    dst_ref[pl.ds(pl.multiple_of(step * rows, rows), rows), :] = src_ref[...].astype(BF16)


def _gla_in_kernel(xc_ref, xl_ref, g_ref, sh_ref, sc_ref, w_ref, w1_ref, w2_ref, bg_ref, *refs, streams):
    outs_c, outs_l, wb_ref = refs[0:5], refs[5:10], refs[10]

    def tile(x_ref, q_ref, k_ref, v_ref, r_ref, la_ref):
        h = _norm_mod(x_ref[...], g_ref[...], sh_ref[...], sc_ref[...]).astype(BF16)
        z = _dot(h, wb_ref[...])
        q_ref[...] = z[:, :GLA_HK]
        k_ref[...] = z[:, GLA_HK:2 * GLA_HK]
        v_ref[...] = z[:, 2 * GLA_HK:2 * GLA_HK + GLA_HV].astype(BF16)
        r_ref[...] = z[:, 2 * GLA_HK + GLA_HV:]
        low_rank = _dot(h, w1_ref[...]).astype(BF16)
        pre = _dot(low_rank, w2_ref[...]) + bg_ref[...]
        la_ref[...] = _log_sigmoid(pre) * (1.0 / GLA_TAU)

    streams.run(lambda step: _round_slab(step, w_ref, wb_ref),
                lambda: tile(xc_ref, *outs_c), lambda: tile(xl_ref, *outs_l))


def _gla_in(xc, xl, lat_seq, mod, layer, g, w_in, w1, w2, bg):
    n_out = 2 * GLA_HK + 2 * GLA_HV
    streams = _TwoStreams(xc.shape[0], xl.shape[0], lat_seq, GLA_WEIGHT_STEPS)
    widths = [(GLA_HK, F32), (GLA_HK, F32), (GLA_HV, BF16), (GLA_HV, F32), (2 * GLA_HK, F32)]
    outs = pl.pallas_call(
        functools.partial(_gla_in_kernel, streams=streams),
        grid=streams.grid,
        in_specs=[
            streams.ctx(D_MODEL), streams.lat(D_MODEL),
            _resident((1, D_MODEL)),
            streams.mod(layer, 0), streams.mod(layer, 1),
            streams.slab(layer, D_MODEL, n_out),
            _resident((D_MODEL, GATE_PAD)),
            _resident((GATE_PAD, 2 * GLA_HK)),
            _resident((1, 2 * GLA_HK)),
        ],
        out_specs=[streams.ctx(w) for w, _ in widths] + [streams.lat(w) for w, _ in widths],
        out_shape=([jax.ShapeDtypeStruct((xc.shape[0], w), dt) for w, dt in widths]
                   + [jax.ShapeDtypeStruct((xl.shape[0], w), dt) for w, dt in widths]),
        scratch_shapes=[pltpu.VMEM((D_MODEL, n_out), BF16)],
        compiler_params=_params("arbitrary"),
        name="gla_in",
    )(xc, xl, g, mod, mod, w_in, w1, w2, bg)
    return outs[:5], outs[5:]


def _split3(x):
    hi = x.astype(BF16)
    rem = x - hi.astype(F32)
    mid = rem.astype(BF16)
    lo = (rem - mid.astype(F32)).astype(BF16)
    return jnp.concatenate([hi, mid, lo], axis=1)


def _fold3(x):
    return x[:, :GLA_DK] + x[:, GLA_DK:2 * GLA_DK] + x[:, 2 * GLA_DK:]


def _gla_scan_kernel(*refs, n_groups, hps, has_s0):
    C, DK, DV, G = GLA_CHUNK, GLA_DK, GLA_DV, GLA_GROUP
    cpg = G // C
    n_chunks = n_groups * cpg
    unroll = max(1, min(n_groups, GLA_CHAINS // hps))
    if has_s0:
        (q_ref, k_ref, v_ref, laf_ref, lab_ref, r_ref, gh_ref, s0f_ref, s0b_ref,
         y_ref, qd_ref, kd_ref, ds_ref, sin_ref, dec_ref) = refs
    else:
        (q_ref, k_ref, v_ref, laf_ref, lab_ref, r_ref, gh_ref,
         y_ref, sf_ref, sb_ref, qd_ref, kd_ref, ds_ref, sin_ref, dec_ref) = refs

    row = lax.broadcasted_iota(jnp.int32, (G, G), 0)
    col = lax.broadcasted_iota(jnp.int32, (G, G), 1)
    same_chunk = (row // C) == (col // C)
    lower = same_chunk & (row >= col)
    upper = same_chunk & (row <= col)
    prefix_sum = jnp.where(lower, 1.0, 0.0).astype(BF16)
    suffix_sum = jnp.where(upper, 1.0, 0.0).astype(BF16)

    def group(g):
        return pl.ds(pl.multiple_of(g * G, G), G)

    def spread(x, first_row):
        return jnp.concatenate(
            [jnp.broadcast_to(x[first_row + c * C:first_row + c * C + 1], (C, x.shape[1])) for c in range(cpg)], axis=0)

    def pass1(g, carry):
        sl = group(g)
        for hd in range(hps):
            dk = slice(hd * DK, (hd + 1) * DK)
            dv = slice(hd * DV, (hd + 1) * DV)
            both = slice(hd * 2 * DK, (hd + 1) * 2 * DK)
            bf = _fold3(_dot(prefix_sum, _split3(laf_ref[sl, dk])))
            bb = _fold3(_dot(suffix_sum, _split3(lab_ref[sl, dk])))
            tot_f = spread(bf, C - 1)
            tot_b = spread(bb, 0)
            q = q_ref[sl, dk] * (DK ** -0.5)
            k = k_ref[sl, dk]
            qd_ref[sl, both] = jnp.concatenate([q * jnp.exp(bf), q * jnp.exp(bb)], axis=1).astype(BF16)
            kd_ref[sl, both] = jnp.concatenate([k * jnp.exp(-bf), k * jnp.exp(-bb)], axis=1).astype(BF16)
            k_end = jnp.concatenate([k * jnp.exp(tot_f - bf), k * jnp.exp(tot_b - bb)], axis=1).astype(BF16)
            v = v_ref[sl, dv]
            for c in range(cpg):
                rows = slice(c * C, (c + 1) * C)
                ds_ref[hd, g * cpg + c] = _dot_tn(v[rows], k_end[rows])
                dec_ref[hd, g * cpg + c] = jnp.exp(
                    jnp.concatenate([tot_f[c * C:c * C + 1], tot_b[c * C:c * C + 1]], axis=1))
        return carry

    lax.fori_loop(0, n_groups, pass1, 0, unroll=unroll)

    for hd in range(hps):
        for forward in (True, False):
            lanes = slice(0, DK) if forward else slice(DK, 2 * DK)
            if has_s0:
                init = (s0f_ref if forward else s0b_ref)[hd].T
            else:
                init = jnp.zeros((DV, DK), F32)

            def step(i, s, hd=hd, forward=forward, lanes=lanes):
                n = i if forward else n_chunks - 1 - i
                sin_ref[hd, n, :, lanes] = s.astype(BF16)
                return s * dec_ref[hd, n, :, lanes] + ds_ref[hd, n, :, lanes]

            final = lax.fori_loop(0, n_chunks, step, init)
            if not has_s0:
                (sf_ref if forward else sb_ref)[hd] = final.T

    gh = gh_ref[...]

    def pass3(g, carry):
        sl = group(g)
        for hd in range(hps):
            dv = slice(hd * DV, (hd + 1) * DV)
            both = slice(hd * 2 * DK, (hd + 1) * 2 * DK)
            qd = qd_ref[sl, both]
            kd = kd_ref[sl, both]
            a_f = _dot_nt(qd[:, :DK], kd[:, :DK])
            a_b = _dot_nt(qd[:, DK:], kd[:, DK:])
            a = jnp.where(lower, a_f, 0.0) + jnp.where(upper, a_b, 0.0)
            inter = [_dot_nt(qd[c * C:(c + 1) * C], sin_ref[hd, g * cpg + c]) for c in range(cpg)]
            o = _dot(a.astype(BF16), v_ref[sl, dv]) + jnp.concatenate(inter, axis=0)
            y_ref[sl, dv] = (_rms(o, gh) * _silu(r_ref[sl, dv])).astype(BF16)
        return carry

    lax.fori_loop(0, n_groups, pass3, 0, unroll=unroll)


def _gla_scan(q, k, v, la, r, g_head, batch, seq, hps, s0f=None, s0b=None):
    steps = GLA_HEADS // hps
    n_chunks = seq // GLA_CHUNK
    has_s0 = s0f is not None
    rows = batch * seq
    tok = lambda w, off: pl.BlockSpec((seq, w * hps), lambda b, h: (b, h + off))
    state = pl.BlockSpec((None, hps, GLA_DK, GLA_DV), lambda b, h: (b, h, 0, 0))
    in_specs = [tok(GLA_DK, 0), tok(GLA_DK, 0), tok(GLA_DV, 0), tok(GLA_DK, 0), tok(GLA_DK, steps),
                tok(GLA_DV, 0), pl.BlockSpec((1, GLA_DV), lambda b, h: (0, 0))]
    args = [q, k, v, la, la, r, g_head]
    y_shape = jax.ShapeDtypeStruct((rows, GLA_HV), BF16)
    st_shape = jax.ShapeDtypeStruct((batch, GLA_HEADS, GLA_DK, GLA_DV), F32)
    scratch = [
        pltpu.VMEM((seq, hps * 2 * GLA_DK), BF16),
        pltpu.VMEM((seq, hps * 2 * GLA_DK), BF16),
        pltpu.VMEM((hps, n_chunks, GLA_DV, 2 * GLA_DK), F32),
        pltpu.VMEM((hps, n_chunks, GLA_DV, 2 * GLA_DK), BF16),
        pltpu.VMEM((hps, n_chunks, 1, 2 * GLA_DK), F32),
    ]
    if has_s0:
        in_specs += [state, state]
        args += [s0f, s0b]
        out_specs = tok(GLA_DV, 0)
        out_shape = y_shape
    else:
        out_specs = [tok(GLA_DV, 0), state, state]
        out_shape = [y_shape, st_shape, st_shape]
    return pl.pallas_call(
        functools.partial(_gla_scan_kernel, n_groups=seq // GLA_GROUP, hps=hps, has_s0=has_s0),
        grid=(batch, steps),
        in_specs=in_specs,
        out_specs=out_specs,
        out_shape=out_shape,
        scratch_shapes=scratch,
        compiler_params=_params("arbitrary", "arbitrary"),
        name="gla_scan",
    )(*args)


def _out_ffn_kernel(*refs, final_norm, streams):
    if final_norm:
        (xc_ref, xl_ref, yc_ref, yl_ref, wo_ref, g1_ref, gn_ref, sh_ref, sc_ref, g2_ref, wg_ref, wu_ref, wd_ref,
         gf_ref, oc_ref, ol_ref, wgb_ref, wub_ref, wdb_ref) = refs
    else:
        (xc_ref, xl_ref, yc_ref, yl_ref, wo_ref, g1_ref, gn_ref, sh_ref, sc_ref, g2_ref, wg_ref, wu_ref, wd_ref,
         oc_ref, ol_ref, wgb_ref, wub_ref, wdb_ref) = refs

    def load_weights(step):
        _round_slab(step, wg_ref, wgb_ref)
        _round_slab(step, wu_ref, wub_ref)
        _round_slab(step, wd_ref, wdb_ref)

    def tile(x_ref, y_ref, o_ref):
        x = x_ref[...] + g1_ref[...] * _dot(y_ref[...], wo_ref[...])
        h = _norm_mod(x, gn_ref[...], sh_ref[...], sc_ref[...]).astype(BF16)
        act = (_silu(_dot(h, wgb_ref[...])) * _dot(h, wub_ref[...])).astype(BF16)
        x = x + g2_ref[...] * _dot(act, wdb_ref[...])
        if final_norm:
            x = _rms(x, gf_ref[...])
        o_ref[...] = x

    streams.run(load_weights, lambda: tile(xc_ref, yc_ref, oc_ref), lambda: tile(xl_ref, yl_ref, ol_ref))


def _out_ffn(xc, xl, yc, yl, lat_seq, mod, layer, w_out, g_ffn, wg, wu, wd, g_final=None):
    final_norm = g_final is not None
    streams = _TwoStreams(xc.shape[0], xl.shape[0], lat_seq, FFN_WEIGHT_STEPS)
    in_specs = [
        streams.ctx(D_MODEL), streams.lat(D_MODEL), streams.ctx(D_MODEL), streams.lat(D_MODEL),
        _resident((D_MODEL, D_MODEL)),
        streams.mod(layer, 2),
        _resident((1, D_MODEL)),
        streams.mod(layer, 3), streams.mod(layer, 4), streams.mod(layer, 5),
        streams.slab(layer, D_MODEL, D_FF), streams.slab(layer, D_MODEL, D_FF), streams.slab(layer, D_FF, D_MODEL),
    ]
    args = [xc, xl, yc, yl, w_out, mod, g_ffn, mod, mod, mod, wg, wu, wd]
    if final_norm:
        in_specs.append(_resident((1, D_MODEL)))
        args.append(g_final)
    return pl.pallas_call(
        functools.partial(_out_ffn_kernel, final_norm=final_norm, streams=streams),
        grid=streams.grid,
        in_specs=in_specs,
        out_specs=[streams.ctx(D_MODEL), streams.lat(D_MODEL)],
        out_shape=[jax.ShapeDtypeStruct(xc.shape, F32), jax.ShapeDtypeStruct(xl.shape, F32)],
        scratch_shapes=[pltpu.VMEM((D_MODEL, D_FF), BF16), pltpu.VMEM((D_MODEL, D_FF), BF16),
                        pltpu.VMEM((D_FF, D_MODEL), BF16)],
        compiler_params=_params("arbitrary"),
        name="out_ffn",
    )(*args)


def _rope_tables(n_tokens):
    rows = n_tokens // GRID_W
    r, col = jnp.meshgrid(jnp.arange(rows), jnp.arange(GRID_W), indexing="ij")
    r = r.reshape(-1).astype(F32)
    col = col.reshape(-1).astype(F32)
    n_freq = DIFF_DH // 4
    inv = ROPE_THETA ** (-jnp.arange(n_freq, dtype=F32) / n_freq)
    ang = jnp.concatenate([r[:, None] * inv, col[:, None] * inv], axis=-1)
    cos, sin = jnp.cos(ang), jnp.sin(ang)
    reps = D_MODEL // DIFF_DH
    return jnp.tile(jnp.concatenate([cos, cos], axis=-1), (1, reps)), jnp.tile(jnp.concatenate([-sin, sin], axis=-1), (1, reps))


def _rope(x, cos, sin_signed):
    half = DIFF_DH // 2
    width = x.shape[1]
    lane = lax.broadcasted_iota(jnp.int32, x.shape, 1)
    from_above = pltpu.roll(x, width - half, axis=1)
    from_below = pltpu.roll(x, half, axis=1)
    swapped = jnp.where(lane % DIFF_DH < half, from_above, from_below)
    return x * cos + swapped * sin_signed


def _qkv_kernel(*refs, rope, seq):
    if rope:
        (x_ref, g_ref, sh_ref, sc_ref, w_ref, cos_ref, sin_ref, q_ref, k_ref, v_ref) = refs
    else:
        (x_ref, g_ref, sh_ref, sc_ref, w_ref, q_ref, k_ref, v_ref, kf_ref, vf_ref) = refs
    for rows in _sub_tiles():
        h = _norm_mod(x_ref[rows, :], g_ref[...], sh_ref[...], sc_ref[...]).astype(BF16)
        z = _dot(h, w_ref[...])
        q = z[:, :D_MODEL]
        k = z[:, D_MODEL:2 * D_MODEL]
        v = z[:, 2 * D_MODEL:]
        if rope:
            q = _rope(q, cos_ref[rows, :], sin_ref[rows, :])
            k = _rope(k, cos_ref[rows, :], sin_ref[rows, :])
        else:
            for b in range(SUB_TILE // seq):
                kf_ref[rows.start // seq + b] = k[b * seq:(b + 1) * seq].T
            by_head = jnp.stack([v[:, head * DIFF_DV:(head + 1) * DIFF_DV] for head in range(DIFF_HEADS)], axis=0)
            vf_ref[rows] = jnp.swapaxes(by_head, 0, 1)
        q_ref[rows, :] = (q * (DIFF_DH ** -0.5 * LOG2E)).astype(BF16)
        k_ref[rows, :] = k.astype(BF16)
        v_ref[rows, :] = v.astype(BF16)


def _qkv(x, mod, layer, rows_per_batch, seq, g, w, tables=None):
    rows = x.shape[0]
    rope = tables is not None
    in_specs = [_row_spec(D_MODEL), _resident((1, D_MODEL)), _mod_spec(layer, 0, rows_per_batch),
                _mod_spec(layer, 1, rows_per_batch), _resident((D_MODEL, 3 * D_MODEL))]
    args = [x, g, mod, mod, w]
    out_specs = [_row_spec(D_MODEL)] * 3
    out_shape = [jax.ShapeDtypeStruct((rows, D_MODEL), BF16)] * 3
    if rope:
        tiles = seq // ROW_TILE
        tab = pl.BlockSpec((ROW_TILE, D_MODEL), lambda i: (i % tiles, 0))
        in_specs += [tab, tab]
        args += list(tables)
    else:
        per_tile = ROW_TILE // seq
        out_specs += [pl.BlockSpec((per_tile, D_MODEL, seq), lambda i: (i, 0, 0)),
                      pl.BlockSpec((ROW_TILE, DIFF_HEADS, DIFF_DV), lambda i: (i, 0, 0))]
        out_shape += [jax.ShapeDtypeStruct((rows // seq, D_MODEL, seq), F32),
                      jax.ShapeDtypeStruct((rows, DIFF_HEADS, DIFF_DV), F32)]
    return pl.pallas_call(
        functools.partial(_qkv_kernel, rope=rope, seq=seq),
        grid=(rows // ROW_TILE,),
        in_specs=in_specs,
        out_specs=out_specs,
        out_shape=out_shape,
        compiler_params=_params("arbitrary"),
        name="diff_qkv",
    )(*args)


def _attn_kernel(*refs, has_cache, lam_init, n_q_tiles, q_tile):
    if has_cache:
        (lq_ref, lk_ref, q_ref, k_ref, v_ref, gh_ref, kc_ref, vc_ref, y_ref, kcb_ref, vcb_ref) = refs
    else:
        (lq_ref, lk_ref, q_ref, k_ref, v_ref, gh_ref, y_ref) = refs
    e = jnp.exp(jnp.sum(lq_ref[...] * lk_ref[...], axis=1, keepdims=True))
    lam = e[0:1, :] - e[1:2, :] + lam_init
    first_map = lax.broadcasted_iota(jnp.int32, (q_tile, 2 * DIFF_DH), 1) < DIFF_DH
    gh = gh_ref[...] * (1.0 - lam_init)

    if has_cache:
        for head in range(DIFF_HEADS):
            kcb_ref[head] = kc_ref[head].astype(BF16)
        vcb_ref[...] = jnp.swapaxes(vc_ref[...], 0, 1).astype(BF16)

    def head_lanes(head):
        return slice(head * 2 * DIFF_DH, (head + 1) * 2 * DIFF_DH)

    def scores(head, sl):
        q = q_ref[sl, head_lanes(head)]
        zero = jnp.zeros_like(q)
        qq = jnp.concatenate([jnp.where(first_map, q, zero), jnp.where(first_map, zero, q)], axis=0)
        s = [_dot_nt(qq, k_ref[:, head_lanes(head)])]
        if has_cache:
            s.insert(0, _dot(qq, kcb_ref[head]))
        return s

    def attend(head, sl, s):
        vals = [v_ref[:, head_lanes(head)]]
        if has_cache:
            vals.insert(0, vcb_ref[head])
        m = functools.reduce(jnp.maximum, [jnp.max(x, axis=-1, keepdims=True) for x in s])
        p = [jnp.exp2(x - m) for x in s]
        denom = functools.reduce(jnp.add, [jnp.sum(x, axis=-1, keepdims=True) for x in p])
        acc = functools.reduce(jnp.add, [_dot(x.astype(BF16), vv) for x, vv in zip(p, vals)])
        o = acc[:q_tile] * (1.0 / denom[:q_tile]) - acc[q_tile:] * (lam / denom[q_tile:])
        y_ref[sl, head_lanes(head)] = _rms(o, gh).astype(BF16)

    def body(t, carry):
        sl = pl.ds(pl.multiple_of(t * q_tile, q_tile), q_tile)
        s_next = scores(0, sl)
        for head in range(DIFF_HEADS):
            s_cur = s_next
            if head + 1 < DIFF_HEADS:
                s_next = scores(head + 1, sl)
            attend(head, sl, s_cur)
        return carry

    lax.fori_loop(0, n_q_tiles, body, 0)


def _attn(q, k, v, lam_q, lam_k, g_head, lam_init, batch, seq, cache_k=None, cache_v=None):
    has_cache = cache_k is not None
    tok = pl.BlockSpec((seq, D_MODEL), lambda b: (b, 0))
    small = lambda shape: pl.BlockSpec(shape, lambda b: (0, 0))
    in_specs = [small((2, DIFF_DH)), small((2, DIFF_DH)), tok, tok, tok, small((1, DIFF_DV))]
    args = [lam_q, lam_k, q, k, v, g_head]
    if has_cache:
        past = cache_k.shape[-1]
        in_specs += [pl.BlockSpec((None, DIFF_HEADS, 2 * DIFF_DH, past), lambda b: (b, 0, 0, 0)),
                     pl.BlockSpec((None, past, DIFF_HEADS, DIFF_DV), lambda b: (b, 0, 0, 0))]
        args += [cache_k, cache_v]
        scratch = [pltpu.VMEM((DIFF_HEADS, 2 * DIFF_DH, past), BF16), pltpu.VMEM((DIFF_HEADS, past, DIFF_DV), BF16)]
    else:
        scratch = []
    q_tile = min(Q_TILE, seq)
    return pl.pallas_call(
        functools.partial(_attn_kernel, has_cache=has_cache, lam_init=lam_init, n_q_tiles=seq // q_tile, q_tile=q_tile),
        grid=(batch,),
        in_specs=in_specs,
        out_specs=tok,
        out_shape=jax.ShapeDtypeStruct((batch * seq, D_MODEL), BF16),
        scratch_shapes=scratch,
        compiler_params=_params("arbitrary"),
        name="diff_attn",
    )(*args)


def kernel(x_prompt, x_sample, state_gla_fwd, state_gla_bwd, cache_diff_k, cache_diff_v, c, c_ctx, w_ada, b_ada, g_mix_norm, g_ffn_norm, w_gla_in, w_gla_g1, w_gla_g2, b_gla_g, g_gla_head, w_gla_out, w_diff_qkv, lam_q, lam_k, g_diff_head, w_diff_out, w_ffn_gate, w_ffn_up, w_ffn_down, g_final):
    bp, tp, _ = x_prompt.shape
    bs, ts, _ = x_sample.shape
    assert SUB_TILE % tp == 0 and ts % ROW_TILE == 0 and (bp * tp) % ROW_TILE == 0
    assert tp % GLA_GROUP == 0 and ts % GLA_GROUP == 0
    xp = x_prompt.reshape(bp * tp, D_MODEL)
    xs = x_sample.reshape(bs * ts, D_MODEL)

    cond = jnp.concatenate([c_ctx[None, :], c, jnp.zeros((MOD_ROWS - 1 - bs, D_MODEL), F32)], axis=0)
    mod = _ada(cond, w_ada, b_ada).reshape(DEPTH, MOD_ROWS, N_MOD, 1, D_MODEL)
    wg, wu, wd = w_ffn_gate, w_ffn_up, w_ffn_down

    j = 0
    w1 = jnp.concatenate([w_gla_g1[j, 0], w_gla_g1[j, 1], jnp.zeros((D_MODEL, GATE_PAD - 2 * GLA_RANK), F32)],
                         axis=1).astype(BF16)
    w2 = jnp.zeros((GATE_PAD, 2 * GLA_HK), F32)
    w2 = w2.at[:GLA_RANK, :GLA_HK].set(w_gla_g2[j, 0]).at[GLA_RANK:2 * GLA_RANK, GLA_HK:].set(w_gla_g2[j, 1]).astype(BF16)
    bg = b_gla_g[j].reshape(1, 2 * GLA_HK)
    g_mix = g_mix_norm[0].reshape(1, D_MODEL)
    g_head = g_gla_head[j].reshape(1, GLA_DV)
    g_ffn = g_ffn_norm[0].reshape(1, D_MODEL)
    w_out = w_gla_out[j].astype(BF16)

    (qp, kp, vp, rp, lap), (qs, ks, vs, rs, las) = _gla_in(xp, xs, ts, mod, 0, g_mix, w_gla_in, w1, w2, bg)
    yp, new_f, new_b = _gla_scan(qp, kp, vp, lap, rp, g_head, bp, tp, GLA_HEADS)
    ys = _gla_scan(qs, ks, vs, las, rs, g_head, bs, ts, 2, state_gla_fwd[:, j], state_gla_bwd[:, j])
    xp, xs = _out_ffn(xp, xs, yp, ys, ts, mod, 0, w_out, g_ffn, wg, wu, wd)

    lam_init = 0.8 - 0.6 * math.exp(-0.3 * 1)
    g_mix = g_mix_norm[1].reshape(1, D_MODEL)
    w_qkv = w_diff_qkv[j].astype(BF16)
    g_head = g_diff_head[j].reshape(1, DIFF_DV)
    g_ffn = g_ffn_norm[1].reshape(1, D_MODEL)
    w_out = w_diff_out[j].astype(BF16)

    qp, kp, vp, kp32, vp32 = _qkv(xp, mod, 1, None, tp, g_mix, w_qkv)
    qs, ks, vs = _qkv(xs, mod, 1, ts, ts, g_mix, w_qkv, _rope_tables(ts))
    yp = _attn(qp, kp, vp, lam_q[j], lam_k[j], g_head, lam_init, bp, tp)
    past = cache_diff_k.shape[2]
    cache_k_t = jnp.transpose(cache_diff_k[:, j], (0, 2, 3, 4, 1)).reshape(bs, DIFF_HEADS, 2 * DIFF_DH, past)
    ys = _attn(qs, ks, vs, lam_q[j], lam_k[j], g_head, lam_init, bs, ts, cache_k_t, cache_diff_v[:, j])
    g_fin = g_final.reshape(1, D_MODEL)
    yp_out, ys_out = _out_ffn(xp, xs, yp, ys, ts, mod, 1, w_out, g_ffn, wg, wu, wd, g_final=g_fin)

    return (yp_out.reshape(bp, tp, D_MODEL),
            ys_out.reshape(bs, ts, D_MODEL),
            new_f[:, None],
            new_b[:, None],
            jnp.transpose(kp32.reshape(bp, 1, DIFF_HEADS, 2, DIFF_DH, tp), (0, 1, 5, 2, 3, 4)),
            vp32.reshape(bp, 1, tp, DIFF_HEADS, DIFF_DV))
```

```python
import functools
import math

import jax
import jax.numpy as jnp
from jax import lax
from jax.experimental import pallas as pl
from jax.experimental.pallas import tpu as pltpu

F32 = jnp.float32
BF16 = jnp.bfloat16

D_MODEL = 1024
DEPTH = 2
GRID_W = 64
GLA_HEADS = 4
GLA_DK = 128
GLA_DV = 256
GLA_HK = GLA_HEADS * GLA_DK
GLA_HV = GLA_HEADS * GLA_DV
GLA_RANK = 16
GLA_TAU = 16.0
GLA_CHUNK = 64
GLA_GROUP = 256
GLA_CHAINS = 8
DIFF_HEADS = 8
DIFF_DH = 64
DIFF_DV = 128
ROPE_THETA = 10000.0
D_FF = 2816
EPS = 1e-6
LOG2E = math.log2(math.e)
N_MOD = 6
MOD_ROWS = 16
GATE_PAD = 128

VMEM_LIMIT_BYTES = 58 * 1024 * 1024
ROW_TILE = 512
FFN_WEIGHT_STEPS = 8
GLA_WEIGHT_STEPS = 4
QKV_WEIGHT_STEPS = 4
Q_TILE = 256


def _params(*sem):
    return pltpu.CompilerParams(dimension_semantics=sem, vmem_limit_bytes=VMEM_LIMIT_BYTES)


def _resident(shape, index=None):
    index = (0,) * len(shape) if index is None else index
    return pl.BlockSpec(shape, lambda *_: index, pipeline_mode=pl.Buffered(1))


def _sigmoid(x):
    return 1.0 / (1.0 + jnp.exp(-x))


def _silu(x):
    return x * _sigmoid(x)


def _rms(x, g):
    return x * lax.rsqrt(jnp.mean(x * x, axis=-1, keepdims=True) + EPS) * g


def _norm_mod(x, g, shift, scale):
    return x * lax.rsqrt(jnp.mean(x * x, axis=-1, keepdims=True) + EPS) * (g * (1.0 + scale)) + shift


def _dot(a, b):
    return jnp.dot(a, b, preferred_element_type=F32)


def _dot_nt(a, b):
    return lax.dot_general(a, b, (((1,), (1,)), ((), ())), preferred_element_type=F32)


def _dot_tn(a, b):
    return lax.dot_general(a, b, (((0,), (0,)), ((), ())), preferred_element_type=F32)


def _ada_kernel(c_ref, w_ref, b_ref, o_ref):
    a = _silu(c_ref[...]).astype(BF16)
    o_ref[...] = _dot(a, w_ref[...].astype(BF16)) + b_ref[...]


def _ada(cond, w_ada, b_ada):
    tn = 1536
    n = N_MOD * D_MODEL
    return pl.pallas_call(
        _ada_kernel,
        grid=(DEPTH, n // tn),
        in_specs=[
            pl.BlockSpec((MOD_ROWS, D_MODEL), lambda l, j: (0, 0)),
            pl.BlockSpec((None, D_MODEL, tn), lambda l, j: (l, 0, j)),
            pl.BlockSpec((None, 1, tn), lambda l, j: (l, 0, j)),
        ],
        out_specs=pl.BlockSpec((None, MOD_ROWS, tn), lambda l, j: (l, 0, j)),
        out_shape=jax.ShapeDtypeStruct((DEPTH, MOD_ROWS, n), F32),
        compiler_params=_params("arbitrary", "arbitrary"),
        name="ada",
    )(cond, w_ada, b_ada.reshape(DEPTH, 1, n))


def _log_sigmoid(x):
    return jnp.minimum(x, 0.0) - jnp.log(1.0 + jnp.exp(-jnp.abs(x)))


class _TwoStreams:
    def __init__(self, ctx_rows, lat_rows, lat_seq, weight_steps):
        self.weight_steps = weight_steps
        self.ctx_tiles = ctx_rows // ROW_TILE
        self.first_lat = weight_steps + self.ctx_tiles
        self.grid = (self.first_lat + lat_rows // ROW_TILE,)
        self.tiles_per_batch = lat_seq // ROW_TILE

    def ctx(self, width):
        return pl.BlockSpec((ROW_TILE, width), lambda i: (jnp.clip(i - self.weight_steps, 0, self.ctx_tiles - 1), 0))

    def lat(self, width):
        return pl.BlockSpec((ROW_TILE, width), lambda i: (jnp.maximum(i - self.first_lat, 0), 0))

    def mod(self, layer, which):
        row = lambda i: jnp.where(i < self.first_lat, 0, 1 + (i - self.first_lat) // self.tiles_per_batch)
        return pl.BlockSpec((None, None, None, 1, D_MODEL), lambda i: (layer, row(i), which, 0, 0))

    def slab(self, layer, rows, cols):
        return pl.BlockSpec((None, rows // self.weight_steps, cols),
                            lambda i: (layer, jnp.minimum(i, self.weight_steps - 1), 0))

    def run(self, load_weights, ctx_tile, lat_tile):
        step = pl.program_id(0)
        pl.when(step < self.weight_steps)(lambda: load_weights(step))
        pl.when((step >= self.weight_steps) & (step < self.first_lat))(ctx_tile)
        pl.when(step >= self.first_lat)(lat_tile)


def _round_slab(step, src_ref, dst_ref):
    rows = src_ref.shape[0]
    dst_ref[pl.ds(pl.multiple_of(step * rows, rows), rows), :] = src_ref[...].astype(BF16)


def _gla_in_kernel(xc_ref, xl_ref, g_ref, sh_ref, sc_ref, w_ref, w1_ref, w2_ref, bg_ref, *refs, streams):
    outs_c, outs_l, wb_ref = refs[0:5], refs[5:10], refs[10]

    def tile(x_ref, q_ref, k_ref, v_ref, r_ref, la_ref):
        h = _norm_mod(x_ref[...], g_ref[...], sh_ref[...], sc_ref[...]).astype(BF16)
        z = _dot(h, wb_ref[...])
        q_ref[...] = z[:, :GLA_HK]
        k_ref[...] = z[:, GLA_HK:2 * GLA_HK]
        v_ref[...] = z[:, 2 * GLA_HK:2 * GLA_HK + GLA_HV].astype(BF16)
        r_ref[...] = z[:, 2 * GLA_HK + GLA_HV:]
        low_rank = _dot(h, w1_ref[...]).astype(BF16)
        pre = _dot(low_rank, w2_ref[...]) + bg_ref[...]
        la_ref[...] = _log_sigmoid(pre) * (1.0 / GLA_TAU)

    streams.run(lambda step: _round_slab(step, w_ref, wb_ref),
                lambda: tile(xc_ref, *outs_c), lambda: tile(xl_ref, *outs_l))


def _gla_in(xc, xl, lat_seq, mod, layer, g, w_in, w_layer, w1, w2, bg):
    n_out = 2 * GLA_HK + 2 * GLA_HV
    streams = _TwoStreams(xc.shape[0], xl.shape[0], lat_seq, GLA_WEIGHT_STEPS)
    widths = [(GLA_HK, F32), (GLA_HK, F32), (GLA_HV, BF16), (GLA_HV, F32), (2 * GLA_HK, F32)]
    outs = pl.pallas_call(
        functools.partial(_gla_in_kernel, streams=streams),
        grid=streams.grid,
        in_specs=[
            streams.ctx(D_MODEL), streams.lat(D_MODEL),
            _resident((1, D_MODEL)),
            streams.mod(layer, 0), streams.mod(layer, 1),
            streams.slab(w_layer, D_MODEL, n_out),
            _resident((D_MODEL, GATE_PAD)),
            _resident((GATE_PAD, 2 * GLA_HK)),
            _resident((1, 2 * GLA_HK)),
        ],
        out_specs=[streams.ctx(w) for w, _ in widths] + [streams.lat(w) for w, _ in widths],
        out_shape=([jax.ShapeDtypeStruct((xc.shape[0], w), dt) for w, dt in widths]
                   + [jax.ShapeDtypeStruct((xl.shape[0], w), dt) for w, dt in widths]),
        scratch_shapes=[pltpu.VMEM((D_MODEL, n_out), BF16)],
        compiler_params=_params("arbitrary"),
        name="gla_in",
    )(xc, xl, g, mod, mod, w_in, w1, w2, bg)
    return outs[:5], outs[5:]


def _split3(x):
    hi = x.astype(BF16)
    rem = x - hi.astype(F32)
    mid = rem.astype(BF16)
    lo = (rem - mid.astype(F32)).astype(BF16)
    return jnp.concatenate([hi, mid, lo], axis=1)


def _fold3(x):
    return x[:, :GLA_DK] + x[:, GLA_DK:2 * GLA_DK] + x[:, 2 * GLA_DK:]


def _gla_scan_kernel(*refs, n_groups, hps, has_s0):
    C, DK, DV, G = GLA_CHUNK, GLA_DK, GLA_DV, GLA_GROUP
    cpg = G // C
    n_chunks = n_groups * cpg
    unroll = max(1, min(n_groups, GLA_CHAINS // hps))
    if has_s0:
        (q_ref, k_ref, v_ref, laf_ref, lab_ref, r_ref, gh_ref, s0f_ref, s0b_ref,
         y_ref, qd_ref, kd_ref, ds_ref, sin_ref, dec_ref) = refs
    else:
        (q_ref, k_ref, v_ref, laf_ref, lab_ref, r_ref, gh_ref,
         y_ref, sf_ref, sb_ref, qd_ref, kd_ref, ds_ref, sin_ref, dec_ref) = refs

    row = lax.broadcasted_iota(jnp.int32, (G, G), 0)
    col = lax.broadcasted_iota(jnp.int32, (G, G), 1)
    same_chunk = (row // C) == (col // C)
    lower = same_chunk & (row >= col)
    upper = same_chunk & (row <= col)
    prefix_sum = jnp.where(lower, 1.0, 0.0).astype(BF16)
    suffix_sum = jnp.where(upper, 1.0, 0.0).astype(BF16)

    def group(g):
        return pl.ds(pl.multiple_of(g * G, G), G)

    def spread(x, first_row):
        return jnp.concatenate(
            [jnp.broadcast_to(x[first_row + c * C:first_row + c * C + 1], (C, x.shape[1])) for c in range(cpg)], axis=0)

    def pass1(g, carry):
        sl = group(g)
        for hd in range(hps):
            dk = slice(hd * DK, (hd + 1) * DK)
            dv = slice(hd * DV, (hd + 1) * DV)
            both = slice(hd * 2 * DK, (hd + 1) * 2 * DK)
            bf = _fold3(_dot(prefix_sum, _split3(laf_ref[sl, dk])))
            bb = _fold3(_dot(suffix_sum, _split3(lab_ref[sl, dk])))
            tot_f = spread(bf, C - 1)
            tot_b = spread(bb, 0)
            q = q_ref[sl, dk] * (DK ** -0.5)
            k = k_ref[sl, dk]
            qd_ref[sl, both] = jnp.concatenate([q * jnp.exp(bf), q * jnp.exp(bb)], axis=1).astype(BF16)
            kd_ref[sl, both] = jnp.concatenate([k * jnp.exp(-bf), k * jnp.exp(-bb)], axis=1).astype(BF16)
            k_end = jnp.concatenate([k * jnp.exp(tot_f - bf), k * jnp.exp(tot_b - bb)], axis=1).astype(BF16)
            v = v_ref[sl, dv]
            for c in range(cpg):
                rows = slice(c * C, (c + 1) * C)
                ds_ref[hd, g * cpg + c] = _dot_tn(v[rows], k_end[rows])
                dec_ref[hd, g * cpg + c] = jnp.exp(
                    jnp.concatenate([tot_f[c * C:c * C + 1], tot_b[c * C:c * C + 1]], axis=1))
        return carry

    lax.fori_loop(0, n_groups, pass1, 0, unroll=unroll)

    for hd in range(hps):
        for forward in (True, False):
            lanes = slice(0, DK) if forward else slice(DK, 2 * DK)
            if has_s0:
                init = (s0f_ref if forward else s0b_ref)[hd].T
            else:
                init = jnp.zeros((DV, DK), F32)

            def step(i, s, hd=hd, forward=forward, lanes=lanes):
                n = i if forward else n_chunks - 1 - i
                sin_ref[hd, n, :, lanes] = s.astype(BF16)
                return s * dec_ref[hd, n, :, lanes] + ds_ref[hd, n, :, lanes]

            final = lax.fori_loop(0, n_chunks, step, init)
            if not has_s0:
                (sf_ref if forward else sb_ref)[hd] = final.T

    gh = gh_ref[...]

    def pass3(g, carry):
        sl = group(g)
        for hd in range(hps):
            dv = slice(hd * DV, (hd + 1) * DV)
            both = slice(hd * 2 * DK, (hd + 1) * 2 * DK)
            qd = qd_ref[sl, both]
            kd = kd_ref[sl, both]
            a_f = _dot_nt(qd[:, :DK], kd[:, :DK])
            a_b = _dot_nt(qd[:, DK:], kd[:, DK:])
            a = jnp.where(lower, a_f, 0.0) + jnp.where(upper, a_b, 0.0)
            inter = [_dot_nt(qd[c * C:(c + 1) * C], sin_ref[hd, g * cpg + c]) for c in range(cpg)]
            o = _dot(a.astype(BF16), v_ref[sl, dv]) + jnp.concatenate(inter, axis=0)
            y_ref[sl, dv] = (_rms(o, gh) * _silu(r_ref[sl, dv])).astype(BF16)
        return carry

    lax.fori_loop(0, n_groups, pass3, 0, unroll=unroll)


def _gla_scan(q, k, v, la, r, g_head, batch, seq, hps, s0f=None, s0b=None):
    steps = GLA_HEADS // hps
    n_chunks = seq // GLA_CHUNK
    has_s0 = s0f is not None
    rows = batch * seq
    tok = lambda w, off: pl.BlockSpec((seq, w * hps), lambda b, h: (b, h + off))
    state = pl.BlockSpec((None, hps, GLA_DK, GLA_DV), lambda b, h: (b, h, 0, 0))
    in_specs = [tok(GLA_DK, 0), tok(GLA_DK, 0), tok(GLA_DV, 0), tok(GLA_DK, 0), tok(GLA_DK, steps),
                tok(GLA_DV, 0), pl.BlockSpec((1, GLA_DV), lambda b, h: (0, 0))]
    args = [q, k, v, la, la, r, g_head]
    y_shape = jax.ShapeDtypeStruct((rows, GLA_HV), BF16)
    st_shape = jax.ShapeDtypeStruct((batch, GLA_HEADS, GLA_DK, GLA_DV), F32)
    scratch = [
        pltpu.VMEM((seq, hps * 2 * GLA_DK), BF16),
        pltpu.VMEM((seq, hps * 2 * GLA_DK), BF16),
        pltpu.VMEM((hps, n_chunks, GLA_DV, 2 * GLA_DK), F32),
        pltpu.VMEM((hps, n_chunks, GLA_DV, 2 * GLA_DK), BF16),
        pltpu.VMEM((hps, n_chunks, 1, 2 * GLA_DK), F32),
    ]
    if has_s0:
        in_specs += [state, state]
        args += [s0f, s0b]
        out_specs = tok(GLA_DV, 0)
        out_shape = y_shape
    else:
        out_specs = [tok(GLA_DV, 0), state, state]
        out_shape = [y_shape, st_shape, st_shape]
    return pl.pallas_call(
        functools.partial(_gla_scan_kernel, n_groups=seq // GLA_GROUP, hps=hps, has_s0=has_s0),
        grid=(batch, steps),
        in_specs=in_specs,
        out_specs=out_specs,
        out_shape=out_shape,
        scratch_shapes=scratch,
        compiler_params=_params("arbitrary", "arbitrary"),
        name="gla_scan",
    )(*args)


def _out_ffn_kernel(*refs, final_norm, streams):
    if final_norm:
        (xc_ref, xl_ref, yc_ref, yl_ref, wo_ref, g1_ref, gn_ref, sh_ref, sc_ref, g2_ref, wg_ref, wu_ref, wd_ref,
         gf_ref, oc_ref, ol_ref, wgb_ref, wub_ref, wdb_ref) = refs
    else:
        (xc_ref, xl_ref, yc_ref, yl_ref, wo_ref, g1_ref, gn_ref, sh_ref, sc_ref, g2_ref, wg_ref, wu_ref, wd_ref,
         oc_ref, ol_ref, wgb_ref, wub_ref, wdb_ref) = refs

    def load_weights(step):
        _round_slab(step, wg_ref, wgb_ref)
        _round_slab(step, wu_ref, wub_ref)
        _round_slab(step, wd_ref, wdb_ref)

    def tile(x_ref, y_ref, o_ref):
        x = x_ref[...] + g1_ref[...] * _dot(y_ref[...], wo_ref[...])
        h = _norm_mod(x, gn_ref[...], sh_ref[...], sc_ref[...]).astype(BF16)
        act = (_silu(_dot(h, wgb_ref[...])) * _dot(h, wub_ref[...])).astype(BF16)
        x = x + g2_ref[...] * _dot(act, wdb_ref[...])
        if final_norm:
            x = _rms(x, gf_ref[...])
        o_ref[...] = x

    streams.run(load_weights, lambda: tile(xc_ref, yc_ref, oc_ref), lambda: tile(xl_ref, yl_ref, ol_ref))


def _out_ffn(xc, xl, yc, yl, lat_seq, mod, layer, w_out, g_ffn, wg, wu, wd, g_final=None):
    final_norm = g_final is not None
    streams = _TwoStreams(xc.shape[0], xl.shape[0], lat_seq, FFN_WEIGHT_STEPS)
    in_specs = [
        streams.ctx(D_MODEL), streams.lat(D_MODEL), streams.ctx(D_MODEL), streams.lat(D_MODEL),
        _resident((D_MODEL, D_MODEL)),
        streams.mod(layer, 2),
        _resident((1, D_MODEL)),
        streams.mod(layer, 3), streams.mod(layer, 4), streams.mod(layer, 5),
        streams.slab(layer, D_MODEL, D_FF), streams.slab(layer, D_MODEL, D_FF), streams.slab(layer, D_FF, D_MODEL),
    ]
    args = [xc, xl, yc, yl, w_out, mod, g_ffn, mod, mod, mod, wg, wu, wd]
    if final_norm:
        in_specs.append(_resident((1, D_MODEL)))
        args.append(g_final)
    return pl.pallas_call(
        functools.partial(_out_ffn_kernel, final_norm=final_norm, streams=streams),
        grid=streams.grid,
        in_specs=in_specs,
        out_specs=[streams.ctx(D_MODEL), streams.lat(D_MODEL)],
        out_shape=[jax.ShapeDtypeStruct(xc.shape, F32), jax.ShapeDtypeStruct(xl.shape, F32)],
        scratch_shapes=[pltpu.VMEM((D_MODEL, D_FF), BF16), pltpu.VMEM((D_MODEL, D_FF), BF16),
                        pltpu.VMEM((D_FF, D_MODEL), BF16)],
        compiler_params=_params("arbitrary"),
        name="out_ffn",
    )(*args)


def _rope_tables(n_tokens):
    rows = n_tokens // GRID_W
    r, col = jnp.meshgrid(jnp.arange(rows), jnp.arange(GRID_W), indexing="ij")
    r = r.reshape(-1).astype(F32)
    col = col.reshape(-1).astype(F32)
    n_freq = DIFF_DH // 4
    inv = ROPE_THETA ** (-jnp.arange(n_freq, dtype=F32) / n_freq)
    ang = jnp.concatenate([r[:, None] * inv, col[:, None] * inv], axis=-1)
    cos, sin = jnp.cos(ang), jnp.sin(ang)
    reps = D_MODEL // DIFF_DH
    return jnp.tile(jnp.concatenate([cos, cos], axis=-1), (1, reps)), jnp.tile(jnp.concatenate([-sin, sin], axis=-1), (1, reps))


def _rope(x, cos, sin_signed):
    half = DIFF_DH // 2
    width = x.shape[1]
    lane = lax.broadcasted_iota(jnp.int32, x.shape, 1)
    from_above = pltpu.roll(x, width - half, axis=1)
    from_below = pltpu.roll(x, half, axis=1)
    swapped = jnp.where(lane % DIFF_DH < half, from_above, from_below)
    return x * cos + swapped * sin_signed


def _qkv_kernel(xc_ref, xl_ref, g_ref, sh_ref, sc_ref, w_ref, cos_ref, sin_ref,
                qc_ref, kc_ref, vc_ref, kf_ref, vf_ref, ql_ref, kl_ref, vl_ref, wb_ref, *, streams, ctx_seq):
    def project(x_ref):
        h = _norm_mod(x_ref[...], g_ref[...], sh_ref[...], sc_ref[...]).astype(BF16)
        z = _dot(h, wb_ref[...])
        return z[:, :D_MODEL], z[:, D_MODEL:2 * D_MODEL], z[:, 2 * D_MODEL:]

    def store(q, k, v, q_ref, k_ref, v_ref):
        q_ref[...] = (q * (DIFF_DH ** -0.5 * LOG2E)).astype(BF16)
        k_ref[...] = k.astype(BF16)
        v_ref[...] = v.astype(BF16)

    def ctx_tile():
        q, k, v = project(xc_ref)
        for b in range(ROW_TILE // ctx_seq):
            kf_ref[b] = k[b * ctx_seq:(b + 1) * ctx_seq].T
        by_head = jnp.stack([v[:, head * DIFF_DV:(head + 1) * DIFF_DV] for head in range(DIFF_HEADS)], axis=0)
        vf_ref[...] = jnp.swapaxes(by_head, 0, 1)
        store(q, k, v, qc_ref, kc_ref, vc_ref)

    def lat_tile():
        q, k, v = project(xl_ref)
        store(_rope(q, cos_ref[...], sin_ref[...]), _rope(k, cos_ref[...], sin_ref[...]), v, ql_ref, kl_ref, vl_ref)

    streams.run(lambda step: _round_slab(step, w_ref, wb_ref), ctx_tile, lat_tile)


def _qkv(xc, xl, ctx_seq, lat_seq, mod, layer, g, w, w_layer, tables):
    streams = _TwoStreams(xc.shape[0], xl.shape[0], lat_seq, QKV_WEIGHT_STEPS)
    per_tile = ROW_TILE // ctx_seq
    parked_ctx = lambda i: jnp.clip(i - streams.weight_steps, 0, streams.ctx_tiles - 1)
    table = pl.BlockSpec((ROW_TILE, D_MODEL),
                         lambda i: (lax.rem(jnp.maximum(i - streams.first_lat, 0), streams.tiles_per_batch), 0))
    bf = lambda rows: jax.ShapeDtypeStruct((rows, D_MODEL), BF16)
    outs = pl.pallas_call(
        functools.partial(_qkv_kernel, streams=streams, ctx_seq=ctx_seq),
        grid=streams.grid,
        in_specs=[streams.ctx(D_MODEL), streams.lat(D_MODEL), _resident((1, D_MODEL)),
                  streams.mod(layer, 0), streams.mod(layer, 1), streams.slab(w_layer, D_MODEL, 3 * D_MODEL),
                  table, table],
        out_specs=[streams.ctx(D_MODEL)] * 3
        + [pl.BlockSpec((per_tile, D_MODEL, ctx_seq), lambda i: (parked_ctx(i), 0, 0)),
           pl.BlockSpec((ROW_TILE, DIFF_HEADS, DIFF_DV), lambda i: (parked_ctx(i), 0, 0))]
        + [streams.lat(D_MODEL)] * 3,
        out_shape=[bf(xc.shape[0])] * 3
        + [jax.ShapeDtypeStruct((xc.shape[0] // ctx_seq, D_MODEL, ctx_seq), F32),
           jax.ShapeDtypeStruct((xc.shape[0], DIFF_HEADS, DIFF_DV), F32)]
        + [bf(xl.shape[0])] * 3,
        scratch_shapes=[pltpu.VMEM((D_MODEL, 3 * D_MODEL), BF16)],
        compiler_params=_params("arbitrary"),
        name="diff_qkv",
    )(xc, xl, g, mod, mod, w, *tables)
    return outs[:5], outs[5:]


def _attn_kernel(*refs, has_cache, lam_init, n_q_tiles, q_tile):
    if has_cache:
        (lq_ref, lk_ref, q_ref, k_ref, v_ref, gh_ref, kc_ref, vc_ref, y_ref, kcb_ref, vcb_ref) = refs
    else:
        (lq_ref, lk_ref, q_ref, k_ref, v_ref, gh_ref, y_ref) = refs
    e = jnp.exp(jnp.sum(lq_ref[...] * lk_ref[...], axis=1, keepdims=True))
    lam = e[0:1, :] - e[1:2, :] + lam_init
    first_map = lax.broadcasted_iota(jnp.int32, (q_tile, 2 * DIFF_DH), 1) < DIFF_DH
    gh = gh_ref[...] * (1.0 - lam_init)

    if has_cache:
        for head in range(DIFF_HEADS):
            kcb_ref[head] = kc_ref[head].astype(BF16)
        vcb_ref[...] = jnp.swapaxes(vc_ref[...], 0, 1).astype(BF16)

    def head_lanes(head):
        return slice(head * 2 * DIFF_DH, (head + 1) * 2 * DIFF_DH)

    def scores(head, sl):
        q = q_ref[sl, head_lanes(head)]
        zero = jnp.zeros_like(q)
        qq = jnp.concatenate([jnp.where(first_map, q, zero), jnp.where(first_map, zero, q)], axis=0)
        s = [_dot_nt(qq, k_ref[:, head_lanes(head)])]
        if has_cache:
            s.insert(0, _dot(qq, kcb_ref[head]))
        return s

    def attend(head, sl, s):
        vals = [v_ref[:, head_lanes(head)]]
        if has_cache:
            vals.insert(0, vcb_ref[head])
        m = functools.reduce(jnp.maximum, [jnp.max(x, axis=-1, keepdims=True) for x in s])
        p = [jnp.exp2(x - m) for x in s]
        denom = functools.reduce(jnp.add, [jnp.sum(x, axis=-1, keepdims=True) for x in p])
        acc = functools.reduce(jnp.add, [_dot(x.astype(BF16), vv) for x, vv in zip(p, vals)])
        o = acc[:q_tile] * (1.0 / denom[:q_tile]) - acc[q_tile:] * (lam / denom[q_tile:])
        y_ref[sl, head_lanes(head)] = _rms(o, gh).astype(BF16)

    def body(t, carry):
        sl = pl.ds(pl.multiple_of(t * q_tile, q_tile), q_tile)
        s_next = scores(0, sl)
        for head in range(DIFF_HEADS):
            s_cur = s_next
            if head + 1 < DIFF_HEADS:
                s_next = scores(head + 1, sl)
            attend(head, sl, s_cur)
        return carry

    lax.fori_loop(0, n_q_tiles, body, 0)


def _attn(q, k, v, lam_q, lam_k, g_head, lam_init, batch, seq, cache_k=None, cache_v=None):
    has_cache = cache_k is not None
    tok = pl.BlockSpec((seq, D_MODEL), lambda b: (b, 0))
    small = lambda shape: pl.BlockSpec(shape, lambda b: (0, 0))
    in_specs = [small((2, DIFF_DH)), small((2, DIFF_DH)), tok, tok, tok, small((1, DIFF_DV))]
    args = [lam_q, lam_k, q, k, v, g_head]
    if has_cache:
        past = cache_k.shape[-1]
        in_specs += [pl.BlockSpec((None, DIFF_HEADS, 2 * DIFF_DH, past), lambda b: (b, 0, 0, 0)),
                     pl.BlockSpec((None, past, DIFF_HEADS, DIFF_DV), lambda b: (b, 0, 0, 0))]
        args += [cache_k, cache_v]
        scratch = [pltpu.VMEM((DIFF_HEADS, 2 * DIFF_DH, past), BF16), pltpu.VMEM((DIFF_HEADS, past, DIFF_DV), BF16)]
    else:
        scratch = []
    q_tile = min(Q_TILE, seq)
    return pl.pallas_call(
        functools.partial(_attn_kernel, has_cache=has_cache, lam_init=lam_init, n_q_tiles=seq // q_tile, q_tile=q_tile),
        grid=(batch,),
        in_specs=in_specs,
        out_specs=tok,
        out_shape=jax.ShapeDtypeStruct((batch * seq, D_MODEL), BF16),
        scratch_shapes=scratch,
        compiler_params=_params("arbitrary"),
        name="diff_attn",
    )(*args)


def kernel(x_prompt, x_sample, state_gla_fwd, state_gla_bwd, cache_diff_k, cache_diff_v, c, c_ctx, w_ada, b_ada, g_mix_norm, g_ffn_norm, w_gla_in, w_gla_g1, w_gla_g2, b_gla_g, g_gla_head, w_gla_out, w_diff_qkv, lam_q, lam_k, g_diff_head, w_diff_out, w_ffn_gate, w_ffn_up, w_ffn_down, g_final):
    bp, tp, _ = x_prompt.shape
    bs, ts, _ = x_sample.shape
    assert ROW_TILE % tp == 0 and ts % ROW_TILE == 0 and (bp * tp) % ROW_TILE == 0
    assert tp % GLA_GROUP == 0 and ts % GLA_GROUP == 0
    xp = x_prompt.reshape(bp * tp, D_MODEL)
    xs = x_sample.reshape(bs * ts, D_MODEL)

    cond = jnp.concatenate([c_ctx[None, :], c, jnp.zeros((MOD_ROWS - 1 - bs, D_MODEL), F32)], axis=0)
    mod = _ada(cond, w_ada, b_ada).reshape(DEPTH, MOD_ROWS, N_MOD, 1, D_MODEL)
    wg, wu, wd = w_ffn_gate, w_ffn_up, w_ffn_down

    j = 0
    w1 = jnp.concatenate([w_gla_g1[j, 0], w_gla_g1[j, 1], jnp.zeros((D_MODEL, GATE_PAD - 2 * GLA_RANK), F32)],
                         axis=1).astype(BF16)
    w2 = jnp.zeros((GATE_PAD, 2 * GLA_HK), F32)
    w2 = w2.at[:GLA_RANK, :GLA_HK].set(w_gla_g2[j, 0]).at[GLA_RANK:2 * GLA_RANK, GLA_HK:].set(w_gla_g2[j, 1]).astype(BF16)
    bg = b_gla_g[j].reshape(1, 2 * GLA_HK)
    g_mix = g_mix_norm[0].reshape(1, D_MODEL)
    g_head = g_gla_head[j].reshape(1, GLA_DV)
    g_ffn = g_ffn_norm[0].reshape(1, D_MODEL)
    w_out = w_gla_out[j].astype(BF16)

    (qp, kp, vp, rp, lap), (qs, ks, vs, rs, las) = _gla_in(xp, xs, ts, mod, 0, g_mix, w_gla_in, j, w1, w2, bg)
    yp, new_f, new_b = _gla_scan(qp, kp, vp, lap, rp, g_head, bp, tp, GLA_HEADS)
    ys = _gla_scan(qs, ks, vs, las, rs, g_head, bs, ts, 2, state_gla_fwd[:, j], state_gla_bwd[:, j])
    xp, xs = _out_ffn(xp, xs, yp, ys, ts, mod, 0, w_out, g_ffn, wg, wu, wd)

    lam_init = 0.8 - 0.6 * math.exp(-0.3 * 1)
    g_mix = g_mix_norm[1].reshape(1, D_MODEL)
    g_head = g_diff_head[j].reshape(1, DIFF_DV)
    g_ffn = g_ffn_norm[1].reshape(1, D_MODEL)
    w_out = w_diff_out[j].astype(BF16)

    (qp, kp, vp, kp32, vp32), (qs, ks, vs) = _qkv(xp, xs, tp, ts, mod, 1, g_mix, w_diff_qkv, j, _rope_tables(ts))
    yp = _attn(qp, kp, vp, lam_q[j], lam_k[j], g_head, lam_init, bp, tp)
    past = cache_diff_k.shape[2]
    cache_k_t = jnp.transpose(cache_diff_k[:, j], (0, 2, 3, 4, 1)).reshape(bs, DIFF_HEADS, 2 * DIFF_DH, past)
    ys = _attn(qs, ks, vs, lam_q[j], lam_k[j], g_head, lam_init, bs, ts, cache_k_t, cache_diff_v[:, j])
    g_fin = g_final.reshape(1, D_MODEL)
    yp_out, ys_out = _out_ffn(xp, xs, yp, ys, ts, mod, 1, w_out, g_ffn, wg, wu, wd, g_final=g_fin)

    return (yp_out.reshape(bp, tp, D_MODEL),
            ys_out.reshape(bs, ts, D_MODEL),
            new_f[:, None],
            new_b[:, None],
            jnp.transpose(kp32.reshape(bp, 1, DIFF_HEADS, 2, DIFF_DH, tp), (0, 1, 5, 2, 3, 4)),
            vp32.reshape(bp, 1, tp, DIFF_HEADS, DIFF_DV))
```

```python
import functools
import math

import jax
import jax.numpy as jnp
from jax import lax
from jax.experimental import pallas as pl
from jax.experimental.pallas import tpu as pltpu

F32 = jnp.float32
BF16 = jnp.bfloat16

D_MODEL = 1024
DEPTH = 2
GRID_W = 64
GLA_HEADS = 4
GLA_DK = 128
GLA_DV = 256
GLA_HK = GLA_HEADS * GLA_DK
GLA_HV = GLA_HEADS * GLA_DV
GLA_RANK = 16
GLA_TAU = 16.0
GLA_CHUNK = 64
GLA_GROUP = 256
GLA_CHAINS = 8
DIFF_HEADS = 8
DIFF_DH = 64
DIFF_DV = 128
ROPE_THETA = 10000.0
D_FF = 2816
EPS = 1e-6
LOG2E = math.log2(math.e)
N_MOD = 6
MOD_ROWS = 16
GATE_PAD = 128

VMEM_LIMIT_BYTES = 58 * 1024 * 1024
ROW_TILE = 512
FFN_WEIGHT_STEPS = 8
GLA_WEIGHT_STEPS = 4
QKV_WEIGHT_STEPS = 4
Q_TILE = 256


def _params(*sem):
    return pltpu.CompilerParams(dimension_semantics=sem, vmem_limit_bytes=VMEM_LIMIT_BYTES)


def _resident(shape, index=None):
    index = (0,) * len(shape) if index is None else index
    return pl.BlockSpec(shape, lambda *_: index, pipeline_mode=pl.Buffered(1))


def _sigmoid(x):
    return 1.0 / (1.0 + jnp.exp(-x))


def _silu(x):
    return x * _sigmoid(x)


def _rms(x, g):
    return x * lax.rsqrt(jnp.mean(x * x, axis=-1, keepdims=True) + EPS) * g


def _norm_mod(x, g, shift, scale):
    return x * lax.rsqrt(jnp.mean(x * x, axis=-1, keepdims=True) + EPS) * (g * (1.0 + scale)) + shift


def _dot(a, b):
    return jnp.dot(a, b, preferred_element_type=F32)


def _dot_nt(a, b):
    return lax.dot_general(a, b, (((1,), (1,)), ((), ())), preferred_element_type=F32)


def _dot_tn(a, b):
    return lax.dot_general(a, b, (((0,), (0,)), ((), ())), preferred_element_type=F32)


def _ada_kernel(c_ref, w_ref, b_ref, o_ref):
    a = _silu(c_ref[...]).astype(BF16)
    o_ref[...] = _dot(a, w_ref[...].astype(BF16)) + b_ref[...]


def _ada(cond, w_ada, b_ada):
    tn = 1536
    n = N_MOD * D_MODEL
    return pl.pallas_call(
        _ada_kernel,
        grid=(DEPTH, n // tn),
        in_specs=[
            pl.BlockSpec((MOD_ROWS, D_MODEL), lambda l, j: (0, 0)),
            pl.BlockSpec((None, D_MODEL, tn), lambda l, j: (l, 0, j)),
            pl.BlockSpec((None, 1, tn), lambda l, j: (l, 0, j)),
        ],
        out_specs=pl.BlockSpec((None, MOD_ROWS, tn), lambda l, j: (l, 0, j)),
        out_shape=jax.ShapeDtypeStruct((DEPTH, MOD_ROWS, n), F32),
        compiler_params=_params("arbitrary", "arbitrary"),
        name="ada",
    )(cond, w_ada, b_ada.reshape(DEPTH, 1, n))


def _log_sigmoid(x):
    return jnp.minimum(x, 0.0) - jnp.log(1.0 + jnp.exp(-jnp.abs(x)))


class _TwoStreams:
    def __init__(self, ctx_rows, lat_rows, lat_seq, weight_steps):
        self.weight_steps = weight_steps
        self.ctx_tiles = ctx_rows // ROW_TILE
        self.first_lat = weight_steps + self.ctx_tiles
        self.grid = (self.first_lat + lat_rows // ROW_TILE,)
        self.tiles_per_batch = lat_seq // ROW_TILE

    def _rows(self, tile, width, groups):
        if groups is None:
            return pl.BlockSpec((ROW_TILE, width), lambda i: (tile(i), 0))
        return pl.BlockSpec((groups, ROW_TILE, width), lambda i: (0, tile(i), 0))

    def ctx(self, width, groups=None):
        return self._rows(lambda i: jnp.clip(i - self.weight_steps, 0, self.ctx_tiles - 1), width, groups)

    def lat(self, width, groups=None):
        return self._rows(lambda i: jnp.maximum(i - self.first_lat, 0), width, groups)

    def mod(self, layer, which):
        row = lambda i: jnp.where(i < self.first_lat, 0, 1 + (i - self.first_lat) // self.tiles_per_batch)
        return pl.BlockSpec((None, None, None, 1, D_MODEL), lambda i: (layer, row(i), which, 0, 0))

    def slab(self, layer, rows, cols):
        return pl.BlockSpec((None, rows // self.weight_steps, cols),
                            lambda i: (layer, jnp.minimum(i, self.weight_steps - 1), 0))

    def run(self, load_weights, ctx_tile, lat_tile):
        step = pl.program_id(0)
        pl.when(step < self.weight_steps)(lambda: load_weights(step))
        pl.when((step >= self.weight_steps) & (step < self.first_lat))(ctx_tile)
        pl.when(step >= self.first_lat)(lat_tile)


def _round_slab(step, src_ref, dst_ref):
    rows = src_ref.shape[0]
    dst_ref[pl.ds(pl.multiple_of(step * rows, rows), rows), :] = src_ref[...].astype(BF16)


def _gla_in_kernel(xc_ref, xl_ref, g_ref, sh_ref, sc_ref, w_ref, w1_ref, w2_ref, bg_ref, *refs, streams):
    outs_c, outs_l, wb_ref = refs[0:6], refs[6:12], refs[12]

    def tile(x_ref, q_ref, k_ref, v_ref, r_ref, laf_ref, lab_ref):
        h = _norm_mod(x_ref[...], g_ref[...], sh_ref[...], sc_ref[...]).astype(BF16)
        z = _dot(h, wb_ref[...])
        low_rank = _dot(h, w1_ref[...]).astype(BF16)
        la = _log_sigmoid(_dot(low_rank, w2_ref[...]) + bg_ref[...]) * (1.0 / GLA_TAU)
        groups = q_ref.shape[0]
        kw, vw = GLA_HK // groups, GLA_HV // groups
        for s in range(groups):
            q_ref[s] = z[:, s * kw:(s + 1) * kw]
            k_ref[s] = z[:, GLA_HK + s * kw:GLA_HK + (s + 1) * kw]
            v_ref[s] = z[:, 2 * GLA_HK + s * vw:2 * GLA_HK + (s + 1) * vw].astype(BF16)
            r_ref[s] = z[:, 2 * GLA_HK + GLA_HV + s * vw:2 * GLA_HK + GLA_HV + (s + 1) * vw]
            laf_ref[s] = la[:, s * kw:(s + 1) * kw]
            lab_ref[s] = la[:, GLA_HK + s * kw:GLA_HK + (s + 1) * kw]

    streams.run(lambda step: _round_slab(step, w_ref, wb_ref),
                lambda: tile(xc_ref, *outs_c), lambda: tile(xl_ref, *outs_l))


def _gla_in(xc, xl, lat_seq, ctx_groups, lat_groups, mod, layer, g, w_in, w_layer, w1, w2, bg):
    n_out = 2 * GLA_HK + 2 * GLA_HV
    streams = _TwoStreams(xc.shape[0], xl.shape[0], lat_seq, GLA_WEIGHT_STEPS)
    widths = [(GLA_HK, F32), (GLA_HK, F32), (GLA_HV, BF16), (GLA_HV, F32), (GLA_HK, F32), (GLA_HK, F32)]
    outs = pl.pallas_call(
        functools.partial(_gla_in_kernel, streams=streams),
        grid=streams.grid,
        in_specs=[
            streams.ctx(D_MODEL), streams.lat(D_MODEL),
            _resident((1, D_MODEL)),
            streams.mod(layer, 0), streams.mod(layer, 1),
            streams.slab(w_layer, D_MODEL, n_out),
            _resident((D_MODEL, GATE_PAD)),
            _resident((GATE_PAD, 2 * GLA_HK)),
            _resident((1, 2 * GLA_HK)),
        ],
        out_specs=([streams.ctx(w // ctx_groups, ctx_groups) for w, _ in widths]
                   + [streams.lat(w // lat_groups, lat_groups) for w, _ in widths]),
        out_shape=([jax.ShapeDtypeStruct((ctx_groups, xc.shape[0], w // ctx_groups), dt) for w, dt in widths]
                   + [jax.ShapeDtypeStruct((lat_groups, xl.shape[0], w // lat_groups), dt) for w, dt in widths]),
        scratch_shapes=[pltpu.VMEM((D_MODEL, n_out), BF16)],
        compiler_params=_params("arbitrary"),
        name="gla_in",
    )(xc, xl, g, mod, mod, w_in, w1, w2, bg)
    return outs[:6], outs[6:]


def _split3(x):
    hi = x.astype(BF16)
    rem = x - hi.astype(F32)
    mid = rem.astype(BF16)
    lo = (rem - mid.astype(F32)).astype(BF16)
    return jnp.concatenate([hi, mid, lo], axis=1)


def _fold3(x):
    return x[:, :GLA_DK] + x[:, GLA_DK:2 * GLA_DK] + x[:, 2 * GLA_DK:]


def _gla_scan_kernel(*refs, n_groups, hps, has_s0):
    C, DK, DV, G = GLA_CHUNK, GLA_DK, GLA_DV, GLA_GROUP
    cpg = G // C
    n_chunks = n_groups * cpg
    unroll = max(1, min(n_groups, GLA_CHAINS // hps))
    if has_s0:
        (q_ref, k_ref, v_ref, laf_ref, lab_ref, r_ref, gh_ref, s0f_ref, s0b_ref,
         y_ref, qd_ref, kd_ref, ds_ref, sin_ref, dec_ref) = refs
    else:
        (q_ref, k_ref, v_ref, laf_ref, lab_ref, r_ref, gh_ref,
         y_ref, sf_ref, sb_ref, qd_ref, kd_ref, ds_ref, sin_ref, dec_ref) = refs

    row = lax.broadcasted_iota(jnp.int32, (G, G), 0)
    col = lax.broadcasted_iota(jnp.int32, (G, G), 1)
    same_chunk = (row // C) == (col // C)
    lower = same_chunk & (row >= col)
    upper = same_chunk & (row <= col)
    prefix_sum = jnp.where(lower, 1.0, 0.0).astype(BF16)
    suffix_sum = jnp.where(upper, 1.0, 0.0).astype(BF16)

    def group(g):
        return pl.ds(pl.multiple_of(g * G, G), G)

    def spread(x, first_row):
        return jnp.concatenate(
            [jnp.broadcast_to(x[first_row + c * C:first_row + c * C + 1], (C, x.shape[1])) for c in range(cpg)], axis=0)

    def pass1(g, carry):
        sl = group(g)
        for hd in range(hps):
            dk = slice(hd * DK, (hd + 1) * DK)
            dv = slice(hd * DV, (hd + 1) * DV)
            both = slice(hd * 2 * DK, (hd + 1) * 2 * DK)
            bf = _fold3(_dot(prefix_sum, _split3(laf_ref[sl, dk])))
            bb = _fold3(_dot(suffix_sum, _split3(lab_ref[sl, dk])))
            tot_f = spread(bf, C - 1)
            tot_b = spread(bb, 0)
            q = q_ref[sl, dk] * (DK ** -0.5)
            k = k_ref[sl, dk]
            qd_ref[sl, both] = jnp.concatenate([q * jnp.exp(bf), q * jnp.exp(bb)], axis=1).astype(BF16)
            kd_ref[sl, both] = jnp.concatenate([k * jnp.exp(-bf), k * jnp.exp(-bb)], axis=1).astype(BF16)
            k_end = jnp.concatenate([k * jnp.exp(tot_f - bf), k * jnp.exp(tot_b - bb)], axis=1).astype(BF16)
            v = v_ref[sl, dv]
            for c in range(cpg):
                rows = slice(c * C, (c + 1) * C)
                ds_ref[hd, g * cpg + c] = _dot_tn(v[rows], k_end[rows])
                dec_ref[hd, g * cpg + c] = jnp.exp(
                    jnp.concatenate([tot_f[c * C:c * C + 1], tot_b[c * C:c * C + 1]], axis=1))
        return carry

    lax.fori_loop(0, n_groups, pass1, 0, unroll=unroll)

    for hd in range(hps):
        for forward in (True, False):
            lanes = slice(0, DK) if forward else slice(DK, 2 * DK)
            if has_s0:
                init = (s0f_ref if forward else s0b_ref)[hd].T
            else:
                init = jnp.zeros((DV, DK), F32)

            def step(i, s, hd=hd, forward=forward, lanes=lanes):
                n = i if forward else n_chunks - 1 - i
                sin_ref[hd, n, :, lanes] = s.astype(BF16)
                return s * dec_ref[hd, n, :, lanes] + ds_ref[hd, n, :, lanes]

            final = lax.fori_loop(0, n_chunks, step, init)
            if not has_s0:
                (sf_ref if forward else sb_ref)[hd] = final.T

    gh = gh_ref[...]

    def pass3(g, carry):
        sl = group(g)
        for hd in range(hps):
            dv = slice(hd * DV, (hd + 1) * DV)
            both = slice(hd * 2 * DK, (hd + 1) * 2 * DK)
            qd = qd_ref[sl, both]
            kd = kd_ref[sl, both]
            a_f = _dot_nt(qd[:, :DK], kd[:, :DK])
            a_b = _dot_nt(qd[:, DK:], kd[:, DK:])
            a = jnp.where(lower, a_f, 0.0) + jnp.where(upper, a_b, 0.0)
            inter = [_dot_nt(qd[c * C:(c + 1) * C], sin_ref[hd, g * cpg + c]) for c in range(cpg)]
            o = _dot(a.astype(BF16), v_ref[sl, dv]) + jnp.concatenate(inter, axis=0)
            y_ref[sl, dv] = (_rms(o, gh) * _silu(r_ref[sl, dv])).astype(BF16)
        return carry

    lax.fori_loop(0, n_groups, pass3, 0, unroll=unroll)


def _gla_scan(q, k, v, r, la_f, la_b, g_head, batch, seq, s0f=None, s0b=None):
    steps = q.shape[0]
    hps = GLA_HEADS // steps
    n_chunks = seq // GLA_CHUNK
    has_s0 = s0f is not None
    rows = batch * seq
    grouped = lambda w: pl.BlockSpec((None, seq, w * hps), lambda b, h: (h, b, 0))
    tok = lambda w, off: pl.BlockSpec((seq, w * hps), lambda b, h: (b, h + off))
    state = pl.BlockSpec((None, hps, GLA_DK, GLA_DV), lambda b, h: (b, h, 0, 0))
    in_specs = [grouped(GLA_DK), grouped(GLA_DK), grouped(GLA_DV), grouped(GLA_DK), grouped(GLA_DK),
                grouped(GLA_DV), pl.BlockSpec((1, GLA_DV), lambda b, h: (0, 0))]
    args = [q, k, v, la_f, la_b, r, g_head]
    y_shape = jax.ShapeDtypeStruct((rows, GLA_HV), BF16)
    st_shape = jax.ShapeDtypeStruct((batch, GLA_HEADS, GLA_DK, GLA_DV), F32)
    scratch = [
        pltpu.VMEM((seq, hps * 2 * GLA_DK), BF16),
        pltpu.VMEM((seq, hps * 2 * GLA_DK), BF16),
        pltpu.VMEM((hps, n_chunks, GLA_DV, 2 * GLA_DK), F32),
        pltpu.VMEM((hps, n_chunks, GLA_DV, 2 * GLA_DK), BF16),
        pltpu.VMEM((hps, n_chunks, 1, 2 * GLA_DK), F32),
    ]
    if has_s0:
        in_specs += [state, state]
        args += [s0f, s0b]
        out_specs = tok(GLA_DV, 0)
        out_shape = y_shape
    else:
        out_specs = [tok(GLA_DV, 0), state, state]
        out_shape = [y_shape, st_shape, st_shape]
    return pl.pallas_call(
        functools.partial(_gla_scan_kernel, n_groups=seq // GLA_GROUP, hps=hps, has_s0=has_s0),
        grid=(batch, steps),
        in_specs=in_specs,
        out_specs=out_specs,
        out_shape=out_shape,
        scratch_shapes=scratch,
        compiler_params=_params("arbitrary", "arbitrary"),
        name="gla_scan",
    )(*args)


def _out_ffn_kernel(*refs, final_norm, streams):
    if final_norm:
        (xc_ref, xl_ref, yc_ref, yl_ref, wo_ref, g1_ref, gn_ref, sh_ref, sc_ref, g2_ref, wg_ref, wu_ref, wd_ref,
         gf_ref, oc_ref, ol_ref, wgb_ref, wub_ref, wdb_ref) = refs
    else:
        (xc_ref, xl_ref, yc_ref, yl_ref, wo_ref, g1_ref, gn_ref, sh_ref, sc_ref, g2_ref, wg_ref, wu_ref, wd_ref,
         oc_ref, ol_ref, wgb_ref, wub_ref, wdb_ref) = refs

    def load_weights(step):
        _round_slab(step, wg_ref, wgb_ref)
        _round_slab(step, wu_ref, wub_ref)
        _round_slab(step, wd_ref, wdb_ref)

    def tile(x_ref, y_ref, o_ref):
        x = x_ref[...] + g1_ref[...] * _dot(y_ref[...], wo_ref[...])
        h = _norm_mod(x, gn_ref[...], sh_ref[...], sc_ref[...]).astype(BF16)
        act = (_silu(_dot(h, wgb_ref[...])) * _dot(h, wub_ref[...])).astype(BF16)
        x = x + g2_ref[...] * _dot(act, wdb_ref[...])
        if final_norm:
            x = _rms(x, gf_ref[...])
        o_ref[...] = x

    streams.run(load_weights, lambda: tile(xc_ref, yc_ref, oc_ref), lambda: tile(xl_ref, yl_ref, ol_ref))


def _out_ffn(xc, xl, yc, yl, lat_seq, mod, layer, w_out, g_ffn, wg, wu, wd, g_final=None):
    final_norm = g_final is not None
    streams = _TwoStreams(xc.shape[0], xl.shape[0], lat_seq, FFN_WEIGHT_STEPS)
    in_specs = [
        streams.ctx(D_MODEL), streams.lat(D_MODEL), streams.ctx(D_MODEL), streams.lat(D_MODEL),
        _resident((D_MODEL, D_MODEL)),
        streams.mod(layer, 2),
        _resident((1, D_MODEL)),
        streams.mod(layer, 3), streams.mod(layer, 4), streams.mod(layer, 5),
        streams.slab(layer, D_MODEL, D_FF), streams.slab(layer, D_MODEL, D_FF), streams.slab(layer, D_FF, D_MODEL),
    ]
    args = [xc, xl, yc, yl, w_out, mod, g_ffn, mod, mod, mod, wg, wu, wd]
    if final_norm:
        in_specs.append(_resident((1, D_MODEL)))
        args.append(g_final)
    return pl.pallas_call(
        functools.partial(_out_ffn_kernel, final_norm=final_norm, streams=streams),
        grid=streams.grid,
        in_specs=in_specs,
        out_specs=[streams.ctx(D_MODEL), streams.lat(D_MODEL)],
        out_shape=[jax.ShapeDtypeStruct(xc.shape, F32), jax.ShapeDtypeStruct(xl.shape, F32)],
        scratch_shapes=[pltpu.VMEM((D_MODEL, D_FF), BF16), pltpu.VMEM((D_MODEL, D_FF), BF16),
                        pltpu.VMEM((D_FF, D_MODEL), BF16)],
        compiler_params=_params("arbitrary"),
        name="out_ffn",
    )(*args)


def _rope_tables(n_tokens):
    rows = n_tokens // GRID_W
    r, col = jnp.meshgrid(jnp.arange(rows), jnp.arange(GRID_W), indexing="ij")
    r = r.reshape(-1).astype(F32)
    col = col.reshape(-1).astype(F32)
    n_freq = DIFF_DH // 4
    inv = ROPE_THETA ** (-jnp.arange(n_freq, dtype=F32) / n_freq)
    ang = jnp.concatenate([r[:, None] * inv, col[:, None] * inv], axis=-1)
    cos, sin = jnp.cos(ang), jnp.sin(ang)
    reps = D_MODEL // DIFF_DH
    return jnp.tile(jnp.concatenate([cos, cos], axis=-1), (1, reps)), jnp.tile(jnp.concatenate([-sin, sin], axis=-1), (1, reps))


def _rope(x, cos, sin_signed):
    half = DIFF_DH // 2
    width = x.shape[1]
    lane = lax.broadcasted_iota(jnp.int32, x.shape, 1)
    from_above = pltpu.roll(x, width - half, axis=1)
    from_below = pltpu.roll(x, half, axis=1)
    swapped = jnp.where(lane % DIFF_DH < half, from_above, from_below)
    return x * cos + swapped * sin_signed


def _qkv_kernel(xc_ref, xl_ref, g_ref, sh_ref, sc_ref, w_ref, cos_ref, sin_ref,
                qc_ref, kc_ref, vc_ref, kf_ref, vf_ref, ql_ref, kl_ref, vl_ref, wb_ref, *, streams, ctx_seq):
    def project(x_ref):
        h = _norm_mod(x_ref[...], g_ref[...], sh_ref[...], sc_ref[...]).astype(BF16)
        z = _dot(h, wb_ref[...])
        return z[:, :D_MODEL], z[:, D_MODEL:2 * D_MODEL], z[:, 2 * D_MODEL:]

    def store(q, k, v, q_ref, k_ref, v_ref):
        q_ref[...] = (q * (DIFF_DH ** -0.5 * LOG2E)).astype(BF16)
        k_ref[...] = k.astype(BF16)
        v_ref[...] = v.astype(BF16)

    def ctx_tile():
        q, k, v = project(xc_ref)
        for b in range(ROW_TILE // ctx_seq):
            kf_ref[b] = k[b * ctx_seq:(b + 1) * ctx_seq].T
        by_head = jnp.stack([v[:, head * DIFF_DV:(head + 1) * DIFF_DV] for head in range(DIFF_HEADS)], axis=0)
        vf_ref[...] = jnp.swapaxes(by_head, 0, 1)
        store(q, k, v, qc_ref, kc_ref, vc_ref)

    def lat_tile():
        q, k, v = project(xl_ref)
        store(_rope(q, cos_ref[...], sin_ref[...]), _rope(k, cos_ref[...], sin_ref[...]), v, ql_ref, kl_ref, vl_ref)

    streams.run(lambda step: _round_slab(step, w_ref, wb_ref), ctx_tile, lat_tile)


def _qkv(xc, xl, ctx_seq, lat_seq, mod, layer, g, w, w_layer, tables):
    streams = _TwoStreams(xc.shape[0], xl.shape[0], lat_seq, QKV_WEIGHT_STEPS)
    per_tile = ROW_TILE // ctx_seq
    parked_ctx = lambda i: jnp.clip(i - streams.weight_steps, 0, streams.ctx_tiles - 1)
    table = pl.BlockSpec((ROW_TILE, D_MODEL),
                         lambda i: (lax.rem(jnp.maximum(i - streams.first_lat, 0), streams.tiles_per_batch), 0))
    bf = lambda rows: jax.ShapeDtypeStruct((rows, D_MODEL), BF16)
    outs = pl.pallas_call(
        functools.partial(_qkv_kernel, streams=streams, ctx_seq=ctx_seq),
        grid=streams.grid,
        in_specs=[streams.ctx(D_MODEL), streams.lat(D_MODEL), _resident((1, D_MODEL)),
                  streams.mod(layer, 0), streams.mod(layer, 1), streams.slab(w_layer, D_MODEL, 3 * D_MODEL),
                  table, table],
        out_specs=[streams.ctx(D_MODEL)] * 3
        + [pl.BlockSpec((per_tile, D_MODEL, ctx_seq), lambda i: (parked_ctx(i), 0, 0)),
           pl.BlockSpec((ROW_TILE, DIFF_HEADS, DIFF_DV), lambda i: (parked_ctx(i), 0, 0))]
        + [streams.lat(D_MODEL)] * 3,
        out_shape=[bf(xc.shape[0])] * 3
        + [jax.ShapeDtypeStruct((xc.shape[0] // ctx_seq, D_MODEL, ctx_seq), F32),
           jax.ShapeDtypeStruct((xc.shape[0], DIFF_HEADS, DIFF_DV), F32)]
        + [bf(xl.shape[0])] * 3,
        scratch_shapes=[pltpu.VMEM((D_MODEL, 3 * D_MODEL), BF16)],
        compiler_params=_params("arbitrary"),
        name="diff_qkv",
    )(xc, xl, g, mod, mod, w, *tables)
    return outs[:5], outs[5:]


def _attn_kernel(*refs, has_cache, lam_init, n_q_tiles, q_tile):
    if has_cache:
        (lq_ref, lk_ref, q_ref, k_ref, v_ref, gh_ref, kc_ref, vc_ref, y_ref, kcb_ref, vcb_ref) = refs
    else:
        (lq_ref, lk_ref, q_ref, k_ref, v_ref, gh_ref, y_ref) = refs
    e = jnp.exp(jnp.sum(lq_ref[...] * lk_ref[...], axis=1, keepdims=True))
    lam = e[0:1, :] - e[1:2, :] + lam_init
    first_map = lax.broadcasted_iota(jnp.int32, (q_tile, 2 * DIFF_DH), 1) < DIFF_DH
    gh = gh_ref[...] * (1.0 - lam_init)

    if has_cache:
        for head in range(DIFF_HEADS):
            kcb_ref[head] = kc_ref[head].astype(BF16)
        vcb_ref[...] = jnp.swapaxes(vc_ref[...], 0, 1).astype(BF16)

    def head_lanes(head):
        return slice(head * 2 * DIFF_DH, (head + 1) * 2 * DIFF_DH)

    def scores(head, sl):
        q = q_ref[sl, head_lanes(head)]
        zero = jnp.zeros_like(q)
        qq = jnp.concatenate([jnp.where(first_map, q, zero), jnp.where(first_map, zero, q)], axis=0)
        s = [_dot_nt(qq, k_ref[:, head_lanes(head)])]
        if has_cache:
            s.insert(0, _dot(qq, kcb_ref[head]))
        return s

    def attend(head, sl, s):
        vals = [v_ref[:, head_lanes(head)]]
        if has_cache:
            vals.insert(0, vcb_ref[head])
        m = functools.reduce(jnp.maximum, [jnp.max(x, axis=-1, keepdims=True) for x in s])
        p = [jnp.exp2(x - m) for x in s]
        denom = functools.reduce(jnp.add, [jnp.sum(x, axis=-1, keepdims=True) for x in p])
        acc = functools.reduce(jnp.add, [_dot(x.astype(BF16), vv) for x, vv in zip(p, vals)])
        o = acc[:q_tile] * (1.0 / denom[:q_tile]) - acc[q_tile:] * (lam / denom[q_tile:])
        y_ref[sl, head_lanes(head)] = _rms(o, gh).astype(BF16)

    def body(t, carry):
        sl = pl.ds(pl.multiple_of(t * q_tile, q_tile), q_tile)
        s_next = scores(0, sl)
        for head in range(DIFF_HEADS):
            s_cur = s_next
            if head + 1 < DIFF_HEADS:
                s_next = scores(head + 1, sl)
            attend(head, sl, s_cur)
        return carry

    lax.fori_loop(0, n_q_tiles, body, 0)


def _attn(q, k, v, lam_q, lam_k, g_head, lam_init, batch, seq, cache_k=None, cache_v=None):
    has_cache = cache_k is not None
    tok = pl.BlockSpec((seq, D_MODEL), lambda b: (b, 0))
    small = lambda shape: pl.BlockSpec(shape, lambda b: (0, 0))
    in_specs = [small((2, DIFF_DH)), small((2, DIFF_DH)), tok, tok, tok, small((1, DIFF_DV))]
    args = [lam_q, lam_k, q, k, v, g_head]
    if has_cache:
        past = cache_k.shape[-1]
        in_specs += [pl.BlockSpec((None, DIFF_HEADS, 2 * DIFF_DH, past), lambda b: (b, 0, 0, 0)),
                     pl.BlockSpec((None, past, DIFF_HEADS, DIFF_DV), lambda b: (b, 0, 0, 0))]
        args += [cache_k, cache_v]
        scratch = [pltpu.VMEM((DIFF_HEADS, 2 * DIFF_DH, past), BF16), pltpu.VMEM((DIFF_HEADS, past, DIFF_DV), BF16)]
    else:
        scratch = []
    q_tile = min(Q_TILE, seq)
    return pl.pallas_call(
        functools.partial(_attn_kernel, has_cache=has_cache, lam_init=lam_init, n_q_tiles=seq // q_tile, q_tile=q_tile),
        grid=(batch,),
        in_specs=in_specs,
        out_specs=tok,
        out_shape=jax.ShapeDtypeStruct((batch * seq, D_MODEL), BF16),
        scratch_shapes=scratch,
        compiler_params=_params("arbitrary"),
        name="diff_attn",
    )(*args)


def kernel(x_prompt, x_sample, state_gla_fwd, state_gla_bwd, cache_diff_k, cache_diff_v, c, c_ctx, w_ada, b_ada, g_mix_norm, g_ffn_norm, w_gla_in, w_gla_g1, w_gla_g2, b_gla_g, g_gla_head, w_gla_out, w_diff_qkv, lam_q, lam_k, g_diff_head, w_diff_out, w_ffn_gate, w_ffn_up, w_ffn_down, g_final):
    bp, tp, _ = x_prompt.shape
    bs, ts, _ = x_sample.shape
    assert ROW_TILE % tp == 0 and ts % ROW_TILE == 0 and (bp * tp) % ROW_TILE == 0
    assert tp % GLA_GROUP == 0 and ts % GLA_GROUP == 0
    xp = x_prompt.reshape(bp * tp, D_MODEL)
    xs = x_sample.reshape(bs * ts, D_MODEL)

    cond = jnp.concatenate([c_ctx[None, :], c, jnp.zeros((MOD_ROWS - 1 - bs, D_MODEL), F32)], axis=0)
    mod = _ada(cond, w_ada, b_ada).reshape(DEPTH, MOD_ROWS, N_MOD, 1, D_MODEL)
    wg, wu, wd = w_ffn_gate, w_ffn_up, w_ffn_down

    j = 0
    w1 = jnp.concatenate([w_gla_g1[j, 0], w_gla_g1[j, 1], jnp.zeros((D_MODEL, GATE_PAD - 2 * GLA_RANK), F32)],
                         axis=1).astype(BF16)
    w2 = jnp.zeros((GATE_PAD, 2 * GLA_HK), F32)
    w2 = w2.at[:GLA_RANK, :GLA_HK].set(w_gla_g2[j, 0]).at[GLA_RANK:2 * GLA_RANK, GLA_HK:].set(w_gla_g2[j, 1]).astype(BF16)
    bg = b_gla_g[j].reshape(1, 2 * GLA_HK)
    g_mix = g_mix_norm[0].reshape(1, D_MODEL)
    g_head = g_gla_head[j].reshape(1, GLA_DV)
    g_ffn = g_ffn_norm[0].reshape(1, D_MODEL)
    w_out = w_gla_out[j].astype(BF16)

    gla_p, gla_s = _gla_in(xp, xs, ts, 1, GLA_HEADS // 2, mod, 0, g_mix, w_gla_in, j, w1, w2, bg)
    yp, new_f, new_b = _gla_scan(*gla_p, g_head, bp, tp)
    ys = _gla_scan(*gla_s, g_head, bs, ts, state_gla_fwd[:, j], state_gla_bwd[:, j])
    xp, xs = _out_ffn(xp, xs, yp, ys, ts, mod, 0, w_out, g_ffn, wg, wu, wd)

    lam_init = 0.8 - 0.6 * math.exp(-0.3 * 1)
    g_mix = g_mix_norm[1].reshape(1, D_MODEL)
    g_head = g_diff_head[j].reshape(1, DIFF_DV)
    g_ffn = g_ffn_norm[1].reshape(1, D_MODEL)
    w_out = w_diff_out[j].astype(BF16)

    (qp, kp, vp, kp32, vp32), (qs, ks, vs) = _qkv(xp, xs, tp, ts, mod, 1, g_mix, w_diff_qkv, j, _rope_tables(ts))
    yp = _attn(qp, kp, vp, lam_q[j], lam_k[j], g_head, lam_init, bp, tp)
    past = cache_diff_k.shape[2]
    cache_k_t = jnp.transpose(cache_diff_k[:, j], (0, 2, 3, 4, 1)).reshape(bs, DIFF_HEADS, 2 * DIFF_DH, past)
    ys = _attn(qs, ks, vs, lam_q[j], lam_k[j], g_head, lam_init, bs, ts, cache_k_t, cache_diff_v[:, j])
    g_fin = g_final.reshape(1, D_MODEL)
    yp_out, ys_out = _out_ffn(xp, xs, yp, ys, ts, mod, 1, w_out, g_ffn, wg, wu, wd, g_final=g_fin)

    return (yp_out.reshape(bp, tp, D_MODEL),
            ys_out.reshape(bs, ts, D_MODEL),
            new_f[:, None],
            new_b[:, None],
            jnp.transpose(kp32.reshape(bp, 1, DIFF_HEADS, 2, DIFF_DH, tp), (0, 1, 5, 2, 3, 4)),
            vp32.reshape(bp, 1, tp, DIFF_HEADS, DIFF_DV))
```

```python
import functools
import math

import jax
import jax.numpy as jnp
from jax import lax
from jax.experimental import pallas as pl
from jax.experimental.pallas import tpu as pltpu

F32 = jnp.float32
BF16 = jnp.bfloat16

D_MODEL = 1024
DEPTH = 2
GRID_W = 64
GLA_HEADS = 4
GLA_DK = 128
GLA_DV = 256
GLA_HK = GLA_HEADS * GLA_DK
GLA_HV = GLA_HEADS * GLA_DV
GLA_RANK = 16
GLA_TAU = 16.0
GLA_CHUNK = 64
GLA_GROUP = 256
GLA_CHAINS = 8
DIFF_HEADS = 8
DIFF_DH = 64
DIFF_DV = 128
ROPE_THETA = 10000.0
D_FF = 2816
EPS = 1e-6
LOG2E = math.log2(math.e)
N_MOD = 6
MOD_ROWS = 16
GATE_PAD = 128

VMEM_LIMIT_BYTES = 58 * 1024 * 1024
ROW_TILE = 512
FFN_WEIGHT_STEPS = 8
GLA_WEIGHT_STEPS = 4
QKV_WEIGHT_STEPS = 4
Q_TILE = 256


def _params(*sem):
    return pltpu.CompilerParams(dimension_semantics=sem, vmem_limit_bytes=VMEM_LIMIT_BYTES)


def _resident(shape, index=None):
    index = (0,) * len(shape) if index is None else index
    return pl.BlockSpec(shape, lambda *_: index, pipeline_mode=pl.Buffered(1))


def _sigmoid(x):
    return 1.0 / (1.0 + jnp.exp(-x))


def _silu(x):
    return x * _sigmoid(x)


def _rms(x, g):
    return x * lax.rsqrt(jnp.mean(x * x, axis=-1, keepdims=True) + EPS) * g


def _norm_mod(x, g, shift, scale):
    return x * lax.rsqrt(jnp.mean(x * x, axis=-1, keepdims=True) + EPS) * (g * (1.0 + scale)) + shift


def _dot(a, b):
    return jnp.dot(a, b, preferred_element_type=F32)


def _dot_nt(a, b):
    return lax.dot_general(a, b, (((1,), (1,)), ((), ())), preferred_element_type=F32)


def _dot_tn(a, b):
    return lax.dot_general(a, b, (((0,), (0,)), ((), ())), preferred_element_type=F32)


def _ada_kernel(c_ref, w_ref, b_ref, o_ref):
    a = _silu(c_ref[...]).astype(BF16)
    o_ref[...] = _dot(a, w_ref[...].astype(BF16)) + b_ref[...]


def _ada(cond, w_ada, b_ada):
    tn = 1536
    n = N_MOD * D_MODEL
    return pl.pallas_call(
        _ada_kernel,
        grid=(DEPTH, n // tn),
        in_specs=[
            pl.BlockSpec((MOD_ROWS, D_MODEL), lambda l, j: (0, 0)),
            pl.BlockSpec((None, D_MODEL, tn), lambda l, j: (l, 0, j)),
            pl.BlockSpec((None, 1, tn), lambda l, j: (l, 0, j)),
        ],
        out_specs=pl.BlockSpec((None, MOD_ROWS, tn), lambda l, j: (l, 0, j)),
        out_shape=jax.ShapeDtypeStruct((DEPTH, MOD_ROWS, n), F32),
        compiler_params=_params("arbitrary", "arbitrary"),
        name="ada",
    )(cond, w_ada, b_ada.reshape(DEPTH, 1, n))


def _log_sigmoid(x):
    return jnp.minimum(x, 0.0) - jnp.log(1.0 + jnp.exp(-jnp.abs(x)))


class _TwoStreams:
    def __init__(self, ctx_rows, lat_rows, lat_seq, weight_steps):
        self.weight_steps = weight_steps
        self.ctx_tiles = ctx_rows // ROW_TILE
        self.first_lat = weight_steps + self.ctx_tiles
        self.grid = (self.first_lat + lat_rows // ROW_TILE,)
        self.tiles_per_batch = lat_seq // ROW_TILE

    def _rows(self, tile, width, groups):
        if groups is None:
            return pl.BlockSpec((ROW_TILE, width), lambda i: (tile(i), 0))
        return pl.BlockSpec((groups, ROW_TILE, width), lambda i: (0, tile(i), 0))

    def ctx(self, width, groups=None):
        return self._rows(lambda i: jnp.clip(i - self.weight_steps, 0, self.ctx_tiles - 1), width, groups)

    def lat(self, width, groups=None):
        return self._rows(lambda i: jnp.maximum(i - self.first_lat, 0), width, groups)

    def mod(self, layer, which):
        row = lambda i: jnp.where(i < self.first_lat, 0, 1 + (i - self.first_lat) // self.tiles_per_batch)
        return pl.BlockSpec((None, None, None, 1, D_MODEL), lambda i: (layer, row(i), which, 0, 0))

    def slab(self, layer, rows, cols):
        return pl.BlockSpec((None, rows // self.weight_steps, cols),
                            lambda i: (layer, jnp.minimum(i, self.weight_steps - 1), 0))

    def run(self, load_weights, ctx_tile, lat_tile):
        step = pl.program_id(0)
        pl.when(step < self.weight_steps)(lambda: load_weights(step))
        pl.when((step >= self.weight_steps) & (step < self.first_lat))(ctx_tile)
        pl.when(step >= self.first_lat)(lat_tile)


def _round_slab(step, src_ref, dst_ref):
    rows = src_ref.shape[0]
    dst_ref[pl.ds(pl.multiple_of(step * rows, rows), rows), :] = src_ref[...].astype(BF16)


def _gla_in_kernel(xc_ref, xl_ref, g_ref, sh_ref, sc_ref, w_ref, w1_ref, w2_ref, bg_ref, *refs, streams):
    outs_c, outs_l, wb_ref = refs[0:6], refs[6:12], refs[12]

    def tile(x_ref, q_ref, k_ref, v_ref, r_ref, laf_ref, lab_ref):
        h = _norm_mod(x_ref[...], g_ref[...], sh_ref[...], sc_ref[...]).astype(BF16)
        z = _dot(h, wb_ref[...])
        low_rank = _dot(h, w1_ref[...]).astype(BF16)
        la = _log_sigmoid(_dot(low_rank, w2_ref[...]) + bg_ref[...]) * (1.0 / GLA_TAU)
        groups = q_ref.shape[0]
        kw, vw = GLA_HK // groups, GLA_HV // groups
        for s in range(groups):
            q_ref[s] = z[:, s * kw:(s + 1) * kw]
            k_ref[s] = z[:, GLA_HK + s * kw:GLA_HK + (s + 1) * kw]
            v_ref[s] = z[:, 2 * GLA_HK + s * vw:2 * GLA_HK + (s + 1) * vw].astype(BF16)
            r_ref[s] = z[:, 2 * GLA_HK + GLA_HV + s * vw:2 * GLA_HK + GLA_HV + (s + 1) * vw]
            laf_ref[s] = la[:, s * kw:(s + 1) * kw]
            lab_ref[s] = la[:, GLA_HK + s * kw:GLA_HK + (s + 1) * kw]

    streams.run(lambda step: _round_slab(step, w_ref, wb_ref),
                lambda: tile(xc_ref, *outs_c), lambda: tile(xl_ref, *outs_l))


def _gla_in(xc, xl, lat_seq, ctx_groups, lat_groups, mod, layer, g, w_in, w_layer, w1, w2, bg):
    n_out = 2 * GLA_HK + 2 * GLA_HV
    streams = _TwoStreams(xc.shape[0], xl.shape[0], lat_seq, GLA_WEIGHT_STEPS)
    widths = [(GLA_HK, F32), (GLA_HK, F32), (GLA_HV, BF16), (GLA_HV, F32), (GLA_HK, F32), (GLA_HK, F32)]
    outs = pl.pallas_call(
        functools.partial(_gla_in_kernel, streams=streams),
        grid=streams.grid,
        in_specs=[
            streams.ctx(D_MODEL), streams.lat(D_MODEL),
            _resident((1, D_MODEL)),
            streams.mod(layer, 0), streams.mod(layer, 1),
            streams.slab(w_layer, D_MODEL, n_out),
            _resident((D_MODEL, GATE_PAD)),
            _resident((GATE_PAD, 2 * GLA_HK)),
            _resident((1, 2 * GLA_HK)),
        ],
        out_specs=([streams.ctx(w // ctx_groups, ctx_groups) for w, _ in widths]
                   + [streams.lat(w // lat_groups, lat_groups) for w, _ in widths]),
        out_shape=([jax.ShapeDtypeStruct((ctx_groups, xc.shape[0], w // ctx_groups), dt) for w, dt in widths]
                   + [jax.ShapeDtypeStruct((lat_groups, xl.shape[0], w // lat_groups), dt) for w, dt in widths]),
        scratch_shapes=[pltpu.VMEM((D_MODEL, n_out), BF16)],
        compiler_params=_params("arbitrary"),
        name="gla_in",
    )(xc, xl, g, mod, mod, w_in, w1, w2, bg)
    return outs[:6], outs[6:]


def _split3(x):
    hi = x.astype(BF16)
    rem = x - hi.astype(F32)
    mid = rem.astype(BF16)
    lo = (rem - mid.astype(F32)).astype(BF16)
    return jnp.concatenate([hi, mid, lo], axis=1)


def _fold3(x):
    return x[:, :GLA_DK] + x[:, GLA_DK:2 * GLA_DK] + x[:, 2 * GLA_DK:]


def _gla_scan_kernel(*refs, n_groups, n_seqs, hps, has_s0):
    C, DK, DV, G = GLA_CHUNK, GLA_DK, GLA_DV, GLA_GROUP
    cpg = G // C
    chunks_per_seq = n_groups * cpg // n_seqs
    unroll = max(1, min(n_groups, GLA_CHAINS // hps))
    if has_s0:
        (q_ref, k_ref, v_ref, laf_ref, lab_ref, r_ref, gh_ref, s0f_ref, s0b_ref,
         y_ref, qd_ref, kd_ref, ds_ref, sin_ref, dec_ref) = refs
    else:
        (q_ref, k_ref, v_ref, laf_ref, lab_ref, r_ref, gh_ref,
         y_ref, sf_ref, sb_ref, qd_ref, kd_ref, ds_ref, sin_ref, dec_ref) = refs

    row = lax.broadcasted_iota(jnp.int32, (G, G), 0)
    col = lax.broadcasted_iota(jnp.int32, (G, G), 1)
    same_chunk = (row // C) == (col // C)
    lower = same_chunk & (row >= col)
    upper = same_chunk & (row <= col)
    prefix_sum = jnp.where(lower, 1.0, 0.0).astype(BF16)
    suffix_sum = jnp.where(upper, 1.0, 0.0).astype(BF16)

    def group(g):
        return pl.ds(pl.multiple_of(g * G, G), G)

    def spread(x, first_row):
        return jnp.concatenate(
            [jnp.broadcast_to(x[first_row + c * C:first_row + c * C + 1], (C, x.shape[1])) for c in range(cpg)], axis=0)

    def pass1(g, carry):
        sl = group(g)
        for hd in range(hps):
            dk = slice(hd * DK, (hd + 1) * DK)
            dv = slice(hd * DV, (hd + 1) * DV)
            both = slice(hd * 2 * DK, (hd + 1) * 2 * DK)
            bf = _fold3(_dot(prefix_sum, _split3(laf_ref[sl, dk])))
            bb = _fold3(_dot(suffix_sum, _split3(lab_ref[sl, dk])))
            tot_f = spread(bf, C - 1)
            tot_b = spread(bb, 0)
            q = q_ref[sl, dk] * (DK ** -0.5)
            k = k_ref[sl, dk]
            qd_ref[sl, both] = jnp.concatenate([q * jnp.exp(bf), q * jnp.exp(bb)], axis=1).astype(BF16)
            kd_ref[sl, both] = jnp.concatenate([k * jnp.exp(-bf), k * jnp.exp(-bb)], axis=1).astype(BF16)
            k_end = jnp.concatenate([k * jnp.exp(tot_f - bf), k * jnp.exp(tot_b - bb)], axis=1).astype(BF16)
            v = v_ref[sl, dv]
            for c in range(cpg):
                rows = slice(c * C, (c + 1) * C)
                ds_ref[hd, g * cpg + c] = _dot_tn(v[rows], k_end[rows])
                dec_ref[hd, g * cpg + c] = jnp.exp(
                    jnp.concatenate([tot_f[c * C:c * C + 1], tot_b[c * C:c * C + 1]], axis=1))
        return carry

    lax.fori_loop(0, n_groups, pass1, 0, unroll=unroll)

    for seq_i in range(n_seqs):
        for hd in range(hps):
            for forward in (True, False):
                lanes = slice(0, DK) if forward else slice(DK, 2 * DK)
                if has_s0:
                    init = (s0f_ref if forward else s0b_ref)[seq_i, hd].T
                else:
                    init = jnp.zeros((DV, DK), F32)

                def step(i, s, seq_i=seq_i, hd=hd, forward=forward, lanes=lanes):
                    n = seq_i * chunks_per_seq + (i if forward else chunks_per_seq - 1 - i)
                    sin_ref[hd, n, :, lanes] = s.astype(BF16)
                    return s * dec_ref[hd, n, :, lanes] + ds_ref[hd, n, :, lanes]

                final = lax.fori_loop(0, chunks_per_seq, step, init)
                if not has_s0:
                    (sf_ref if forward else sb_ref)[seq_i, hd] = final.T

    gh = gh_ref[...]

    def pass3(g, carry):
        sl = group(g)
        for hd in range(hps):
            dv = slice(hd * DV, (hd + 1) * DV)
            both = slice(hd * 2 * DK, (hd + 1) * 2 * DK)
            qd = qd_ref[sl, both]
            kd = kd_ref[sl, both]
            a_f = _dot_nt(qd[:, :DK], kd[:, :DK])
            a_b = _dot_nt(qd[:, DK:], kd[:, DK:])
            a = jnp.where(lower, a_f, 0.0) + jnp.where(upper, a_b, 0.0)
            inter = [_dot_nt(qd[c * C:(c + 1) * C], sin_ref[hd, g * cpg + c]) for c in range(cpg)]
            o = _dot(a.astype(BF16), v_ref[sl, dv]) + jnp.concatenate(inter, axis=0)
            y_ref[sl, dv] = (_rms(o, gh) * _silu(r_ref[sl, dv])).astype(BF16)
        return carry

    lax.fori_loop(0, n_groups, pass3, 0, unroll=unroll)


def _gla_scan(q, k, v, r, la_f, la_b, g_head, batch, seq, s0f=None, s0b=None):
    steps = q.shape[0]
    hps = GLA_HEADS // steps
    has_s0 = s0f is not None
    rows = batch * seq
    n_seqs = max(1, min(batch, GLA_CHAINS // (hps * (seq // GLA_GROUP))))
    assert batch % n_seqs == 0
    batch, seq = batch // n_seqs, seq * n_seqs
    n_chunks = seq // GLA_CHUNK
    grouped = lambda w: pl.BlockSpec((None, seq, w * hps), lambda b, h: (h, b, 0))
    tok = lambda w, off: pl.BlockSpec((seq, w * hps), lambda b, h: (b, h + off))
    state = pl.BlockSpec((n_seqs, hps, GLA_DK, GLA_DV), lambda b, h: (b, h, 0, 0))
    in_specs = [grouped(GLA_DK), grouped(GLA_DK), grouped(GLA_DV), grouped(GLA_DK), grouped(GLA_DK),
                grouped(GLA_DV), pl.BlockSpec((1, GLA_DV), lambda b, h: (0, 0))]
    args = [q, k, v, la_f, la_b, r, g_head]
    y_shape = jax.ShapeDtypeStruct((rows, GLA_HV), BF16)
    st_shape = jax.ShapeDtypeStruct((batch * n_seqs, GLA_HEADS, GLA_DK, GLA_DV), F32)
    scratch = [
        pltpu.VMEM((seq, hps * 2 * GLA_DK), BF16),
        pltpu.VMEM((seq, hps * 2 * GLA_DK), BF16),
        pltpu.VMEM((hps, n_chunks, GLA_DV, 2 * GLA_DK), F32),
        pltpu.VMEM((hps, n_chunks, GLA_DV, 2 * GLA_DK), BF16),
        pltpu.VMEM((hps, n_chunks, 1, 2 * GLA_DK), F32),
    ]
    if has_s0:
        in_specs += [state, state]
        args += [s0f, s0b]
        out_specs = tok(GLA_DV, 0)
        out_shape = y_shape
    else:
        out_specs = [tok(GLA_DV, 0), state, state]
        out_shape = [y_shape, st_shape, st_shape]
    return pl.pallas_call(
        functools.partial(_gla_scan_kernel, n_groups=seq // GLA_GROUP, n_seqs=n_seqs, hps=hps, has_s0=has_s0),
        grid=(batch, steps),
        in_specs=in_specs,
        out_specs=out_specs,
        out_shape=out_shape,
        scratch_shapes=scratch,
        compiler_params=_params("arbitrary", "arbitrary"),
        name="gla_scan",
    )(*args)


def _out_ffn_kernel(*refs, final_norm, streams):
    if final_norm:
        (xc_ref, xl_ref, yc_ref, yl_ref, wo_ref, g1_ref, gn_ref, sh_ref, sc_ref, g2_ref, wg_ref, wu_ref, wd_ref,
         gf_ref, oc_ref, ol_ref, wgb_ref, wub_ref, wdb_ref) = refs
    else:
        (xc_ref, xl_ref, yc_ref, yl_ref, wo_ref, g1_ref, gn_ref, sh_ref, sc_ref, g2_ref, wg_ref, wu_ref, wd_ref,
         oc_ref, ol_ref, wgb_ref, wub_ref, wdb_ref) = refs

    def load_weights(step):
        _round_slab(step, wg_ref, wgb_ref)
        _round_slab(step, wu_ref, wub_ref)
        _round_slab(step, wd_ref, wdb_ref)

    def tile(x_ref, y_ref, o_ref):
        x = x_ref[...] + g1_ref[...] * _dot(y_ref[...], wo_ref[...])
        h = _norm_mod(x, gn_ref[...], sh_ref[...], sc_ref[...]).astype(BF16)
        act = (_silu(_dot(h, wgb_ref[...])) * _dot(h, wub_ref[...])).astype(BF16)
        x = x + g2_ref[...] * _dot(act, wdb_ref[...])
        if final_norm:
            x = _rms(x, gf_ref[...])
        o_ref[...] = x

    streams.run(load_weights, lambda: tile(xc_ref, yc_ref, oc_ref), lambda: tile(xl_ref, yl_ref, ol_ref))


def _out_ffn(xc, xl, yc, yl, lat_seq, mod, layer, w_out, g_ffn, wg, wu, wd, g_final=None):
    final_norm = g_final is not None
    streams = _TwoStreams(xc.shape[0], xl.shape[0], lat_seq, FFN_WEIGHT_STEPS)
    in_specs = [
        streams.ctx(D_MODEL), streams.lat(D_MODEL), streams.ctx(D_MODEL), streams.lat(D_MODEL),
        _resident((D_MODEL, D_MODEL)),
        streams.mod(layer, 2),
        _resident((1, D_MODEL)),
        streams.mod(layer, 3), streams.mod(layer, 4), streams.mod(layer, 5),
        streams.slab(layer, D_MODEL, D_FF), streams.slab(layer, D_MODEL, D_FF), streams.slab(layer, D_FF, D_MODEL),
    ]
    args = [xc, xl, yc, yl, w_out, mod, g_ffn, mod, mod, mod, wg, wu, wd]
    if final_norm:
        in_specs.append(_resident((1, D_MODEL)))
        args.append(g_final)
    return pl.pallas_call(
        functools.partial(_out_ffn_kernel, final_norm=final_norm, streams=streams),
        grid=streams.grid,
        in_specs=in_specs,
        out_specs=[streams.ctx(D_MODEL), streams.lat(D_MODEL)],
        out_shape=[jax.ShapeDtypeStruct(xc.shape, F32), jax.ShapeDtypeStruct(xl.shape, F32)],
        scratch_shapes=[pltpu.VMEM((D_MODEL, D_FF), BF16), pltpu.VMEM((D_MODEL, D_FF), BF16),
                        pltpu.VMEM((D_FF, D_MODEL), BF16)],
        compiler_params=_params("arbitrary"),
        name="out_ffn",
    )(*args)


def _rope_tables(n_tokens):
    rows = n_tokens // GRID_W
    r, col = jnp.meshgrid(jnp.arange(rows), jnp.arange(GRID_W), indexing="ij")
    r = r.reshape(-1).astype(F32)
    col = col.reshape(-1).astype(F32)
    n_freq = DIFF_DH // 4
    inv = ROPE_THETA ** (-jnp.arange(n_freq, dtype=F32) / n_freq)
    ang = jnp.concatenate([r[:, None] * inv, col[:, None] * inv], axis=-1)
    cos, sin = jnp.cos(ang), jnp.sin(ang)
    reps = D_MODEL // DIFF_DH
    return jnp.tile(jnp.concatenate([cos, cos], axis=-1), (1, reps)), jnp.tile(jnp.concatenate([-sin, sin], axis=-1), (1, reps))


def _rope(x, cos, sin_signed):
    half = DIFF_DH // 2
    width = x.shape[1]
    lane = lax.broadcasted_iota(jnp.int32, x.shape, 1)
    from_above = pltpu.roll(x, width - half, axis=1)
    from_below = pltpu.roll(x, half, axis=1)
    swapped = jnp.where(lane % DIFF_DH < half, from_above, from_below)
    return x * cos + swapped * sin_signed


def _qkv_kernel(xc_ref, xl_ref, g_ref, sh_ref, sc_ref, w_ref, cos_ref, sin_ref,
                qc_ref, kc_ref, vc_ref, kf_ref, vf_ref, ql_ref, kl_ref, vl_ref, wb_ref, *, streams, ctx_seq):
    def project(x_ref):
        h = _norm_mod(x_ref[...], g_ref[...], sh_ref[...], sc_ref[...]).astype(BF16)
        z = _dot(h, wb_ref[...])
        return z[:, :D_MODEL], z[:, D_MODEL:2 * D_MODEL], z[:, 2 * D_MODEL:]

    def store(q, k, v, q_ref, k_ref, v_ref):
        q_ref[...] = (q * (DIFF_DH ** -0.5 * LOG2E)).astype(BF16)
        k_ref[...] = k.astype(BF16)
        v_ref[...] = v.astype(BF16)

    def ctx_tile():
        q, k, v = project(xc_ref)
        for b in range(ROW_TILE // ctx_seq):
            kf_ref[b] = k[b * ctx_seq:(b + 1) * ctx_seq].T
        by_head = jnp.stack([v[:, head * DIFF_DV:(head + 1) * DIFF_DV] for head in range(DIFF_HEADS)], axis=0)
        vf_ref[...] = jnp.swapaxes(by_head, 0, 1)
        store(q, k, v, qc_ref, kc_ref, vc_ref)

    def lat_tile():
        q, k, v = project(xl_ref)
        store(_rope(q, cos_ref[...], sin_ref[...]), _rope(k, cos_ref[...], sin_ref[...]), v, ql_ref, kl_ref, vl_ref)

    streams.run(lambda step: _round_slab(step, w_ref, wb_ref), ctx_tile, lat_tile)


def _qkv(xc, xl, ctx_seq, lat_seq, mod, layer, g, w, w_layer, tables):
    streams = _TwoStreams(xc.shape[0], xl.shape[0], lat_seq, QKV_WEIGHT_STEPS)
    per_tile = ROW_TILE // ctx_seq
    parked_ctx = lambda i: jnp.clip(i - streams.weight_steps, 0, streams.ctx_tiles - 1)
    table = pl.BlockSpec((ROW_TILE, D_MODEL),
                         lambda i: (lax.rem(jnp.maximum(i - streams.first_lat, 0), streams.tiles_per_batch), 0))
    bf = lambda rows: jax.ShapeDtypeStruct((rows, D_MODEL), BF16)
    outs = pl.pallas_call(
        functools.partial(_qkv_kernel, streams=streams, ctx_seq=ctx_seq),
        grid=streams.grid,
        in_specs=[streams.ctx(D_MODEL), streams.lat(D_MODEL), _resident((1, D_MODEL)),
                  streams.mod(layer, 0), streams.mod(layer, 1), streams.slab(w_layer, D_MODEL, 3 * D_MODEL),
                  table, table],
        out_specs=[streams.ctx(D_MODEL)] * 3
        + [pl.BlockSpec((per_tile, D_MODEL, ctx_seq), lambda i: (parked_ctx(i), 0, 0)),
           pl.BlockSpec((ROW_TILE, DIFF_HEADS, DIFF_DV), lambda i: (parked_ctx(i), 0, 0))]
        + [streams.lat(D_MODEL)] * 3,
        out_shape=[bf(xc.shape[0])] * 3
        + [jax.ShapeDtypeStruct((xc.shape[0] // ctx_seq, D_MODEL, ctx_seq), F32),
           jax.ShapeDtypeStruct((xc.shape[0], DIFF_HEADS, DIFF_DV), F32)]
        + [bf(xl.shape[0])] * 3,
        scratch_shapes=[pltpu.VMEM((D_MODEL, 3 * D_MODEL), BF16)],
        compiler_params=_params("arbitrary"),
        name="diff_qkv",
    )(xc, xl, g, mod, mod, w, *tables)
    return outs[:5], outs[5:]


def _attn_kernel(*refs, has_cache, lam_init, n_q_tiles, q_tile):
    if has_cache:
        (lq_ref, lk_ref, q_ref, k_ref, v_ref, gh_ref, kc_ref, vc_ref, y_ref, kcb_ref, vcb_ref) = refs
    else:
        (lq_ref, lk_ref, q_ref, k_ref, v_ref, gh_ref, y_ref) = refs
    e = jnp.exp(jnp.sum(lq_ref[...] * lk_ref[...], axis=1, keepdims=True))
    lam = e[0:1, :] - e[1:2, :] + lam_init
    first_map = lax.broadcasted_iota(jnp.int32, (q_tile, 2 * DIFF_DH), 1) < DIFF_DH
    gh = gh_ref[...] * (1.0 - lam_init)

    if has_cache:
        for head in range(DIFF_HEADS):
            kcb_ref[head] = kc_ref[head].astype(BF16)
        vcb_ref[...] = jnp.swapaxes(vc_ref[...], 0, 1).astype(BF16)

    def head_lanes(head):
        return slice(head * 2 * DIFF_DH, (head + 1) * 2 * DIFF_DH)

    def scores(head, sl):
        q = q_ref[sl, head_lanes(head)]
        zero = jnp.zeros_like(q)
        qq = jnp.concatenate([jnp.where(first_map, q, zero), jnp.where(first_map, zero, q)], axis=0)
        s = [_dot_nt(qq, k_ref[:, head_lanes(head)])]
        if has_cache:
            s.insert(0, _dot(qq, kcb_ref[head]))
        return s

    def attend(head, sl, s):
        vals = [v_ref[:, head_lanes(head)]]
        if has_cache:
            vals.insert(0, vcb_ref[head])
        m = functools.reduce(jnp.maximum, [jnp.max(x, axis=-1, keepdims=True) for x in s])
        p = [jnp.exp2(x - m) for x in s]
        denom = functools.reduce(jnp.add, [jnp.sum(x, axis=-1, keepdims=True) for x in p])
        acc = functools.reduce(jnp.add, [_dot(x.astype(BF16), vv) for x, vv in zip(p, vals)])
        o = acc[:q_tile] * (1.0 / denom[:q_tile]) - acc[q_tile:] * (lam / denom[q_tile:])
        y_ref[sl, head_lanes(head)] = _rms(o, gh).astype(BF16)

    def body(t, carry):
        sl = pl.ds(pl.multiple_of(t * q_tile, q_tile), q_tile)
        s_next = scores(0, sl)
        for head in range(DIFF_HEADS):
            s_cur = s_next
            if head + 1 < DIFF_HEADS:
                s_next = scores(head + 1, sl)
            attend(head, sl, s_cur)
        return carry

    lax.fori_loop(0, n_q_tiles, body, 0)


def _attn(q, k, v, lam_q, lam_k, g_head, lam_init, batch, seq, cache_k=None, cache_v=None):
    has_cache = cache_k is not None
    tok = pl.BlockSpec((seq, D_MODEL), lambda b: (b, 0))
    small = lambda shape: pl.BlockSpec(shape, lambda b: (0, 0))
    in_specs = [small((2, DIFF_DH)), small((2, DIFF_DH)), tok, tok, tok, small((1, DIFF_DV))]
    args = [lam_q, lam_k, q, k, v, g_head]
    if has_cache:
        past = cache_k.shape[-1]
        in_specs += [pl.BlockSpec((None, DIFF_HEADS, 2 * DIFF_DH, past), lambda b: (b, 0, 0, 0)),
                     pl.BlockSpec((None, past, DIFF_HEADS, DIFF_DV), lambda b: (b, 0, 0, 0))]
        args += [cache_k, cache_v]
        scratch = [pltpu.VMEM((DIFF_HEADS, 2 * DIFF_DH, past), BF16), pltpu.VMEM((DIFF_HEADS, past, DIFF_DV), BF16)]
    else:
        scratch = []
    q_tile = min(Q_TILE, seq)
    return pl.pallas_call(
        functools.partial(_attn_kernel, has_cache=has_cache, lam_init=lam_init, n_q_tiles=seq // q_tile, q_tile=q_tile),
        grid=(batch,),
        in_specs=in_specs,
        out_specs=tok,
        out_shape=jax.ShapeDtypeStruct((batch * seq, D_MODEL), BF16),
        scratch_shapes=scratch,
        compiler_params=_params("arbitrary"),
        name="diff_attn",
    )(*args)


def kernel(x_prompt, x_sample, state_gla_fwd, state_gla_bwd, cache_diff_k, cache_diff_v, c, c_ctx, w_ada, b_ada, g_mix_norm, g_ffn_norm, w_gla_in, w_gla_g1, w_gla_g2, b_gla_g, g_gla_head, w_gla_out, w_diff_qkv, lam_q, lam_k, g_diff_head, w_diff_out, w_ffn_gate, w_ffn_up, w_ffn_down, g_final):
    bp, tp, _ = x_prompt.shape
    bs, ts, _ = x_sample.shape
    assert ROW_TILE % tp == 0 and ts % ROW_TILE == 0 and (bp * tp) % ROW_TILE == 0
    assert tp % GLA_GROUP == 0 and ts % GLA_GROUP == 0
    xp = x_prompt.reshape(bp * tp, D_MODEL)
    xs = x_sample.reshape(bs * ts, D_MODEL)

    cond = jnp.concatenate([c_ctx[None, :], c, jnp.zeros((MOD_ROWS - 1 - bs, D_MODEL), F32)], axis=0)
    mod = _ada(cond, w_ada, b_ada).reshape(DEPTH, MOD_ROWS, N_MOD, 1, D_MODEL)
    wg, wu, wd = w_ffn_gate, w_ffn_up, w_ffn_down

    j = 0
    w1 = jnp.concatenate([w_gla_g1[j, 0], w_gla_g1[j, 1], jnp.zeros((D_MODEL, GATE_PAD - 2 * GLA_RANK), F32)],
                         axis=1).astype(BF16)
    w2 = jnp.zeros((GATE_PAD, 2 * GLA_HK), F32)
    w2 = w2.at[:GLA_RANK, :GLA_HK].set(w_gla_g2[j, 0]).at[GLA_RANK:2 * GLA_RANK, GLA_HK:].set(w_gla_g2[j, 1]).astype(BF16)
    bg = b_gla_g[j].reshape(1, 2 * GLA_HK)
    g_mix = g_mix_norm[0].reshape(1, D_MODEL)
    g_head = g_gla_head[j].reshape(1, GLA_DV)
    g_ffn = g_ffn_norm[0].reshape(1, D_MODEL)
    w_out = w_gla_out[j].astype(BF16)

    gla_p, gla_s = _gla_in(xp, xs, ts, 1, GLA_HEADS // 2, mod, 0, g_mix, w_gla_in, j, w1, w2, bg)
    yp, new_f, new_b = _gla_scan(*gla_p, g_head, bp, tp)
    ys = _gla_scan(*gla_s, g_head, bs, ts, state_gla_fwd[:, j], state_gla_bwd[:, j])
    xp, xs = _out_ffn(xp, xs, yp, ys, ts, mod, 0, w_out, g_ffn, wg, wu, wd)

    lam_init = 0.8 - 0.6 * math.exp(-0.3 * 1)
    g_mix = g_mix_norm[1].reshape(1, D_MODEL)
    g_head = g_diff_head[j].reshape(1, DIFF_DV)
    g_ffn = g_ffn_norm[1].reshape(1, D_MODEL)
    w_out = w_diff_out[j].astype(BF16)

    (qp, kp, vp, kp32, vp32), (qs, ks, vs) = _qkv(xp, xs, tp, ts, mod, 1, g_mix, w_diff_qkv, j, _rope_tables(ts))
    yp = _attn(qp, kp, vp, lam_q[j], lam_k[j], g_head, lam_init, bp, tp)
    past = cache_diff_k.shape[2]
    cache_k_t = jnp.transpose(cache_diff_k[:, j], (0, 2, 3, 4, 1)).reshape(bs, DIFF_HEADS, 2 * DIFF_DH, past)
    ys = _attn(qs, ks, vs, lam_q[j], lam_k[j], g_head, lam_init, bs, ts, cache_k_t, cache_diff_v[:, j])
    g_fin = g_final.reshape(1, D_MODEL)
    yp_out, ys_out = _out_ffn(xp, xs, yp, ys, ts, mod, 1, w_out, g_ffn, wg, wu, wd, g_final=g_fin)

    return (yp_out.reshape(bp, tp, D_MODEL),
            ys_out.reshape(bs, ts, D_MODEL),
            new_f[:, None],
            new_b[:, None],
            jnp.transpose(kp32.reshape(bp, 1, DIFF_HEADS, 2, DIFF_DH, tp), (0, 1, 5, 2, 3, 4)),
            vp32.reshape(bp, 1, tp, DIFF_HEADS, DIFF_DV))
```

```python
import functools
import math

import jax
import jax.numpy as jnp
from jax import lax
from jax.experimental import pallas as pl
from jax.experimental.pallas import tpu as pltpu

F32 = jnp.float32
BF16 = jnp.bfloat16

D_MODEL = 1024
DEPTH = 2
GRID_W = 64
GLA_HEADS = 4
GLA_DK = 128
GLA_DV = 256
GLA_HK = GLA_HEADS * GLA_DK
GLA_HV = GLA_HEADS * GLA_DV
GLA_RANK = 16
GLA_TAU = 16.0
GLA_CHUNK = 64
GLA_GROUP = 256
GLA_CHAINS = 8
DIFF_HEADS = 8
DIFF_DH = 64
DIFF_DV = 128
ROPE_THETA = 10000.0
D_FF = 2816
EPS = 1e-6
LOG2E = math.log2(math.e)
N_MOD = 6
MOD_ROWS = 16
GATE_PAD = 128

VMEM_LIMIT_BYTES = 58 * 1024 * 1024
ROW_TILE = 512
FFN_WEIGHT_STEPS = 8
GLA_WEIGHT_STEPS = 4
QKV_WEIGHT_STEPS = 4
Q_TILE = 256


def _params(*sem):
    return pltpu.CompilerParams(dimension_semantics=sem, vmem_limit_bytes=VMEM_LIMIT_BYTES)


def _resident(shape, index=None):
    index = (0,) * len(shape) if index is None else index
    return pl.BlockSpec(shape, lambda *_: index, pipeline_mode=pl.Buffered(1))


def _sigmoid(x):
    return 1.0 / (1.0 + jnp.exp(-x))


def _silu(x):
    return x * _sigmoid(x)


def _rms(x, g):
    return x * lax.rsqrt(jnp.mean(x * x, axis=-1, keepdims=True) + EPS) * g


def _norm_mod(x, g, shift, scale):
    return x * lax.rsqrt(jnp.mean(x * x, axis=-1, keepdims=True) + EPS) * (g * (1.0 + scale)) + shift


def _dot(a, b):
    return jnp.dot(a, b, preferred_element_type=F32)


def _dot_nt(a, b):
    return lax.dot_general(a, b, (((1,), (1,)), ((), ())), preferred_element_type=F32)


def _dot_tn(a, b):
    return lax.dot_general(a, b, (((0,), (0,)), ((), ())), preferred_element_type=F32)


def _ada_kernel(c_ref, w_ref, b_ref, o_ref):
    a = _silu(c_ref[...]).astype(BF16)
    o_ref[...] = _dot(a, w_ref[...].astype(BF16)) + b_ref[...]


def _ada(cond, w_ada, b_ada):
    tn = 1536
    n = N_MOD * D_MODEL
    return pl.pallas_call(
        _ada_kernel,
        grid=(DEPTH, n // tn),
        in_specs=[
            pl.BlockSpec((MOD_ROWS, D_MODEL), lambda l, j: (0, 0)),
            pl.BlockSpec((None, D_MODEL, tn), lambda l, j: (l, 0, j)),
            pl.BlockSpec((None, 1, tn), lambda l, j: (l, 0, j)),
        ],
        out_specs=pl.BlockSpec((None, MOD_ROWS, tn), lambda l, j: (l, 0, j)),
        out_shape=jax.ShapeDtypeStruct((DEPTH, MOD_ROWS, n), F32),
        compiler_params=_params("arbitrary", "arbitrary"),
        name="ada",
    )(cond, w_ada, b_ada.reshape(DEPTH, 1, n))


def _log_sigmoid(x):
    return jnp.minimum(x, 0.0) - jnp.log(1.0 + jnp.exp(-jnp.abs(x)))


class _TwoStreams:
    def __init__(self, ctx_rows, lat_rows, lat_seq, weight_steps):
        self.weight_steps = weight_steps
        self.ctx_tiles = ctx_rows // ROW_TILE
        self.first_lat = weight_steps + self.ctx_tiles
        self.grid = (self.first_lat + lat_rows // ROW_TILE,)
        self.tiles_per_batch = lat_seq // ROW_TILE

    def _rows(self, tile, width, groups):
        if groups is None:
            return pl.BlockSpec((ROW_TILE, width), lambda i: (tile(i), 0))
        return pl.BlockSpec((groups, ROW_TILE, width), lambda i: (0, tile(i), 0))

    def ctx(self, width, groups=None):
        return self._rows(lambda i: jnp.clip(i - self.weight_steps, 0, self.ctx_tiles - 1), width, groups)

    def lat(self, width, groups=None):
        return self._rows(lambda i: jnp.maximum(i - self.first_lat, 0), width, groups)

    def mod(self, layer, which):
        row = lambda i: jnp.where(i < self.first_lat, 0, 1 + (i - self.first_lat) // self.tiles_per_batch)
        return pl.BlockSpec((None, None, None, 1, D_MODEL), lambda i: (layer, row(i), which, 0, 0))

    def slab(self, layer, rows, cols):
        return pl.BlockSpec((None, rows // self.weight_steps, cols),
                            lambda i: (layer, jnp.minimum(i, self.weight_steps - 1), 0))

    def run(self, load_weights, ctx_tile, lat_tile):
        step = pl.program_id(0)
        pl.when(step < self.weight_steps)(lambda: load_weights(step))
        pl.when((step >= self.weight_steps) & (step < self.first_lat))(ctx_tile)
        pl.when(step >= self.first_lat)(lat_tile)


def _round_slab(step, src_ref, dst_ref):
    rows = src_ref.shape[0]
    dst_ref[pl.ds(pl.multiple_of(step * rows, rows), rows), :] = src_ref[...].astype(BF16)


def _gla_in_kernel(xc_ref, xl_ref, g_ref, sh_ref, sc_ref, w_ref, w1_ref, w2_ref, bg_ref, *refs, streams):
    outs_c, outs_l, wb_ref = refs[0:6], refs[6:12], refs[12]

    def tile(x_ref, q_ref, k_ref, v_ref, r_ref, laf_ref, lab_ref):
        h = _norm_mod(x_ref[...], g_ref[...], sh_ref[...], sc_ref[...]).astype(BF16)
        z = _dot(h, wb_ref[...])
        low_rank = _dot(h, w1_ref[...]).astype(BF16)
        la = _log_sigmoid(_dot(low_rank, w2_ref[...]) + bg_ref[...]) * (1.0 / GLA_TAU)
        groups = q_ref.shape[0]
        kw, vw = GLA_HK // groups, GLA_HV // groups
        for s in range(groups):
            q_ref[s] = z[:, s * kw:(s + 1) * kw]
            k_ref[s] = z[:, GLA_HK + s * kw:GLA_HK + (s + 1) * kw]
            v_ref[s] = z[:, 2 * GLA_HK + s * vw:2 * GLA_HK + (s + 1) * vw].astype(BF16)
            r_ref[s] = z[:, 2 * GLA_HK + GLA_HV + s * vw:2 * GLA_HK + GLA_HV + (s + 1) * vw]
            laf_ref[s] = la[:, s * kw:(s + 1) * kw]
            lab_ref[s] = la[:, GLA_HK + s * kw:GLA_HK + (s + 1) * kw]

    streams.run(lambda step: _round_slab(step, w_ref, wb_ref),
                lambda: tile(xc_ref, *outs_c), lambda: tile(xl_ref, *outs_l))


def _gla_in(xc, xl, lat_seq, ctx_groups, lat_groups, mod, layer, g, w_in, w_layer, w1, w2, bg):
    n_out = 2 * GLA_HK + 2 * GLA_HV
    streams = _TwoStreams(xc.shape[0], xl.shape[0], lat_seq, GLA_WEIGHT_STEPS)
    widths = [(GLA_HK, F32), (GLA_HK, F32), (GLA_HV, BF16), (GLA_HV, F32), (GLA_HK, F32), (GLA_HK, F32)]
    outs = pl.pallas_call(
        functools.partial(_gla_in_kernel, streams=streams),
        grid=streams.grid,
        in_specs=[
            streams.ctx(D_MODEL), streams.lat(D_MODEL),
            _resident((1, D_MODEL)),
            streams.mod(layer, 0), streams.mod(layer, 1),
            streams.slab(w_layer, D_MODEL, n_out),
            _resident((D_MODEL, GATE_PAD)),
            _resident((GATE_PAD, 2 * GLA_HK)),
            _resident((1, 2 * GLA_HK)),
        ],
        out_specs=([streams.ctx(w // ctx_groups, ctx_groups) for w, _ in widths]
                   + [streams.lat(w // lat_groups, lat_groups) for w, _ in widths]),
        out_shape=([jax.ShapeDtypeStruct((ctx_groups, xc.shape[0], w // ctx_groups), dt) for w, dt in widths]
                   + [jax.ShapeDtypeStruct((lat_groups, xl.shape[0], w // lat_groups), dt) for w, dt in widths]),
        scratch_shapes=[pltpu.VMEM((D_MODEL, n_out), BF16)],
        compiler_params=_params("arbitrary"),
        name="gla_in",
    )(xc, xl, g, mod, mod, w_in, w1, w2, bg)
    return outs[:6], outs[6:]


def _split3(x):
    hi = x.astype(BF16)
    rem = x - hi.astype(F32)
    mid = rem.astype(BF16)
    lo = (rem - mid.astype(F32)).astype(BF16)
    return jnp.concatenate([hi, mid, lo], axis=1)


def _fold3(x):
    return x[:, :GLA_DK] + x[:, GLA_DK:2 * GLA_DK] + x[:, 2 * GLA_DK:]


def _gla_scan_kernel(*refs, n_groups, hps, has_s0):
    C, DK, DV, G = GLA_CHUNK, GLA_DK, GLA_DV, GLA_GROUP
    cpg = G // C
    n_chunks = n_groups * cpg
    unroll = max(1, min(n_groups, GLA_CHAINS // hps))
    if has_s0:
        (q_ref, k_ref, v_ref, laf_ref, lab_ref, r_ref, gh_ref, s0f_ref, s0b_ref,
         y_ref, qd_ref, kd_ref, ds_ref, sin_ref, dec_ref) = refs
    else:
        (q_ref, k_ref, v_ref, laf_ref, lab_ref, r_ref, gh_ref,
         y_ref, sf_ref, sb_ref, qd_ref, kd_ref, ds_ref, sin_ref, dec_ref) = refs

    row = lax.broadcasted_iota(jnp.int32, (G, G), 0)
    col = lax.broadcasted_iota(jnp.int32, (G, G), 1)
    same_chunk = (row // C) == (col // C)
    lower = same_chunk & (row >= col)
    upper = same_chunk & (row <= col)
    prefix_sum = jnp.where(lower, 1.0, 0.0).astype(BF16)
    suffix_sum = jnp.where(upper, 1.0, 0.0).astype(BF16)

    def group(g):
        return pl.ds(pl.multiple_of(g * G, G), G)

    def spread(x, first_row):
        return jnp.concatenate(
            [jnp.broadcast_to(x[first_row + c * C:first_row + c * C + 1], (C, x.shape[1])) for c in range(cpg)], axis=0)

    def pass1(g, carry):
        sl = group(g)
        for hd in range(hps):
            dk = slice(hd * DK, (hd + 1) * DK)
            dv = slice(hd * DV, (hd + 1) * DV)
            both = slice(hd * 2 * DK, (hd + 1) * 2 * DK)
            bf = _fold3(_dot(prefix_sum, _split3(laf_ref[sl, dk])))
            bb = _fold3(_dot(suffix_sum, _split3(lab_ref[sl, dk])))
            tot_f = spread(bf, C - 1)
            tot_b = spread(bb, 0)
            q = q_ref[sl, dk] * (DK ** -0.5)
            k = k_ref[sl, dk]
            qd_ref[sl, both] = jnp.concatenate([q * jnp.exp(bf), q * jnp.exp(bb)], axis=1).astype(BF16)
            kd_ref[sl, both] = jnp.concatenate([k * jnp.exp(-bf), k * jnp.exp(-bb)], axis=1).astype(BF16)
            k_end = jnp.concatenate([k * jnp.exp(tot_f - bf), k * jnp.exp(tot_b - bb)], axis=1).astype(BF16)
            v = v_ref[sl, dv]
            for c in range(cpg):
                rows = slice(c * C, (c + 1) * C)
                ds_ref[hd, g * cpg + c] = _dot_tn(v[rows], k_end[rows])
                dec_ref[hd, g * cpg + c] = jnp.exp(
                    jnp.concatenate([tot_f[c * C:c * C + 1], tot_b[c * C:c * C + 1]], axis=1))
        return carry

    lax.fori_loop(0, n_groups, pass1, 0, unroll=unroll)

    for hd in range(hps):
        for forward in (True, False):
            lanes = slice(0, DK) if forward else slice(DK, 2 * DK)
            if has_s0:
                init = (s0f_ref if forward else s0b_ref)[hd].T
            else:
                init = jnp.zeros((DV, DK), F32)

            def step(i, s, hd=hd, forward=forward, lanes=lanes):
                n = i if forward else n_chunks - 1 - i
                sin_ref[hd, n, :, lanes] = s.astype(BF16)
                return s * dec_ref[hd, n, :, lanes] + ds_ref[hd, n, :, lanes]

            final = lax.fori_loop(0, n_chunks, step, init)
            if not has_s0:
                (sf_ref if forward else sb_ref)[hd] = final.T

    gh = gh_ref[...]

    def pass3(g, carry):
        sl = group(g)
        for hd in range(hps):
            dv = slice(hd * DV, (hd + 1) * DV)
            both = slice(hd * 2 * DK, (hd + 1) * 2 * DK)
            qd = qd_ref[sl, both]
            kd = kd_ref[sl, both]
            a_f = _dot_nt(qd[:, :DK], kd[:, :DK])
            a_b = _dot_nt(qd[:, DK:], kd[:, DK:])
            a = jnp.where(lower, a_f, 0.0) + jnp.where(upper, a_b, 0.0)
            inter = [_dot_nt(qd[c * C:(c + 1) * C], sin_ref[hd, g * cpg + c]) for c in range(cpg)]
            o = _dot(a.astype(BF16), v_ref[sl, dv]) + jnp.concatenate(inter, axis=0)
            y_ref[sl, dv] = (_rms(o, gh) * _silu(r_ref[sl, dv])).astype(BF16)
        return carry

    lax.fori_loop(0, n_groups, pass3, 0, unroll=unroll)


def _gla_scan(q, k, v, r, la_f, la_b, g_head, batch, seq, s0f=None, s0b=None):
    steps = q.shape[0]
    hps = GLA_HEADS // steps
    n_chunks = seq // GLA_CHUNK
    has_s0 = s0f is not None
    rows = batch * seq
    grouped = lambda w: pl.BlockSpec((None, seq, w * hps), lambda b, h: (h, b, 0))
    tok = lambda w, off: pl.BlockSpec((seq, w * hps), lambda b, h: (b, h + off))
    state = pl.BlockSpec((None, hps, GLA_DK, GLA_DV), lambda b, h: (b, h, 0, 0))
    in_specs = [grouped(GLA_DK), grouped(GLA_DK), grouped(GLA_DV), grouped(GLA_DK), grouped(GLA_DK),
                grouped(GLA_DV), pl.BlockSpec((1, GLA_DV), lambda b, h: (0, 0))]
    args = [q, k, v, la_f, la_b, r, g_head]
    y_shape = jax.ShapeDtypeStruct((rows, GLA_HV), BF16)
    st_shape = jax.ShapeDtypeStruct((batch, GLA_HEADS, GLA_DK, GLA_DV), F32)
    scratch = [
        pltpu.VMEM((seq, hps * 2 * GLA_DK), BF16),
        pltpu.VMEM((seq, hps * 2 * GLA_DK), BF16),
        pltpu.VMEM((hps, n_chunks, GLA_DV, 2 * GLA_DK), F32),
        pltpu.VMEM((hps, n_chunks, GLA_DV, 2 * GLA_DK), BF16),
        pltpu.VMEM((hps, n_chunks, 1, 2 * GLA_DK), F32),
    ]
    if has_s0:
        in_specs += [state, state]
        args += [s0f, s0b]
        out_specs = tok(GLA_DV, 0)
        out_shape = y_shape
    else:
        out_specs = [tok(GLA_DV, 0), state, state]
        out_shape = [y_shape, st_shape, st_shape]
    return pl.pallas_call(
        functools.partial(_gla_scan_kernel, n_groups=seq // GLA_GROUP, hps=hps, has_s0=has_s0),
        grid=(batch, steps),
        in_specs=in_specs,
        out_specs=out_specs,
        out_shape=out_shape,
        scratch_shapes=scratch,
        compiler_params=_params("arbitrary", "arbitrary"),
        name="gla_scan",
    )(*args)


def _out_ffn_kernel(*refs, final_norm, streams):
    if final_norm:
        (xc_ref, xl_ref, yc_ref, yl_ref, wo_ref, g1_ref, gn_ref, sh_ref, sc_ref, g2_ref, wg_ref, wu_ref, wd_ref,
         gf_ref, oc_ref, ol_ref, wgb_ref, wub_ref, wdb_ref) = refs
    else:
        (xc_ref, xl_ref, yc_ref, yl_ref, wo_ref, g1_ref, gn_ref, sh_ref, sc_ref, g2_ref, wg_ref, wu_ref, wd_ref,
         oc_ref, ol_ref, wgb_ref, wub_ref, wdb_ref) = refs

    def load_weights(step):
        _round_slab(step, wg_ref, wgb_ref)
        _round_slab(step, wu_ref, wub_ref)
        _round_slab(step, wd_ref, wdb_ref)

    def tile(x_ref, y_ref, o_ref):
        x = x_ref[...] + g1_ref[...] * _dot(y_ref[...], wo_ref[...])
        h = _norm_mod(x, gn_ref[...], sh_ref[...], sc_ref[...]).astype(BF16)
        act = (_silu(_dot(h, wgb_ref[...])) * _dot(h, wub_ref[...])).astype(BF16)
        x = x + g2_ref[...] * _dot(act, wdb_ref[...])
        if final_norm:
            x = _rms(x, gf_ref[...])
        o_ref[...] = x

    streams.run(load_weights, lambda: tile(xc_ref, yc_ref, oc_ref), lambda: tile(xl_ref, yl_ref, ol_ref))


def _out_ffn(xc, xl, yc, yl, lat_seq, mod, layer, w_out, g_ffn, wg, wu, wd, g_final=None):
    final_norm = g_final is not None
    streams = _TwoStreams(xc.shape[0], xl.shape[0], lat_seq, FFN_WEIGHT_STEPS)
    in_specs = [
        streams.ctx(D_MODEL), streams.lat(D_MODEL), streams.ctx(D_MODEL), streams.lat(D_MODEL),
        _resident((D_MODEL, D_MODEL)),
        streams.mod(layer, 2),
        _resident((1, D_MODEL)),
        streams.mod(layer, 3), streams.mod(layer, 4), streams.mod(layer, 5),
        streams.slab(layer, D_MODEL, D_FF), streams.slab(layer, D_MODEL, D_FF), streams.slab(layer, D_FF, D_MODEL),
    ]
    args = [xc, xl, yc, yl, w_out, mod, g_ffn, mod, mod, mod, wg, wu, wd]
    if final_norm:
        in_specs.append(_resident((1, D_MODEL)))
        args.append(g_final)
    return pl.pallas_call(
        functools.partial(_out_ffn_kernel, final_norm=final_norm, streams=streams),
        grid=streams.grid,
        in_specs=in_specs,
        out_specs=[streams.ctx(D_MODEL), streams.lat(D_MODEL)],
        out_shape=[jax.ShapeDtypeStruct(xc.shape, F32), jax.ShapeDtypeStruct(xl.shape, F32)],
        scratch_shapes=[pltpu.VMEM((D_MODEL, D_FF), BF16), pltpu.VMEM((D_MODEL, D_FF), BF16),
                        pltpu.VMEM((D_FF, D_MODEL), BF16)],
        compiler_params=_params("arbitrary"),
        name="out_ffn",
    )(*args)


def _rope_tables(n_tokens):
    rows = n_tokens // GRID_W
    r, col = jnp.meshgrid(jnp.arange(rows), jnp.arange(GRID_W), indexing="ij")
    r = r.reshape(-1).astype(F32)
    col = col.reshape(-1).astype(F32)
    n_freq = DIFF_DH // 4
    inv = ROPE_THETA ** (-jnp.arange(n_freq, dtype=F32) / n_freq)
    ang = jnp.concatenate([r[:, None] * inv, col[:, None] * inv], axis=-1)
    cos, sin = jnp.cos(ang), jnp.sin(ang)
    reps = D_MODEL // DIFF_DH
    return jnp.tile(jnp.concatenate([cos, cos], axis=-1), (1, reps)), jnp.tile(jnp.concatenate([-sin, sin], axis=-1), (1, reps))


def _rope(x, cos, sin_signed):
    half = DIFF_DH // 2
    width = x.shape[1]
    lane = lax.broadcasted_iota(jnp.int32, x.shape, 1)
    from_above = pltpu.roll(x, width - half, axis=1)
    from_below = pltpu.roll(x, half, axis=1)
    swapped = jnp.where(lane % DIFF_DH < half, from_above, from_below)
    return x * cos + swapped * sin_signed


def _qkv_kernel(xc_ref, xl_ref, g_ref, sh_ref, sc_ref, w_ref, cos_ref, sin_ref,
                qc_ref, kc_ref, vc_ref, kf_ref, vf_ref, ql_ref, kl_ref, vl_ref, wb_ref, *, streams, ctx_seq):
    def project(x_ref):
        h = _norm_mod(x_ref[...], g_ref[...], sh_ref[...], sc_ref[...]).astype(BF16)
        z = _dot(h, wb_ref[...])
        return z[:, :D_MODEL], z[:, D_MODEL:2 * D_MODEL], z[:, 2 * D_MODEL:]

    def store(q, k, v, q_ref, k_ref, v_ref):
        q_ref[...] = (q * (DIFF_DH ** -0.5 * LOG2E)).astype(BF16)
        k_ref[...] = k.astype(BF16)
        v_ref[...] = v.astype(BF16)

    def ctx_tile():
        q, k, v = project(xc_ref)
        for b in range(ROW_TILE // ctx_seq):
            rows = slice(b * ctx_seq, (b + 1) * ctx_seq)
            kf_ref[b] = k[rows].T
            qc_ref[b] = (q[rows] * (DIFF_DH ** -0.5 * LOG2E)).T.astype(BF16)
            vc_ref[b] = v[rows].T.astype(BF16)
        by_head = jnp.stack([v[:, head * DIFF_DV:(head + 1) * DIFF_DV] for head in range(DIFF_HEADS)], axis=0)
        vf_ref[...] = jnp.swapaxes(by_head, 0, 1)
        kc_ref[...] = k.astype(BF16)

    def lat_tile():
        q, k, v = project(xl_ref)
        store(_rope(q, cos_ref[...], sin_ref[...]), _rope(k, cos_ref[...], sin_ref[...]), v, ql_ref, kl_ref, vl_ref)

    streams.run(lambda step: _round_slab(step, w_ref, wb_ref), ctx_tile, lat_tile)


def _qkv(xc, xl, ctx_seq, lat_seq, mod, layer, g, w, w_layer, tables):
    streams = _TwoStreams(xc.shape[0], xl.shape[0], lat_seq, QKV_WEIGHT_STEPS)
    per_tile = ROW_TILE // ctx_seq
    parked_ctx = lambda i: jnp.clip(i - streams.weight_steps, 0, streams.ctx_tiles - 1)
    table = pl.BlockSpec((ROW_TILE, D_MODEL),
                         lambda i: (lax.rem(jnp.maximum(i - streams.first_lat, 0), streams.tiles_per_batch), 0))
    bf = lambda rows: jax.ShapeDtypeStruct((rows, D_MODEL), BF16)
    per_batch = pl.BlockSpec((per_tile, D_MODEL, ctx_seq), lambda i: (parked_ctx(i), 0, 0))
    feature_major = lambda dtype: jax.ShapeDtypeStruct((xc.shape[0] // ctx_seq, D_MODEL, ctx_seq), dtype)
    outs = pl.pallas_call(
        functools.partial(_qkv_kernel, streams=streams, ctx_seq=ctx_seq),
        grid=streams.grid,
        in_specs=[streams.ctx(D_MODEL), streams.lat(D_MODEL), _resident((1, D_MODEL)),
                  streams.mod(layer, 0), streams.mod(layer, 1), streams.slab(w_layer, D_MODEL, 3 * D_MODEL),
                  table, table],
        out_specs=[per_batch, streams.ctx(D_MODEL), per_batch, per_batch,
                   pl.BlockSpec((ROW_TILE, DIFF_HEADS, DIFF_DV), lambda i: (parked_ctx(i), 0, 0))]
        + [streams.lat(D_MODEL)] * 3,
        out_shape=[feature_major(BF16), bf(xc.shape[0]), feature_major(BF16), feature_major(F32),
           jax.ShapeDtypeStruct((xc.shape[0], DIFF_HEADS, DIFF_DV), F32)]
        + [bf(xl.shape[0])] * 3,
        scratch_shapes=[pltpu.VMEM((D_MODEL, 3 * D_MODEL), BF16)],
        compiler_params=_params("arbitrary"),
        name="diff_qkv",
    )(xc, xl, g, mod, mod, w, *tables)
    return outs[:5], outs[5:]


def _attn_ctx_kernel(lq_ref, lk_ref, q_ref, k_ref, v_ref, gh_ref, y_ref, *, lam_init):
    seq = k_ref.shape[0]
    e = jnp.exp(jnp.sum(lq_ref[...] * lk_ref[...], axis=1, keepdims=True))
    lam = e[0:1, :] - e[1:2, :] + lam_init
    first_map = lax.broadcasted_iota(jnp.int32, (2 * DIFF_DH, seq), 0) < DIFF_DH
    gh = gh_ref[...] * (1.0 - lam_init)
    def features(head):
        return slice(head * 2 * DIFF_DH, (head + 1) * 2 * DIFF_DH)

    def scores(head):
        q_t = q_ref[features(head), :]
        zero = jnp.zeros_like(q_t)
        qq = jnp.concatenate([jnp.where(first_map, q_t, zero), jnp.where(first_map, zero, q_t)], axis=1)
        return _dot(k_ref[:, features(head)], qq)

    def attend(head, s):
        p = jnp.exp2(s - jnp.max(s, axis=0, keepdims=True))
        denom = jnp.sum(p, axis=0, keepdims=True)
        acc = _dot(v_ref[features(head), :], p.astype(BF16))
        o = acc[:, :seq] * (1.0 / denom[:, :seq]) - acc[:, seq:] * (lam / denom[:, seq:])
        o = o * lax.rsqrt(jnp.mean(o * o, axis=0, keepdims=True) + EPS) * gh
        y_ref[:, features(head)] = o.T.astype(BF16)

    s_next = scores(0)
    for head in range(DIFF_HEADS):
        s_cur = s_next
        if head + 1 < DIFF_HEADS:
            s_next = scores(head + 1)
        attend(head, s_cur)


def _attn_ctx(q_t, k, v_t, lam_q, lam_k, g_head, lam_init, batch, seq):
    small = lambda shape: pl.BlockSpec(shape, lambda b: (0, 0))
    feature_major = pl.BlockSpec((None, D_MODEL, seq), lambda b: (b, 0, 0))
    tok = pl.BlockSpec((seq, D_MODEL), lambda b: (b, 0))
    return pl.pallas_call(
        functools.partial(_attn_ctx_kernel, lam_init=lam_init),
        grid=(batch,),
        in_specs=[small((2, DIFF_DH)), small((2, DIFF_DH)), feature_major, tok, feature_major, small((DIFF_DV, 1))],
        out_specs=tok,
        out_shape=jax.ShapeDtypeStruct((batch * seq, D_MODEL), BF16),
        compiler_params=_params("arbitrary"),
        name="diff_attn_ctx",
    )(lam_q, lam_k, q_t, k, v_t, g_head)


def _attn_kernel(*refs, has_cache, lam_init, n_q_tiles, q_tile):
    if has_cache:
        (lq_ref, lk_ref, q_ref, k_ref, v_ref, gh_ref, kc_ref, vc_ref, y_ref, kcb_ref, vcb_ref) = refs
    else:
        (lq_ref, lk_ref, q_ref, k_ref, v_ref, gh_ref, y_ref) = refs
    e = jnp.exp(jnp.sum(lq_ref[...] * lk_ref[...], axis=1, keepdims=True))
    lam = e[0:1, :] - e[1:2, :] + lam_init
    first_map = lax.broadcasted_iota(jnp.int32, (q_tile, 2 * DIFF_DH), 1) < DIFF_DH
    gh = gh_ref[...] * (1.0 - lam_init)

    if has_cache:
        for head in range(DIFF_HEADS):
            kcb_ref[head] = kc_ref[head].astype(BF16)
        vcb_ref[...] = jnp.swapaxes(vc_ref[...], 0, 1).astype(BF16)

    def head_lanes(head):
        return slice(head * 2 * DIFF_DH, (head + 1) * 2 * DIFF_DH)

    def scores(head, sl):
        q = q_ref[sl, head_lanes(head)]
        zero = jnp.zeros_like(q)
        qq = jnp.concatenate([jnp.where(first_map, q, zero), jnp.where(first_map, zero, q)], axis=0)
        s = [_dot_nt(qq, k_ref[:, head_lanes(head)])]
        if has_cache:
            s.insert(0, _dot(qq, kcb_ref[head]))
        return s

    def attend(head, sl, s):
        vals = [v_ref[:, head_lanes(head)]]
        if has_cache:
            vals.insert(0, vcb_ref[head])
        m = functools.reduce(jnp.maximum, [jnp.max(x, axis=-1, keepdims=True) for x in s])
        p = [jnp.exp2(x - m) for x in s]
        denom = functools.reduce(jnp.add, [jnp.sum(x, axis=-1, keepdims=True) for x in p])
        acc = functools.reduce(jnp.add, [_dot(x.astype(BF16), vv) for x, vv in zip(p, vals)])
        o = acc[:q_tile] * (1.0 / denom[:q_tile]) - acc[q_tile:] * (lam / denom[q_tile:])
        y_ref[sl, head_lanes(head)] = _rms(o, gh).astype(BF16)

    def body(t, carry):
        sl = pl.ds(pl.multiple_of(t * q_tile, q_tile), q_tile)
        s_next = scores(0, sl)
        for head in range(DIFF_HEADS):
            s_cur = s_next
            if head + 1 < DIFF_HEADS:
                s_next = scores(head + 1, sl)
            attend(head, sl, s_cur)
        return carry

    lax.fori_loop(0, n_q_tiles, body, 0)


def _attn(q, k, v, lam_q, lam_k, g_head, lam_init, batch, seq, cache_k=None, cache_v=None):
    has_cache = cache_k is not None
    tok = pl.BlockSpec((seq, D_MODEL), lambda b: (b, 0))
    small = lambda shape: pl.BlockSpec(shape, lambda b: (0, 0))
    in_specs = [small((2, DIFF_DH)), small((2, DIFF_DH)), tok, tok, tok, small((1, DIFF_DV))]
    args = [lam_q, lam_k, q, k, v, g_head]
    if has_cache:
        past = cache_k.shape[-1]
        in_specs += [pl.BlockSpec((None, DIFF_HEADS, 2 * DIFF_DH, past), lambda b: (b, 0, 0, 0)),
                     pl.BlockSpec((None, past, DIFF_HEADS, DIFF_DV), lambda b: (b, 0, 0, 0))]
        args += [cache_k, cache_v]
        scratch = [pltpu.VMEM((DIFF_HEADS, 2 * DIFF_DH, past), BF16), pltpu.VMEM((DIFF_HEADS, past, DIFF_DV), BF16)]
    else:
        scratch = []
    q_tile = min(Q_TILE, seq)
    return pl.pallas_call(
        functools.partial(_attn_kernel, has_cache=has_cache, lam_init=lam_init, n_q_tiles=seq // q_tile, q_tile=q_tile),
        grid=(batch,),
        in_specs=in_specs,
        out_specs=tok,
        out_shape=jax.ShapeDtypeStruct((batch * seq, D_MODEL), BF16),
        scratch_shapes=scratch,
        compiler_params=_params("arbitrary"),
        name="diff_attn",
    )(*args)


def kernel(x_prompt, x_sample, state_gla_fwd, state_gla_bwd, cache_diff_k, cache_diff_v, c, c_ctx, w_ada, b_ada, g_mix_norm, g_ffn_norm, w_gla_in, w_gla_g1, w_gla_g2, b_gla_g, g_gla_head, w_gla_out, w_diff_qkv, lam_q, lam_k, g_diff_head, w_diff_out, w_ffn_gate, w_ffn_up, w_ffn_down, g_final):
    bp, tp, _ = x_prompt.shape
    bs, ts, _ = x_sample.shape
    assert ROW_TILE % tp == 0 and ts % ROW_TILE == 0 and (bp * tp) % ROW_TILE == 0
    assert tp % GLA_GROUP == 0 and ts % GLA_GROUP == 0
    xp = x_prompt.reshape(bp * tp, D_MODEL)
    xs = x_sample.reshape(bs * ts, D_MODEL)

    cond = jnp.concatenate([c_ctx[None, :], c, jnp.zeros((MOD_ROWS - 1 - bs, D_MODEL), F32)], axis=0)
    mod = _ada(cond, w_ada, b_ada).reshape(DEPTH, MOD_ROWS, N_MOD, 1, D_MODEL)
    wg, wu, wd = w_ffn_gate, w_ffn_up, w_ffn_down

    j = 0
    w1 = jnp.concatenate([w_gla_g1[j, 0], w_gla_g1[j, 1], jnp.zeros((D_MODEL, GATE_PAD - 2 * GLA_RANK), F32)],
                         axis=1).astype(BF16)
    w2 = jnp.zeros((GATE_PAD, 2 * GLA_HK), F32)
    w2 = w2.at[:GLA_RANK, :GLA_HK].set(w_gla_g2[j, 0]).at[GLA_RANK:2 * GLA_RANK, GLA_HK:].set(w_gla_g2[j, 1]).astype(BF16)
    bg = b_gla_g[j].reshape(1, 2 * GLA_HK)
    g_mix = g_mix_norm[0].reshape(1, D_MODEL)
    g_head = g_gla_head[j].reshape(1, GLA_DV)
    g_ffn = g_ffn_norm[0].reshape(1, D_MODEL)
    w_out = w_gla_out[j].astype(BF16)

    gla_p, gla_s = _gla_in(xp, xs, ts, 1, GLA_HEADS // 2, mod, 0, g_mix, w_gla_in, j, w1, w2, bg)
    yp, new_f, new_b = _gla_scan(*gla_p, g_head, bp, tp)
    ys = _gla_scan(*gla_s, g_head, bs, ts, state_gla_fwd[:, j], state_gla_bwd[:, j])
    xp, xs = _out_ffn(xp, xs, yp, ys, ts, mod, 0, w_out, g_ffn, wg, wu, wd)

    lam_init = 0.8 - 0.6 * math.exp(-0.3 * 1)
    g_mix = g_mix_norm[1].reshape(1, D_MODEL)
    g_head = g_diff_head[j].reshape(1, DIFF_DV)
    g_ffn = g_ffn_norm[1].reshape(1, D_MODEL)
    w_out = w_diff_out[j].astype(BF16)

    (qp, kp, vp, kp32, vp32), (qs, ks, vs) = _qkv(xp, xs, tp, ts, mod, 1, g_mix, w_diff_qkv, j, _rope_tables(ts))
    yp = _attn_ctx(qp, kp, vp, lam_q[j], lam_k[j], g_head.reshape(DIFF_DV, 1), lam_init, bp, tp)
    past = cache_diff_k.shape[2]
    cache_k_t = jnp.transpose(cache_diff_k[:, j], (0, 2, 3, 4, 1)).reshape(bs, DIFF_HEADS, 2 * DIFF_DH, past)
    ys = _attn(qs, ks, vs, lam_q[j], lam_k[j], g_head, lam_init, bs, ts, cache_k_t, cache_diff_v[:, j])
    g_fin = g_final.reshape(1, D_MODEL)
    yp_out, ys_out = _out_ffn(xp, xs, yp, ys, ts, mod, 1, w_out, g_ffn, wg, wu, wd, g_final=g_fin)

    return (yp_out.reshape(bp, tp, D_MODEL),
            ys_out.reshape(bs, ts, D_MODEL),
            new_f[:, None],
            new_b[:, None],
            jnp.transpose(kp32.reshape(bp, 1, DIFF_HEADS, 2, DIFF_DH, tp), (0, 1, 5, 2, 3, 4)),
            vp32.reshape(bp, 1, tp, DIFF_HEADS, DIFF_DV))
```

```python
import functools
import math

import jax
import jax.numpy as jnp
from jax import lax
from jax.experimental import pallas as pl
from jax.experimental.pallas import tpu as pltpu

F32 = jnp.float32
BF16 = jnp.bfloat16

D_MODEL = 1024
DEPTH = 2
GRID_W = 64
GLA_HEADS = 4
GLA_DK = 128
GLA_DV = 256
GLA_HK = GLA_HEADS * GLA_DK
GLA_HV = GLA_HEADS * GLA_DV
GLA_RANK = 16
GLA_TAU = 16.0
GLA_CHUNK = 64
GLA_GROUP = 256
GLA_CHAINS = 8
DIFF_HEADS = 8
DIFF_DH = 64
DIFF_DV = 128
ROPE_THETA = 10000.0
D_FF = 2816
EPS = 1e-6
LOG2E = math.log2(math.e)
N_MOD = 6
MOD_ROWS = 16
GATE_PAD = 128

VMEM_LIMIT_BYTES = 58 * 1024 * 1024
ROW_TILE = 512
FFN_WEIGHT_STEPS = 8
GLA_WEIGHT_STEPS = 4
QKV_WEIGHT_STEPS = 4
Q_TILE = 256


def _params(*sem):
    return pltpu.CompilerParams(dimension_semantics=sem, vmem_limit_bytes=VMEM_LIMIT_BYTES)


def _resident(shape, index=None):
    index = (0,) * len(shape) if index is None else index
    return pl.BlockSpec(shape, lambda *_: index, pipeline_mode=pl.Buffered(1))


def _sigmoid(x):
    return 1.0 / (1.0 + jnp.exp(-x))


def _silu(x):
    return x * _sigmoid(x)


def _rms(x, g):
    return x * lax.rsqrt(jnp.mean(x * x, axis=-1, keepdims=True) + EPS) * g


def _norm_mod(x, g, shift, scale):
    return x * lax.rsqrt(jnp.mean(x * x, axis=-1, keepdims=True) + EPS) * (g * (1.0 + scale)) + shift


def _dot(a, b):
    return jnp.dot(a, b, preferred_element_type=F32)


def _dot_nt(a, b):
    return lax.dot_general(a, b, (((1,), (1,)), ((), ())), preferred_element_type=F32)


def _dot_tn(a, b):
    return lax.dot_general(a, b, (((0,), (0,)), ((), ())), preferred_element_type=F32)


def _ada_kernel(c_ref, w_ref, b_ref, o_ref):
    a = _silu(c_ref[...]).astype(BF16)
    o_ref[...] = _dot(a, w_ref[...].astype(BF16)) + b_ref[...]


def _ada(cond, w_ada, b_ada):
    tn = 1536
    n = N_MOD * D_MODEL
    return pl.pallas_call(
        _ada_kernel,
        grid=(DEPTH, n // tn),
        in_specs=[
            pl.BlockSpec((MOD_ROWS, D_MODEL), lambda l, j: (0, 0)),
            pl.BlockSpec((None, D_MODEL, tn), lambda l, j: (l, 0, j)),
            pl.BlockSpec((None, 1, tn), lambda l, j: (l, 0, j)),
        ],
        out_specs=pl.BlockSpec((None, MOD_ROWS, tn), lambda l, j: (l, 0, j)),
        out_shape=jax.ShapeDtypeStruct((DEPTH, MOD_ROWS, n), F32),
        compiler_params=_params("arbitrary", "arbitrary"),
        name="ada",
    )(cond, w_ada, b_ada.reshape(DEPTH, 1, n))


def _log_sigmoid(x):
    return jnp.minimum(x, 0.0) - jnp.log(1.0 + jnp.exp(-jnp.abs(x)))


class _TwoStreams:
    def __init__(self, ctx_rows, lat_rows, lat_seq, weight_steps):
        self.weight_steps = weight_steps
        self.ctx_tiles = ctx_rows // ROW_TILE
        self.first_lat = weight_steps + self.ctx_tiles
        self.grid = (self.first_lat + lat_rows // ROW_TILE,)
        self.tiles_per_batch = lat_seq // ROW_TILE

    def _rows(self, tile, width, groups):
        if groups is None:
            return pl.BlockSpec((ROW_TILE, width), lambda i: (tile(i), 0))
        return pl.BlockSpec((groups, ROW_TILE, width), lambda i: (0, tile(i), 0))

    def ctx(self, width, groups=None):
        return self._rows(lambda i: jnp.clip(i - self.weight_steps, 0, self.ctx_tiles - 1), width, groups)

    def lat(self, width, groups=None):
        return self._rows(lambda i: jnp.maximum(i - self.first_lat, 0), width, groups)

    def mod(self, layer, which):
        row = lambda i: jnp.where(i < self.first_lat, 0, 1 + (i - self.first_lat) // self.tiles_per_batch)
        return pl.BlockSpec((None, None, None, 1, D_MODEL), lambda i: (layer, row(i), which, 0, 0))

    def slab(self, layer, rows, cols):
        return pl.BlockSpec((None, rows // self.weight_steps, cols),
                            lambda i: (layer, jnp.minimum(i, self.weight_steps - 1), 0))

    def run(self, load_weights, ctx_tile, lat_tile):
        step = pl.program_id(0)
        pl.when(step < self.weight_steps)(lambda: load_weights(step))
        pl.when((step >= self.weight_steps) & (step < self.first_lat))(ctx_tile)
        pl.when(step >= self.first_lat)(lat_tile)


def _round_slab(step, src_ref, dst_ref):
    rows = src_ref.shape[0]
    dst_ref[pl.ds(pl.multiple_of(step * rows, rows), rows), :] = src_ref[...].astype(BF16)


def _gla_in_kernel(xc_ref, xl_ref, g_ref, sh_ref, sc_ref, w_ref, w1_ref, w2_ref, bg_ref, *refs, streams):
    outs_c, outs_l, wb_ref = refs[0:6], refs[6:12], refs[12]

    def tile(x_ref, q_ref, k_ref, v_ref, r_ref, laf_ref, lab_ref):
        h = _norm_mod(x_ref[...], g_ref[...], sh_ref[...], sc_ref[...]).astype(BF16)
        z = _dot(h, wb_ref[...])
        low_rank = _dot(h, w1_ref[...]).astype(BF16)
        la = _log_sigmoid(_dot(low_rank, w2_ref[...]) + bg_ref[...]) * (1.0 / GLA_TAU)
        groups = q_ref.shape[0]
        kw, vw = GLA_HK // groups, GLA_HV // groups
        for s in range(groups):
            q_ref[s] = z[:, s * kw:(s + 1) * kw]
            k_ref[s] = z[:, GLA_HK + s * kw:GLA_HK + (s + 1) * kw]
            v_ref[s] = z[:, 2 * GLA_HK + s * vw:2 * GLA_HK + (s + 1) * vw].astype(BF16)
            r_ref[s] = z[:, 2 * GLA_HK + GLA_HV + s * vw:2 * GLA_HK + GLA_HV + (s + 1) * vw]
            laf_ref[s] = la[:, s * kw:(s + 1) * kw]
            lab_ref[s] = la[:, GLA_HK + s * kw:GLA_HK + (s + 1) * kw]

    streams.run(lambda step: _round_slab(step, w_ref, wb_ref),
                lambda: tile(xc_ref, *outs_c), lambda: tile(xl_ref, *outs_l))


def _gla_in(xc, xl, lat_seq, ctx_groups, lat_groups, mod, layer, g, w_in, w_layer, w1, w2, bg):
    n_out = 2 * GLA_HK + 2 * GLA_HV
    streams = _TwoStreams(xc.shape[0], xl.shape[0], lat_seq, GLA_WEIGHT_STEPS)
    widths = [(GLA_HK, F32), (GLA_HK, F32), (GLA_HV, BF16), (GLA_HV, F32), (GLA_HK, F32), (GLA_HK, F32)]
    outs = pl.pallas_call(
        functools.partial(_gla_in_kernel, streams=streams),
        grid=streams.grid,
        in_specs=[
            streams.ctx(D_MODEL), streams.lat(D_MODEL),
            _resident((1, D_MODEL)),
            streams.mod(layer, 0), streams.mod(layer, 1),
            streams.slab(w_layer, D_MODEL, n_out),
            _resident((D_MODEL, GATE_PAD)),
            _resident((GATE_PAD, 2 * GLA_HK)),
            _resident((1, 2 * GLA_HK)),
        ],
        out_specs=([streams.ctx(w // ctx_groups, ctx_groups) for w, _ in widths]
                   + [streams.lat(w // lat_groups, lat_groups) for w, _ in widths]),
        out_shape=([jax.ShapeDtypeStruct((ctx_groups, xc.shape[0], w // ctx_groups), dt) for w, dt in widths]
                   + [jax.ShapeDtypeStruct((lat_groups, xl.shape[0], w // lat_groups), dt) for w, dt in widths]),
        scratch_shapes=[pltpu.VMEM((D_MODEL, n_out), BF16)],
        compiler_params=_params("arbitrary"),
        name="gla_in",
    )(xc, xl, g, mod, mod, w_in, w1, w2, bg)
    return outs[:6], outs[6:]


def _split3(x):
    hi = x.astype(BF16)
    rem = x - hi.astype(F32)
    mid = rem.astype(BF16)
    lo = (rem - mid.astype(F32)).astype(BF16)
    return jnp.concatenate([hi, mid, lo], axis=1)


def _fold3(x):
    return x[:, :GLA_DK] + x[:, GLA_DK:2 * GLA_DK] + x[:, 2 * GLA_DK:]


def _gla_scan_kernel(*refs, n_groups, hps, has_s0):
    C, DK, DV, G = GLA_CHUNK, GLA_DK, GLA_DV, GLA_GROUP
    cpg = G // C
    n_chunks = n_groups * cpg
    unroll = max(1, min(n_groups, GLA_CHAINS // hps))
    if has_s0:
        (q_ref, k_ref, v_ref, laf_ref, lab_ref, r_ref, gh_ref, s0f_ref, s0b_ref,
         y_ref, qd_ref, kd_ref, ds_ref, sin_ref, dec_ref) = refs
    else:
        (q_ref, k_ref, v_ref, laf_ref, lab_ref, r_ref, gh_ref,
         y_ref, sf_ref, sb_ref, qd_ref, kd_ref, ds_ref, sin_ref, dec_ref) = refs

    row = lax.broadcasted_iota(jnp.int32, (G, G), 0)
    col = lax.broadcasted_iota(jnp.int32, (G, G), 1)
    same_chunk = (row // C) == (col // C)
    lower = same_chunk & (row >= col)
    upper = same_chunk & (row <= col)
    prefix_sum = jnp.where(lower, 1.0, 0.0).astype(BF16)
    suffix_sum = jnp.where(upper, 1.0, 0.0).astype(BF16)

    def group(g):
        return pl.ds(pl.multiple_of(g * G, G), G)

    def spread(x, first_row):
        return jnp.concatenate(
            [jnp.broadcast_to(x[first_row + c * C:first_row + c * C + 1], (C, x.shape[1])) for c in range(cpg)], axis=0)

    def pass1(g, carry):
        sl = group(g)

        def log_decays(hd):
            dk = slice(hd * DK, (hd + 1) * DK)
            bf = _fold3(_dot(prefix_sum, _split3(laf_ref[sl, dk])))
            bb = _fold3(_dot(suffix_sum, _split3(lab_ref[sl, dk])))
            return bf, bb

        nxt = log_decays(0)
        for hd in range(hps):
            dk = slice(hd * DK, (hd + 1) * DK)
            dv = slice(hd * DV, (hd + 1) * DV)
            both = slice(hd * 2 * DK, (hd + 1) * 2 * DK)
            bf, bb = nxt
            if hd + 1 < hps:
                nxt = log_decays(hd + 1)
            tot_f = spread(bf, C - 1)
            tot_b = spread(bb, 0)
            q = q_ref[sl, dk] * (DK ** -0.5)
            k = k_ref[sl, dk]
            qd_ref[sl, both] = jnp.concatenate([q * jnp.exp(bf), q * jnp.exp(bb)], axis=1).astype(BF16)
            kd_ref[sl, both] = jnp.concatenate([k * jnp.exp(-bf), k * jnp.exp(-bb)], axis=1).astype(BF16)
            k_end = jnp.concatenate([k * jnp.exp(tot_f - bf), k * jnp.exp(tot_b - bb)], axis=1).astype(BF16)
            v = v_ref[sl, dv]
            for c in range(cpg):
                rows = slice(c * C, (c + 1) * C)
                ds_ref[hd, g * cpg + c] = _dot_tn(v[rows], k_end[rows])
                dec_ref[hd, g * cpg + c] = jnp.exp(
                    jnp.concatenate([tot_f[c * C:c * C + 1], tot_b[c * C:c * C + 1]], axis=1))
        return carry

    lax.fori_loop(0, n_groups, pass1, 0, unroll=unroll)

    for hd in range(hps):
        for forward in (True, False):
            lanes = slice(0, DK) if forward else slice(DK, 2 * DK)
            if has_s0:
                init = (s0f_ref if forward else s0b_ref)[hd].T
            else:
                init = jnp.zeros((DV, DK), F32)

            def step(i, s, hd=hd, forward=forward, lanes=lanes):
                n = i if forward else n_chunks - 1 - i
                sin_ref[hd, n, :, lanes] = s.astype(BF16)
                return s * dec_ref[hd, n, :, lanes] + ds_ref[hd, n, :, lanes]

            final = lax.fori_loop(0, n_chunks, step, init)
            if not has_s0:
                (sf_ref if forward else sb_ref)[hd] = final.T

    gh = gh_ref[...]

    def pass3(g, carry):
        sl = group(g)
        def intra_scores(hd):
            both = slice(hd * 2 * DK, (hd + 1) * 2 * DK)
            qd = qd_ref[sl, both]
            kd = kd_ref[sl, both]
            return qd, _dot_nt(qd[:, :DK], kd[:, :DK]), _dot_nt(qd[:, DK:], kd[:, DK:])

        nxt = intra_scores(0)
        for hd in range(hps):
            dv = slice(hd * DV, (hd + 1) * DV)
            qd, a_f, a_b = nxt
            if hd + 1 < hps:
                nxt = intra_scores(hd + 1)
            a = jnp.where(lower, a_f, 0.0) + jnp.where(upper, a_b, 0.0)
            inter = [_dot_nt(qd[c * C:(c + 1) * C], sin_ref[hd, g * cpg + c]) for c in range(cpg)]
            o = _dot(a.astype(BF16), v_ref[sl, dv]) + jnp.concatenate(inter, axis=0)
            y_ref[sl, dv] = (_rms(o, gh) * _silu(r_ref[sl, dv])).astype(BF16)
        return carry

    lax.fori_loop(0, n_groups, pass3, 0, unroll=unroll)


def _gla_scan(q, k, v, r, la_f, la_b, g_head, batch, seq, s0f=None, s0b=None):
    steps = q.shape[0]
    hps = GLA_HEADS // steps
    n_chunks = seq // GLA_CHUNK
    has_s0 = s0f is not None
    rows = batch * seq
    grouped = lambda w: pl.BlockSpec((None, seq, w * hps), lambda b, h: (h, b, 0))
    tok = lambda w, off: pl.BlockSpec((seq, w * hps), lambda b, h: (b, h + off))
    state = pl.BlockSpec((None, hps, GLA_DK, GLA_DV), lambda b, h: (b, h, 0, 0))
    in_specs = [grouped(GLA_DK), grouped(GLA_DK), grouped(GLA_DV), grouped(GLA_DK), grouped(GLA_DK),
                grouped(GLA_DV), pl.BlockSpec((1, GLA_DV), lambda b, h: (0, 0))]
    args = [q, k, v, la_f, la_b, r, g_head]
    y_shape = jax.ShapeDtypeStruct((rows, GLA_HV), BF16)
    st_shape = jax.ShapeDtypeStruct((batch, GLA_HEADS, GLA_DK, GLA_DV), F32)
    scratch = [
        pltpu.VMEM((seq, hps * 2 * GLA_DK), BF16),
        pltpu.VMEM((seq, hps * 2 * GLA_DK), BF16),
        pltpu.VMEM((hps, n_chunks, GLA_DV, 2 * GLA_DK), F32),
        pltpu.VMEM((hps, n_chunks, GLA_DV, 2 * GLA_DK), BF16),
        pltpu.VMEM((hps, n_chunks, 1, 2 * GLA_DK), F32),
    ]
    if has_s0:
        in_specs += [state, state]
        args += [s0f, s0b]
        out_specs = tok(GLA_DV, 0)
        out_shape = y_shape
    else:
        out_specs = [tok(GLA_DV, 0), state, state]
        out_shape = [y_shape, st_shape, st_shape]
    return pl.pallas_call(
        functools.partial(_gla_scan_kernel, n_groups=seq // GLA_GROUP, hps=hps, has_s0=has_s0),
        grid=(batch, steps),
        in_specs=in_specs,
        out_specs=out_specs,
        out_shape=out_shape,
        scratch_shapes=scratch,
        compiler_params=_params("arbitrary", "arbitrary"),
        name="gla_scan",
    )(*args)


def _out_ffn_kernel(*refs, final_norm, streams):
    if final_norm:
        (xc_ref, xl_ref, yc_ref, yl_ref, wo_ref, g1_ref, gn_ref, sh_ref, sc_ref, g2_ref, wg_ref, wu_ref, wd_ref,
         gf_ref, oc_ref, ol_ref, wgb_ref, wub_ref, wdb_ref) = refs
    else:
        (xc_ref, xl_ref, yc_ref, yl_ref, wo_ref, g1_ref, gn_ref, sh_ref, sc_ref, g2_ref, wg_ref, wu_ref, wd_ref,
         oc_ref, ol_ref, wgb_ref, wub_ref, wdb_ref) = refs

    def load_weights(step):
        _round_slab(step, wg_ref, wgb_ref)
        _round_slab(step, wu_ref, wub_ref)
        _round_slab(step, wd_ref, wdb_ref)

    def tile(x_ref, y_ref, o_ref):
        x = x_ref[...] + g1_ref[...] * _dot(y_ref[...], wo_ref[...])
        h = _norm_mod(x, gn_ref[...], sh_ref[...], sc_ref[...]).astype(BF16)
        act = (_silu(_dot(h, wgb_ref[...])) * _dot(h, wub_ref[...])).astype(BF16)
        x = x + g2_ref[...] * _dot(act, wdb_ref[...])
        if final_norm:
            x = _rms(x, gf_ref[...])
        o_ref[...] = x

    streams.run(load_weights, lambda: tile(xc_ref, yc_ref, oc_ref), lambda: tile(xl_ref, yl_ref, ol_ref))


def _out_ffn(xc, xl, yc, yl, lat_seq, mod, layer, w_out, g_ffn, wg, wu, wd, g_final=None):
    final_norm = g_final is not None
    streams = _TwoStreams(xc.shape[0], xl.shape[0], lat_seq, FFN_WEIGHT_STEPS)
    in_specs = [
        streams.ctx(D_MODEL), streams.lat(D_MODEL), streams.ctx(D_MODEL), streams.lat(D_MODEL),
        _resident((D_MODEL, D_MODEL)),
        streams.mod(layer, 2),
        _resident((1, D_MODEL)),
        streams.mod(layer, 3), streams.mod(layer, 4), streams.mod(layer, 5),
        streams.slab(layer, D_MODEL, D_FF), streams.slab(layer, D_MODEL, D_FF), streams.slab(layer, D_FF, D_MODEL),
    ]
    args = [xc, xl, yc, yl, w_out, mod, g_ffn, mod, mod, mod, wg, wu, wd]
    if final_norm:
        in_specs.append(_resident((1, D_MODEL)))
        args.append(g_final)
    return pl.pallas_call(
        functools.partial(_out_ffn_kernel, final_norm=final_norm, streams=streams),
        grid=streams.grid,
        in_specs=in_specs,
        out_specs=[streams.ctx(D_MODEL), streams.lat(D_MODEL)],
        out_shape=[jax.ShapeDtypeStruct(xc.shape, F32), jax.ShapeDtypeStruct(xl.shape, F32)],
        scratch_shapes=[pltpu.VMEM((D_MODEL, D_FF), BF16), pltpu.VMEM((D_MODEL, D_FF), BF16),
                        pltpu.VMEM((D_FF, D_MODEL), BF16)],
        compiler_params=_params("arbitrary"),
        name="out_ffn",
    )(*args)


def _rope_tables(n_tokens):
    rows = n_tokens // GRID_W
    r, col = jnp.meshgrid(jnp.arange(rows), jnp.arange(GRID_W), indexing="ij")
    r = r.reshape(-1).astype(F32)
    col = col.reshape(-1).astype(F32)
    n_freq = DIFF_DH // 4
    inv = ROPE_THETA ** (-jnp.arange(n_freq, dtype=F32) / n_freq)
    ang = jnp.concatenate([r[:, None] * inv, col[:, None] * inv], axis=-1)
    cos, sin = jnp.cos(ang), jnp.sin(ang)
    reps = D_MODEL // DIFF_DH
    return jnp.tile(jnp.concatenate([cos, cos], axis=-1), (1, reps)), jnp.tile(jnp.concatenate([-sin, sin], axis=-1), (1, reps))


def _rope(x, cos, sin_signed):
    half = DIFF_DH // 2
    width = x.shape[1]
    lane = lax.broadcasted_iota(jnp.int32, x.shape, 1)
    from_above = pltpu.roll(x, width - half, axis=1)
    from_below = pltpu.roll(x, half, axis=1)
    swapped = jnp.where(lane % DIFF_DH < half, from_above, from_below)
    return x * cos + swapped * sin_signed


def _qkv_kernel(xc_ref, xl_ref, g_ref, sh_ref, sc_ref, w_ref, cos_ref, sin_ref,
                qc_ref, kc_ref, vc_ref, kf_ref, vf_ref, ql_ref, kl_ref, vl_ref, wb_ref, *, streams, ctx_seq):
    def project(x_ref):
        h = _norm_mod(x_ref[...], g_ref[...], sh_ref[...], sc_ref[...]).astype(BF16)
        z = _dot(h, wb_ref[...])
        return z[:, :D_MODEL], z[:, D_MODEL:2 * D_MODEL], z[:, 2 * D_MODEL:]

    def store(q, k, v, q_ref, k_ref, v_ref):
        q_ref[...] = (q * (DIFF_DH ** -0.5 * LOG2E)).astype(BF16)
        k_ref[...] = k.astype(BF16)
        v_ref[...] = v.astype(BF16)

    def ctx_tile():
        q, k, v = project(xc_ref)
        for b in range(ROW_TILE // ctx_seq):
            rows = slice(b * ctx_seq, (b + 1) * ctx_seq)
            kf_ref[b] = k[rows].T
            qc_ref[b] = (q[rows] * (DIFF_DH ** -0.5 * LOG2E)).T.astype(BF16)
            vc_ref[b] = v[rows].T.astype(BF16)
        by_head = jnp.stack([v[:, head * DIFF_DV:(head + 1) * DIFF_DV] for head in range(DIFF_HEADS)], axis=0)
        vf_ref[...] = jnp.swapaxes(by_head, 0, 1)
        kc_ref[...] = k.astype(BF16)

    def lat_tile():
        q, k, v = project(xl_ref)
        store(_rope(q, cos_ref[...], sin_ref[...]), _rope(k, cos_ref[...], sin_ref[...]), v, ql_ref, kl_ref, vl_ref)

    streams.run(lambda step: _round_slab(step, w_ref, wb_ref), ctx_tile, lat_tile)


def _qkv(xc, xl, ctx_seq, lat_seq, mod, layer, g, w, w_layer, tables):
    streams = _TwoStreams(xc.shape[0], xl.shape[0], lat_seq, QKV_WEIGHT_STEPS)
    per_tile = ROW_TILE // ctx_seq
    parked_ctx = lambda i: jnp.clip(i - streams.weight_steps, 0, streams.ctx_tiles - 1)
    table = pl.BlockSpec((ROW_TILE, D_MODEL),
                         lambda i: (lax.rem(jnp.maximum(i - streams.first_lat, 0), streams.tiles_per_batch), 0))
    bf = lambda rows: jax.ShapeDtypeStruct((rows, D_MODEL), BF16)
    per_batch = pl.BlockSpec((per_tile, D_MODEL, ctx_seq), lambda i: (parked_ctx(i), 0, 0))
    feature_major = lambda dtype: jax.ShapeDtypeStruct((xc.shape[0] // ctx_seq, D_MODEL, ctx_seq), dtype)
    outs = pl.pallas_call(
        functools.partial(_qkv_kernel, streams=streams, ctx_seq=ctx_seq),
        grid=streams.grid,
        in_specs=[streams.ctx(D_MODEL), streams.lat(D_MODEL), _resident((1, D_MODEL)),
                  streams.mod(layer, 0), streams.mod(layer, 1), streams.slab(w_layer, D_MODEL, 3 * D_MODEL),
                  table, table],
        out_specs=[per_batch, streams.ctx(D_MODEL), per_batch, per_batch,
                   pl.BlockSpec((ROW_TILE, DIFF_HEADS, DIFF_DV), lambda i: (parked_ctx(i), 0, 0))]
        + [streams.lat(D_MODEL)] * 3,
        out_shape=[feature_major(BF16), bf(xc.shape[0]), feature_major(BF16), feature_major(F32),
           jax.ShapeDtypeStruct((xc.shape[0], DIFF_HEADS, DIFF_DV), F32)]
        + [bf(xl.shape[0])] * 3,
        scratch_shapes=[pltpu.VMEM((D_MODEL, 3 * D_MODEL), BF16)],
        compiler_params=_params("arbitrary"),
        name="diff_qkv",
    )(xc, xl, g, mod, mod, w, *tables)
    return outs[:5], outs[5:]


def _attn_ctx_kernel(lq_ref, lk_ref, q_ref, k_ref, v_ref, gh_ref, y_ref, *, lam_init):
    seq = k_ref.shape[0]
    e = jnp.exp(jnp.sum(lq_ref[...] * lk_ref[...], axis=1, keepdims=True))
    lam = e[0:1, :] - e[1:2, :] + lam_init
    first_map = lax.broadcasted_iota(jnp.int32, (2 * DIFF_DH, seq), 0) < DIFF_DH
    gh = gh_ref[...] * (1.0 - lam_init)
    def features(head):
        return slice(head * 2 * DIFF_DH, (head + 1) * 2 * DIFF_DH)

    def scores(head):
        q_t = q_ref[features(head), :]
        zero = jnp.zeros_like(q_t)
        qq = jnp.concatenate([jnp.where(first_map, q_t, zero), jnp.where(first_map, zero, q_t)], axis=1)
        return _dot(k_ref[:, features(head)], qq)

    def attend(head, s):
        p = jnp.exp2(s - jnp.max(s, axis=0, keepdims=True))
        denom = jnp.sum(p, axis=0, keepdims=True)
        acc = _dot(v_ref[features(head), :], p.astype(BF16))
        o = acc[:, :seq] * (1.0 / denom[:, :seq]) - acc[:, seq:] * (lam / denom[:, seq:])
        o = o * lax.rsqrt(jnp.mean(o * o, axis=0, keepdims=True) + EPS) * gh
        y_ref[:, features(head)] = o.T.astype(BF16)

    s_next = scores(0)
    for head in range(DIFF_HEADS):
        s_cur = s_next
        if head + 1 < DIFF_HEADS:
            s_next = scores(head + 1)
        attend(head, s_cur)


def _attn_ctx(q_t, k, v_t, lam_q, lam_k, g_head, lam_init, batch, seq):
    small = lambda shape: pl.BlockSpec(shape, lambda b: (0, 0))
    feature_major = pl.BlockSpec((None, D_MODEL, seq), lambda b: (b, 0, 0))
    tok = pl.BlockSpec((seq, D_MODEL), lambda b: (b, 0))
    return pl.pallas_call(
        functools.partial(_attn_ctx_kernel, lam_init=lam_init),
        grid=(batch,),
        in_specs=[small((2, DIFF_DH)), small((2, DIFF_DH)), feature_major, tok, feature_major, small((DIFF_DV, 1))],
        out_specs=tok,
        out_shape=jax.ShapeDtypeStruct((batch * seq, D_MODEL), BF16),
        compiler_params=_params("arbitrary"),
        name="diff_attn_ctx",
    )(lam_q, lam_k, q_t, k, v_t, g_head)


def _attn_kernel(*refs, has_cache, lam_init, n_q_tiles, q_tile):
    if has_cache:
        (lq_ref, lk_ref, q_ref, k_ref, v_ref, gh_ref, kc_ref, vc_ref, y_ref, kcb_ref, vcb_ref) = refs
    else:
        (lq_ref, lk_ref, q_ref, k_ref, v_ref, gh_ref, y_ref) = refs
    e = jnp.exp(jnp.sum(lq_ref[...] * lk_ref[...], axis=1, keepdims=True))
    lam = e[0:1, :] - e[1:2, :] + lam_init
    first_map = lax.broadcasted_iota(jnp.int32, (q_tile, 2 * DIFF_DH), 1) < DIFF_DH
    gh = gh_ref[...] * (1.0 - lam_init)

    if has_cache:
        for head in range(DIFF_HEADS):
            kcb_ref[head] = kc_ref[head].astype(BF16)
        vcb_ref[...] = jnp.swapaxes(vc_ref[...], 0, 1).astype(BF16)

    def head_lanes(head):
        return slice(head * 2 * DIFF_DH, (head + 1) * 2 * DIFF_DH)

    def scores(head, sl):
        q = q_ref[sl, head_lanes(head)]
        zero = jnp.zeros_like(q)
        qq = jnp.concatenate([jnp.where(first_map, q, zero), jnp.where(first_map, zero, q)], axis=0)
        s = [_dot_nt(qq, k_ref[:, head_lanes(head)])]
        if has_cache:
            s.insert(0, _dot(qq, kcb_ref[head]))
        return s

    def attend(head, sl, s):
        vals = [v_ref[:, head_lanes(head)]]
        if has_cache:
            vals.insert(0, vcb_ref[head])
        m = functools.reduce(jnp.maximum, [jnp.max(x, axis=-1, keepdims=True) for x in s])
        p = [jnp.exp2(x - m) for x in s]
        denom = functools.reduce(jnp.add, [jnp.sum(x, axis=-1, keepdims=True) for x in p])
        acc = functools.reduce(jnp.add, [_dot(x.astype(BF16), vv) for x, vv in zip(p, vals)])
        o = acc[:q_tile] * (1.0 / denom[:q_tile]) - acc[q_tile:] * (lam / denom[q_tile:])
        y_ref[sl, head_lanes(head)] = _rms(o, gh).astype(BF16)

    def body(t, carry):
        sl = pl.ds(pl.multiple_of(t * q_tile, q_tile), q_tile)
        s_next = scores(0, sl)
        for head in range(DIFF_HEADS):
            s_cur = s_next
            if head + 1 < DIFF_HEADS:
                s_next = scores(head + 1, sl)
            attend(head, sl, s_cur)
        return carry

    lax.fori_loop(0, n_q_tiles, body, 0)


def _attn(q, k, v, lam_q, lam_k, g_head, lam_init, batch, seq, cache_k=None, cache_v=None):
    has_cache = cache_k is not None
    tok = pl.BlockSpec((seq, D_MODEL), lambda b: (b, 0))
    small = lambda shape: pl.BlockSpec(shape, lambda b: (0, 0))
    in_specs = [small((2, DIFF_DH)), small((2, DIFF_DH)), tok, tok, tok, small((1, DIFF_DV))]
    args = [lam_q, lam_k, q, k, v, g_head]
    if has_cache:
        past = cache_k.shape[-1]
        in_specs += [pl.BlockSpec((None, DIFF_HEADS, 2 * DIFF_DH, past), lambda b: (b, 0, 0, 0)),
                     pl.BlockSpec((None, past, DIFF_HEADS, DIFF_DV), lambda b: (b, 0, 0, 0))]
        args += [cache_k, cache_v]
        scratch = [pltpu.VMEM((DIFF_HEADS, 2 * DIFF_DH, past), BF16), pltpu.VMEM((DIFF_HEADS, past, DIFF_DV), BF16)]
    else:
        scratch = []
    q_tile = min(Q_TILE, seq)
    return pl.pallas_call(
        functools.partial(_attn_kernel, has_cache=has_cache, lam_init=lam_init, n_q_tiles=seq // q_tile, q_tile=q_tile),
        grid=(batch,),
        in_specs=in_specs,
        out_specs=tok,
        out_shape=jax.ShapeDtypeStruct((batch * seq, D_MODEL), BF16),
        scratch_shapes=scratch,
        compiler_params=_params("arbitrary"),
        name="diff_attn",
    )(*args)


def kernel(x_prompt, x_sample, state_gla_fwd, state_gla_bwd, cache_diff_k, cache_diff_v, c, c_ctx, w_ada, b_ada, g_mix_norm, g_ffn_norm, w_gla_in, w_gla_g1, w_gla_g2, b_gla_g, g_gla_head, w_gla_out, w_diff_qkv, lam_q, lam_k, g_diff_head, w_diff_out, w_ffn_gate, w_ffn_up, w_ffn_down, g_final):
    bp, tp, _ = x_prompt.shape
    bs, ts, _ = x_sample.shape
    assert ROW_TILE % tp == 0 and ts % ROW_TILE == 0 and (bp * tp) % ROW_TILE == 0
    assert tp % GLA_GROUP == 0 and ts % GLA_GROUP == 0
    xp = x_prompt.reshape(bp * tp, D_MODEL)
    xs = x_sample.reshape(bs * ts, D_MODEL)

    cond = jnp.concatenate([c_ctx[None, :], c, jnp.zeros((MOD_ROWS - 1 - bs, D_MODEL), F32)], axis=0)
    mod = _ada(cond, w_ada, b_ada).reshape(DEPTH, MOD_ROWS, N_MOD, 1, D_MODEL)
    wg, wu, wd = w_ffn_gate, w_ffn_up, w_ffn_down

    j = 0
    w1 = jnp.concatenate([w_gla_g1[j, 0], w_gla_g1[j, 1], jnp.zeros((D_MODEL, GATE_PAD - 2 * GLA_RANK), F32)],
                         axis=1).astype(BF16)
    w2 = jnp.zeros((GATE_PAD, 2 * GLA_HK), F32)
    w2 = w2.at[:GLA_RANK, :GLA_HK].set(w_gla_g2[j, 0]).at[GLA_RANK:2 * GLA_RANK, GLA_HK:].set(w_gla_g2[j, 1]).astype(BF16)
    bg = b_gla_g[j].reshape(1, 2 * GLA_HK)
    g_mix = g_mix_norm[0].reshape(1, D_MODEL)
    g_head = g_gla_head[j].reshape(1, GLA_DV)
    g_ffn = g_ffn_norm[0].reshape(1, D_MODEL)
    w_out = w_gla_out[j].astype(BF16)

    gla_p, gla_s = _gla_in(xp, xs, ts, 1, GLA_HEADS // 2, mod, 0, g_mix, w_gla_in, j, w1, w2, bg)
    yp, new_f, new_b = _gla_scan(*gla_p, g_head, bp, tp)
    ys = _gla_scan(*gla_s, g_head, bs, ts, state_gla_fwd[:, j], state_gla_bwd[:, j])
    xp, xs = _out_ffn(xp, xs, yp, ys, ts, mod, 0, w_out, g_ffn, wg, wu, wd)

    lam_init = 0.8 - 0.6 * math.exp(-0.3 * 1)
    g_mix = g_mix_norm[1].reshape(1, D_MODEL)
    g_head = g_diff_head[j].reshape(1, DIFF_DV)
    g_ffn = g_ffn_norm[1].reshape(1, D_MODEL)
    w_out = w_diff_out[j].astype(BF16)

    (qp, kp, vp, kp32, vp32), (qs, ks, vs) = _qkv(xp, xs, tp, ts, mod, 1, g_mix, w_diff_qkv, j, _rope_tables(ts))
    yp = _attn_ctx(qp, kp, vp, lam_q[j], lam_k[j], g_head.reshape(DIFF_DV, 1), lam_init, bp, tp)
    past = cache_diff_k.shape[2]
    cache_k_t = jnp.transpose(cache_diff_k[:, j], (0, 2, 3, 4, 1)).reshape(bs, DIFF_HEADS, 2 * DIFF_DH, past)
    ys = _attn(qs, ks, vs, lam_q[j], lam_k[j], g_head, lam_init, bs, ts, cache_k_t, cache_diff_v[:, j])
    g_fin = g_final.reshape(1, D_MODEL)
    yp_out, ys_out = _out_ffn(xp, xs, yp, ys, ts, mod, 1, w_out, g_ffn, wg, wu, wd, g_final=g_fin)

    return (yp_out.reshape(bp, tp, D_MODEL),
            ys_out.reshape(bs, ts, D_MODEL),
            new_f[:, None],
            new_b[:, None],
            jnp.transpose(kp32.reshape(bp, 1, DIFF_HEADS, 2, DIFF_DH, tp), (0, 1, 5, 2, 3, 4)),
            vp32.reshape(bp, 1, tp, DIFF_HEADS, DIFF_DV))
```

```python
import functools
import math

import jax
import jax.numpy as jnp
from jax import lax
from jax.experimental import pallas as pl
from jax.experimental.pallas import tpu as pltpu

F32 = jnp.float32
BF16 = jnp.bfloat16

D_MODEL = 1024
DEPTH = 2
GRID_W = 64
GLA_HEADS = 4
GLA_DK = 128
GLA_DV = 256
GLA_HK = GLA_HEADS * GLA_DK
GLA_HV = GLA_HEADS * GLA_DV
GLA_RANK = 16
GLA_TAU = 16.0
GLA_CHUNK = 64
GLA_GROUP = 256
GLA_CHAINS = 8
DIFF_HEADS = 8
DIFF_DH = 64
DIFF_DV = 128
ROPE_THETA = 10000.0
D_FF = 2816
EPS = 1e-6
LOG2E = math.log2(math.e)
N_MOD = 6
MOD_ROWS = 16
GATE_PAD = 128

VMEM_LIMIT_BYTES = 58 * 1024 * 1024
ROW_TILE = 512
FFN_WEIGHT_STEPS = 8
GLA_WEIGHT_STEPS = 4
QKV_WEIGHT_STEPS = 4
Q_TILE = 256


def _params(*sem):
    return pltpu.CompilerParams(dimension_semantics=sem, vmem_limit_bytes=VMEM_LIMIT_BYTES)


def _resident(shape, index=None):
    index = (0,) * len(shape) if index is None else index
    return pl.BlockSpec(shape, lambda *_: index, pipeline_mode=pl.Buffered(1))


def _sigmoid(x):
    return 1.0 / (1.0 + jnp.exp(-x))


def _silu(x):
    return x * _sigmoid(x)


def _rms(x, g):
    return x * lax.rsqrt(jnp.mean(x * x, axis=-1, keepdims=True) + EPS) * g


def _norm_mod(x, g, shift, scale):
    return x * lax.rsqrt(jnp.mean(x * x, axis=-1, keepdims=True) + EPS) * (g * (1.0 + scale)) + shift


def _dot(a, b):
    return jnp.dot(a, b, preferred_element_type=F32)


def _dot_nt(a, b):
    return lax.dot_general(a, b, (((1,), (1,)), ((), ())), preferred_element_type=F32)


def _dot_tn(a, b):
    return lax.dot_general(a, b, (((0,), (0,)), ((), ())), preferred_element_type=F32)


def _ada_kernel(c_ref, w_ref, b_ref, o_ref):
    a = _silu(c_ref[...]).astype(BF16)
    o_ref[...] = _dot(a, w_ref[...].astype(BF16)) + b_ref[...]


def _ada(cond, w_ada, b_ada):
    tn = 1536
    n = N_MOD * D_MODEL
    return pl.pallas_call(
        _ada_kernel,
        grid=(DEPTH, n // tn),
        in_specs=[
            pl.BlockSpec((MOD_ROWS, D_MODEL), lambda l, j: (0, 0)),
            pl.BlockSpec((None, D_MODEL, tn), lambda l, j: (l, 0, j)),
            pl.BlockSpec((None, 1, tn), lambda l, j: (l, 0, j)),
        ],
        out_specs=pl.BlockSpec((None, MOD_ROWS, tn), lambda l, j: (l, 0, j)),
        out_shape=jax.ShapeDtypeStruct((DEPTH, MOD_ROWS, n), F32),
        compiler_params=_params("arbitrary", "arbitrary"),
        name="ada",
    )(cond, w_ada, b_ada.reshape(DEPTH, 1, n))


def _log_sigmoid(x):
    return jnp.minimum(x, 0.0) - jnp.log(1.0 + jnp.exp(-jnp.abs(x)))


class _TwoStreams:
    def __init__(self, ctx_rows, lat_rows, lat_seq, weight_steps):
        self.weight_steps = weight_steps
        self.ctx_tiles = ctx_rows // ROW_TILE
        self.first_lat = weight_steps + self.ctx_tiles
        self.grid = (self.first_lat + lat_rows // ROW_TILE,)
        self.tiles_per_batch = lat_seq // ROW_TILE

    def _rows(self, tile, width, groups):
        if groups is None:
            return pl.BlockSpec((ROW_TILE, width), lambda i: (tile(i), 0))
        return pl.BlockSpec((groups, ROW_TILE, width), lambda i: (0, tile(i), 0))

    def ctx(self, width, groups=None):
        return self._rows(lambda i: jnp.clip(i - self.weight_steps, 0, self.ctx_tiles - 1), width, groups)

    def lat(self, width, groups=None):
        return self._rows(lambda i: jnp.maximum(i - self.first_lat, 0), width, groups)

    def mod(self, layer, which):
        row = lambda i: jnp.where(i < self.first_lat, 0, 1 + (i - self.first_lat) // self.tiles_per_batch)
        return pl.BlockSpec((None, None, None, 1, D_MODEL), lambda i: (layer, row(i), which, 0, 0))

    def slab(self, layer, rows, cols):
        return pl.BlockSpec((None, rows // self.weight_steps, cols),
                            lambda i: (layer, jnp.minimum(i, self.weight_steps - 1), 0))

    def run(self, load_weights, ctx_tile, lat_tile):
        step = pl.program_id(0)
        pl.when(step < self.weight_steps)(lambda: load_weights(step))
        pl.when((step >= self.weight_steps) & (step < self.first_lat))(ctx_tile)
        pl.when(step >= self.first_lat)(lat_tile)


def _round_slab(step, src_ref, dst_ref):
    rows = src_ref.shape[0]
    dst_ref[pl.ds(pl.multiple_of(step * rows, rows), rows), :] = src_ref[...].astype(BF16)


def _gla_in_kernel(xc_ref, xl_ref, g_ref, sh_ref, sc_ref, w_ref, w1_ref, w2_ref, bg_ref, *refs, streams):
    outs_c, outs_l, wb_ref = refs[0:6], refs[6:12], refs[12]

    def tile(x_ref, q_ref, k_ref, v_ref, r_ref, laf_ref, lab_ref):
        h = _norm_mod(x_ref[...], g_ref[...], sh_ref[...], sc_ref[...]).astype(BF16)
        z = _dot(h, wb_ref[...])
        low_rank = _dot(h, w1_ref[...]).astype(BF16)
        la = _log_sigmoid(_dot(low_rank, w2_ref[...]) + bg_ref[...]) * (1.0 / GLA_TAU)
        groups = q_ref.shape[0]
        kw, vw = GLA_HK // groups, GLA_HV // groups
        for s in range(groups):
            q_ref[s] = z[:, s * kw:(s + 1) * kw]
            k_ref[s] = z[:, GLA_HK + s * kw:GLA_HK + (s + 1) * kw]
            v_ref[s] = z[:, 2 * GLA_HK + s * vw:2 * GLA_HK + (s + 1) * vw].astype(BF16)
            r_ref[s] = z[:, 2 * GLA_HK + GLA_HV + s * vw:2 * GLA_HK + GLA_HV + (s + 1) * vw]
            laf_ref[s] = la[:, s * kw:(s + 1) * kw]
            lab_ref[s] = la[:, GLA_HK + s * kw:GLA_HK + (s + 1) * kw]

    streams.run(lambda step: _round_slab(step, w_ref, wb_ref),
                lambda: tile(xc_ref, *outs_c), lambda: tile(xl_ref, *outs_l))


def _gla_in(xc, xl, lat_seq, ctx_groups, lat_groups, mod, layer, g, w_in, w_layer, w1, w2, bg):
    n_out = 2 * GLA_HK + 2 * GLA_HV
    streams = _TwoStreams(xc.shape[0], xl.shape[0], lat_seq, GLA_WEIGHT_STEPS)
    widths = [(GLA_HK, F32), (GLA_HK, F32), (GLA_HV, BF16), (GLA_HV, F32), (GLA_HK, F32), (GLA_HK, F32)]
    outs = pl.pallas_call(
        functools.partial(_gla_in_kernel, streams=streams),
        grid=streams.grid,
        in_specs=[
            streams.ctx(D_MODEL), streams.lat(D_MODEL),
            _resident((1, D_MODEL)),
            streams.mod(layer, 0), streams.mod(layer, 1),
            streams.slab(w_layer, D_MODEL, n_out),
            _resident((D_MODEL, GATE_PAD)),
            _resident((GATE_PAD, 2 * GLA_HK)),
            _resident((1, 2 * GLA_HK)),
        ],
        out_specs=([streams.ctx(w // ctx_groups, ctx_groups) for w, _ in widths]
                   + [streams.lat(w // lat_groups, lat_groups) for w, _ in widths]),
        out_shape=([jax.ShapeDtypeStruct((ctx_groups, xc.shape[0], w // ctx_groups), dt) for w, dt in widths]
                   + [jax.ShapeDtypeStruct((lat_groups, xl.shape[0], w // lat_groups), dt) for w, dt in widths]),
        scratch_shapes=[pltpu.VMEM((D_MODEL, n_out), BF16)],
        compiler_params=_params("arbitrary"),
        name="gla_in",
    )(xc, xl, g, mod, mod, w_in, w1, w2, bg)
    return outs[:6], outs[6:]


def _split3(x):
    hi = x.astype(BF16)
    rem = x - hi.astype(F32)
    mid = rem.astype(BF16)
    lo = (rem - mid.astype(F32)).astype(BF16)
    return jnp.concatenate([hi, mid, lo], axis=1)


def _fold3(x):
    return x[:, :GLA_DK] + x[:, GLA_DK:2 * GLA_DK] + x[:, 2 * GLA_DK:]


def _gla_scan_kernel(*refs, n_groups, hps, has_s0):
    C, DK, DV, G = GLA_CHUNK, GLA_DK, GLA_DV, GLA_GROUP
    cpg = G // C
    n_chunks = n_groups * cpg
    unroll = max(1, min(n_groups, GLA_CHAINS // hps))
    if has_s0:
        (q_ref, k_ref, v_ref, laf_ref, lab_ref, r_ref, gh_ref, s0f_ref, s0b_ref,
         y_ref, qd_ref, kd_ref, ds_ref, sin_ref, dec_ref) = refs
    else:
        (q_ref, k_ref, v_ref, laf_ref, lab_ref, r_ref, gh_ref,
         y_ref, sf_ref, sb_ref, qd_ref, kd_ref, ds_ref, sin_ref, dec_ref) = refs

    row = lax.broadcasted_iota(jnp.int32, (G, G), 0)
    col = lax.broadcasted_iota(jnp.int32, (G, G), 1)
    same_chunk = (row // C) == (col // C)
    lower = same_chunk & (row >= col)
    upper = same_chunk & (row <= col)
    prefix_sum = jnp.where(lower, 1.0, 0.0).astype(BF16)
    suffix_sum = jnp.where(upper, 1.0, 0.0).astype(BF16)

    def group(g):
        return pl.ds(pl.multiple_of(g * G, G), G)

    def spread(x, first_row):
        return jnp.concatenate(
            [jnp.broadcast_to(x[first_row + c * C:first_row + c * C + 1], (C, x.shape[1])) for c in range(cpg)], axis=0)

    def pass1(g, carry):
        sl = group(g)

        def log_decays(hd):
            dk = slice(hd * DK, (hd + 1) * DK)
            bf = _fold3(_dot(prefix_sum, _split3(laf_ref[sl, dk])))
            bb = _fold3(_dot(suffix_sum, _split3(lab_ref[sl, dk])))
            return bf, bb

        nxt = log_decays(0)
        for hd in range(hps):
            dk = slice(hd * DK, (hd + 1) * DK)
            dv = slice(hd * DV, (hd + 1) * DV)
            both = slice(hd * 2 * DK, (hd + 1) * 2 * DK)
            bf, bb = nxt
            if hd + 1 < hps:
                nxt = log_decays(hd + 1)
            tot_f = spread(bf, C - 1)
            tot_b = spread(bb, 0)
            q = q_ref[sl, dk] * (DK ** -0.5)
            k = k_ref[sl, dk]
            qd_ref[sl, both] = jnp.concatenate([q * jnp.exp(bf), q * jnp.exp(bb)], axis=1).astype(BF16)
            kd_ref[sl, both] = jnp.concatenate([k * jnp.exp(-bf), k * jnp.exp(-bb)], axis=1).astype(BF16)
            k_end = jnp.concatenate([k * jnp.exp(tot_f - bf), k * jnp.exp(tot_b - bb)], axis=1).astype(BF16)
            v = v_ref[sl, dv]
            for c in range(cpg):
                rows = slice(c * C, (c + 1) * C)
                ds_ref[hd, g * cpg + c] = _dot_tn(v[rows], k_end[rows])
                dec_ref[hd, g * cpg + c] = jnp.exp(
                    jnp.concatenate([tot_f[c * C:c * C + 1], tot_b[c * C:c * C + 1]], axis=1))
        return carry

    lax.fori_loop(0, n_groups, pass1, 0, unroll=unroll)

    for hd in range(hps):
        for forward in (True, False):
            lanes = slice(0, DK) if forward else slice(DK, 2 * DK)
            if has_s0:
                init = (s0f_ref if forward else s0b_ref)[hd].T
            else:
                init = jnp.zeros((DV, DK), F32)

            def step(i, s, hd=hd, forward=forward, lanes=lanes):
                n = i if forward else n_chunks - 1 - i
                sin_ref[hd, n, :, lanes] = s.astype(BF16)
                return s * dec_ref[hd, n, :, lanes] + ds_ref[hd, n, :, lanes]

            final = lax.fori_loop(0, n_chunks, step, init)
            if not has_s0:
                (sf_ref if forward else sb_ref)[hd] = final.T

    gh = gh_ref[...]

    def pass3(g, carry):
        sl = group(g)
        def intra_scores(hd):
            both = slice(hd * 2 * DK, (hd + 1) * 2 * DK)
            qd = qd_ref[sl, both]
            kd = kd_ref[sl, both]
            return qd, _dot_nt(qd[:, :DK], kd[:, :DK]), _dot_nt(qd[:, DK:], kd[:, DK:])

        nxt = intra_scores(0)
        for hd in range(hps):
            dv = slice(hd * DV, (hd + 1) * DV)
            qd, a_f, a_b = nxt
            if hd + 1 < hps:
                nxt = intra_scores(hd + 1)
            a = jnp.where(lower, a_f, 0.0) + jnp.where(upper, a_b, 0.0)
            inter = [_dot_nt(qd[c * C:(c + 1) * C], sin_ref[hd, g * cpg + c]) for c in range(cpg)]
            o = _dot(a.astype(BF16), v_ref[sl, dv]) + jnp.concatenate(inter, axis=0)
            y_ref[sl, dv] = (_rms(o, gh) * _silu(r_ref[sl, dv])).astype(BF16)
        return carry

    lax.fori_loop(0, n_groups, pass3, 0, unroll=unroll)


def _gla_scan(q, k, v, r, la_f, la_b, g_head, batch, seq, s0f=None, s0b=None):
    steps = q.shape[0]
    hps = GLA_HEADS // steps
    n_chunks = seq // GLA_CHUNK
    has_s0 = s0f is not None
    rows = batch * seq
    grouped = lambda w: pl.BlockSpec((None, seq, w * hps), lambda b, h: (h, b, 0))
    tok = lambda w, off: pl.BlockSpec((seq, w * hps), lambda b, h: (b, h + off))
    state = pl.BlockSpec((None, hps, GLA_DK, GLA_DV), lambda b, h: (b, h, 0, 0))
    in_specs = [grouped(GLA_DK), grouped(GLA_DK), grouped(GLA_DV), grouped(GLA_DK), grouped(GLA_DK),
                grouped(GLA_DV), pl.BlockSpec((1, GLA_DV), lambda b, h: (0, 0))]
    args = [q, k, v, la_f, la_b, r, g_head]
    y_shape = jax.ShapeDtypeStruct((rows, GLA_HV), BF16)
    st_shape = jax.ShapeDtypeStruct((batch, GLA_HEADS, GLA_DK, GLA_DV), F32)
    scratch = [
        pltpu.VMEM((seq, hps * 2 * GLA_DK), BF16),
        pltpu.VMEM((seq, hps * 2 * GLA_DK), BF16),
        pltpu.VMEM((hps, n_chunks, GLA_DV, 2 * GLA_DK), F32),
        pltpu.VMEM((hps, n_chunks, GLA_DV, 2 * GLA_DK), BF16),
        pltpu.VMEM((hps, n_chunks, 1, 2 * GLA_DK), F32),
    ]
    if has_s0:
        in_specs += [state, state]
        args += [s0f, s0b]
        out_specs = tok(GLA_DV, 0)
        out_shape = y_shape
    else:
        out_specs = [tok(GLA_DV, 0), state, state]
        out_shape = [y_shape, st_shape, st_shape]
    return pl.pallas_call(
        functools.partial(_gla_scan_kernel, n_groups=seq // GLA_GROUP, hps=hps, has_s0=has_s0),
        grid=(batch, steps),
        in_specs=in_specs,
        out_specs=out_specs,
        out_shape=out_shape,
        scratch_shapes=scratch,
        compiler_params=_params("arbitrary", "arbitrary"),
        name="gla_scan",
    )(*args)


def _out_ffn_kernel(*refs, final_norm, streams):
    if final_norm:
        (xc_ref, xl_ref, yc_ref, yl_ref, wo_ref, g1_ref, gn_ref, sh_ref, sc_ref, g2_ref, wg_ref, wu_ref, wd_ref,
         gf_ref, oc_ref, ol_ref, wgb_ref, wub_ref, wdb_ref) = refs
    else:
        (xc_ref, xl_ref, yc_ref, yl_ref, wo_ref, g1_ref, gn_ref, sh_ref, sc_ref, g2_ref, wg_ref, wu_ref, wd_ref,
         oc_ref, ol_ref, wgb_ref, wub_ref, wdb_ref) = refs

    def load_weights(step):
        _round_slab(step, wg_ref, wgb_ref)
        _round_slab(step, wu_ref, wub_ref)
        _round_slab(step, wd_ref, wdb_ref)

    def tile(x_ref, y_ref, o_ref):
        x = x_ref[...] + g1_ref[...] * _dot(y_ref[...], wo_ref[...])
        h = _norm_mod(x, gn_ref[...], sh_ref[...], sc_ref[...]).astype(BF16)
        split = (D_FF // 512 + 1) * 256
        halves = [slice(0, split), slice(split, D_FF)]
        gate_up = lambda c: (_dot(h, wgb_ref[:, c]), _dot(h, wub_ref[:, c]))
        nxt = gate_up(halves[0])
        down = None
        for i, c in enumerate(halves):
            gate, up = nxt
            if i + 1 < len(halves):
                nxt = gate_up(halves[i + 1])
            part = _dot((_silu(gate) * up).astype(BF16), wdb_ref[c, :])
            down = part if down is None else down + part
        x = x + g2_ref[...] * down
        if final_norm:
            x = _rms(x, gf_ref[...])
        o_ref[...] = x

    streams.run(load_weights, lambda: tile(xc_ref, yc_ref, oc_ref), lambda: tile(xl_ref, yl_ref, ol_ref))


def _out_ffn(xc, xl, yc, yl, lat_seq, mod, layer, w_out, g_ffn, wg, wu, wd, g_final=None):
    final_norm = g_final is not None
    streams = _TwoStreams(xc.shape[0], xl.shape[0], lat_seq, FFN_WEIGHT_STEPS)
    in_specs = [
        streams.ctx(D_MODEL), streams.lat(D_MODEL), streams.ctx(D_MODEL), streams.lat(D_MODEL),
        _resident((D_MODEL, D_MODEL)),
        streams.mod(layer, 2),
        _resident((1, D_MODEL)),
        streams.mod(layer, 3), streams.mod(layer, 4), streams.mod(layer, 5),
        streams.slab(layer, D_MODEL, D_FF), streams.slab(layer, D_MODEL, D_FF), streams.slab(layer, D_FF, D_MODEL),
    ]
    args = [xc, xl, yc, yl, w_out, mod, g_ffn, mod, mod, mod, wg, wu, wd]
    if final_norm:
        in_specs.append(_resident((1, D_MODEL)))
        args.append(g_final)
    return pl.pallas_call(
        functools.partial(_out_ffn_kernel, final_norm=final_norm, streams=streams),
        grid=streams.grid,
        in_specs=in_specs,
        out_specs=[streams.ctx(D_MODEL), streams.lat(D_MODEL)],
        out_shape=[jax.ShapeDtypeStruct(xc.shape, F32), jax.ShapeDtypeStruct(xl.shape, F32)],
        scratch_shapes=[pltpu.VMEM((D_MODEL, D_FF), BF16), pltpu.VMEM((D_MODEL, D_FF), BF16),
                        pltpu.VMEM((D_FF, D_MODEL), BF16)],
        compiler_params=_params("arbitrary"),
        name="out_ffn",
    )(*args)


def _rope_tables(n_tokens):
    rows = n_tokens // GRID_W
    r, col = jnp.meshgrid(jnp.arange(rows), jnp.arange(GRID_W), indexing="ij")
    r = r.reshape(-1).astype(F32)
    col = col.reshape(-1).astype(F32)
    n_freq = DIFF_DH // 4
    inv = ROPE_THETA ** (-jnp.arange(n_freq, dtype=F32) / n_freq)
    ang = jnp.concatenate([r[:, None] * inv, col[:, None] * inv], axis=-1)
    cos, sin = jnp.cos(ang), jnp.sin(ang)
    reps = D_MODEL // DIFF_DH
    return jnp.tile(jnp.concatenate([cos, cos], axis=-1), (1, reps)), jnp.tile(jnp.concatenate([-sin, sin], axis=-1), (1, reps))


def _rope(x, cos, sin_signed):
    half = DIFF_DH // 2
    width = x.shape[1]
    lane = lax.broadcasted_iota(jnp.int32, x.shape, 1)
    from_above = pltpu.roll(x, width - half, axis=1)
    from_below = pltpu.roll(x, half, axis=1)
    swapped = jnp.where(lane % DIFF_DH < half, from_above, from_below)
    return x * cos + swapped * sin_signed


def _qkv_kernel(xc_ref, xl_ref, g_ref, sh_ref, sc_ref, w_ref, cos_ref, sin_ref,
                qc_ref, kc_ref, vc_ref, kf_ref, vf_ref, ql_ref, kl_ref, vl_ref, wb_ref, *, streams, ctx_seq):
    def project(x_ref):
        h = _norm_mod(x_ref[...], g_ref[...], sh_ref[...], sc_ref[...]).astype(BF16)
        z = _dot(h, wb_ref[...])
        return z[:, :D_MODEL], z[:, D_MODEL:2 * D_MODEL], z[:, 2 * D_MODEL:]

    def store(q, k, v, q_ref, k_ref, v_ref):
        q_ref[...] = (q * (DIFF_DH ** -0.5 * LOG2E)).astype(BF16)
        k_ref[...] = k.astype(BF16)
        v_ref[...] = v.astype(BF16)

    def ctx_tile():
        q, k, v = project(xc_ref)
        for b in range(ROW_TILE // ctx_seq):
            rows = slice(b * ctx_seq, (b + 1) * ctx_seq)
            kf_ref[b] = k[rows].T
            qc_ref[b] = (q[rows] * (DIFF_DH ** -0.5 * LOG2E)).T.astype(BF16)
            vc_ref[b] = v[rows].T.astype(BF16)
        by_head = jnp.stack([v[:, head * DIFF_DV:(head + 1) * DIFF_DV] for head in range(DIFF_HEADS)], axis=0)
        vf_ref[...] = jnp.swapaxes(by_head, 0, 1)
        kc_ref[...] = k.astype(BF16)

    def lat_tile():
        q, k, v = project(xl_ref)
        store(_rope(q, cos_ref[...], sin_ref[...]), _rope(k, cos_ref[...], sin_ref[...]), v, ql_ref, kl_ref, vl_ref)

    streams.run(lambda step: _round_slab(step, w_ref, wb_ref), ctx_tile, lat_tile)


def _qkv(xc, xl, ctx_seq, lat_seq, mod, layer, g, w, w_layer, tables):
    streams = _TwoStreams(xc.shape[0], xl.shape[0], lat_seq, QKV_WEIGHT_STEPS)
    per_tile = ROW_TILE // ctx_seq
    parked_ctx = lambda i: jnp.clip(i - streams.weight_steps, 0, streams.ctx_tiles - 1)
    table = pl.BlockSpec((ROW_TILE, D_MODEL),
                         lambda i: (lax.rem(jnp.maximum(i - streams.first_lat, 0), streams.tiles_per_batch), 0))
    bf = lambda rows: jax.ShapeDtypeStruct((rows, D_MODEL), BF16)
    per_batch = pl.BlockSpec((per_tile, D_MODEL, ctx_seq), lambda i: (parked_ctx(i), 0, 0))
    feature_major = lambda dtype: jax.ShapeDtypeStruct((xc.shape[0] // ctx_seq, D_MODEL, ctx_seq), dtype)
    outs = pl.pallas_call(
        functools.partial(_qkv_kernel, streams=streams, ctx_seq=ctx_seq),
        grid=streams.grid,
        in_specs=[streams.ctx(D_MODEL), streams.lat(D_MODEL), _resident((1, D_MODEL)),
                  streams.mod(layer, 0), streams.mod(layer, 1), streams.slab(w_layer, D_MODEL, 3 * D_MODEL),
                  table, table],
        out_specs=[per_batch, streams.ctx(D_MODEL), per_batch, per_batch,
                   pl.BlockSpec((ROW_TILE, DIFF_HEADS, DIFF_DV), lambda i: (parked_ctx(i), 0, 0))]
        + [streams.lat(D_MODEL)] * 3,
        out_shape=[feature_major(BF16), bf(xc.shape[0]), feature_major(BF16), feature_major(F32),
           jax.ShapeDtypeStruct((xc.shape[0], DIFF_HEADS, DIFF_DV), F32)]
        + [bf(xl.shape[0])] * 3,
        scratch_shapes=[pltpu.VMEM((D_MODEL, 3 * D_MODEL), BF16)],
        compiler_params=_params("arbitrary"),
        name="diff_qkv",
    )(xc, xl, g, mod, mod, w, *tables)
    return outs[:5], outs[5:]


def _attn_ctx_kernel(lq_ref, lk_ref, q_ref, k_ref, v_ref, gh_ref, y_ref, *, lam_init):
    seq = k_ref.shape[0]
    e = jnp.exp(jnp.sum(lq_ref[...] * lk_ref[...], axis=1, keepdims=True))
    lam = e[0:1, :] - e[1:2, :] + lam_init
    first_map = lax.broadcasted_iota(jnp.int32, (2 * DIFF_DH, seq), 0) < DIFF_DH
    gh = gh_ref[...] * (1.0 - lam_init)
    def features(head):
        return slice(head * 2 * DIFF_DH, (head + 1) * 2 * DIFF_DH)

    def scores(head):
        q_t = q_ref[features(head), :]
        zero = jnp.zeros_like(q_t)
        qq = jnp.concatenate([jnp.where(first_map, q_t, zero), jnp.where(first_map, zero, q_t)], axis=1)
        return _dot(k_ref[:, features(head)], qq)

    def attend(head, s):
        p = jnp.exp2(s - jnp.max(s, axis=0, keepdims=True))
        denom = jnp.sum(p, axis=0, keepdims=True)
        acc = _dot(v_ref[features(head), :], p.astype(BF16))
        o = acc[:, :seq] * (1.0 / denom[:, :seq]) - acc[:, seq:] * (lam / denom[:, seq:])
        o = o * lax.rsqrt(jnp.mean(o * o, axis=0, keepdims=True) + EPS) * gh
        y_ref[:, features(head)] = o.T.astype(BF16)

    s_next = scores(0)
    for head in range(DIFF_HEADS):
        s_cur = s_next
        if head + 1 < DIFF_HEADS:
            s_next = scores(head + 1)
        attend(head, s_cur)


def _attn_ctx(q_t, k, v_t, lam_q, lam_k, g_head, lam_init, batch, seq):
    small = lambda shape: pl.BlockSpec(shape, lambda b: (0, 0))
    feature_major = pl.BlockSpec((None, D_MODEL, seq), lambda b: (b, 0, 0))
    tok = pl.BlockSpec((seq, D_MODEL), lambda b: (b, 0))
    return pl.pallas_call(
        functools.partial(_attn_ctx_kernel, lam_init=lam_init),
        grid=(batch,),
        in_specs=[small((2, DIFF_DH)), small((2, DIFF_DH)), feature_major, tok, feature_major, small((DIFF_DV, 1))],
        out_specs=tok,
        out_shape=jax.ShapeDtypeStruct((batch * seq, D_MODEL), BF16),
        compiler_params=_params("arbitrary"),
        name="diff_attn_ctx",
    )(lam_q, lam_k, q_t, k, v_t, g_head)


def _attn_kernel(*refs, has_cache, lam_init, n_q_tiles, q_tile):
    if has_cache:
        (lq_ref, lk_ref, q_ref, k_ref, v_ref, gh_ref, kc_ref, vc_ref, y_ref, kcb_ref, vcb_ref) = refs
    else:
        (lq_ref, lk_ref, q_ref, k_ref, v_ref, gh_ref, y_ref) = refs
    e = jnp.exp(jnp.sum(lq_ref[...] * lk_ref[...], axis=1, keepdims=True))
    lam = e[0:1, :] - e[1:2, :] + lam_init
    first_map = lax.broadcasted_iota(jnp.int32, (q_tile, 2 * DIFF_DH), 1) < DIFF_DH
    gh = gh_ref[...] * (1.0 - lam_init)

    if has_cache:
        for head in range(DIFF_HEADS):
            kcb_ref[head] = kc_ref[head].astype(BF16)
        vcb_ref[...] = jnp.swapaxes(vc_ref[...], 0, 1).astype(BF16)

    def head_lanes(head):
        return slice(head * 2 * DIFF_DH, (head + 1) * 2 * DIFF_DH)

    def scores(head, sl):
        q = q_ref[sl, head_lanes(head)]
        zero = jnp.zeros_like(q)
        qq = jnp.concatenate([jnp.where(first_map, q, zero), jnp.where(first_map, zero, q)], axis=0)
        s = [_dot_nt(qq, k_ref[:, head_lanes(head)])]
        if has_cache:
            s.insert(0, _dot(qq, kcb_ref[head]))
        return s

    def attend(head, sl, s):
        vals = [v_ref[:, head_lanes(head)]]
        if has_cache:
            vals.insert(0, vcb_ref[head])
        m = functools.reduce(jnp.maximum, [jnp.max(x, axis=-1, keepdims=True) for x in s])
        p = [jnp.exp2(x - m) for x in s]
        denom = functools.reduce(jnp.add, [jnp.sum(x, axis=-1, keepdims=True) for x in p])
        acc = functools.reduce(jnp.add, [_dot(x.astype(BF16), vv) for x, vv in zip(p, vals)])
        o = acc[:q_tile] * (1.0 / denom[:q_tile]) - acc[q_tile:] * (lam / denom[q_tile:])
        y_ref[sl, head_lanes(head)] = _rms(o, gh).astype(BF16)

    def body(t, carry):
        sl = pl.ds(pl.multiple_of(t * q_tile, q_tile), q_tile)
        s_next = scores(0, sl)
        for head in range(DIFF_HEADS):
            s_cur = s_next
            if head + 1 < DIFF_HEADS:
                s_next = scores(head + 1, sl)
            attend(head, sl, s_cur)
        return carry

    lax.fori_loop(0, n_q_tiles, body, 0)


def _attn(q, k, v, lam_q, lam_k, g_head, lam_init, batch, seq, cache_k=None, cache_v=None):
    has_cache = cache_k is not None
    tok = pl.BlockSpec((seq, D_MODEL), lambda b: (b, 0))
    small = lambda shape: pl.BlockSpec(shape, lambda b: (0, 0))
    in_specs = [small((2, DIFF_DH)), small((2, DIFF_DH)), tok, tok, tok, small((1, DIFF_DV))]
    args = [lam_q, lam_k, q, k, v, g_head]
    if has_cache:
        past = cache_k.shape[-1]
        in_specs += [pl.BlockSpec((None, DIFF_HEADS, 2 * DIFF_DH, past), lambda b: (b, 0, 0, 0)),
                     pl.BlockSpec((None, past, DIFF_HEADS, DIFF_DV), lambda b: (b, 0, 0, 0))]
        args += [cache_k, cache_v]
        scratch = [pltpu.VMEM((DIFF_HEADS, 2 * DIFF_DH, past), BF16), pltpu.VMEM((DIFF_HEADS, past, DIFF_DV), BF16)]
    else:
        scratch = []
    q_tile = min(Q_TILE, seq)
    return pl.pallas_call(
        functools.partial(_attn_kernel, has_cache=has_cache, lam_init=lam_init, n_q_tiles=seq // q_tile, q_tile=q_tile),
        grid=(batch,),
        in_specs=in_specs,
        out_specs=tok,
        out_shape=jax.ShapeDtypeStruct((batch * seq, D_MODEL), BF16),
        scratch_shapes=scratch,
        compiler_params=_params("arbitrary"),
        name="diff_attn",
    )(*args)


def kernel(x_prompt, x_sample, state_gla_fwd, state_gla_bwd, cache_diff_k, cache_diff_v, c, c_ctx, w_ada, b_ada, g_mix_norm, g_ffn_norm, w_gla_in, w_gla_g1, w_gla_g2, b_gla_g, g_gla_head, w_gla_out, w_diff_qkv, lam_q, lam_k, g_diff_head, w_diff_out, w_ffn_gate, w_ffn_up, w_ffn_down, g_final):
    bp, tp, _ = x_prompt.shape
    bs, ts, _ = x_sample.shape
    assert ROW_TILE % tp == 0 and ts % ROW_TILE == 0 and (bp * tp) % ROW_TILE == 0
    assert tp % GLA_GROUP == 0 and ts % GLA_GROUP == 0
    xp = x_prompt.reshape(bp * tp, D_MODEL)
    xs = x_sample.reshape(bs * ts, D_MODEL)

    cond = jnp.concatenate([c_ctx[None, :], c, jnp.zeros((MOD_ROWS - 1 - bs, D_MODEL), F32)], axis=0)
    mod = _ada(cond, w_ada, b_ada).reshape(DEPTH, MOD_ROWS, N_MOD, 1, D_MODEL)
    wg, wu, wd = w_ffn_gate, w_ffn_up, w_ffn_down

    j = 0
    w1 = jnp.concatenate([w_gla_g1[j, 0], w_gla_g1[j, 1], jnp.zeros((D_MODEL, GATE_PAD - 2 * GLA_RANK), F32)],
                         axis=1).astype(BF16)
    w2 = jnp.zeros((GATE_PAD, 2 * GLA_HK), F32)
    w2 = w2.at[:GLA_RANK, :GLA_HK].set(w_gla_g2[j, 0]).at[GLA_RANK:2 * GLA_RANK, GLA_HK:].set(w_gla_g2[j, 1]).astype(BF16)
    bg = b_gla_g[j].reshape(1, 2 * GLA_HK)
    g_mix = g_mix_norm[0].reshape(1, D_MODEL)
    g_head = g_gla_head[j].reshape(1, GLA_DV)
    g_ffn = g_ffn_norm[0].reshape(1, D_MODEL)
    w_out = w_gla_out[j].astype(BF16)

    gla_p, gla_s = _gla_in(xp, xs, ts, 1, GLA_HEADS // 2, mod, 0, g_mix, w_gla_in, j, w1, w2, bg)
    yp, new_f, new_b = _gla_scan(*gla_p, g_head, bp, tp)
    ys = _gla_scan(*gla_s, g_head, bs, ts, state_gla_fwd[:, j], state_gla_bwd[:, j])
    xp, xs = _out_ffn(xp, xs, yp, ys, ts, mod, 0, w_out, g_ffn, wg, wu, wd)

    lam_init = 0.8 - 0.6 * math.exp(-0.3 * 1)
    g_mix = g_mix_norm[1].reshape(1, D_MODEL)
    g_head = g_diff_head[j].reshape(1, DIFF_DV)
    g_ffn = g_ffn_norm[1].reshape(1, D_MODEL)
    w_out = w_diff_out[j].astype(BF16)

    (qp, kp, vp, kp32, vp32), (qs, ks, vs) = _qkv(xp, xs, tp, ts, mod, 1, g_mix, w_diff_qkv, j, _rope_tables(ts))
    yp = _attn_ctx(qp, kp, vp, lam_q[j], lam_k[j], g_head.reshape(DIFF_DV, 1), lam_init, bp, tp)
    past = cache_diff_k.shape[2]
    cache_k_t = jnp.transpose(cache_diff_k[:, j], (0, 2, 3, 4, 1)).reshape(bs, DIFF_HEADS, 2 * DIFF_DH, past)
    ys = _attn(qs, ks, vs, lam_q[j], lam_k[j], g_head, lam_init, bs, ts, cache_k_t, cache_diff_v[:, j])
    g_fin = g_final.reshape(1, D_MODEL)
    yp_out, ys_out = _out_ffn(xp, xs, yp, ys, ts, mod, 1, w_out, g_ffn, wg, wu, wd, g_final=g_fin)

    return (yp_out.reshape(bp, tp, D_MODEL),
            ys_out.reshape(bs, ts, D_MODEL),
            new_f[:, None],
            new_b[:, None],
            jnp.transpose(kp32.reshape(bp, 1, DIFF_HEADS, 2, DIFF_DH, tp), (0, 1, 5, 2, 3, 4)),
            vp32.reshape(bp, 1, tp, DIFF_HEADS, DIFF_DV))
```
